```python
import math
import jax, jax.numpy as jnp
from jax import lax
import numpy as np

D_MODEL = 2048
BATCH = 1
SEQ = 8192
DEPTH = 1
DEC_BATCH = 16
DEC_SEQ = 16
PAST_LEN = 1024

CHUNK = 64
N_META = 16
HEAD_DIM = 64
N_Q_HEADS = 32
N_KV_HEADS = 4
GQA_GROUP = N_Q_HEADS // N_KV_HEADS
WINDOW = 128
WIN_CHUNKS = WINDOW // CHUNK
CACHE_ROWS = min(WINDOW, PAST_LEN)
ATTN_WIDTH = N_Q_HEADS * HEAD_DIM
KV_WIDTH = N_KV_HEADS * HEAD_DIM
M_HEADS = 8
M_QK_DIM = 128
M_V_DIM = 256
M_QK_WIDTH = M_HEADS * M_QK_DIM
M_V_WIDTH = M_HEADS * M_V_DIM
EPS = 1e-6
NEG_INF = -1e30
COL_SIZES = (ATTN_WIDTH, KV_WIDTH, KV_WIDTH, ATTN_WIDTH, M_QK_WIDTH, M_QK_WIDTH, M_V_WIDTH, M_V_WIDTH,
             M_HEADS, M_HEADS, M_V_WIDTH, D_MODEL, D_MODEL)
IN_COLS = sum(COL_SIZES)

kernel_name = "swa_sink_mlstm_parallel_gated_stream"


def _rmsnorm(x, g):
    x32 = x.astype(jnp.float32)
    y = x32 * lax.rsqrt(jnp.mean(x32 * x32, axis=-1, keepdims=True) + EPS)
    return (y * g.astype(jnp.float32)).astype(x.dtype)


def _project(xn, w_in, b_i, b_f):
    B, T, _ = xn.shape
    u = xn @ w_in
    split_points = [int(s) for s in np.cumsum(COL_SIZES)[:-1]]
    qa, ka, va, za, qm, km, vm, om, ig, fg, zm, ga, gm = jnp.split(u, split_points, axis=-1)
    q_att = qa.reshape(B, T, N_KV_HEADS, GQA_GROUP, HEAD_DIM)
    k_att = ka.reshape(B, T, N_KV_HEADS, HEAD_DIM)
    v_att = va.reshape(B, T, N_KV_HEADS, HEAD_DIM)
    heads = lambda a, d: a.reshape(B, T, M_HEADS, d).transpose(0, 2, 1, 3).astype(jnp.float32)
    q_m = heads(qm, M_QK_DIM)
    k_m = heads(km, M_QK_DIM) * (M_QK_DIM ** -0.5)
    v_m = heads(vm, M_V_DIM)
    log_i = (ig + b_i).astype(jnp.float32).transpose(0, 2, 1)
    log_f = jax.nn.log_sigmoid((fg + b_f).astype(jnp.float32)).transpose(0, 2, 1)
    return q_att, k_att, v_att, (q_m, k_m, v_m, log_i, log_f), (za, om, zm, ga, gm)


def _attend(q, k, v, sinks, valid=None):
    s = jnp.einsum('...qhgd,...khd->...hgqk', q, k).astype(jnp.float32) * (1.0 / math.sqrt(HEAD_DIM))
    if valid is not None:
        s = jnp.where(valid[..., None, None, :, :], s, NEG_INF)
    sink = jnp.broadcast_to(sinks.reshape(N_KV_HEADS, GQA_GROUP)[:, :, None, None].astype(jnp.float32),
                            s.shape[:-1] + (1,))
    p = jax.nn.softmax(jnp.concatenate([s, sink], axis=-1), axis=-1)[..., :-1].astype(v.dtype)
    return jnp.einsum('...hgqk,...khd->...qhgd', p, v)


def _band(a, n_chunks):
    B = a.shape[0]
    ap = jnp.pad(a, ((0, 0), (WIN_CHUNKS * CHUNK, 0), (0, 0), (0, 0)))
    ap = ap.reshape((B, n_chunks + WIN_CHUNKS, CHUNK) + a.shape[2:])
    return jnp.concatenate([ap[:, j:j + n_chunks] for j in range(WIN_CHUNKS + 1)], axis=2)


def _mlstm_block(carry, blk):
    C, n, m = carry
    q, k, v, log_i, log_f = blk
    L = q.shape[-2]
    b = jnp.cumsum(log_f, axis=-1)
    tril = jnp.tril(jnp.ones((L, L), dtype=bool))
    dmat = jnp.where(tril, b[..., :, None] - b[..., None, :] + log_i[..., None, :], -jnp.inf)
    inter = b + m[..., None]
    m_t = jnp.maximum(inter, jnp.max(dmat, axis=-1))
    w = jnp.exp(dmat - m_t[..., None])
    a = jnp.exp(inter - m_t)
    wqk = w * jnp.einsum('bhtd,bhsd->bhts', q, k)
    num = a[..., None] * jnp.einsum('bhtd,bhde->bhte', q, C) + jnp.einsum('bhts,bhse->bhte', wqk, v)
    den = a * jnp.einsum('bhtd,bhd->bht', q, n) + jnp.sum(wqk, axis=-1)
    h = num / jnp.maximum(jnp.abs(den), jnp.exp(-m_t))[..., None]
    m_new = m_t[..., -1]
    ws = jnp.exp(b[..., -1:] - b + log_i - m_new[..., None])
    a_last = jnp.exp(inter[..., -1] - m_new)
    C_new = a_last[..., None, None] * C + jnp.einsum('bhs,bhsd,bhse->bhde', ws, k, v)
    n_new = a_last[..., None] * n + jnp.einsum('bhs,bhsd->bhd', ws, k)
    return (C_new, n_new, m_new), h


def _mlstm_scan(state, q, k, v, log_i, log_f):
    B, H, T, _ = q.shape
    nb = T // CHUNK
    blocks = lambda a: jnp.moveaxis(a.reshape(a.shape[:2] + (nb, CHUNK) + a.shape[3:]), 2, 0)
    state, h = lax.scan(_mlstm_block, state, (blocks(q), blocks(k), blocks(v), blocks(log_i), blocks(log_f)))
    return state, jnp.moveaxis(h, 0, 2).reshape(B, H, T, M_V_DIM)


def _merge(x, o_att, h_m, gates, g_mh, w_pa, w_pm, w_out):
    za, om, zm, ga, gm = gates
    B, T, _ = x.shape
    hm = h_m.transpose(0, 2, 1, 3)
    hm = hm * lax.rsqrt(jnp.mean(hm * hm, axis=-1, keepdims=True) + EPS)
    hm = (hm.reshape(B, T, M_V_WIDTH) * g_mh.astype(jnp.float32)).astype(x.dtype)
    ya = (o_att * jax.nn.silu(za)) @ w_pa
    ym = (hm * jax.nn.sigmoid(om) * jax.nn.silu(zm)) @ w_pm
    merged = jax.nn.sigmoid(ga) * ya + jax.nn.sigmoid(gm) * ym
    return x + merged @ w_out


def setup_inputs(seed: int = 0) -> dict:
    key = jax.random.key(seed)
    ks = jax.random.split(key, 20)
    f32 = jnp.float32
    nrm = lambda k, shape, s: jax.random.normal(k, shape, f32) * s
    return {
        "x_prompt": nrm(ks[0], (BATCH, SEQ, D_MODEL), 1.0),
        "x_sample": nrm(ks[1], (DEC_BATCH, DEC_SEQ, D_MODEL), 1.0),
        "cache_k": nrm(ks[2], (DEPTH, DEC_BATCH, CACHE_ROWS, N_KV_HEADS, HEAD_DIM), 1.0),
        "cache_v": nrm(ks[3], (DEPTH, DEC_BATCH, CACHE_ROWS, N_KV_HEADS, HEAD_DIM), 1.0),
        "state_C": nrm(ks[4], (DEPTH, DEC_BATCH, M_HEADS, M_QK_DIM, M_V_DIM), 0.05),
        "state_n": nrm(ks[5], (DEPTH, DEC_BATCH, M_HEADS, M_QK_DIM), 0.1),
        "state_m": jax.random.uniform(ks[6], (DEPTH, DEC_BATCH, M_HEADS), f32, 0.0, 2.0),
        "meta_tokens": nrm(ks[7], (N_META, D_MODEL), 1.0),
        "g_norm": 1.0 + nrm(ks[8], (DEPTH, D_MODEL), 0.05),
        "w_in": nrm(ks[9], (DEPTH, D_MODEL, IN_COLS), D_MODEL ** -0.5),
        "b_igate": nrm(ks[10], (DEPTH, M_HEADS), 0.1),
        "b_fgate": 3.0 + nrm(ks[11], (DEPTH, M_HEADS), 0.1),
        "attn_sinks": nrm(ks[12], (DEPTH, N_Q_HEADS), 0.5),
        "g_mhnorm": 1.0 + nrm(ks[13], (DEPTH, M_V_WIDTH), 0.05),
        "w_pa": nrm(ks[14], (DEPTH, ATTN_WIDTH, D_MODEL), ATTN_WIDTH ** -0.5),
        "w_pm": nrm(ks[15], (DEPTH, M_V_WIDTH, D_MODEL), M_V_WIDTH ** -0.5),
        "w_out": nrm(ks[16], (DEPTH, D_MODEL, D_MODEL), D_MODEL ** -0.5),
        "g_final": 1.0 + nrm(ks[17], (D_MODEL,), 0.05),
    }


def reference(x_prompt, x_sample, cache_k, cache_v, state_C, state_n, state_m, meta_tokens, g_norm, w_in,
              b_igate, b_fgate, attn_sinks, g_mhnorm, w_pa, w_pm, w_out, g_final):
    f32 = jnp.float32
    B, S, _ = x_prompt.shape
    DB, DS, _ = x_sample.shape
    n_chunks = S // CHUNK
    band_valid = (jnp.arange(n_chunks)[:, None] - WIN_CHUNKS + jnp.arange(WIN_CHUNKS + 1)[None, :]) >= 0
    band_valid = jnp.repeat(band_valid, CHUNK, axis=1)
    valid_p = jnp.concatenate([jnp.ones((n_chunks, N_META), dtype=bool), band_valid], axis=1)[:, None, :]

    xm = meta_tokens.astype(x_prompt.dtype)
    xp, xs = x_prompt, x_sample
    kp_l, vp_l, Cp_l, np_l, mp_l = [], [], [], [], []
    ks_l, vs_l, Cs_l, ns_l, ms_l = [], [], [], [], []
    for l in range(DEPTH):
        wl = (w_in[l], b_igate[l], b_fgate[l])
        mix = (g_mhnorm[l], w_pa[l], w_pm[l], w_out[l])

        q_meta, k_meta, v_meta, cell_meta, gates_meta = _project(_rmsnorm(xm, g_norm[l])[None], *wl)
        zero = (jnp.zeros((1, M_HEADS, M_QK_DIM, M_V_DIM), f32), jnp.zeros((1, M_HEADS, M_QK_DIM), f32),
                jnp.zeros((1, M_HEADS), f32))
        meta_state, h_meta = _mlstm_block(zero, cell_meta)
        k_meta, v_meta = k_meta[0], v_meta[0]

        qa, ka, va, cell_p, gates_p = _project(_rmsnorm(xp, g_norm[l]), *wl)
        keys_p = jnp.concatenate([jnp.broadcast_to(k_meta, (B, n_chunks) + k_meta.shape), _band(ka, n_chunks)], axis=2)
        vals_p = jnp.concatenate([jnp.broadcast_to(v_meta, (B, n_chunks) + v_meta.shape), _band(va, n_chunks)], axis=2)
        o_p = _attend(qa.reshape(B, n_chunks, CHUNK, N_KV_HEADS, GQA_GROUP, HEAD_DIM), keys_p, vals_p,
                      attn_sinks[l], valid_p)
        init_p = tuple(jnp.broadcast_to(a, (B,) + a.shape[1:]) for a in meta_state)
        st_p, h_p = _mlstm_scan(init_p, *cell_p)
        xp_next = _merge(xp, o_p.reshape(B, S, ATTN_WIDTH), h_p, gates_p, *mix)
        kp_l.append(ka[:, S - CACHE_ROWS:])
        vp_l.append(va[:, S - CACHE_ROWS:])
        Cp_l.append(st_p[0]); np_l.append(st_p[1]); mp_l.append(st_p[2])

        qs, ksn, vsn, cell_s, gates_s = _project(_rmsnorm(xs, g_norm[l]), *wl)
        kwin = jnp.concatenate([cache_k[l].astype(ksn.dtype), ksn], axis=1)
        vwin = jnp.concatenate([cache_v[l].astype(vsn.dtype), vsn], axis=1)
        keys_s = jnp.concatenate([jnp.broadcast_to(k_meta, (DB,) + k_meta.shape), kwin], axis=1)
        vals_s = jnp.concatenate([jnp.broadcast_to(v_meta, (DB,) + v_meta.shape), vwin], axis=1)
        o_s = _attend(qs, keys_s, vals_s, attn_sinks[l])
        carry_s = (state_C[l].astype(f32), state_n[l].astype(f32), state_m[l].astype(f32))
        st_s, h_s = _mlstm_block(carry_s, cell_s)
        xs_next = _merge(xs, o_s.reshape(DB, DS, ATTN_WIDTH), h_s, gates_s, *mix)
        ks_l.append(kwin[:, -CACHE_ROWS:])
        vs_l.append(vwin[:, -CACHE_ROWS:])
        Cs_l.append(st_s[0]); ns_l.append(st_s[1]); ms_l.append(st_s[2])

        if l + 1 < DEPTH:
            o_meta = _attend(q_meta[0], k_meta, v_meta, attn_sinks[l])
            xm = _merge(xm[None], o_meta.reshape(1, N_META, ATTN_WIDTH), h_meta, gates_meta, *mix)[0]
        xp, xs = xp_next, xs_next

    return (_rmsnorm(xp, g_final), _rmsnorm(xs, g_final),
            jnp.stack(kp_l), jnp.stack(vp_l), jnp.stack(Cp_l), jnp.stack(np_l), jnp.stack(mp_l),
            jnp.stack(ks_l), jnp.stack(vs_l), jnp.stack(Cs_l), jnp.stack(ns_l), jnp.stack(ms_l))
```

```python
import functools
import math

import jax
import jax.numpy as jnp
from jax import lax
from jax.experimental import pallas as pl
from jax.experimental.pallas import tpu as pltpu

F32 = jnp.float32
BF16 = jnp.bfloat16

D_MODEL = 2048
CHUNK = 64
N_META = 16
HEAD_DIM = 64
N_Q_HEADS = 32
N_KV_HEADS = 4
GQA_GROUP = N_Q_HEADS // N_KV_HEADS
WINDOW = 128
WIN_CHUNKS = WINDOW // CHUNK
ATTN_WIDTH = N_Q_HEADS * HEAD_DIM
KV_WIDTH = N_KV_HEADS * HEAD_DIM
M_HEADS = 8
M_QK_DIM = 128
M_V_DIM = 256
M_QK_WIDTH = M_HEADS * M_QK_DIM
M_V_WIDTH = M_HEADS * M_V_DIM
EPS = 1e-6
NEG_INF = -1e30
COL_SIZES = (ATTN_WIDTH, KV_WIDTH, KV_WIDTH, ATTN_WIDTH, M_QK_WIDTH, M_QK_WIDTH, M_V_WIDTH, M_V_WIDTH,
             M_HEADS, M_HEADS, M_V_WIDTH, D_MODEL, D_MODEL)

LANES = 128
VMEM_LIMIT = 56 * 1024 * 1024

KV_OUT = 2 * KV_WIDTH + LANES
GATE_BLOCK = 2 * KV_WIDTH // LANES
MAIN_OUT = 8 * D_MODEL
COL_QA, COL_ZA, COL_QKM, COL_VM, COL_OM, COL_ZM, COL_GA, COL_GM = range(8)
PROJ_TN = 1024
MLSTM_K_SCALE = M_QK_DIM ** -0.5


def _sigmoid(x):
    return 1.0 / (1.0 + jnp.exp(-x))


def _silu(x):
    return x * _sigmoid(x)


def _proj_body(x_ref, g_ref, wkv_ref, w_ref, o32_ref, o16_ref, xn_ref):
    @pl.when(pl.program_id(1) == 0)
    def _():
        x = x_ref[...]
        ms = jnp.mean(x * x, axis=-1, keepdims=True)
        xn_ref[...] = ((x * lax.rsqrt(ms + EPS)) * g_ref[...]).astype(BF16)
        o32_ref[...] = jnp.dot(xn_ref[...], wkv_ref[...], preferred_element_type=F32)

    o16_ref[...] = jnp.dot(xn_ref[...], w_ref[...], preferred_element_type=F32).astype(BF16)


def _project(x, g, wkv, wmain, tm):
    rows = x.shape[0]
    assert rows % tm == 0
    return pl.pallas_call(
        _proj_body,
        grid=(rows // tm, MAIN_OUT // PROJ_TN),
        in_specs=[
            pl.BlockSpec((tm, D_MODEL), lambda i, j: (i, 0)),
            pl.BlockSpec((1, D_MODEL), lambda i, j: (0, 0)),
            pl.BlockSpec((D_MODEL, KV_OUT), lambda i, j: (0, 0)),
            pl.BlockSpec((D_MODEL, PROJ_TN), lambda i, j: (0, j)),
        ],
        out_specs=[
            pl.BlockSpec((tm, KV_OUT), lambda i, j: (i, 0)),
            pl.BlockSpec((tm, PROJ_TN), lambda i, j: (i, j)),
        ],
        out_shape=[jax.ShapeDtypeStruct((rows, KV_OUT), F32),
                   jax.ShapeDtypeStruct((rows, MAIN_OUT), BF16)],
        scratch_shapes=[pltpu.VMEM((tm, D_MODEL), BF16)],
        compiler_params=pltpu.CompilerParams(
            dimension_semantics=("parallel", "arbitrary"), vmem_limit_bytes=VMEM_LIMIT),
        name="project",
    )(x, g, wkv, wmain)


def _attend_heads(q, k, v, valid_row, sink_ref, o_ref, lq):
    for h in range(N_KV_HEADS):
        kh = k[:, h * HEAD_DIM:(h + 1) * HEAD_DIM]
        vh = v[:, h * HEAD_DIM:(h + 1) * HEAD_DIM]
        heads = [h * GQA_GROUP + g for g in range(GQA_GROUP)]
        qh = jnp.concatenate([q[:, n * HEAD_DIM:(n + 1) * HEAD_DIM] for n in heads], axis=0)
        s = lax.dot_general(qh, kh, (((1,), (1,)), ((), ())), preferred_element_type=F32)
        if valid_row is not None:
            s = jnp.where(valid_row, s, NEG_INF)
        sink = jnp.concatenate(
            [jnp.broadcast_to(sink_ref[n:n + 1, 0:1], (lq, 1)) for n in heads], axis=0)
        m = jnp.maximum(jnp.max(s, axis=-1, keepdims=True), sink)
        e = jnp.exp(s - m)
        denom = jnp.sum(e, axis=-1, keepdims=True) + jnp.exp(sink - m)
        oh = jnp.dot(e.astype(BF16), vh, preferred_element_type=F32) * (1.0 / denom)
        o_ref[:, h * GQA_GROUP * HEAD_DIM:(h + 1) * GQA_GROUP * HEAD_DIM] = jnp.concatenate(
            [oh[g * lq:(g + 1) * lq, :] for g in range(GQA_GROUP)], axis=1).astype(o_ref.dtype)


def _attn_prompt_body(q_ref, k0_ref, k1_ref, k2_ref, v0_ref, v1_ref, v2_ref, km_ref, vm_ref, sink_ref, o_ref):
    c = pl.program_id(0)
    k = jnp.concatenate([k0_ref[...], k1_ref[...], k2_ref[...], km_ref[...]], axis=0).astype(BF16)
    v = jnp.concatenate([v0_ref[...], v1_ref[...], v2_ref[...], vm_ref[...]], axis=0).astype(BF16)
    pos = lax.broadcasted_iota(jnp.int32, (1, (WIN_CHUNKS + 1) * CHUNK + N_META), 1)
    first_valid = jnp.where(c >= 2, 0, jnp.where(c >= 1, CHUNK, 2 * CHUNK))
    _attend_heads(q_ref[...], k, v, pos >= first_valid, sink_ref, o_ref, CHUNK)


def _attend_prompt(o16, o32, kv_small, meta_block, sinks):
    seq = o16.shape[0]
    n_chunks = seq // CHUNK
    band = lambda back, col: pl.BlockSpec((CHUNK, KV_WIDTH), lambda c: (jnp.maximum(c - back, 0), col))
    meta = lambda col: pl.BlockSpec((N_META, KV_WIDTH), lambda c: (meta_block, col))
    return pl.pallas_call(
        _attn_prompt_body,
        grid=(n_chunks,),
        in_specs=[
            pl.BlockSpec((CHUNK, ATTN_WIDTH), lambda c: (c, COL_QA)),
            band(2, 0), band(1, 0), band(0, 0),
            band(2, 1), band(1, 1), band(0, 1),
            meta(0), meta(1),
            pl.BlockSpec((N_Q_HEADS, LANES), lambda c: (0, 0)),
        ],
        out_specs=pl.BlockSpec((CHUNK, ATTN_WIDTH), lambda c: (c, 0)),
        out_shape=jax.ShapeDtypeStruct((seq, ATTN_WIDTH), BF16),
        compiler_params=pltpu.CompilerParams(
            dimension_semantics=("parallel",), vmem_limit_bytes=VMEM_LIMIT),
        name="attend_prompt",
    )(o16, o32, o32, o32, o32, o32, o32, kv_small, kv_small, sinks)


def _attn_sample_body(q_ref, ck_ref, cv_ref, kn_ref, vn_ref, km_ref, vm_ref, sink_ref, o_ref):
    k = jnp.concatenate([ck_ref[...], kn_ref[...], km_ref[...]], axis=0).astype(BF16)
    v = jnp.concatenate([cv_ref[...], vn_ref[...], vm_ref[...]], axis=0).astype(BF16)
    _attend_heads(q_ref[...], k, v, None, sink_ref, o_ref, q_ref.shape[0])


def _attend_sample(o16, o32, cache_k, cache_v, meta_block, sinks, n_streams, ds):
    cache_rows = cache_k.shape[1]
    new = lambda col: pl.BlockSpec((ds, KV_WIDTH), lambda s: (s, col))
    meta = lambda col: pl.BlockSpec((N_META, KV_WIDTH), lambda s: (meta_block, col))
    cache = pl.BlockSpec((None, cache_rows, KV_WIDTH), lambda s: (s, 0, 0))
    return pl.pallas_call(
        _attn_sample_body,
        grid=(n_streams,),
        in_specs=[
            pl.BlockSpec((ds, ATTN_WIDTH), lambda s: (s, COL_QA)),
            cache, cache, new(0), new(1), meta(0), meta(1),
            pl.BlockSpec((N_Q_HEADS, LANES), lambda s: (0, 0)),
        ],
        out_specs=pl.BlockSpec((ds, ATTN_WIDTH), lambda s: (s, 0)),
        out_shape=jax.ShapeDtypeStruct((n_streams * ds, ATTN_WIDTH), BF16),
        compiler_params=pltpu.CompilerParams(
            dimension_semantics=("parallel",), vmem_limit_bytes=VMEM_LIMIT),
        name="attend_sample",
    )(o16, cache_k, cache_v, o32, o32, o32, o32, sinks)


def _mlstm_body(q_ref, k_ref, v_ref, gt_ref, bias_ref, c0_ref, n0_ref, m0_ref,
                hn_ref, c_ref, n_ref, m_ref, *, blk):
    @pl.when(pl.program_id(1) == 0)
    def _():
        c_ref[...] = c0_ref[...]
        n_ref[...] = n0_ref[...]
        m_ref[...] = m0_ref[...]

    x = gt_ref[...] + bias_ref[...]
    lane = lax.broadcasted_iota(jnp.int32, x.shape, 1)
    log_f = jnp.minimum(x, 0.0) - jnp.log(1.0 + jnp.exp(-jnp.abs(x)))
    gates = jnp.where(lane < M_HEADS, x, log_f)
    row = lax.broadcasted_iota(jnp.int32, (blk, blk), 0)
    col = lax.broadcasted_iota(jnp.int32, (blk, blk), 1)
    causal = row >= col
    csum = jnp.dot(causal.astype(F32), gates, precision=lax.Precision.HIGHEST,
                   preferred_element_type=F32)
    z = jnp.where(lane < M_HEADS, gates, csum)
    pad = (-blk) % LANES
    zsq = z if pad == 0 else jnp.concatenate([z, jnp.zeros((pad, LANES), F32)], axis=0)
    zt = zsq.T[:, :blk]

    for h in range(M_HEADS):
        li_c = z[:, h:h + 1]
        b_c = z[:, M_HEADS + h:M_HEADS + h + 1]
        g_r = zt[h:h + 1, :] - zt[M_HEADS + h:M_HEADS + h + 1, :]
        m_prev = m_ref[h:h + 1, 0:1]
        d0 = jnp.where(causal, g_r, -jnp.inf)
        mm = jnp.maximum(m_prev, jnp.max(d0, axis=-1, keepdims=True))
        m_t = b_c + mm
        w = jnp.exp(d0 - mm)
        a = jnp.exp(m_prev - mm)

        qh = q_ref[:, h * M_QK_DIM:(h + 1) * M_QK_DIM]
        kh = k_ref[:, h * M_QK_DIM:(h + 1) * M_QK_DIM]
        vh = v_ref[:, h * M_V_DIM:(h + 1) * M_V_DIM]
        c_h = c_ref[h]
        n_h = n_ref[h:h + 1, :]

        qk = lax.dot_general(qh, kh, (((1,), (1,)), ((), ())), preferred_element_type=F32)
        wqk = w * (qk * MLSTM_K_SCALE)
        num = a * jnp.dot(qh, c_h.astype(BF16), preferred_element_type=F32) + jnp.dot(
            wqk.astype(BF16), vh, preferred_element_type=F32)
        den = a * jnp.sum(qh.astype(F32) * n_h, axis=-1, keepdims=True) + jnp.sum(wqk, axis=-1, keepdims=True)
        hv = num * (1.0 / jnp.maximum(jnp.abs(den), jnp.exp(-m_t)))
        hv = hv * lax.rsqrt(jnp.mean(hv * hv, axis=-1, keepdims=True) + EPS)
        hn_ref[:, h * M_V_DIM:(h + 1) * M_V_DIM] = hv.astype(hn_ref.dtype)

        m_new = m_t[blk - 1:blk, :]
        b_last = b_c[blk - 1:blk, :]
        ws = jnp.exp(b_last - b_c + li_c - m_new) * MLSTM_K_SCALE
        a_last = jnp.exp(b_last + m_prev - m_new)
        vs = (vh.astype(F32) * ws).astype(BF16)
        ktv = lax.dot_general(kh, vs, (((0,), (0,)), ((), ())), preferred_element_type=F32)
        c_ref[h] = a_last * c_h + ktv
        n_ref[h:h + 1, :] = a_last * n_h + jnp.sum(kh.astype(F32) * ws, axis=0, keepdims=True)
        m_ref[h:h + 1, :] = jnp.broadcast_to(m_new, (1, LANES))


def _mlstm(o16, o32, bias, c0, n0, m0, *, blk, n_streams, n_blocks, first_block, shared_init):
    rb = lambda s, c: first_block + s * n_blocks + c
    st = (lambda s: 0) if shared_init else (lambda s: s)
    return pl.pallas_call(
        functools.partial(_mlstm_body, blk=blk),
        grid=(n_streams, n_blocks),
        in_specs=[
            pl.BlockSpec((blk, M_QK_WIDTH), lambda s, c: (rb(s, c), 2 * COL_QKM)),
            pl.BlockSpec((blk, M_QK_WIDTH), lambda s, c: (rb(s, c), 2 * COL_QKM + 1)),
            pl.BlockSpec((blk, M_V_WIDTH), lambda s, c: (rb(s, c), COL_VM)),
            pl.BlockSpec((blk, LANES), lambda s, c: (rb(s, c), GATE_BLOCK)),
            pl.BlockSpec((1, LANES), lambda s, c: (0, 0)),
            pl.BlockSpec((None, M_HEADS, M_QK_DIM, M_V_DIM), lambda s, c: (st(s), 0, 0, 0)),
            pl.BlockSpec((None, M_HEADS, M_QK_DIM), lambda s, c: (st(s), 0, 0)),
            pl.BlockSpec((None, M_HEADS, LANES), lambda s, c: (st(s), 0, 0)),
        ],
        out_specs=[
            pl.BlockSpec((blk, M_V_WIDTH), lambda s, c: (s * n_blocks + c, 0)),
            pl.BlockSpec((None, M_HEADS, M_QK_DIM, M_V_DIM), lambda s, c: (s, 0, 0, 0)),
            pl.BlockSpec((None, M_HEADS, M_QK_DIM), lambda s, c: (s, 0, 0)),
            pl.BlockSpec((None, M_HEADS, LANES), lambda s, c: (s, 0, 0)),
        ],
        out_shape=[
            jax.ShapeDtypeStruct((n_streams * n_blocks * blk, M_V_WIDTH), BF16),
            jax.ShapeDtypeStruct((n_streams, M_HEADS, M_QK_DIM, M_V_DIM), F32),
            jax.ShapeDtypeStruct((n_streams, M_HEADS, M_QK_DIM), F32),
            jax.ShapeDtypeStruct((n_streams, M_HEADS, LANES), F32),
        ],
        compiler_params=pltpu.CompilerParams(
            dimension_semantics=("parallel", "arbitrary"), vmem_limit_bytes=VMEM_LIMIT),
        name="mlstm_blk%d" % blk,
    )(o16, o16, o16, o32, bias, c0, n0, m0)


def _merge_body(x_ref, oa_ref, hn_ref, za_ref, om_ref, zm_ref, ga_ref, gm_ref, gmh_ref,
                wpa_ref, wpm_ref, wout_ref, gf_ref, y_ref):
    a_in = oa_ref[...].astype(F32) * _silu(za_ref[...].astype(F32))
    ya = jnp.dot(a_in.astype(BF16), wpa_ref[...], preferred_element_type=F32)
    m_in = (hn_ref[...].astype(F32) * gmh_ref[...]) * _sigmoid(om_ref[...].astype(F32)) * _silu(
        zm_ref[...].astype(F32))
    ym = jnp.dot(m_in.astype(BF16), wpm_ref[...], preferred_element_type=F32)
    merged = _sigmoid(ga_ref[...].astype(F32)) * ya + _sigmoid(gm_ref[...].astype(F32)) * ym
    xo = x_ref[...] + jnp.dot(merged.astype(BF16), wout_ref[...], preferred_element_type=F32)
    ms = jnp.mean(xo * xo, axis=-1, keepdims=True)
    y_ref[...] = (xo * lax.rsqrt(ms + EPS)) * gf_ref[...]


def _merge(x, o_att, hn, o16, g_mh, w_pa, w_pm, w_out, g_final, rows, tm):
    assert rows % tm == 0
    tile = lambda col: pl.BlockSpec((tm, D_MODEL), lambda i: (i, col))
    vec = pl.BlockSpec((1, D_MODEL), lambda i: (0, 0))
    weight = pl.BlockSpec((D_MODEL, D_MODEL), lambda i: (0, 0), pipeline_mode=pl.Buffered(1))
    return pl.pallas_call(
        _merge_body,
        grid=(rows // tm,),
        in_specs=[tile(0), tile(0), tile(0), tile(COL_ZA), tile(COL_OM), tile(COL_ZM), tile(COL_GA),
                  tile(COL_GM), vec, weight, weight, weight, vec],
        out_specs=tile(0),
        out_shape=jax.ShapeDtypeStruct((rows, D_MODEL), F32),
        compiler_params=pltpu.CompilerParams(
            dimension_semantics=("parallel",), vmem_limit_bytes=VMEM_LIMIT),
        name="merge",
    )(x, o_att, hn, o16, o16, o16, o16, o16, g_mh, w_pa, w_pm, w_out, g_final)


def _prep_in_weights(w_in):
    split_points = []
    acc = 0
    for size in COL_SIZES[:-1]:
        acc += size
        split_points.append(acc)
    qa, ka, va, za, qm, km, vm, om, ig, fg, zm, ga, gm = jnp.split(w_in, split_points, axis=-1)
    pad = jnp.zeros((D_MODEL, LANES - 2 * M_HEADS), w_in.dtype)
    wkv = jnp.concatenate([ka, va, ig, fg, pad], axis=-1).astype(BF16)
    wmain = jnp.concatenate([qa * (1.0 / math.sqrt(HEAD_DIM)), za, qm, km, vm, om, zm, ga, gm],
                            axis=-1).astype(BF16)
    return wkv, wmain


def kernel(x_prompt, x_sample, cache_k, cache_v, state_C, state_n, state_m, meta_tokens, g_norm, w_in,
           b_igate, b_fgate, attn_sinks, g_mhnorm, w_pa, w_pm, w_out, g_final):
    batch, seq, _ = x_prompt.shape
    db, ds, _ = x_sample.shape
    depth = w_in.shape[0]
    assert batch == 1 and depth == 1 and ds == N_META
    cache_rows = cache_k.shape[2]
    mlstm_blk = 256 if seq % 256 == 0 else CHUNK

    wkv, wmain = _prep_in_weights(w_in[0])
    g_in = g_norm[0].reshape(1, D_MODEL)
    bias = jnp.concatenate([b_igate[0], b_fgate[0], jnp.zeros((LANES - 2 * M_HEADS,), F32)]).reshape(1, LANES)
    sinks = jnp.broadcast_to(attn_sinks[0][:, None], (N_Q_HEADS, LANES))
    g_mh = g_mhnorm[0].reshape(1, M_V_WIDTH)
    g_fin = g_final.reshape(1, D_MODEL)
    wpa, wpm, wout = w_pa[0].astype(BF16), w_pm[0].astype(BF16), w_out[0].astype(BF16)

    xp = x_prompt.reshape(seq, D_MODEL)
    xs_rows = db * ds
    x_small = jnp.concatenate([x_sample.reshape(xs_rows, D_MODEL), meta_tokens.astype(F32)], axis=0)
    meta_block = xs_rows // N_META
    p32, p16 = _project(xp, g_in, wkv, wmain, tm=min(seq, 1024))
    s32, s16 = _project(x_small, g_in, wkv, wmain, tm=x_small.shape[0])

    zeros = lambda *shape: jnp.zeros(shape, F32)
    _, c_meta, n_meta, m_meta = _mlstm(
        s16, s32, bias, zeros(1, M_HEADS, M_QK_DIM, M_V_DIM), zeros(1, M_HEADS, M_QK_DIM),
        zeros(1, M_HEADS, LANES), blk=N_META, n_streams=1, n_blocks=1, first_block=meta_block,
        shared_init=True)

    oa_p = _attend_prompt(p16, p32, s32, meta_block, sinks)
    hn_p, c_p, n_p, m_p = _mlstm(p16, p32, bias, c_meta, n_meta, m_meta, blk=mlstm_blk, n_streams=1,
                                 n_blocks=seq // mlstm_blk, first_block=0, shared_init=True)
    y_p = _merge(xp, oa_p, hn_p, p16, g_mh, wpa, wpm, wout, g_fin, rows=seq, tm=min(seq, 256))

    ck = cache_k[0].reshape(db, cache_rows, KV_WIDTH)
    cv = cache_v[0].reshape(db, cache_rows, KV_WIDTH)
    oa_s = _attend_sample(s16, s32, ck, cv, meta_block, sinks, db, ds)
    m0_s = jnp.broadcast_to(state_m[0][:, :, None], (db, M_HEADS, LANES))
    hn_s, c_s, n_s, m_s = _mlstm(s16, s32, bias, state_C[0], state_n[0], m0_s, blk=ds, n_streams=db,
                                 n_blocks=1, first_block=0, shared_init=False)
    y_s = _merge(x_small, oa_s, hn_s, s16, g_mh, wpa, wpm, wout, g_fin, rows=xs_rows, tm=xs_rows)

    kv_shape = (1, batch, cache_rows, N_KV_HEADS, HEAD_DIM)
    k_p = p32[seq - cache_rows:, :KV_WIDTH].reshape(kv_shape)
    v_p = p32[seq - cache_rows:, KV_WIDTH:2 * KV_WIDTH].reshape(kv_shape)
    new_k = s32[:xs_rows, :KV_WIDTH].reshape(db, ds, KV_WIDTH)
    new_v = s32[:xs_rows, KV_WIDTH:2 * KV_WIDTH].reshape(db, ds, KV_WIDTH)
    skv_shape = (1, db, cache_rows, N_KV_HEADS, HEAD_DIM)
    k_s = jnp.concatenate([ck, new_k], axis=1)[:, -cache_rows:].reshape(skv_shape)
    v_s = jnp.concatenate([cv, new_v], axis=1)[:, -cache_rows:].reshape(skv_shape)

    return (y_p.reshape(batch, seq, D_MODEL), y_s.reshape(db, ds, D_MODEL),
            k_p, v_p, c_p[None], n_p[None], m_p[None, :, :, 0],
            k_s, v_s, c_s[None], n_s[None], m_s[None, :, :, 0])
```

```python
import functools
import math

import jax
import jax.numpy as jnp
import numpy as np
from jax import lax
from jax.experimental import pallas as pl
from jax.experimental.pallas import tpu as pltpu

F32 = jnp.float32
BF16 = jnp.bfloat16

D_MODEL = 2048
CHUNK = 64
N_META = 16
HEAD_DIM = 64
N_Q_HEADS = 32
N_KV_HEADS = 4
GQA_GROUP = N_Q_HEADS // N_KV_HEADS
WINDOW = 128
WIN_CHUNKS = WINDOW // CHUNK
ATTN_WIDTH = N_Q_HEADS * HEAD_DIM
KV_WIDTH = N_KV_HEADS * HEAD_DIM
M_HEADS = 8
M_QK_DIM = 128
M_V_DIM = 256
M_QK_WIDTH = M_HEADS * M_QK_DIM
M_V_WIDTH = M_HEADS * M_V_DIM
EPS = 1e-6
NEG_INF = -1e30
COL_SIZES = (ATTN_WIDTH, KV_WIDTH, KV_WIDTH, ATTN_WIDTH, M_QK_WIDTH, M_QK_WIDTH, M_V_WIDTH, M_V_WIDTH,
             M_HEADS, M_HEADS, M_V_WIDTH, D_MODEL, D_MODEL)
COL_STARTS = tuple(int(v) for v in np.cumsum((0,) + COL_SIZES[:-1]))
(SRC_QA, SRC_KA, SRC_VA, SRC_ZA, SRC_QM, SRC_KM, SRC_VM, SRC_OM, SRC_IG, SRC_FG, SRC_ZM, SRC_GA,
 SRC_GM) = COL_STARTS

LANES = 128
VMEM_LIMIT = 56 * 1024 * 1024

KV_OUT = 2 * KV_WIDTH + LANES
GATE_BLOCK = 2 * KV_WIDTH // LANES
MAIN_OUT = 8 * D_MODEL
COL_QA, COL_ZA, COL_QKM, COL_VM, COL_OM, COL_ZM, COL_GA, COL_GM = range(8)
PROJ_TN = 1024
MLSTM_K_SCALE = M_QK_DIM ** -0.5
Q_SCALE = 1.0 / math.sqrt(HEAD_DIM)
PREP_TN = 512
GATE_LANE_SHIFT = 2 * M_HEADS
QV_ROWS = ATTN_WIDTH + KV_WIDTH
PAIR = 2 * CHUNK
PAIR_KEYS = 2 * PAIR + N_META
ONES_ROWS = 16


def _sigmoid(x):
    return 1.0 / (1.0 + jnp.exp(-x))


def _silu(x):
    return x * _sigmoid(x)


def _main_weight_plan():
    groups = [(SRC_QA, ATTN_WIDTH, True), (SRC_ZA, ATTN_WIDTH, False), (SRC_QM, M_QK_WIDTH, False),
              (SRC_KM, M_QK_WIDTH, False), (SRC_VM, M_V_WIDTH, False), (SRC_OM, M_V_WIDTH, False),
              (SRC_ZM, M_V_WIDTH, False), (SRC_GA, D_MODEL, False), (SRC_GM, D_MODEL, False)]
    src, nb, shifted, is_q = [], [], [], []
    for start, width, q in groups:
        shift = start % LANES
        assert shift in (0, GATE_LANE_SHIFT) and (start - shift) % PREP_TN == 0 and width % PREP_TN == 0
        for b in range(width // PREP_TN):
            blk = (start - shift) // PREP_TN + b
            src.append(blk)
            nb.append((blk + 1) * (PREP_TN // LANES))
            shifted.append(int(shift != 0))
            is_q.append(int(q))
    return tuple(np.asarray(a, np.int32) for a in (src, nb, shifted, is_q))


def _prep_main_body(src_ref, nb_ref, shifted_ref, isq_ref, wa_ref, wb_ref, o_ref):
    j = pl.program_id(0)

    @pl.when(shifted_ref[j] == 0)
    def _():
        scale = jnp.where(isq_ref[j] != 0, Q_SCALE, 1.0)
        o_ref[...] = (wa_ref[...] * scale).astype(BF16)

    @pl.when(shifted_ref[j] != 0)
    def _():
        w = jnp.concatenate([wa_ref[:, GATE_LANE_SHIFT:], wb_ref[:, :GATE_LANE_SHIFT]], axis=1)
        o_ref[...] = w.astype(BF16)


def _prep_main(w_in):
    plan = _main_weight_plan()
    n_blocks = plan[0].shape[0]
    assert n_blocks * PREP_TN == MAIN_OUT
    return pl.pallas_call(
        _prep_main_body,
        grid_spec=pltpu.PrefetchScalarGridSpec(
            num_scalar_prefetch=4,
            grid=(n_blocks,),
            in_specs=[
                pl.BlockSpec((D_MODEL, PREP_TN), lambda j, src, nb, sh, q: (0, src[j])),
                pl.BlockSpec((D_MODEL, LANES), lambda j, src, nb, sh, q: (0, nb[j])),
            ],
            out_specs=pl.BlockSpec((D_MODEL, PREP_TN), lambda j, src, nb, sh, q: (0, j)),
        ),
        out_shape=jax.ShapeDtypeStruct((D_MODEL, MAIN_OUT), BF16),
        compiler_params=pltpu.CompilerParams(
            dimension_semantics=("arbitrary",), vmem_limit_bytes=VMEM_LIMIT),
        name="prep_main",
    )(*[jnp.asarray(a) for a in plan], w_in, w_in)


def _prep_kv_body(kv_ref, gate_ref, o_ref):
    o_ref[:, :2 * KV_WIDTH] = kv_ref[...].astype(BF16)
    lane = lax.broadcasted_iota(jnp.int32, gate_ref.shape, 1)
    o_ref[:, 2 * KV_WIDTH:] = jnp.where(lane < GATE_LANE_SHIFT, gate_ref[...], 0.0).astype(BF16)


def _prep_kv(w_in):
    assert SRC_KA % (2 * KV_WIDTH) == 0 and SRC_VA == SRC_KA + KV_WIDTH
    assert SRC_IG % LANES == 0 and SRC_FG == SRC_IG + M_HEADS
    return pl.pallas_call(
        _prep_kv_body,
        grid=(1,),
        in_specs=[pl.BlockSpec((D_MODEL, 2 * KV_WIDTH), lambda i: (0, SRC_KA // (2 * KV_WIDTH))),
                  pl.BlockSpec((D_MODEL, LANES), lambda i: (0, SRC_IG // LANES))],
        out_specs=pl.BlockSpec((D_MODEL, KV_OUT), lambda i: (0, 0)),
        out_shape=jax.ShapeDtypeStruct((D_MODEL, KV_OUT), BF16),
        compiler_params=pltpu.CompilerParams(vmem_limit_bytes=VMEM_LIMIT),
        name="prep_kv",
    )(w_in, w_in)


def _prep_t_body(w_ref, o_ref):
    scale = jnp.where(pl.program_id(0) < ATTN_WIDTH // KV_WIDTH, Q_SCALE, 1.0)
    o_ref[...] = (w_ref[...].T * scale).astype(BF16)


def _prep_t(w_in):
    assert SRC_QA == 0 and SRC_VA % KV_WIDTH == 0
    n_q = ATTN_WIDTH // KV_WIDTH
    return pl.pallas_call(
        _prep_t_body,
        grid=(n_q + 1,),
        in_specs=[pl.BlockSpec((D_MODEL, KV_WIDTH), lambda i: (0, jnp.where(i < n_q, i, SRC_VA // KV_WIDTH)))],
        out_specs=pl.BlockSpec((KV_WIDTH, D_MODEL), lambda i: (i, 0)),
        out_shape=jax.ShapeDtypeStruct((QV_ROWS, D_MODEL), BF16),
        compiler_params=pltpu.CompilerParams(
            dimension_semantics=("arbitrary",), vmem_limit_bytes=VMEM_LIMIT),
        name="prep_t",
    )(w_in)


def _proj_body(x_ref, g_ref, wkv_ref, w_ref, o32_ref, o16_ref, xn_ref):
    @pl.when(pl.program_id(1) == 0)
    def _():
        x = x_ref[...]
        ms = jnp.mean(x * x, axis=-1, keepdims=True)
        xn_ref[...] = ((x * lax.rsqrt(ms + EPS)) * g_ref[...]).astype(BF16)
        o32_ref[...] = jnp.dot(xn_ref[...], wkv_ref[...], preferred_element_type=F32)

    o16_ref[...] = jnp.dot(xn_ref[...], w_ref[...], preferred_element_type=F32).astype(BF16)


def _project(x, g, wkv, wmain, tm, col0):
    rows = x.shape[0]
    assert rows % tm == 0
    tile0 = col0 * (D_MODEL // PROJ_TN)
    n_tiles = MAIN_OUT // PROJ_TN - tile0
    return pl.pallas_call(
        _proj_body,
        grid=(rows // tm, n_tiles),
        in_specs=[
            pl.BlockSpec((tm, D_MODEL), lambda i, j: (i, 0)),
            pl.BlockSpec((1, D_MODEL), lambda i, j: (0, 0)),
            pl.BlockSpec((D_MODEL, KV_OUT), lambda i, j: (0, 0), pipeline_mode=pl.Buffered(1)),
            pl.BlockSpec((D_MODEL, PROJ_TN), lambda i, j: (0, j + tile0)),
        ],
        out_specs=[
            pl.BlockSpec((tm, KV_OUT), lambda i, j: (i, 0)),
            pl.BlockSpec((tm, PROJ_TN), lambda i, j: (i, j)),
            pl.BlockSpec((tm, D_MODEL), lambda i, j: (i, 0)),
        ],
        out_shape=[jax.ShapeDtypeStruct((rows, KV_OUT), F32),
                   jax.ShapeDtypeStruct((rows, n_tiles * PROJ_TN), BF16),
                   jax.ShapeDtypeStruct((rows, D_MODEL), BF16)],
        compiler_params=pltpu.CompilerParams(
            dimension_semantics=("parallel", "arbitrary"), vmem_limit_bytes=VMEM_LIMIT),
        name="project",
    )(x, g, wkv, wmain)


def _proj_t_body(wt_ref, xn_ref, o_ref):
    o_ref[...] = lax.dot_general(wt_ref[...], xn_ref[...], (((1,), (1,)), ((), ())),
                                 preferred_element_type=F32).astype(BF16)


def _project_t(wt, xn, tm):
    rows = xn.shape[0]
    assert rows % tm == 0
    return pl.pallas_call(
        _proj_t_body,
        grid=(rows // tm,),
        in_specs=[pl.BlockSpec((QV_ROWS, D_MODEL), lambda i: (0, 0), pipeline_mode=pl.Buffered(1)),
                  pl.BlockSpec((tm, D_MODEL), lambda i: (i, 0))],
        out_specs=pl.BlockSpec((QV_ROWS, tm), lambda i: (0, i)),
        out_shape=jax.ShapeDtypeStruct((QV_ROWS, rows), BF16),
        compiler_params=pltpu.CompilerParams(
            dimension_semantics=("parallel",), vmem_limit_bytes=VMEM_LIMIT),
        name="project_t",
    )(wt, xn)


def _attn_pair_body(qt_ref, ka_ref, kb_ref, km_ref, vta_ref, vtb_ref, vtm_ref, sink_ref, o_ref):
    p = pl.program_id(0)
    k = jnp.concatenate([ka_ref[...], kb_ref[...], km_ref[...]], axis=0).astype(BF16)
    vt = jnp.concatenate([vta_ref[...], vtb_ref[...], vtm_ref[...]], axis=1)
    ones = jnp.ones((ONES_ROWS, PAIR_KEYS), BF16)
    n_lanes = GQA_GROUP * PAIR

    key = lax.broadcasted_iota(jnp.int32, (PAIR_KEYS, n_lanes), 0)
    odd = (lax.broadcasted_iota(jnp.int32, (PAIR_KEYS, n_lanes), 1) % PAIR) >= CHUNK
    lo = jnp.where(odd, CHUNK, 0)
    lo = jnp.where(p == 0, PAIR, lo)
    hi = jnp.where(odd, 2 * PAIR, PAIR + CHUNK)
    valid = ((key >= lo) & (key < hi)) | (key >= 2 * PAIR)

    pieces = []
    for h in range(N_KV_HEADS):
        kh = k[:, h * HEAD_DIM:(h + 1) * HEAD_DIM]
        rows = [(h * GQA_GROUP + g) * HEAD_DIM for g in range(GQA_GROUP)]
        qth = jnp.concatenate([qt_ref[r:r + HEAD_DIM, :] for r in rows], axis=1)
        st = jnp.dot(kh, qth, preferred_element_type=F32)
        st = jnp.where(valid, st, NEG_INF)
        sink = sink_ref[h]
        m = jnp.maximum(jnp.max(st, axis=0, keepdims=True), sink)
        e = jnp.exp(st - m).astype(BF16)
        vth = jnp.concatenate([vt[h * HEAD_DIM:(h + 1) * HEAD_DIM, :], ones], axis=0)
        ot = jnp.dot(vth, e, preferred_element_type=F32)
        denom = ot[HEAD_DIM:HEAD_DIM + 1, :] + jnp.exp(sink - m)
        ot = ot[:HEAD_DIM, :] * (1.0 / denom)
        pieces += [ot[:, g * PAIR:(g + 1) * PAIR] for g in range(GQA_GROUP)]
    o_ref[...] = jnp.concatenate(pieces, axis=0).T.astype(o_ref.dtype)


def _attend_prompt(qvt, o32, k_meta, vt_meta, sinks_t):
    seq = qvt.shape[1]
    assert seq % PAIR == 0
    prev = lambda p: jnp.maximum(p - 1, 0)
    v_row = ATTN_WIDTH // KV_WIDTH
    return pl.pallas_call(
        _attn_pair_body,
        grid=(seq // PAIR,),
        in_specs=[
            pl.BlockSpec((ATTN_WIDTH, PAIR), lambda p: (0, p)),
            pl.BlockSpec((PAIR, KV_WIDTH), lambda p: (prev(p), 0)),
            pl.BlockSpec((PAIR, KV_WIDTH), lambda p: (p, 0)),
            pl.BlockSpec((N_META, KV_WIDTH), lambda p: (0, 0)),
            pl.BlockSpec((KV_WIDTH, PAIR), lambda p: (v_row, prev(p))),
            pl.BlockSpec((KV_WIDTH, PAIR), lambda p: (v_row, p)),
            pl.BlockSpec((KV_WIDTH, N_META), lambda p: (0, 0)),
            pl.BlockSpec((N_KV_HEADS, 1, GQA_GROUP * PAIR), lambda p: (0, 0, 0)),
        ],
        out_specs=pl.BlockSpec((PAIR, ATTN_WIDTH), lambda p: (p, 0)),
        out_shape=jax.ShapeDtypeStruct((seq, ATTN_WIDTH), BF16),
        compiler_params=pltpu.CompilerParams(
            dimension_semantics=("parallel",), vmem_limit_bytes=VMEM_LIMIT),
        name="attend_prompt",
    )(qvt, o32, o32, k_meta, qvt, qvt, vt_meta, sinks_t)


def _attend_heads(q, k, v, sink_ref, o_ref, lq):
    for h in range(N_KV_HEADS):
        kh = k[:, h * HEAD_DIM:(h + 1) * HEAD_DIM]
        vh = v[:, h * HEAD_DIM:(h + 1) * HEAD_DIM]
        heads = [h * GQA_GROUP + g for g in range(GQA_GROUP)]
        qh = jnp.concatenate([q[:, n * HEAD_DIM:(n + 1) * HEAD_DIM] for n in heads], axis=0)
        s = lax.dot_general(qh, kh, (((1,), (1,)), ((), ())), preferred_element_type=F32)
        sink = jnp.concatenate(
            [jnp.broadcast_to(sink_ref[n:n + 1, 0:1], (lq, 1)) for n in heads], axis=0)
        m = jnp.maximum(jnp.max(s, axis=-1, keepdims=True), sink)
        e = jnp.exp(s - m)
        denom = jnp.sum(e, axis=-1, keepdims=True) + jnp.exp(sink - m)
        oh = jnp.dot(e.astype(BF16), vh, preferred_element_type=F32) * (1.0 / denom)
        o_ref[:, h * GQA_GROUP * HEAD_DIM:(h + 1) * GQA_GROUP * HEAD_DIM] = jnp.concatenate(
            [oh[g * lq:(g + 1) * lq, :] for g in range(GQA_GROUP)], axis=1).astype(o_ref.dtype)


def _attn_sample_body(q_ref, ck_ref, cv_ref, kn_ref, vn_ref, km_ref, vm_ref, sink_ref, o_ref):
    k = jnp.concatenate([ck_ref[...], kn_ref[...], km_ref[...]], axis=0).astype(BF16)
    v = jnp.concatenate([cv_ref[...], vn_ref[...], vm_ref[...]], axis=0).astype(BF16)
    _attend_heads(q_ref[...], k, v, sink_ref, o_ref, q_ref.shape[0])


def _attend_sample(o16, o32, cache_k, cache_v, meta_block, sinks, n_streams, ds):
    cache_rows = cache_k.shape[1]
    new = lambda col: pl.BlockSpec((ds, KV_WIDTH), lambda s: (s, col))
    meta = lambda col: pl.BlockSpec((N_META, KV_WIDTH), lambda s: (meta_block, col))
    cache = pl.BlockSpec((None, cache_rows, KV_WIDTH), lambda s: (s, 0, 0))
    return pl.pallas_call(
        _attn_sample_body,
        grid=(n_streams,),
        in_specs=[
            pl.BlockSpec((ds, ATTN_WIDTH), lambda s: (s, COL_QA)),
            cache, cache, new(0), new(1), meta(0), meta(1),
            pl.BlockSpec((N_Q_HEADS, LANES), lambda s: (0, 0)),
        ],
        out_specs=pl.BlockSpec((ds, ATTN_WIDTH), lambda s: (s, 0)),
        out_shape=jax.ShapeDtypeStruct((n_streams * ds, ATTN_WIDTH), BF16),
        compiler_params=pltpu.CompilerParams(
            dimension_semantics=("parallel",), vmem_limit_bytes=VMEM_LIMIT),
        name="attend_sample",
    )(o16, cache_k, cache_v, o32, o32, o32, o32, sinks)


def _mlstm_body(q_ref, k_ref, v_ref, gt_ref, bias_ref, c0_ref, n0_ref, m0_ref,
                hn_ref, c_ref, n_ref, m_ref, *, blk):
    @pl.when(pl.program_id(1) == 0)
    def _():
        c_ref[...] = c0_ref[...]
        n_ref[...] = n0_ref[...]
        m_ref[...] = m0_ref[...]

    x = gt_ref[...] + bias_ref[...]
    lane = lax.broadcasted_iota(jnp.int32, x.shape, 1)
    log_f = jnp.minimum(x, 0.0) - jnp.log(1.0 + jnp.exp(-jnp.abs(x)))
    gates = jnp.where(lane < M_HEADS, x, log_f)
    row = lax.broadcasted_iota(jnp.int32, (blk, blk), 0)
    col = lax.broadcasted_iota(jnp.int32, (blk, blk), 1)
    causal = row >= col
    csum = jnp.dot(causal.astype(F32), gates, precision=lax.Precision.HIGHEST,
                   preferred_element_type=F32)
    z = jnp.where(lane < M_HEADS, gates, csum)
    pad = (-blk) % LANES
    zsq = z if pad == 0 else jnp.concatenate([z, jnp.zeros((pad, LANES), F32)], axis=0)
    zt = zsq.T[:, :blk]

    for h in range(M_HEADS):
        li_c = z[:, h:h + 1]
        b_c = z[:, M_HEADS + h:M_HEADS + h + 1]
        g_r = zt[h:h + 1, :] - zt[M_HEADS + h:M_HEADS + h + 1, :]
        m_prev = m_ref[h:h + 1, 0:1]
        d0 = jnp.where(causal, g_r, -jnp.inf)
        mm = jnp.maximum(m_prev, jnp.max(d0, axis=-1, keepdims=True))
        m_t = b_c + mm
        w = jnp.exp(d0 - mm)
        a = jnp.exp(m_prev - mm)

        qh = q_ref[:, h * M_QK_DIM:(h + 1) * M_QK_DIM]
        kh = k_ref[:, h * M_QK_DIM:(h + 1) * M_QK_DIM]
        vh = v_ref[:, h * M_V_DIM:(h + 1) * M_V_DIM]
        c_h = c_ref[h]
        n_h = n_ref[h:h + 1, :]

        qk = lax.dot_general(qh, kh, (((1,), (1,)), ((), ())), preferred_element_type=F32)
        wqk = w * (qk * MLSTM_K_SCALE)
        num = a * jnp.dot(qh, c_h.astype(BF16), preferred_element_type=F32) + jnp.dot(
            wqk.astype(BF16), vh, preferred_element_type=F32)
        den = a * jnp.sum(qh.astype(F32) * n_h, axis=-1, keepdims=True) + jnp.sum(wqk, axis=-1, keepdims=True)
        hv = num * (1.0 / jnp.maximum(jnp.abs(den), jnp.exp(-m_t)))
        hv = hv * lax.rsqrt(jnp.mean(hv * hv, axis=-1, keepdims=True) + EPS)
        hn_ref[:, h * M_V_DIM:(h + 1) * M_V_DIM] = hv.astype(hn_ref.dtype)

        m_new = m_t[blk - 1:blk, :]
        b_last = b_c[blk - 1:blk, :]
        ws = jnp.exp(b_last - b_c + li_c - m_new) * MLSTM_K_SCALE
        a_last = jnp.exp(b_last + m_prev - m_new)
        vs = (vh.astype(F32) * ws).astype(BF16)
        ktv = lax.dot_general(kh, vs, (((0,), (0,)), ((), ())), preferred_element_type=F32)
        c_ref[h] = a_last * c_h + ktv
        n_ref[h:h + 1, :] = a_last * n_h + jnp.sum(kh.astype(F32) * ws, axis=0, keepdims=True)
        m_ref[h:h + 1, :] = jnp.broadcast_to(m_new, (1, LANES))


def _mlstm(o16, o32, bias, c0, n0, m0, *, blk, n_streams, n_blocks, first_block, shared_init, col0):
    rb = lambda s, c: first_block + s * n_blocks + c
    st = (lambda s: 0) if shared_init else (lambda s: s)
    qk_blk = 2 * (COL_QKM - col0)
    return pl.pallas_call(
        functools.partial(_mlstm_body, blk=blk),
        grid=(n_streams, n_blocks),
        in_specs=[
            pl.BlockSpec((blk, M_QK_WIDTH), lambda s, c: (rb(s, c), qk_blk)),
            pl.BlockSpec((blk, M_QK_WIDTH), lambda s, c: (rb(s, c), qk_blk + 1)),
            pl.BlockSpec((blk, M_V_WIDTH), lambda s, c: (rb(s, c), COL_VM - col0)),
            pl.BlockSpec((blk, LANES), lambda s, c: (rb(s, c), GATE_BLOCK)),
            pl.BlockSpec((1, LANES), lambda s, c: (0, 0)),
            pl.BlockSpec((None, M_HEADS, M_QK_DIM, M_V_DIM), lambda s, c: (st(s), 0, 0, 0)),
            pl.BlockSpec((None, M_HEADS, M_QK_DIM), lambda s, c: (st(s), 0, 0)),
            pl.BlockSpec((None, M_HEADS, LANES), lambda s, c: (st(s), 0, 0)),
        ],
        out_specs=[
            pl.BlockSpec((blk, M_V_WIDTH), lambda s, c: (s * n_blocks + c, 0)),
            pl.BlockSpec((None, M_HEADS, M_QK_DIM, M_V_DIM), lambda s, c: (s, 0, 0, 0)),
            pl.BlockSpec((None, M_HEADS, M_QK_DIM), lambda s, c: (s, 0, 0)),
            pl.BlockSpec((None, M_HEADS, LANES), lambda s, c: (s, 0, 0)),
        ],
        out_shape=[
            jax.ShapeDtypeStruct((n_streams * n_blocks * blk, M_V_WIDTH), BF16),
            jax.ShapeDtypeStruct((n_streams, M_HEADS, M_QK_DIM, M_V_DIM), F32),
            jax.ShapeDtypeStruct((n_streams, M_HEADS, M_QK_DIM), F32),
            jax.ShapeDtypeStruct((n_streams, M_HEADS, LANES), F32),
        ],
        compiler_params=pltpu.CompilerParams(
            dimension_semantics=("parallel", "arbitrary"), vmem_limit_bytes=VMEM_LIMIT),
        name="mlstm_blk%d" % blk,
    )(o16, o16, o16, o32, bias, c0, n0, m0)


def _merge_body(x_ref, oa_ref, hn_ref, za_ref, om_ref, zm_ref, ga_ref, gm_ref, gmh_ref,
                wpa_ref, wpm_ref, wout_ref, gf_ref, y_ref):
    a_in = oa_ref[...].astype(F32) * _silu(za_ref[...].astype(F32))
    ya = jnp.dot(a_in.astype(BF16), wpa_ref[...], preferred_element_type=F32)
    m_in = (hn_ref[...].astype(F32) * gmh_ref[...]) * _sigmoid(om_ref[...].astype(F32)) * _silu(
        zm_ref[...].astype(F32))
    ym = jnp.dot(m_in.astype(BF16), wpm_ref[...], preferred_element_type=F32)
    merged = _sigmoid(ga_ref[...].astype(F32)) * ya + _sigmoid(gm_ref[...].astype(F32)) * ym
    xo = x_ref[...] + jnp.dot(merged.astype(BF16), wout_ref[...], preferred_element_type=F32)
    ms = jnp.mean(xo * xo, axis=-1, keepdims=True)
    y_ref[...] = (xo * lax.rsqrt(ms + EPS)) * gf_ref[...]


def _merge(x, o_att, hn, o16, g_mh, w_pa, w_pm, w_out, g_final, rows, tm, col0):
    assert rows % tm == 0
    tile = lambda col: pl.BlockSpec((tm, D_MODEL), lambda i: (i, col))
    vec = pl.BlockSpec((1, D_MODEL), lambda i: (0, 0))
    weight = pl.BlockSpec((D_MODEL, D_MODEL), lambda i: (0, 0), pipeline_mode=pl.Buffered(1))
    return pl.pallas_call(
        _merge_body,
        grid=(rows // tm,),
        in_specs=[tile(0), tile(0), tile(0), tile(COL_ZA - col0), tile(COL_OM - col0), tile(COL_ZM - col0),
                  tile(COL_GA - col0), tile(COL_GM - col0), vec, weight, weight, weight, vec],
        out_specs=tile(0),
        out_shape=jax.ShapeDtypeStruct((rows, D_MODEL), F32),
        compiler_params=pltpu.CompilerParams(
            dimension_semantics=("parallel",), vmem_limit_bytes=VMEM_LIMIT),
        name="merge",
    )(x, o_att, hn, o16, o16, o16, o16, o16, g_mh, w_pa, w_pm, w_out, g_final)


def kernel(x_prompt, x_sample, cache_k, cache_v, state_C, state_n, state_m, meta_tokens, g_norm, w_in,
           b_igate, b_fgate, attn_sinks, g_mhnorm, w_pa, w_pm, w_out, g_final):
    batch, seq, _ = x_prompt.shape
    db, ds, _ = x_sample.shape
    depth = w_in.shape[0]
    assert batch == 1 and depth == 1 and ds == N_META
    cache_rows = cache_k.shape[2]
    mlstm_blk = 256 if seq % 256 == 0 else CHUNK

    wkv = _prep_kv(w_in[0])
    wmain = _prep_main(w_in[0])
    wt = _prep_t(w_in[0])
    g_in = g_norm[0].reshape(1, D_MODEL)
    bias = jnp.concatenate([b_igate[0], b_fgate[0], jnp.zeros((LANES - 2 * M_HEADS,), F32)]).reshape(1, LANES)
    sinks = jnp.broadcast_to(attn_sinks[0][:, None], (N_Q_HEADS, LANES))
    sinks_t = jnp.broadcast_to(attn_sinks[0].reshape(N_KV_HEADS, 1, GQA_GROUP, 1),
                               (N_KV_HEADS, 1, GQA_GROUP, PAIR)).reshape(N_KV_HEADS, 1, GQA_GROUP * PAIR)
    g_mh = g_mhnorm[0].reshape(1, M_V_WIDTH)
    g_fin = g_final.reshape(1, D_MODEL)
    wpa, wpm, wout = w_pa[0].astype(BF16), w_pm[0].astype(BF16), w_out[0].astype(BF16)

    xp = x_prompt.reshape(seq, D_MODEL)
    xs_rows = db * ds
    x_small = jnp.concatenate([x_sample.reshape(xs_rows, D_MODEL), meta_tokens.astype(F32)], axis=0)
    meta_block = xs_rows // N_META
    p_tm = min(seq, 1024)
    p32, p16, xn_p = _project(xp, g_in, wkv, wmain, tm=p_tm, col0=COL_ZA)
    s32, s16, _ = _project(x_small, g_in, wkv, wmain, tm=x_small.shape[0], col0=COL_QA)
    qvt = _project_t(wt, xn_p, tm=p_tm)

    zeros = lambda *shape: jnp.zeros(shape, F32)
    _, c_meta, n_meta, m_meta = _mlstm(
        s16, s32, bias, zeros(1, M_HEADS, M_QK_DIM, M_V_DIM), zeros(1, M_HEADS, M_QK_DIM),
        zeros(1, M_HEADS, LANES), blk=N_META, n_streams=1, n_blocks=1, first_block=meta_block,
        shared_init=True, col0=COL_QA)

    k_meta = s32[xs_rows:, :KV_WIDTH]
    vt_meta = s32[xs_rows:, KV_WIDTH:2 * KV_WIDTH].T.astype(BF16)
    oa_p = _attend_prompt(qvt, p32, k_meta, vt_meta, sinks_t)
    hn_p, c_p, n_p, m_p = _mlstm(p16, p32, bias, c_meta, n_meta, m_meta, blk=mlstm_blk, n_streams=1,
                                 n_blocks=seq // mlstm_blk, first_block=0, shared_init=True, col0=COL_ZA)
    y_p = _merge(xp, oa_p, hn_p, p16, g_mh, wpa, wpm, wout, g_fin, rows=seq, tm=min(seq, 256), col0=COL_ZA)

    ck = cache_k[0].reshape(db, cache_rows, KV_WIDTH)
    cv = cache_v[0].reshape(db, cache_rows, KV_WIDTH)
    oa_s = _attend_sample(s16, s32, ck, cv, meta_block, sinks, db, ds)
    m0_s = jnp.broadcast_to(state_m[0][:, :, None], (db, M_HEADS, LANES))
    hn_s, c_s, n_s, m_s = _mlstm(s16, s32, bias, state_C[0], state_n[0], m0_s, blk=ds, n_streams=db,
                                 n_blocks=1, first_block=0, shared_init=False, col0=COL_QA)
    y_s = _merge(x_small, oa_s, hn_s, s16, g_mh, wpa, wpm, wout, g_fin, rows=xs_rows, tm=xs_rows, col0=COL_QA)

    kv_shape = (1, batch, cache_rows, N_KV_HEADS, HEAD_DIM)
    k_p = p32[seq - cache_rows:, :KV_WIDTH].reshape(kv_shape)
    v_p = p32[seq - cache_rows:, KV_WIDTH:2 * KV_WIDTH].reshape(kv_shape)
    new_k = s32[:xs_rows, :KV_WIDTH].reshape(db, ds, KV_WIDTH)
    new_v = s32[:xs_rows, KV_WIDTH:2 * KV_WIDTH].reshape(db, ds, KV_WIDTH)
    skv_shape = (1, db, cache_rows, N_KV_HEADS, HEAD_DIM)
    k_s = jnp.concatenate([ck, new_k], axis=1)[:, -cache_rows:].reshape(skv_shape)
    v_s = jnp.concatenate([cv, new_v], axis=1)[:, -cache_rows:].reshape(skv_shape)

    return (y_p.reshape(batch, seq, D_MODEL), y_s.reshape(db, ds, D_MODEL),
            k_p, v_p, c_p[None], n_p[None], m_p[None, :, :, 0],
            k_s, v_s, c_s[None], n_s[None], m_s[None, :, :, 0])
```

```python
import functools
import math

import jax
import jax.numpy as jnp
import numpy as np
from jax import lax
from jax.experimental import pallas as pl
from jax.experimental.pallas import tpu as pltpu

F32 = jnp.float32
BF16 = jnp.bfloat16

D_MODEL = 2048
CHUNK = 64
N_META = 16
HEAD_DIM = 64
N_Q_HEADS = 32
N_KV_HEADS = 4
GQA_GROUP = N_Q_HEADS // N_KV_HEADS
WINDOW = 128
WIN_CHUNKS = WINDOW // CHUNK
ATTN_WIDTH = N_Q_HEADS * HEAD_DIM
KV_WIDTH = N_KV_HEADS * HEAD_DIM
M_HEADS = 8
M_QK_DIM = 128
M_V_DIM = 256
M_QK_WIDTH = M_HEADS * M_QK_DIM
M_V_WIDTH = M_HEADS * M_V_DIM
EPS = 1e-6
NEG_INF = -1e30
COL_SIZES = (ATTN_WIDTH, KV_WIDTH, KV_WIDTH, ATTN_WIDTH, M_QK_WIDTH, M_QK_WIDTH, M_V_WIDTH, M_V_WIDTH,
             M_HEADS, M_HEADS, M_V_WIDTH, D_MODEL, D_MODEL)
COL_STARTS = tuple(int(v) for v in np.cumsum((0,) + COL_SIZES[:-1]))
(SRC_QA, SRC_KA, SRC_VA, SRC_ZA, SRC_QM, SRC_KM, SRC_VM, SRC_OM, SRC_IG, SRC_FG, SRC_ZM, SRC_GA,
 SRC_GM) = COL_STARTS

LANES = 128
VMEM_LIMIT = 56 * 1024 * 1024

KV_OUT = 2 * KV_WIDTH + LANES
GATE_BLOCK = 2 * KV_WIDTH // LANES
MAIN_OUT = 8 * D_MODEL
COL_QA, COL_ZA, COL_QKM, COL_VM, COL_OM, COL_ZM, COL_GA, COL_GM = range(8)
PROJ_TN = 1024
MLSTM_K_SCALE = M_QK_DIM ** -0.5
Q_SCALE = 1.0 / math.sqrt(HEAD_DIM)
PREP_TN = 512
GATE_ROW_SHIFT = 2 * M_HEADS
QV_ROWS = ATTN_WIDTH + KV_WIDTH
PAIR = 2 * CHUNK
PAIR_KEYS = 2 * PAIR + N_META
ONES_ROWS = 16


def _sigmoid(x):
    return 1.0 / (1.0 + jnp.exp(-x))


def _silu(x):
    return x * _sigmoid(x)


def _main_weight_plan():
    groups = [(SRC_QA, ATTN_WIDTH, True), (SRC_ZA, ATTN_WIDTH, False), (SRC_QM, M_QK_WIDTH, False),
              (SRC_KM, M_QK_WIDTH, False), (SRC_VM, M_V_WIDTH, False), (SRC_OM, M_V_WIDTH, False),
              (SRC_ZM, M_V_WIDTH, False), (SRC_GA, D_MODEL, False), (SRC_GM, D_MODEL, False)]
    src, nb, shifted, is_q = [], [], [], []
    for start, width, q in groups:
        shift = start % PREP_TN
        assert shift in (0, GATE_ROW_SHIFT) and width % PREP_TN == 0
        for b in range(width // PREP_TN):
            blk = (start - shift) // PREP_TN + b
            src.append(blk)
            nb.append((blk + 1) * (PREP_TN // GATE_ROW_SHIFT))
            shifted.append(int(shift != 0))
            is_q.append(int(q))
    return tuple(np.asarray(a, np.int32) for a in (src, nb, shifted, is_q))


def _prep_main_body(src_ref, nb_ref, shifted_ref, isq_ref, wa_ref, wb_ref, o_ref):
    j = pl.program_id(0)

    @pl.when(shifted_ref[j] == 0)
    def _():
        scale = jnp.where(isq_ref[j] != 0, Q_SCALE, 1.0)
        o_ref[...] = (wa_ref[...] * scale).astype(BF16)

    @pl.when(shifted_ref[j] != 0)
    def _():
        o_ref[:PREP_TN - GATE_ROW_SHIFT, :] = wa_ref[GATE_ROW_SHIFT:, :].astype(BF16)
        o_ref[PREP_TN - GATE_ROW_SHIFT:, :] = wb_ref[...].astype(BF16)


def _prep_main(w_t):
    plan = _main_weight_plan()
    n_blocks = plan[0].shape[0]
    assert n_blocks * PREP_TN == MAIN_OUT
    return pl.pallas_call(
        _prep_main_body,
        grid_spec=pltpu.PrefetchScalarGridSpec(
            num_scalar_prefetch=4,
            grid=(n_blocks,),
            in_specs=[
                pl.BlockSpec((PREP_TN, D_MODEL), lambda j, src, nb, sh, q: (src[j], 0)),
                pl.BlockSpec((GATE_ROW_SHIFT, D_MODEL), lambda j, src, nb, sh, q: (nb[j], 0)),
            ],
            out_specs=pl.BlockSpec((PREP_TN, D_MODEL), lambda j, src, nb, sh, q: (j, 0)),
        ),
        out_shape=jax.ShapeDtypeStruct((MAIN_OUT, D_MODEL), BF16),
        compiler_params=pltpu.CompilerParams(
            dimension_semantics=("arbitrary",), vmem_limit_bytes=VMEM_LIMIT),
        name="prep_main",
    )(*[jnp.asarray(a) for a in plan], w_t, w_t)


def _prep_kv_body(kv_ref, gate_ref, o_ref):
    o_ref[:2 * KV_WIDTH, :] = kv_ref[...].astype(BF16)
    o_ref[2 * KV_WIDTH:2 * KV_WIDTH + GATE_ROW_SHIFT, :] = gate_ref[...].astype(BF16)
    o_ref[2 * KV_WIDTH + GATE_ROW_SHIFT:, :] = jnp.zeros((LANES - GATE_ROW_SHIFT, D_MODEL), BF16)


def _prep_kv(w_t):
    assert SRC_KA % (2 * KV_WIDTH) == 0 and SRC_VA == SRC_KA + KV_WIDTH
    assert SRC_IG % GATE_ROW_SHIFT == 0 and SRC_FG == SRC_IG + M_HEADS
    return pl.pallas_call(
        _prep_kv_body,
        grid=(1,),
        in_specs=[pl.BlockSpec((2 * KV_WIDTH, D_MODEL), lambda i: (SRC_KA // (2 * KV_WIDTH), 0)),
                  pl.BlockSpec((GATE_ROW_SHIFT, D_MODEL), lambda i: (SRC_IG // GATE_ROW_SHIFT, 0))],
        out_specs=pl.BlockSpec((KV_OUT, D_MODEL), lambda i: (0, 0)),
        out_shape=jax.ShapeDtypeStruct((KV_OUT, D_MODEL), BF16),
        compiler_params=pltpu.CompilerParams(vmem_limit_bytes=VMEM_LIMIT),
        name="prep_kv",
    )(w_t, w_t)


_NT = (((1,), (1,)), ((), ()))


def _proj_body(x_ref, g_ref, wkv_ref, w_ref, o32_ref, o16_ref, xn_ref):
    @pl.when(pl.program_id(1) == 0)
    def _():
        x = x_ref[...]
        ms = jnp.mean(x * x, axis=-1, keepdims=True)
        xn_ref[...] = ((x * lax.rsqrt(ms + EPS)) * g_ref[...]).astype(BF16)
        o32_ref[...] = lax.dot_general(xn_ref[...], wkv_ref[...], _NT, preferred_element_type=F32)

    o16_ref[...] = lax.dot_general(xn_ref[...], w_ref[...], _NT, preferred_element_type=F32).astype(BF16)


def _project(x, g, wkv, wmain, tm, col0):
    rows = x.shape[0]
    assert rows % tm == 0
    tile0 = col0 * (D_MODEL // PROJ_TN)
    n_tiles = MAIN_OUT // PROJ_TN - tile0
    return pl.pallas_call(
        _proj_body,
        grid=(rows // tm, n_tiles),
        in_specs=[
            pl.BlockSpec((tm, D_MODEL), lambda i, j: (i, 0)),
            pl.BlockSpec((1, D_MODEL), lambda i, j: (0, 0)),
            pl.BlockSpec((KV_OUT, D_MODEL), lambda i, j: (0, 0), pipeline_mode=pl.Buffered(1)),
            pl.BlockSpec((PROJ_TN, D_MODEL), lambda i, j: (j + tile0, 0)),
        ],
        out_specs=[
            pl.BlockSpec((tm, KV_OUT), lambda i, j: (i, 0)),
            pl.BlockSpec((tm, PROJ_TN), lambda i, j: (i, j)),
            pl.BlockSpec((tm, D_MODEL), lambda i, j: (i, 0)),
        ],
        out_shape=[jax.ShapeDtypeStruct((rows, KV_OUT), F32),
                   jax.ShapeDtypeStruct((rows, n_tiles * PROJ_TN), BF16),
                   jax.ShapeDtypeStruct((rows, D_MODEL), BF16)],
        compiler_params=pltpu.CompilerParams(
            dimension_semantics=("parallel", "arbitrary"), vmem_limit_bytes=VMEM_LIMIT),
        name="project",
    )(x, g, wkv, wmain)


def _proj_t_body(wq_ref, wv_ref, xn_ref, o_ref):
    xn = xn_ref[...]
    o_ref[:ATTN_WIDTH, :] = lax.dot_general(wq_ref[...], xn, _NT, preferred_element_type=F32).astype(BF16)
    o_ref[ATTN_WIDTH:, :] = lax.dot_general(wv_ref[...], xn, _NT, preferred_element_type=F32).astype(BF16)


def _project_t(wmain, wkv, xn, tm):
    rows = xn.shape[0]
    assert rows % tm == 0
    return pl.pallas_call(
        _proj_t_body,
        grid=(rows // tm,),
        in_specs=[pl.BlockSpec((ATTN_WIDTH, D_MODEL), lambda i: (COL_QA, 0), pipeline_mode=pl.Buffered(1)),
                  pl.BlockSpec((KV_WIDTH, D_MODEL), lambda i: (1, 0), pipeline_mode=pl.Buffered(1)),
                  pl.BlockSpec((tm, D_MODEL), lambda i: (i, 0))],
        out_specs=pl.BlockSpec((QV_ROWS, tm), lambda i: (0, i)),
        out_shape=jax.ShapeDtypeStruct((QV_ROWS, rows), BF16),
        compiler_params=pltpu.CompilerParams(
            dimension_semantics=("parallel",), vmem_limit_bytes=VMEM_LIMIT),
        name="project_t",
    )(wmain, wkv, xn)


def _attn_pair_body(qt_ref, ka_ref, kb_ref, km_ref, vta_ref, vtb_ref, vtm_ref, sink_ref, o_ref):
    p = pl.program_id(0)
    k = jnp.concatenate([ka_ref[...], kb_ref[...], km_ref[...]], axis=0).astype(BF16)
    vt = jnp.concatenate([vta_ref[...], vtb_ref[...], vtm_ref[...]], axis=1)
    ones = jnp.ones((ONES_ROWS, PAIR_KEYS), BF16)
    n_lanes = GQA_GROUP * PAIR

    key = lax.broadcasted_iota(jnp.int32, (PAIR_KEYS, n_lanes), 0)
    odd = (lax.broadcasted_iota(jnp.int32, (PAIR_KEYS, n_lanes), 1) % PAIR) >= CHUNK
    lo = jnp.where(odd, CHUNK, 0)
    lo = jnp.where(p == 0, PAIR, lo)
    hi = jnp.where(odd, 2 * PAIR, PAIR + CHUNK)
    valid = ((key >= lo) & (key < hi)) | (key >= 2 * PAIR)

    pieces = []
    for h in range(N_KV_HEADS):
        kh = k[:, h * HEAD_DIM:(h + 1) * HEAD_DIM]
        rows = [(h * GQA_GROUP + g) * HEAD_DIM for g in range(GQA_GROUP)]
        qth = jnp.concatenate([qt_ref[r:r + HEAD_DIM, :] for r in rows], axis=1)
        st = jnp.dot(kh, qth, preferred_element_type=F32)
        st = jnp.where(valid, st, NEG_INF)
        sink = sink_ref[h]
        m = jnp.maximum(jnp.max(st, axis=0, keepdims=True), sink)
        e = jnp.exp(st - m).astype(BF16)
        vth = jnp.concatenate([vt[h * HEAD_DIM:(h + 1) * HEAD_DIM, :], ones], axis=0)
        ot = jnp.dot(vth, e, preferred_element_type=F32)
        denom = ot[HEAD_DIM:HEAD_DIM + 1, :] + jnp.exp(sink - m)
        ot = ot[:HEAD_DIM, :] * (1.0 / denom)
        pieces += [ot[:, g * PAIR:(g + 1) * PAIR] for g in range(GQA_GROUP)]
    o_ref[...] = jnp.concatenate(pieces, axis=0).T.astype(o_ref.dtype)


def _attend_prompt(qvt, o32, k_meta, vt_meta, sinks_t):
    seq = qvt.shape[1]
    assert seq % PAIR == 0
    prev = lambda p: jnp.maximum(p - 1, 0)
    v_row = ATTN_WIDTH // KV_WIDTH
    return pl.pallas_call(
        _attn_pair_body,
        grid=(seq // PAIR,),
        in_specs=[
            pl.BlockSpec((ATTN_WIDTH, PAIR), lambda p: (0, p)),
            pl.BlockSpec((PAIR, KV_WIDTH), lambda p: (prev(p), 0)),
            pl.BlockSpec((PAIR, KV_WIDTH), lambda p: (p, 0)),
            pl.BlockSpec((N_META, KV_WIDTH), lambda p: (0, 0)),
            pl.BlockSpec((KV_WIDTH, PAIR), lambda p: (v_row, prev(p))),
            pl.BlockSpec((KV_WIDTH, PAIR), lambda p: (v_row, p)),
            pl.BlockSpec((KV_WIDTH, N_META), lambda p: (0, 0)),
            pl.BlockSpec((N_KV_HEADS, 1, GQA_GROUP * PAIR), lambda p: (0, 0, 0)),
        ],
        out_specs=pl.BlockSpec((PAIR, ATTN_WIDTH), lambda p: (p, 0)),
        out_shape=jax.ShapeDtypeStruct((seq, ATTN_WIDTH), BF16),
        compiler_params=pltpu.CompilerParams(
            dimension_semantics=("parallel",), vmem_limit_bytes=VMEM_LIMIT),
        name="attend_prompt",
    )(qvt, o32, o32, k_meta, qvt, qvt, vt_meta, sinks_t)


def _attend_heads(q, k, v, sink_ref, o_ref, lq):
    for h in range(N_KV_HEADS):
        kh = k[:, h * HEAD_DIM:(h + 1) * HEAD_DIM]
        vh = v[:, h * HEAD_DIM:(h + 1) * HEAD_DIM]
        heads = [h * GQA_GROUP + g for g in range(GQA_GROUP)]
        qh = jnp.concatenate([q[:, n * HEAD_DIM:(n + 1) * HEAD_DIM] for n in heads], axis=0)
        s = lax.dot_general(qh, kh, (((1,), (1,)), ((), ())), preferred_element_type=F32)
        sink = jnp.concatenate(
            [jnp.broadcast_to(sink_ref[n:n + 1, 0:1], (lq, 1)) for n in heads], axis=0)
        m = jnp.maximum(jnp.max(s, axis=-1, keepdims=True), sink)
        e = jnp.exp(s - m)
        denom = jnp.sum(e, axis=-1, keepdims=True) + jnp.exp(sink - m)
        oh = jnp.dot(e.astype(BF16), vh, preferred_element_type=F32) * (1.0 / denom)
        o_ref[:, h * GQA_GROUP * HEAD_DIM:(h + 1) * GQA_GROUP * HEAD_DIM] = jnp.concatenate(
            [oh[g * lq:(g + 1) * lq, :] for g in range(GQA_GROUP)], axis=1).astype(o_ref.dtype)


def _attn_sample_body(q_ref, ck_ref, cv_ref, kn_ref, vn_ref, km_ref, vm_ref, sink_ref, o_ref):
    k = jnp.concatenate([ck_ref[...], kn_ref[...], km_ref[...]], axis=0).astype(BF16)
    v = jnp.concatenate([cv_ref[...], vn_ref[...], vm_ref[...]], axis=0).astype(BF16)
    _attend_heads(q_ref[...], k, v, sink_ref, o_ref, q_ref.shape[0])


def _attend_sample(o16, o32, cache_k, cache_v, meta_block, sinks, n_streams, ds):
    cache_rows = cache_k.shape[1]
    new = lambda col: pl.BlockSpec((ds, KV_WIDTH), lambda s: (s, col))
    meta = lambda col: pl.BlockSpec((N_META, KV_WIDTH), lambda s: (meta_block, col))
    cache = pl.BlockSpec((None, cache_rows, KV_WIDTH), lambda s: (s, 0, 0))
    return pl.pallas_call(
        _attn_sample_body,
        grid=(n_streams,),
        in_specs=[
            pl.BlockSpec((ds, ATTN_WIDTH), lambda s: (s, COL_QA)),
            cache, cache, new(0), new(1), meta(0), meta(1),
            pl.BlockSpec((N_Q_HEADS, LANES), lambda s: (0, 0)),
        ],
        out_specs=pl.BlockSpec((ds, ATTN_WIDTH), lambda s: (s, 0)),
        out_shape=jax.ShapeDtypeStruct((n_streams * ds, ATTN_WIDTH), BF16),
        compiler_params=pltpu.CompilerParams(
            dimension_semantics=("parallel",), vmem_limit_bytes=VMEM_LIMIT),
        name="attend_sample",
    )(o16, cache_k, cache_v, o32, o32, o32, o32, sinks)


def _mlstm_body(q_ref, k_ref, v_ref, gt_ref, bias_ref, c0_ref, n0_ref, m0_ref,
                hn_ref, c_ref, n_ref, m_ref, *, blk):
    @pl.when(pl.program_id(1) == 0)
    def _():
        c_ref[...] = c0_ref[...]
        n_ref[...] = n0_ref[...]
        m_ref[...] = m0_ref[...]

    x = gt_ref[...] + bias_ref[...]
    lane = lax.broadcasted_iota(jnp.int32, x.shape, 1)
    log_f = jnp.minimum(x, 0.0) - jnp.log(1.0 + jnp.exp(-jnp.abs(x)))
    gates = jnp.where(lane < M_HEADS, x, log_f)
    row = lax.broadcasted_iota(jnp.int32, (blk, blk), 0)
    col = lax.broadcasted_iota(jnp.int32, (blk, blk), 1)
    causal = row >= col
    csum = jnp.dot(causal.astype(F32), gates, precision=lax.Precision.HIGHEST,
                   preferred_element_type=F32)
    z = jnp.where(lane < M_HEADS, gates, csum)
    pad = (-blk) % LANES
    zsq = z if pad == 0 else jnp.concatenate([z, jnp.zeros((pad, LANES), F32)], axis=0)
    zt = zsq.T[:, :blk]

    for h in range(M_HEADS):
        li_c = z[:, h:h + 1]
        b_c = z[:, M_HEADS + h:M_HEADS + h + 1]
        g_r = zt[h:h + 1, :] - zt[M_HEADS + h:M_HEADS + h + 1, :]
        m_prev = m_ref[h:h + 1, 0:1]
        d0 = jnp.where(causal, g_r, -jnp.inf)
        mm = jnp.maximum(m_prev, jnp.max(d0, axis=-1, keepdims=True))
        m_t = b_c + mm
        w = jnp.exp(d0 - mm)
        a = jnp.exp(m_prev - mm)

        qh = q_ref[:, h * M_QK_DIM:(h + 1) * M_QK_DIM]
        kh = k_ref[:, h * M_QK_DIM:(h + 1) * M_QK_DIM]
        vh = v_ref[:, h * M_V_DIM:(h + 1) * M_V_DIM]
        c_h = c_ref[h]
        n_h = n_ref[h:h + 1, :]

        qk = lax.dot_general(qh, kh, (((1,), (1,)), ((), ())), preferred_element_type=F32)
        wqk = w * (qk * MLSTM_K_SCALE)
        num = a * jnp.dot(qh, c_h.astype(BF16), preferred_element_type=F32) + jnp.dot(
            wqk.astype(BF16), vh, preferred_element_type=F32)
        den = a * jnp.sum(qh.astype(F32) * n_h, axis=-1, keepdims=True) + jnp.sum(wqk, axis=-1, keepdims=True)
        hv = num * (1.0 / jnp.maximum(jnp.abs(den), jnp.exp(-m_t)))
        hv = hv * lax.rsqrt(jnp.mean(hv * hv, axis=-1, keepdims=True) + EPS)
        hn_ref[:, h * M_V_DIM:(h + 1) * M_V_DIM] = hv.astype(hn_ref.dtype)

        m_new = m_t[blk - 1:blk, :]
        b_last = b_c[blk - 1:blk, :]
        ws = jnp.exp(b_last - b_c + li_c - m_new) * MLSTM_K_SCALE
        a_last = jnp.exp(b_last + m_prev - m_new)
        vs = (vh.astype(F32) * ws).astype(BF16)
        ktv = lax.dot_general(kh, vs, (((0,), (0,)), ((), ())), preferred_element_type=F32)
        c_ref[h] = a_last * c_h + ktv
        n_ref[h:h + 1, :] = a_last * n_h + jnp.sum(kh.astype(F32) * ws, axis=0, keepdims=True)
        m_ref[h:h + 1, :] = jnp.broadcast_to(m_new, (1, LANES))


def _mlstm(o16, o32, bias, c0, n0, m0, *, blk, n_streams, n_blocks, first_block, shared_init, col0):
    rb = lambda s, c: first_block + s * n_blocks + c
    st = (lambda s: 0) if shared_init else (lambda s: s)
    qk_blk = 2 * (COL_QKM - col0)
    return pl.pallas_call(
        functools.partial(_mlstm_body, blk=blk),
        grid=(n_streams, n_blocks),
        in_specs=[
            pl.BlockSpec((blk, M_QK_WIDTH), lambda s, c: (rb(s, c), qk_blk)),
            pl.BlockSpec((blk, M_QK_WIDTH), lambda s, c: (rb(s, c), qk_blk + 1)),
            pl.BlockSpec((blk, M_V_WIDTH), lambda s, c: (rb(s, c), COL_VM - col0)),
            pl.BlockSpec((blk, LANES), lambda s, c: (rb(s, c), GATE_BLOCK)),
            pl.BlockSpec((1, LANES), lambda s, c: (0, 0)),
            pl.BlockSpec((None, M_HEADS, M_QK_DIM, M_V_DIM), lambda s, c: (st(s), 0, 0, 0)),
            pl.BlockSpec((None, M_HEADS, M_QK_DIM), lambda s, c: (st(s), 0, 0)),
            pl.BlockSpec((None, M_HEADS, LANES), lambda s, c: (st(s), 0, 0)),
        ],
        out_specs=[
            pl.BlockSpec((blk, M_V_WIDTH), lambda s, c: (s * n_blocks + c, 0)),
            pl.BlockSpec((None, M_HEADS, M_QK_DIM, M_V_DIM), lambda s, c: (s, 0, 0, 0)),
            pl.BlockSpec((None, M_HEADS, M_QK_DIM), lambda s, c: (s, 0, 0)),
            pl.BlockSpec((None, M_HEADS, LANES), lambda s, c: (s, 0, 0)),
        ],
        out_shape=[
            jax.ShapeDtypeStruct((n_streams * n_blocks * blk, M_V_WIDTH), BF16),
            jax.ShapeDtypeStruct((n_streams, M_HEADS, M_QK_DIM, M_V_DIM), F32),
            jax.ShapeDtypeStruct((n_streams, M_HEADS, M_QK_DIM), F32),
            jax.ShapeDtypeStruct((n_streams, M_HEADS, LANES), F32),
        ],
        compiler_params=pltpu.CompilerParams(
            dimension_semantics=("parallel", "arbitrary"), vmem_limit_bytes=VMEM_LIMIT),
        name="mlstm_blk%d" % blk,
    )(o16, o16, o16, o32, bias, c0, n0, m0)


def _merge_body(x_ref, oa_ref, hn_ref, za_ref, om_ref, zm_ref, ga_ref, gm_ref, gmh_ref,
                wpa_ref, wpm_ref, wout_ref, gf_ref, y_ref):
    a_in = oa_ref[...].astype(F32) * _silu(za_ref[...].astype(F32))
    ya = jnp.dot(a_in.astype(BF16), wpa_ref[...], preferred_element_type=F32)
    m_in = (hn_ref[...].astype(F32) * gmh_ref[...]) * _sigmoid(om_ref[...].astype(F32)) * _silu(
        zm_ref[...].astype(F32))
    ym = jnp.dot(m_in.astype(BF16), wpm_ref[...], preferred_element_type=F32)
    merged = _sigmoid(ga_ref[...].astype(F32)) * ya + _sigmoid(gm_ref[...].astype(F32)) * ym
    xo = x_ref[...] + jnp.dot(merged.astype(BF16), wout_ref[...], preferred_element_type=F32)
    ms = jnp.mean(xo * xo, axis=-1, keepdims=True)
    y_ref[...] = (xo * lax.rsqrt(ms + EPS)) * gf_ref[...]


def _merge(x, o_att, hn, o16, g_mh, w_pa, w_pm, w_out, g_final, rows, tm, col0):
    assert rows % tm == 0
    tile = lambda col: pl.BlockSpec((tm, D_MODEL), lambda i: (i, col))
    vec = pl.BlockSpec((1, D_MODEL), lambda i: (0, 0))
    weight = pl.BlockSpec((D_MODEL, D_MODEL), lambda i: (0, 0), pipeline_mode=pl.Buffered(1))
    return pl.pallas_call(
        _merge_body,
        grid=(rows // tm,),
        in_specs=[tile(0), tile(0), tile(0), tile(COL_ZA - col0), tile(COL_OM - col0), tile(COL_ZM - col0),
                  tile(COL_GA - col0), tile(COL_GM - col0), vec, weight, weight, weight, vec],
        out_specs=tile(0),
        out_shape=jax.ShapeDtypeStruct((rows, D_MODEL), F32),
        compiler_params=pltpu.CompilerParams(
            dimension_semantics=("parallel",), vmem_limit_bytes=VMEM_LIMIT),
        name="merge",
    )(x, o_att, hn, o16, o16, o16, o16, o16, g_mh, w_pa, w_pm, w_out, g_final)


def kernel(x_prompt, x_sample, cache_k, cache_v, state_C, state_n, state_m, meta_tokens, g_norm, w_in,
           b_igate, b_fgate, attn_sinks, g_mhnorm, w_pa, w_pm, w_out, g_final):
    batch, seq, _ = x_prompt.shape
    db, ds, _ = x_sample.shape
    depth = w_in.shape[0]
    assert batch == 1 and depth == 1 and ds == N_META
    cache_rows = cache_k.shape[2]
    mlstm_blk = 256 if seq % 256 == 0 else CHUNK

    w_t = w_in[0].T
    wkv = _prep_kv(w_t)
    wmain = _prep_main(w_t)
    g_in = g_norm[0].reshape(1, D_MODEL)
    bias = jnp.concatenate([b_igate[0], b_fgate[0], jnp.zeros((LANES - 2 * M_HEADS,), F32)]).reshape(1, LANES)
    sinks = jnp.broadcast_to(attn_sinks[0][:, None], (N_Q_HEADS, LANES))
    sinks_t = jnp.broadcast_to(attn_sinks[0].reshape(N_KV_HEADS, 1, GQA_GROUP, 1),
                               (N_KV_HEADS, 1, GQA_GROUP, PAIR)).reshape(N_KV_HEADS, 1, GQA_GROUP * PAIR)
    g_mh = g_mhnorm[0].reshape(1, M_V_WIDTH)
    g_fin = g_final.reshape(1, D_MODEL)
    wpa, wpm, wout = w_pa[0].astype(BF16), w_pm[0].astype(BF16), w_out[0].astype(BF16)

    xp = x_prompt.reshape(seq, D_MODEL)
    xs_rows = db * ds
    x_small = jnp.concatenate([x_sample.reshape(xs_rows, D_MODEL), meta_tokens.astype(F32)], axis=0)
    meta_block = xs_rows // N_META
    p_tm = min(seq, 1024)
    p32, p16, xn_p = _project(xp, g_in, wkv, wmain, tm=p_tm, col0=COL_ZA)
    s32, s16, _ = _project(x_small, g_in, wkv, wmain, tm=x_small.shape[0], col0=COL_QA)
    qvt = _project_t(wmain, wkv, xn_p, tm=p_tm)

    zeros = lambda *shape: jnp.zeros(shape, F32)
    _, c_meta, n_meta, m_meta = _mlstm(
        s16, s32, bias, zeros(1, M_HEADS, M_QK_DIM, M_V_DIM), zeros(1, M_HEADS, M_QK_DIM),
        zeros(1, M_HEADS, LANES), blk=N_META, n_streams=1, n_blocks=1, first_block=meta_block,
        shared_init=True, col0=COL_QA)

    k_meta = s32[xs_rows:, :KV_WIDTH]
    vt_meta = s32[xs_rows:, KV_WIDTH:2 * KV_WIDTH].T.astype(BF16)
    oa_p = _attend_prompt(qvt, p32, k_meta, vt_meta, sinks_t)
    hn_p, c_p, n_p, m_p = _mlstm(p16, p32, bias, c_meta, n_meta, m_meta, blk=mlstm_blk, n_streams=1,
                                 n_blocks=seq // mlstm_blk, first_block=0, shared_init=True, col0=COL_ZA)
    y_p = _merge(xp, oa_p, hn_p, p16, g_mh, wpa, wpm, wout, g_fin, rows=seq, tm=min(seq, 256), col0=COL_ZA)

    ck = cache_k[0].reshape(db, cache_rows, KV_WIDTH)
    cv = cache_v[0].reshape(db, cache_rows, KV_WIDTH)
    oa_s = _attend_sample(s16, s32, ck, cv, meta_block, sinks, db, ds)
    m0_s = jnp.broadcast_to(state_m[0][:, :, None], (db, M_HEADS, LANES))
    hn_s, c_s, n_s, m_s = _mlstm(s16, s32, bias, state_C[0], state_n[0], m0_s, blk=ds, n_streams=db,
                                 n_blocks=1, first_block=0, shared_init=False, col0=COL_QA)
    y_s = _merge(x_small, oa_s, hn_s, s16, g_mh, wpa, wpm, wout, g_fin, rows=xs_rows, tm=xs_rows, col0=COL_QA)

    kv_shape = (1, batch, cache_rows, N_KV_HEADS, HEAD_DIM)
    k_p = p32[seq - cache_rows:, :KV_WIDTH].reshape(kv_shape)
    v_p = p32[seq - cache_rows:, KV_WIDTH:2 * KV_WIDTH].reshape(kv_shape)
    new_k = s32[:xs_rows, :KV_WIDTH].reshape(db, ds, KV_WIDTH)
    new_v = s32[:xs_rows, KV_WIDTH:2 * KV_WIDTH].reshape(db, ds, KV_WIDTH)
    skv_shape = (1, db, cache_rows, N_KV_HEADS, HEAD_DIM)
    k_s = jnp.concatenate([ck, new_k], axis=1)[:, -cache_rows:].reshape(skv_shape)
    v_s = jnp.concatenate([cv, new_v], axis=1)[:, -cache_rows:].reshape(skv_shape)

    return (y_p.reshape(batch, seq, D_MODEL), y_s.reshape(db, ds, D_MODEL),
            k_p, v_p, c_p[None], n_p[None], m_p[None, :, :, 0],
            k_s, v_s, c_s[None], n_s[None], m_s[None, :, :, 0])
```

```python
import functools
import math

import jax
import jax.numpy as jnp
import numpy as np
from jax import lax
from jax.experimental import pallas as pl
from jax.experimental.pallas import tpu as pltpu

F32 = jnp.float32
BF16 = jnp.bfloat16

D_MODEL = 2048
CHUNK = 64
N_META = 16
HEAD_DIM = 64
N_Q_HEADS = 32
N_KV_HEADS = 4
GQA_GROUP = N_Q_HEADS // N_KV_HEADS
WINDOW = 128
WIN_CHUNKS = WINDOW // CHUNK
ATTN_WIDTH = N_Q_HEADS * HEAD_DIM
KV_WIDTH = N_KV_HEADS * HEAD_DIM
M_HEADS = 8
M_QK_DIM = 128
M_V_DIM = 256
M_QK_WIDTH = M_HEADS * M_QK_DIM
M_V_WIDTH = M_HEADS * M_V_DIM
EPS = 1e-6
NEG_INF = -1e30
COL_SIZES = (ATTN_WIDTH, KV_WIDTH, KV_WIDTH, ATTN_WIDTH, M_QK_WIDTH, M_QK_WIDTH, M_V_WIDTH, M_V_WIDTH,
             M_HEADS, M_HEADS, M_V_WIDTH, D_MODEL, D_MODEL)
COL_STARTS = tuple(int(v) for v in np.cumsum((0,) + COL_SIZES[:-1]))
(SRC_QA, SRC_KA, SRC_VA, SRC_ZA, SRC_QM, SRC_KM, SRC_VM, SRC_OM, SRC_IG, SRC_FG, SRC_ZM, SRC_GA,
 SRC_GM) = COL_STARTS

LANES = 128
VMEM_LIMIT = 56 * 1024 * 1024

KV_OUT = 2 * KV_WIDTH + LANES
GATE_BLOCK = 2 * KV_WIDTH // LANES
MAIN_OUT = 8 * D_MODEL
COL_QA, COL_ZA, COL_QKM, COL_VM, COL_OM, COL_ZM, COL_GA, COL_GM = range(8)
PROJ_TN = 1024
MLSTM_K_SCALE = M_QK_DIM ** -0.5
Q_SCALE = 1.0 / math.sqrt(HEAD_DIM)
PREP_TN = 512
GATE_ROW_SHIFT = 2 * M_HEADS
QV_ROWS = ATTN_WIDTH + KV_WIDTH
PAIR = 2 * CHUNK
PAIR_KEYS = 2 * PAIR + N_META
ONES_ROWS = 16


def _sigmoid(x):
    return 1.0 / (1.0 + jnp.exp(-x))


def _silu(x):
    return x * _sigmoid(x)


def _main_weight_plan():
    groups = [(SRC_QA, ATTN_WIDTH, True), (SRC_ZA, ATTN_WIDTH, False), (SRC_QM, M_QK_WIDTH, False),
              (SRC_KM, M_QK_WIDTH, False), (SRC_VM, M_V_WIDTH, False), (SRC_OM, M_V_WIDTH, False),
              (SRC_ZM, M_V_WIDTH, False), (SRC_GA, D_MODEL, False), (SRC_GM, D_MODEL, False)]
    src, nb, shifted, is_q = [], [], [], []
    for start, width, q in groups:
        shift = start % PREP_TN
        assert shift in (0, GATE_ROW_SHIFT) and width % PREP_TN == 0
        for b in range(width // PREP_TN):
            blk = (start - shift) // PREP_TN + b
            src.append(blk)
            nb.append((blk + 1) * (PREP_TN // GATE_ROW_SHIFT))
            shifted.append(int(shift != 0))
            is_q.append(int(q))
    return tuple(np.asarray(a, np.int32) for a in (src, nb, shifted, is_q))


def _prep_main_body(src_ref, nb_ref, shifted_ref, isq_ref, wa_ref, wb_ref, o_ref):
    j = pl.program_id(0)

    @pl.when(shifted_ref[j] == 0)
    def _():
        scale = jnp.where(isq_ref[j] != 0, Q_SCALE, 1.0)
        o_ref[...] = (wa_ref[...] * scale).astype(BF16)

    @pl.when(shifted_ref[j] != 0)
    def _():
        o_ref[:PREP_TN - GATE_ROW_SHIFT, :] = wa_ref[GATE_ROW_SHIFT:, :].astype(BF16)
        o_ref[PREP_TN - GATE_ROW_SHIFT:, :] = wb_ref[...].astype(BF16)


def _prep_main(w_t):
    plan = _main_weight_plan()
    n_blocks = plan[0].shape[0]
    assert n_blocks * PREP_TN == MAIN_OUT
    return pl.pallas_call(
        _prep_main_body,
        grid_spec=pltpu.PrefetchScalarGridSpec(
            num_scalar_prefetch=4,
            grid=(n_blocks,),
            in_specs=[
                pl.BlockSpec((PREP_TN, D_MODEL), lambda j, src, nb, sh, q: (src[j], 0)),
                pl.BlockSpec((GATE_ROW_SHIFT, D_MODEL), lambda j, src, nb, sh, q: (nb[j], 0)),
            ],
            out_specs=pl.BlockSpec((PREP_TN, D_MODEL), lambda j, src, nb, sh, q: (j, 0)),
        ),
        out_shape=jax.ShapeDtypeStruct((MAIN_OUT, D_MODEL), BF16),
        compiler_params=pltpu.CompilerParams(
            dimension_semantics=("arbitrary",), vmem_limit_bytes=VMEM_LIMIT),
        name="prep_main",
    )(*[jnp.asarray(a) for a in plan], w_t, w_t)


def _prep_kv_body(kv_ref, gate_ref, o_ref):
    o_ref[:2 * KV_WIDTH, :] = kv_ref[...].astype(BF16)
    o_ref[2 * KV_WIDTH:2 * KV_WIDTH + GATE_ROW_SHIFT, :] = gate_ref[...].astype(BF16)
    o_ref[2 * KV_WIDTH + GATE_ROW_SHIFT:, :] = jnp.zeros((LANES - GATE_ROW_SHIFT, D_MODEL), BF16)


def _prep_kv(w_t):
    assert SRC_KA % (2 * KV_WIDTH) == 0 and SRC_VA == SRC_KA + KV_WIDTH
    assert SRC_IG % GATE_ROW_SHIFT == 0 and SRC_FG == SRC_IG + M_HEADS
    return pl.pallas_call(
        _prep_kv_body,
        grid=(1,),
        in_specs=[pl.BlockSpec((2 * KV_WIDTH, D_MODEL), lambda i: (SRC_KA // (2 * KV_WIDTH), 0)),
                  pl.BlockSpec((GATE_ROW_SHIFT, D_MODEL), lambda i: (SRC_IG // GATE_ROW_SHIFT, 0))],
        out_specs=pl.BlockSpec((KV_OUT, D_MODEL), lambda i: (0, 0)),
        out_shape=jax.ShapeDtypeStruct((KV_OUT, D_MODEL), BF16),
        compiler_params=pltpu.CompilerParams(vmem_limit_bytes=VMEM_LIMIT),
        name="prep_kv",
    )(w_t, w_t)


_NT = (((1,), (1,)), ((), ()))


def _proj_body(x_ref, g_ref, wkv_ref, w_ref, o32_ref, o16_ref, xn_ref):
    @pl.when(pl.program_id(1) == 0)
    def _():
        x = x_ref[...]
        ms = jnp.mean(x * x, axis=-1, keepdims=True)
        xn_ref[...] = ((x * lax.rsqrt(ms + EPS)) * g_ref[...]).astype(BF16)
        o32_ref[...] = lax.dot_general(xn_ref[...], wkv_ref[...], _NT, preferred_element_type=F32)

    o16_ref[...] = lax.dot_general(xn_ref[...], w_ref[...], _NT, preferred_element_type=F32).astype(BF16)


def _project(x, g, wkv, wmain, tm, col0):
    rows = x.shape[0]
    assert rows % tm == 0
    tile0 = col0 * (D_MODEL // PROJ_TN)
    n_tiles = MAIN_OUT // PROJ_TN - tile0
    return pl.pallas_call(
        _proj_body,
        grid=(rows // tm, n_tiles),
        in_specs=[
            pl.BlockSpec((tm, D_MODEL), lambda i, j: (i, 0)),
            pl.BlockSpec((1, D_MODEL), lambda i, j: (0, 0)),
            pl.BlockSpec((KV_OUT, D_MODEL), lambda i, j: (0, 0), pipeline_mode=pl.Buffered(1)),
            pl.BlockSpec((PROJ_TN, D_MODEL), lambda i, j: (j + tile0, 0)),
        ],
        out_specs=[
            pl.BlockSpec((tm, KV_OUT), lambda i, j: (i, 0)),
            pl.BlockSpec((tm, PROJ_TN), lambda i, j: (i, j)),
            pl.BlockSpec((tm, D_MODEL), lambda i, j: (i, 0)),
        ],
        out_shape=[jax.ShapeDtypeStruct((rows, KV_OUT), F32),
                   jax.ShapeDtypeStruct((rows, n_tiles * PROJ_TN), BF16),
                   jax.ShapeDtypeStruct((rows, D_MODEL), BF16)],
        compiler_params=pltpu.CompilerParams(
            dimension_semantics=("parallel", "arbitrary"), vmem_limit_bytes=VMEM_LIMIT),
        name="project",
    )(x, g, wkv, wmain)


def _proj_t_body(wq_ref, wv_ref, xn_ref, o_ref):
    xn = xn_ref[...]
    o_ref[:ATTN_WIDTH, :] = lax.dot_general(wq_ref[...], xn, _NT, preferred_element_type=F32).astype(BF16)
    o_ref[ATTN_WIDTH:, :] = lax.dot_general(wv_ref[...], xn, _NT, preferred_element_type=F32).astype(BF16)


def _project_t(wmain, wkv, xn, tm):
    rows = xn.shape[0]
    assert rows % tm == 0
    return pl.pallas_call(
        _proj_t_body,
        grid=(rows // tm,),
        in_specs=[pl.BlockSpec((ATTN_WIDTH, D_MODEL), lambda i: (COL_QA, 0), pipeline_mode=pl.Buffered(1)),
                  pl.BlockSpec((KV_WIDTH, D_MODEL), lambda i: (1, 0), pipeline_mode=pl.Buffered(1)),
                  pl.BlockSpec((tm, D_MODEL), lambda i: (i, 0))],
        out_specs=pl.BlockSpec((QV_ROWS, tm), lambda i: (0, i)),
        out_shape=jax.ShapeDtypeStruct((QV_ROWS, rows), BF16),
        compiler_params=pltpu.CompilerParams(
            dimension_semantics=("parallel",), vmem_limit_bytes=VMEM_LIMIT),
        name="project_t",
    )(wmain, wkv, xn)


def _attn_pair_setup(p, ka_ref, kb_ref, km_ref, vta_ref, vtb_ref, vtm_ref):
    k = jnp.concatenate([ka_ref[...], kb_ref[...], km_ref[...]], axis=0).astype(BF16)
    vt = jnp.concatenate([vta_ref[...], vtb_ref[...], vtm_ref[...]], axis=1)
    n_lanes = GQA_GROUP * PAIR
    key = lax.broadcasted_iota(jnp.int32, (PAIR_KEYS, n_lanes), 0)
    odd = (lax.broadcasted_iota(jnp.int32, (PAIR_KEYS, n_lanes), 1) % PAIR) >= CHUNK
    lo = jnp.where(odd, CHUNK, 0)
    lo = jnp.where(p == 0, PAIR, lo)
    hi = jnp.where(odd, 2 * PAIR, PAIR + CHUNK)
    valid = ((key >= lo) & (key < hi)) | (key >= 2 * PAIR)
    return k, vt, valid


def _attn_pair_head(setup, h, qt_ref, sink_ref):
    k, vt, valid = setup
    kh = k[:, h * HEAD_DIM:(h + 1) * HEAD_DIM]
    rows = [(h * GQA_GROUP + g) * HEAD_DIM for g in range(GQA_GROUP)]
    qth = jnp.concatenate([qt_ref[r:r + HEAD_DIM, :] for r in rows], axis=1)
    st = jnp.dot(kh, qth, preferred_element_type=F32)
    st = jnp.where(valid, st, NEG_INF)
    sink = sink_ref[h]
    m = jnp.maximum(jnp.max(st, axis=0, keepdims=True), sink)
    e = jnp.exp(st - m).astype(BF16)
    ones = jnp.ones((ONES_ROWS, PAIR_KEYS), BF16)
    vth = jnp.concatenate([vt[h * HEAD_DIM:(h + 1) * HEAD_DIM, :], ones], axis=0)
    ot = jnp.dot(vth, e, preferred_element_type=F32)
    denom = ot[HEAD_DIM:HEAD_DIM + 1, :] + jnp.exp(sink - m)
    ot = ot[:HEAD_DIM, :] * (1.0 / denom)
    return jnp.concatenate([ot[:, g * PAIR:(g + 1) * PAIR] for g in range(GQA_GROUP)], axis=0).T


def _attend_heads(q, k, v, sink_ref, o_ref, lq):
    for h in range(N_KV_HEADS):
        kh = k[:, h * HEAD_DIM:(h + 1) * HEAD_DIM]
        vh = v[:, h * HEAD_DIM:(h + 1) * HEAD_DIM]
        heads = [h * GQA_GROUP + g for g in range(GQA_GROUP)]
        qh = jnp.concatenate([q[:, n * HEAD_DIM:(n + 1) * HEAD_DIM] for n in heads], axis=0)
        s = lax.dot_general(qh, kh, (((1,), (1,)), ((), ())), preferred_element_type=F32)
        sink = jnp.concatenate(
            [jnp.broadcast_to(sink_ref[n:n + 1, 0:1], (lq, 1)) for n in heads], axis=0)
        m = jnp.maximum(jnp.max(s, axis=-1, keepdims=True), sink)
        e = jnp.exp(s - m)
        denom = jnp.sum(e, axis=-1, keepdims=True) + jnp.exp(sink - m)
        oh = jnp.dot(e.astype(BF16), vh, preferred_element_type=F32) * (1.0 / denom)
        o_ref[:, h * GQA_GROUP * HEAD_DIM:(h + 1) * GQA_GROUP * HEAD_DIM] = jnp.concatenate(
            [oh[g * lq:(g + 1) * lq, :] for g in range(GQA_GROUP)], axis=1).astype(o_ref.dtype)


def _attn_sample_body(q_ref, ck_ref, cv_ref, kn_ref, vn_ref, km_ref, vm_ref, sink_ref, o_ref):
    k = jnp.concatenate([ck_ref[...], kn_ref[...], km_ref[...]], axis=0).astype(BF16)
    v = jnp.concatenate([cv_ref[...], vn_ref[...], vm_ref[...]], axis=0).astype(BF16)
    _attend_heads(q_ref[...], k, v, sink_ref, o_ref, q_ref.shape[0])


def _attend_sample(o16, o32, cache_k, cache_v, meta_block, sinks, n_streams, ds):
    cache_rows = cache_k.shape[1]
    new = lambda col: pl.BlockSpec((ds, KV_WIDTH), lambda s: (s, col))
    meta = lambda col: pl.BlockSpec((N_META, KV_WIDTH), lambda s: (meta_block, col))
    cache = pl.BlockSpec((None, cache_rows, KV_WIDTH), lambda s: (s, 0, 0))
    return pl.pallas_call(
        _attn_sample_body,
        grid=(n_streams,),
        in_specs=[
            pl.BlockSpec((ds, ATTN_WIDTH), lambda s: (s, COL_QA)),
            cache, cache, new(0), new(1), meta(0), meta(1),
            pl.BlockSpec((N_Q_HEADS, LANES), lambda s: (0, 0)),
        ],
        out_specs=pl.BlockSpec((ds, ATTN_WIDTH), lambda s: (s, 0)),
        out_shape=jax.ShapeDtypeStruct((n_streams * ds, ATTN_WIDTH), BF16),
        compiler_params=pltpu.CompilerParams(
            dimension_semantics=("parallel",), vmem_limit_bytes=VMEM_LIMIT),
        name="attend_sample",
    )(o16, cache_k, cache_v, o32, o32, o32, o32, sinks)


def _mlstm_body(q_ref, k_ref, v_ref, gt_ref, bias_ref, c0_ref, n0_ref, m0_ref,
                hn_ref, c_ref, n_ref, m_ref, *, blk):
    @pl.when(pl.program_id(1) == 0)
    def _():
        c_ref[...] = c0_ref[...]
        n_ref[...] = n0_ref[...]
        m_ref[...] = m0_ref[...]

    setup = _mlstm_setup(gt_ref, bias_ref, blk)
    for h in range(M_HEADS):
        hv = _mlstm_head(setup, h, q_ref, k_ref, v_ref, c_ref, n_ref, m_ref, blk, None)
        hn_ref[:, h * M_V_DIM:(h + 1) * M_V_DIM] = hv.astype(hn_ref.dtype)


def _mlstm_setup(gt_ref, bias_ref, blk):
    x = gt_ref[...] + bias_ref[...]
    lane = lax.broadcasted_iota(jnp.int32, x.shape, 1)
    log_f = jnp.minimum(x, 0.0) - jnp.log(1.0 + jnp.exp(-jnp.abs(x)))
    gates = jnp.where(lane < M_HEADS, x, log_f)
    row = lax.broadcasted_iota(jnp.int32, (blk, blk), 0)
    col = lax.broadcasted_iota(jnp.int32, (blk, blk), 1)
    causal = row >= col
    csum = jnp.dot(causal.astype(F32), gates, precision=lax.Precision.HIGHEST,
                   preferred_element_type=F32)
    z = jnp.where(lane < M_HEADS, gates, csum)
    pad = (-blk) % LANES
    zsq = z if pad == 0 else jnp.concatenate([z, jnp.zeros((pad, LANES), F32)], axis=0)
    zt = zsq.T[:, :blk]
    return z, zt, causal


def _mlstm_head(setup, h, q_ref, k_ref, v_ref, c_ref, n_ref, m_ref, blk, live):
    z, zt, causal = setup
    keep = (lambda new, old: new) if live is None else (lambda new, old: jnp.where(live, new, old))
    li_c = z[:, h:h + 1]
    b_c = z[:, M_HEADS + h:M_HEADS + h + 1]
    g_r = zt[h:h + 1, :] - zt[M_HEADS + h:M_HEADS + h + 1, :]
    m_prev = m_ref[h:h + 1, 0:1]
    d0 = jnp.where(causal, g_r, -jnp.inf)
    mm = jnp.maximum(m_prev, jnp.max(d0, axis=-1, keepdims=True))
    m_t = b_c + mm
    w = jnp.exp(d0 - mm)
    a = jnp.exp(m_prev - mm)

    qh = q_ref[:, h * M_QK_DIM:(h + 1) * M_QK_DIM]
    kh = k_ref[:, h * M_QK_DIM:(h + 1) * M_QK_DIM]
    vh = v_ref[:, h * M_V_DIM:(h + 1) * M_V_DIM]
    c_h = c_ref[h]
    n_h = n_ref[h:h + 1, :]

    qk = lax.dot_general(qh, kh, _NT, preferred_element_type=F32)
    wqk = w * (qk * MLSTM_K_SCALE)
    num = a * jnp.dot(qh, c_h.astype(BF16), preferred_element_type=F32) + jnp.dot(
        wqk.astype(BF16), vh, preferred_element_type=F32)
    den = a * jnp.sum(qh.astype(F32) * n_h, axis=-1, keepdims=True) + jnp.sum(wqk, axis=-1, keepdims=True)
    hv = num * (1.0 / jnp.maximum(jnp.abs(den), jnp.exp(-m_t)))
    hv = hv * lax.rsqrt(jnp.mean(hv * hv, axis=-1, keepdims=True) + EPS)

    m_new = m_t[blk - 1:blk, :]
    b_last = b_c[blk - 1:blk, :]
    ws = jnp.exp(b_last - b_c + li_c - m_new) * MLSTM_K_SCALE
    a_last = jnp.exp(b_last + m_prev - m_new)
    vs = (vh.astype(F32) * ws).astype(BF16)
    ktv = lax.dot_general(kh, vs, (((0,), (0,)), ((), ())), preferred_element_type=F32)
    c_ref[h] = keep(a_last * c_h + ktv, c_h)
    n_ref[h:h + 1, :] = keep(a_last * n_h + jnp.sum(kh.astype(F32) * ws, axis=0, keepdims=True), n_h)
    m_ref[h:h + 1, :] = keep(jnp.broadcast_to(m_new, (1, LANES)), m_ref[h:h + 1, :])
    return hv


def _mlstm(o16, o32, bias, c0, n0, m0, *, blk, n_streams, n_blocks, first_block, shared_init, col0):
    rb = lambda s, c: first_block + s * n_blocks + c
    st = (lambda s: 0) if shared_init else (lambda s: s)
    qk_blk = 2 * (COL_QKM - col0)
    return pl.pallas_call(
        functools.partial(_mlstm_body, blk=blk),
        grid=(n_streams, n_blocks),
        in_specs=[
            pl.BlockSpec((blk, M_QK_WIDTH), lambda s, c: (rb(s, c), qk_blk)),
            pl.BlockSpec((blk, M_QK_WIDTH), lambda s, c: (rb(s, c), qk_blk + 1)),
            pl.BlockSpec((blk, M_V_WIDTH), lambda s, c: (rb(s, c), COL_VM - col0)),
            pl.BlockSpec((blk, LANES), lambda s, c: (rb(s, c), GATE_BLOCK)),
            pl.BlockSpec((1, LANES), lambda s, c: (0, 0)),
            pl.BlockSpec((None, M_HEADS, M_QK_DIM, M_V_DIM), lambda s, c: (st(s), 0, 0, 0)),
            pl.BlockSpec((None, M_HEADS, M_QK_DIM), lambda s, c: (st(s), 0, 0)),
            pl.BlockSpec((None, M_HEADS, LANES), lambda s, c: (st(s), 0, 0)),
        ],
        out_specs=[
            pl.BlockSpec((blk, M_V_WIDTH), lambda s, c: (s * n_blocks + c, 0)),
            pl.BlockSpec((None, M_HEADS, M_QK_DIM, M_V_DIM), lambda s, c: (s, 0, 0, 0)),
            pl.BlockSpec((None, M_HEADS, M_QK_DIM), lambda s, c: (s, 0, 0)),
            pl.BlockSpec((None, M_HEADS, LANES), lambda s, c: (s, 0, 0)),
        ],
        out_shape=[
            jax.ShapeDtypeStruct((n_streams * n_blocks * blk, M_V_WIDTH), BF16),
            jax.ShapeDtypeStruct((n_streams, M_HEADS, M_QK_DIM, M_V_DIM), F32),
            jax.ShapeDtypeStruct((n_streams, M_HEADS, M_QK_DIM), F32),
            jax.ShapeDtypeStruct((n_streams, M_HEADS, LANES), F32),
        ],
        compiler_params=pltpu.CompilerParams(
            dimension_semantics=("parallel", "arbitrary"), vmem_limit_bytes=VMEM_LIMIT),
        name="mlstm_blk%d" % blk,
    )(o16, o16, o16, o32, bias, c0, n0, m0)


def _merge_body(x_ref, oa_ref, hn_ref, za_ref, om_ref, zm_ref, ga_ref, gm_ref, gmh_ref,
                wpa_ref, wpm_ref, wout_ref, gf_ref, y_ref):
    _merge_compute(x_ref, oa_ref, hn_ref, za_ref, om_ref, zm_ref, ga_ref, gm_ref, gmh_ref,
                   wpa_ref, wpm_ref, wout_ref, gf_ref, y_ref)


def _merge_compute(x_ref, oa_ref, hn_ref, za_ref, om_ref, zm_ref, ga_ref, gm_ref, gmh_ref,
                   wpa_ref, wpm_ref, wout_ref, gf_ref, y_ref):
    a_in = oa_ref[...].astype(F32) * _silu(za_ref[...].astype(F32))
    ya = jnp.dot(a_in.astype(BF16), wpa_ref[...], preferred_element_type=F32)
    m_in = (hn_ref[...].astype(F32) * gmh_ref[...]) * _sigmoid(om_ref[...].astype(F32)) * _silu(
        zm_ref[...].astype(F32))
    ym = jnp.dot(m_in.astype(BF16), wpm_ref[...], preferred_element_type=F32)
    merged = _sigmoid(ga_ref[...].astype(F32)) * ya + _sigmoid(gm_ref[...].astype(F32)) * ym
    xo = x_ref[...] + jnp.dot(merged.astype(BF16), wout_ref[...], preferred_element_type=F32)
    ms = jnp.mean(xo * xo, axis=-1, keepdims=True)
    y_ref[...] = (xo * lax.rsqrt(ms + EPS)) * gf_ref[...]


def _merge(x, o_att, hn, o16, g_mh, w_pa, w_pm, w_out, g_final, rows, tm, col0):
    assert rows % tm == 0
    tile = lambda col: pl.BlockSpec((tm, D_MODEL), lambda i: (i, col))
    vec = pl.BlockSpec((1, D_MODEL), lambda i: (0, 0))
    weight = pl.BlockSpec((D_MODEL, D_MODEL), lambda i: (0, 0), pipeline_mode=pl.Buffered(1))
    return pl.pallas_call(
        _merge_body,
        grid=(rows // tm,),
        in_specs=[tile(0), tile(0), tile(0), tile(COL_ZA - col0), tile(COL_OM - col0), tile(COL_ZM - col0),
                  tile(COL_GA - col0), tile(COL_GM - col0), vec, weight, weight, weight, vec],
        out_specs=tile(0),
        out_shape=jax.ShapeDtypeStruct((rows, D_MODEL), F32),
        compiler_params=pltpu.CompilerParams(
            dimension_semantics=("parallel",), vmem_limit_bytes=VMEM_LIMIT),
        name="merge",
    )(x, o_att, hn, o16, o16, o16, o16, o16, g_mh, w_pa, w_pm, w_out, g_final)


def _mixer_body(qt_ref, ka_ref, kb_ref, km_ref, vta_ref, vtb_ref, vtm_ref, sink_ref,
                q_ref, k_ref, v_ref, gt_ref, bias_ref, c0_ref, n0_ref, m0_ref,
                x_ref, za_ref, om_ref, zm_ref, ga_ref, gm_ref, gmh_ref, wpa_ref, wpm_ref, wout_ref, gf_ref,
                y_ref, c_ref, n_ref, m_ref, a_scr, m_scr, *, n_blk):
    c = pl.program_id(0)

    @pl.when(c == 0)
    def _():
        c_ref[...] = c0_ref[...]
        n_ref[...] = n0_ref[...]
        m_ref[...] = m0_ref[...]
        a_scr[...] = jnp.zeros(a_scr.shape, a_scr.dtype)
        m_scr[...] = jnp.zeros(m_scr.shape, m_scr.dtype)

    wr = c % 2
    rd = 1 - wr
    blk = jnp.minimum(c, n_blk - 1)
    live = c < n_blk
    attn = _attn_pair_setup(blk, ka_ref, kb_ref, km_ref, vta_ref, vtb_ref, vtm_ref)
    cell = _mlstm_setup(gt_ref, bias_ref, PAIR)
    a_in = a_scr[rd]
    m_in = m_scr[rd]
    att_cols = GQA_GROUP * HEAD_DIM
    merged = []
    for i in range(M_HEADS):
        cols = slice(i * M_V_DIM, (i + 1) * M_V_DIM)
        ya = jnp.dot(a_in, wpa_ref[:, cols], preferred_element_type=F32)
        ym = jnp.dot(m_in, wpm_ref[:, cols], preferred_element_type=F32)
        merged.append((_sigmoid(ga_ref[:, cols].astype(F32)) * ya
                       + _sigmoid(gm_ref[:, cols].astype(F32)) * ym).astype(BF16))

        hv = _mlstm_head(cell, i, q_ref, k_ref, v_ref, c_ref, n_ref, m_ref, PAIR, live)
        m_scr[wr, :, cols] = ((hv * gmh_ref[:, cols]) * _sigmoid(om_ref[:, cols].astype(F32)) * _silu(
            zm_ref[:, cols].astype(F32))).astype(BF16)
        if i % 2 == 0:
            h = i // 2
            hcols = slice(h * att_cols, (h + 1) * att_cols)
            oa = _attn_pair_head(attn, h, qt_ref, sink_ref)
            a_scr[wr, :, hcols] = (oa * _silu(za_ref[:, hcols].astype(F32))).astype(BF16)

    xo = x_ref[...] + jnp.dot(jnp.concatenate(merged, axis=1), wout_ref[...], preferred_element_type=F32)
    ms = jnp.mean(xo * xo, axis=-1, keepdims=True)
    y_ref[...] = (xo * lax.rsqrt(ms + EPS)) * gf_ref[...]


def _mixer(x, qvt, o32, o16, k_meta, vt_meta, sinks_t, bias, c0, n0, m0, g_mh, w_pa, w_pm, w_out, g_final,
           col0):
    seq = x.shape[0]
    assert seq % PAIR == 0
    n_blk = seq // PAIR
    cur = lambda c: jnp.minimum(c, n_blk - 1)
    prv = lambda c: jnp.maximum(cur(c) - 1, 0)
    mrg = lambda c: jnp.maximum(c - 1, 0)
    v_row = ATTN_WIDTH // KV_WIDTH
    qk_blk = 2 * (COL_QKM - col0)
    tile = lambda col: pl.BlockSpec((PAIR, D_MODEL), lambda c: (mrg(c), col))
    tile_cur = lambda col: pl.BlockSpec((PAIR, D_MODEL), lambda c: (cur(c), col))
    vec = pl.BlockSpec((1, D_MODEL), lambda c: (0, 0))
    weight = pl.BlockSpec((D_MODEL, D_MODEL), lambda c: (0, 0), pipeline_mode=pl.Buffered(1))
    state = lambda *blk: pl.BlockSpec((None,) + blk, lambda c: (0,) * (len(blk) + 1))
    return pl.pallas_call(
        functools.partial(_mixer_body, n_blk=n_blk),
        grid=(n_blk + 1,),
        in_specs=[
            pl.BlockSpec((ATTN_WIDTH, PAIR), lambda c: (0, cur(c))),
            pl.BlockSpec((PAIR, KV_WIDTH), lambda c: (prv(c), 0)),
            pl.BlockSpec((PAIR, KV_WIDTH), lambda c: (cur(c), 0)),
            pl.BlockSpec((N_META, KV_WIDTH), lambda c: (0, 0)),
            pl.BlockSpec((KV_WIDTH, PAIR), lambda c: (v_row, prv(c))),
            pl.BlockSpec((KV_WIDTH, PAIR), lambda c: (v_row, cur(c))),
            pl.BlockSpec((KV_WIDTH, N_META), lambda c: (0, 0)),
            pl.BlockSpec((N_KV_HEADS, 1, GQA_GROUP * PAIR), lambda c: (0, 0, 0)),
            pl.BlockSpec((PAIR, M_QK_WIDTH), lambda c: (cur(c), qk_blk)),
            pl.BlockSpec((PAIR, M_QK_WIDTH), lambda c: (cur(c), qk_blk + 1)),
            pl.BlockSpec((PAIR, M_V_WIDTH), lambda c: (cur(c), COL_VM - col0)),
            pl.BlockSpec((PAIR, LANES), lambda c: (cur(c), GATE_BLOCK)),
            pl.BlockSpec((1, LANES), lambda c: (0, 0)),
            state(M_HEADS, M_QK_DIM, M_V_DIM), state(M_HEADS, M_QK_DIM), state(M_HEADS, LANES),
            tile(0), tile_cur(COL_ZA - col0), tile_cur(COL_OM - col0), tile_cur(COL_ZM - col0),
            tile(COL_GA - col0), tile(COL_GM - col0), vec, weight, weight, weight, vec,
        ],
        out_specs=[
            tile(0),
            state(M_HEADS, M_QK_DIM, M_V_DIM), state(M_HEADS, M_QK_DIM), state(M_HEADS, LANES),
        ],
        out_shape=[
            jax.ShapeDtypeStruct((seq, D_MODEL), F32),
            jax.ShapeDtypeStruct((1, M_HEADS, M_QK_DIM, M_V_DIM), F32),
            jax.ShapeDtypeStruct((1, M_HEADS, M_QK_DIM), F32),
            jax.ShapeDtypeStruct((1, M_HEADS, LANES), F32),
        ],
        scratch_shapes=[pltpu.VMEM((2, PAIR, ATTN_WIDTH), BF16), pltpu.VMEM((2, PAIR, M_V_WIDTH), BF16)],
        compiler_params=pltpu.CompilerParams(
            dimension_semantics=("arbitrary",), vmem_limit_bytes=VMEM_LIMIT),
        name="mixer",
    )(qvt, o32, o32, k_meta, qvt, qvt, vt_meta, sinks_t,
      o16, o16, o16, o32, bias, c0, n0, m0,
      x, o16, o16, o16, o16, o16, g_mh, w_pa, w_pm, w_out, g_final)


def kernel(x_prompt, x_sample, cache_k, cache_v, state_C, state_n, state_m, meta_tokens, g_norm, w_in,
           b_igate, b_fgate, attn_sinks, g_mhnorm, w_pa, w_pm, w_out, g_final):
    batch, seq, _ = x_prompt.shape
    db, ds, _ = x_sample.shape
    depth = w_in.shape[0]
    assert batch == 1 and depth == 1 and ds == N_META
    cache_rows = cache_k.shape[2]

    w_t = w_in[0].T
    wkv = _prep_kv(w_t)
    wmain = _prep_main(w_t)
    g_in = g_norm[0].reshape(1, D_MODEL)
    bias = jnp.concatenate([b_igate[0], b_fgate[0], jnp.zeros((LANES - 2 * M_HEADS,), F32)]).reshape(1, LANES)
    sinks = jnp.broadcast_to(attn_sinks[0][:, None], (N_Q_HEADS, LANES))
    sinks_t = jnp.broadcast_to(attn_sinks[0].reshape(N_KV_HEADS, 1, GQA_GROUP, 1),
                               (N_KV_HEADS, 1, GQA_GROUP, PAIR)).reshape(N_KV_HEADS, 1, GQA_GROUP * PAIR)
    g_mh = g_mhnorm[0].reshape(1, M_V_WIDTH)
    g_fin = g_final.reshape(1, D_MODEL)
    wpa, wpm, wout = w_pa[0].astype(BF16), w_pm[0].astype(BF16), w_out[0].astype(BF16)

    xp = x_prompt.reshape(seq, D_MODEL)
    xs_rows = db * ds
    x_small = jnp.concatenate([x_sample.reshape(xs_rows, D_MODEL), meta_tokens.astype(F32)], axis=0)
    meta_block = xs_rows // N_META
    p_tm = min(seq, 1024)
    p32, p16, xn_p = _project(xp, g_in, wkv, wmain, tm=p_tm, col0=COL_ZA)
    s32, s16, _ = _project(x_small, g_in, wkv, wmain, tm=x_small.shape[0], col0=COL_QA)
    qvt = _project_t(wmain, wkv, xn_p, tm=p_tm)

    zeros = lambda *shape: jnp.zeros(shape, F32)
    _, c_meta, n_meta, m_meta = _mlstm(
        s16, s32, bias, zeros(1, M_HEADS, M_QK_DIM, M_V_DIM), zeros(1, M_HEADS, M_QK_DIM),
        zeros(1, M_HEADS, LANES), blk=N_META, n_streams=1, n_blocks=1, first_block=meta_block,
        shared_init=True, col0=COL_QA)

    k_meta = s32[xs_rows:, :KV_WIDTH]
    vt_meta = s32[xs_rows:, KV_WIDTH:2 * KV_WIDTH].T.astype(BF16)
    y_p, c_p, n_p, m_p = _mixer(xp, qvt, p32, p16, k_meta, vt_meta, sinks_t, bias, c_meta, n_meta, m_meta,
                                g_mh, wpa, wpm, wout, g_fin, col0=COL_ZA)

    ck = cache_k[0].reshape(db, cache_rows, KV_WIDTH)
    cv = cache_v[0].reshape(db, cache_rows, KV_WIDTH)
    oa_s = _attend_sample(s16, s32, ck, cv, meta_block, sinks, db, ds)
    m0_s = jnp.broadcast_to(state_m[0][:, :, None], (db, M_HEADS, LANES))
    hn_s, c_s, n_s, m_s = _mlstm(s16, s32, bias, state_C[0], state_n[0], m0_s, blk=ds, n_streams=db,
                                 n_blocks=1, first_block=0, shared_init=False, col0=COL_QA)
    y_s = _merge(x_small, oa_s, hn_s, s16, g_mh, wpa, wpm, wout, g_fin, rows=xs_rows, tm=xs_rows, col0=COL_QA)

    kv_shape = (1, batch, cache_rows, N_KV_HEADS, HEAD_DIM)
    k_p = p32[seq - cache_rows:, :KV_WIDTH].reshape(kv_shape)
    v_p = p32[seq - cache_rows:, KV_WIDTH:2 * KV_WIDTH].reshape(kv_shape)
    new_k = s32[:xs_rows, :KV_WIDTH].reshape(db, ds, KV_WIDTH)
    new_v = s32[:xs_rows, KV_WIDTH:2 * KV_WIDTH].reshape(db, ds, KV_WIDTH)
    skv_shape = (1, db, cache_rows, N_KV_HEADS, HEAD_DIM)
    k_s = jnp.concatenate([ck, new_k], axis=1)[:, -cache_rows:].reshape(skv_shape)
    v_s = jnp.concatenate([cv, new_v], axis=1)[:, -cache_rows:].reshape(skv_shape)

    return (y_p.reshape(batch, seq, D_MODEL), y_s.reshape(db, ds, D_MODEL),
            k_p, v_p, c_p[None], n_p[None], m_p[None, :, :, 0],
            k_s, v_s, c_s[None], n_s[None], m_s[None, :, :, 0])
```

```python
import functools
import math

import jax
import jax.numpy as jnp
import numpy as np
from jax import lax
from jax.experimental import pallas as pl
from jax.experimental.pallas import tpu as pltpu

F32 = jnp.float32
BF16 = jnp.bfloat16

D_MODEL = 2048
CHUNK = 64
N_META = 16
HEAD_DIM = 64
N_Q_HEADS = 32
N_KV_HEADS = 4
GQA_GROUP = N_Q_HEADS // N_KV_HEADS
WINDOW = 128
WIN_CHUNKS = WINDOW // CHUNK
ATTN_WIDTH = N_Q_HEADS * HEAD_DIM
KV_WIDTH = N_KV_HEADS * HEAD_DIM
M_HEADS = 8
M_QK_DIM = 128
M_V_DIM = 256
M_QK_WIDTH = M_HEADS * M_QK_DIM
M_V_WIDTH = M_HEADS * M_V_DIM
EPS = 1e-6
NEG_INF = -1e30
COL_SIZES = (ATTN_WIDTH, KV_WIDTH, KV_WIDTH, ATTN_WIDTH, M_QK_WIDTH, M_QK_WIDTH, M_V_WIDTH, M_V_WIDTH,
             M_HEADS, M_HEADS, M_V_WIDTH, D_MODEL, D_MODEL)
COL_STARTS = tuple(int(v) for v in np.cumsum((0,) + COL_SIZES[:-1]))
(SRC_QA, SRC_KA, SRC_VA, SRC_ZA, SRC_QM, SRC_KM, SRC_VM, SRC_OM, SRC_IG, SRC_FG, SRC_ZM, SRC_GA,
 SRC_GM) = COL_STARTS

LANES = 128
VMEM_LIMIT = 56 * 1024 * 1024

KV_OUT = 2 * KV_WIDTH + LANES
GATE_BLOCK = 2 * KV_WIDTH // LANES
MAIN_OUT = 8 * D_MODEL
COL_QA, COL_ZA, COL_QKM, COL_VM, COL_OM, COL_ZM, COL_GA, COL_GM = range(8)
PROJ_TN = 1024
MLSTM_K_SCALE = M_QK_DIM ** -0.5
Q_SCALE = 1.0 / math.sqrt(HEAD_DIM)
PREP_TN = 512
GATE_ROW_SHIFT = 2 * M_HEADS
QV_ROWS = ATTN_WIDTH + KV_WIDTH
PAIR = 2 * CHUNK
PAIR_KEYS = 2 * PAIR + N_META
ONES_ROWS = 16


def _sigmoid(x):
    return 0.5 * jnp.tanh(0.5 * x) + 0.5


def _silu(x):
    return x * _sigmoid(x)


def _main_weight_plan():
    groups = [(SRC_QA, ATTN_WIDTH, True), (SRC_ZA, ATTN_WIDTH, False), (SRC_QM, M_QK_WIDTH, False),
              (SRC_KM, M_QK_WIDTH, False), (SRC_VM, M_V_WIDTH, False), (SRC_OM, M_V_WIDTH, False),
              (SRC_ZM, M_V_WIDTH, False), (SRC_GA, D_MODEL, False), (SRC_GM, D_MODEL, False)]
    src, nb, shifted, is_q = [], [], [], []
    for start, width, q in groups:
        shift = start % PREP_TN
        assert shift in (0, GATE_ROW_SHIFT) and width % PREP_TN == 0
        for b in range(width // PREP_TN):
            blk = (start - shift) // PREP_TN + b
            src.append(blk)
            nb.append((blk + 1) * (PREP_TN // GATE_ROW_SHIFT))
            shifted.append(int(shift != 0))
            is_q.append(int(q))
    return tuple(np.asarray(a, np.int32) for a in (src, nb, shifted, is_q))


def _prep_main_body(src_ref, nb_ref, shifted_ref, isq_ref, wa_ref, wb_ref, o_ref):
    j = pl.program_id(0)

    @pl.when(shifted_ref[j] == 0)
    def _():
        scale = jnp.where(isq_ref[j] != 0, Q_SCALE, 1.0)
        o_ref[...] = (wa_ref[...] * scale).astype(BF16)

    @pl.when(shifted_ref[j] != 0)
    def _():
        o_ref[:PREP_TN - GATE_ROW_SHIFT, :] = wa_ref[GATE_ROW_SHIFT:, :].astype(BF16)
        o_ref[PREP_TN - GATE_ROW_SHIFT:, :] = wb_ref[...].astype(BF16)


def _prep_main(w_t):
    plan = _main_weight_plan()
    n_blocks = plan[0].shape[0]
    assert n_blocks * PREP_TN == MAIN_OUT
    return pl.pallas_call(
        _prep_main_body,
        grid_spec=pltpu.PrefetchScalarGridSpec(
            num_scalar_prefetch=4,
            grid=(n_blocks,),
            in_specs=[
                pl.BlockSpec((PREP_TN, D_MODEL), lambda j, src, nb, sh, q: (src[j], 0)),
                pl.BlockSpec((GATE_ROW_SHIFT, D_MODEL), lambda j, src, nb, sh, q: (nb[j], 0)),
            ],
            out_specs=pl.BlockSpec((PREP_TN, D_MODEL), lambda j, src, nb, sh, q: (j, 0)),
        ),
        out_shape=jax.ShapeDtypeStruct((MAIN_OUT, D_MODEL), BF16),
        compiler_params=pltpu.CompilerParams(
            dimension_semantics=("arbitrary",), vmem_limit_bytes=VMEM_LIMIT),
        name="prep_main",
    )(*[jnp.asarray(a) for a in plan], w_t, w_t)


def _prep_kv_body(kv_ref, gate_ref, o_ref):
    o_ref[:2 * KV_WIDTH, :] = kv_ref[...].astype(BF16)
    o_ref[2 * KV_WIDTH:2 * KV_WIDTH + GATE_ROW_SHIFT, :] = gate_ref[...].astype(BF16)
    o_ref[2 * KV_WIDTH + GATE_ROW_SHIFT:, :] = jnp.zeros((LANES - GATE_ROW_SHIFT, D_MODEL), BF16)


def _prep_kv(w_t):
    assert SRC_KA % (2 * KV_WIDTH) == 0 and SRC_VA == SRC_KA + KV_WIDTH
    assert SRC_IG % GATE_ROW_SHIFT == 0 and SRC_FG == SRC_IG + M_HEADS
    return pl.pallas_call(
        _prep_kv_body,
        grid=(1,),
        in_specs=[pl.BlockSpec((2 * KV_WIDTH, D_MODEL), lambda i: (SRC_KA // (2 * KV_WIDTH), 0)),
                  pl.BlockSpec((GATE_ROW_SHIFT, D_MODEL), lambda i: (SRC_IG // GATE_ROW_SHIFT, 0))],
        out_specs=pl.BlockSpec((KV_OUT, D_MODEL), lambda i: (0, 0)),
        out_shape=jax.ShapeDtypeStruct((KV_OUT, D_MODEL), BF16),
        compiler_params=pltpu.CompilerParams(vmem_limit_bytes=VMEM_LIMIT),
        name="prep_kv",
    )(w_t, w_t)


_NT = (((1,), (1,)), ((), ()))


def _proj_body(x_ref, g_ref, wkv_ref, w_ref, o32_ref, o16_ref, xn_ref, *, tile0):
    @pl.when(pl.program_id(1) == 0)
    def _():
        x = x_ref[...]
        ms = jnp.mean(x * x, axis=-1, keepdims=True)
        xn_ref[...] = ((x * lax.rsqrt(ms + EPS)) * g_ref[...]).astype(BF16)
        o32_ref[...] = lax.dot_general(xn_ref[...], wkv_ref[...], _NT, preferred_element_type=F32)

    group = (pl.program_id(1) + tile0) // (D_MODEL // PROJ_TN)
    is_sigmoid = (group == COL_OM) | (group == COL_GA) | (group == COL_GM)
    is_silu = (group == COL_ZA) | (group == COL_ZM)

    def tile(act):
        acc = lax.dot_general(xn_ref[...], w_ref[...], _NT, preferred_element_type=F32)
        o16_ref[...] = act(acc).astype(BF16)

    pl.when(is_sigmoid)(lambda: tile(_sigmoid))
    pl.when(is_silu)(lambda: tile(_silu))
    pl.when(jnp.logical_not(is_sigmoid | is_silu))(lambda: tile(lambda acc: acc))


def _project(x, g, wkv, wmain, tm, col0):
    rows = x.shape[0]
    assert rows % tm == 0
    tile0 = col0 * (D_MODEL // PROJ_TN)
    n_tiles = MAIN_OUT // PROJ_TN - tile0
    return pl.pallas_call(
        functools.partial(_proj_body, tile0=tile0),
        grid=(rows // tm, n_tiles),
        in_specs=[
            pl.BlockSpec((tm, D_MODEL), lambda i, j: (i, 0)),
            pl.BlockSpec((1, D_MODEL), lambda i, j: (0, 0)),
            pl.BlockSpec((KV_OUT, D_MODEL), lambda i, j: (0, 0), pipeline_mode=pl.Buffered(1)),
            pl.BlockSpec((PROJ_TN, D_MODEL), lambda i, j: (j + tile0, 0)),
        ],
        out_specs=[
            pl.BlockSpec((tm, KV_OUT), lambda i, j: (i, 0)),
            pl.BlockSpec((tm, PROJ_TN), lambda i, j: (i, j)),
            pl.BlockSpec((tm, D_MODEL), lambda i, j: (i, 0)),
        ],
        out_shape=[jax.ShapeDtypeStruct((rows, KV_OUT), F32),
                   jax.ShapeDtypeStruct((rows, n_tiles * PROJ_TN), BF16),
                   jax.ShapeDtypeStruct((rows, D_MODEL), BF16)],
        compiler_params=pltpu.CompilerParams(
            dimension_semantics=("parallel", "arbitrary"), vmem_limit_bytes=VMEM_LIMIT),
        name="project",
    )(x, g, wkv, wmain)


def _proj_t_body(wq_ref, wv_ref, xn_ref, o_ref):
    xn = xn_ref[...]
    o_ref[:ATTN_WIDTH, :] = lax.dot_general(wq_ref[...], xn, _NT, preferred_element_type=F32).astype(BF16)
    o_ref[ATTN_WIDTH:, :] = lax.dot_general(wv_ref[...], xn, _NT, preferred_element_type=F32).astype(BF16)


def _project_t(wmain, wkv, xn, tm):
    rows = xn.shape[0]
    assert rows % tm == 0
    return pl.pallas_call(
        _proj_t_body,
        grid=(rows // tm,),
        in_specs=[pl.BlockSpec((ATTN_WIDTH, D_MODEL), lambda i: (COL_QA, 0), pipeline_mode=pl.Buffered(1)),
                  pl.BlockSpec((KV_WIDTH, D_MODEL), lambda i: (1, 0), pipeline_mode=pl.Buffered(1)),
                  pl.BlockSpec((tm, D_MODEL), lambda i: (i, 0))],
        out_specs=pl.BlockSpec((QV_ROWS, tm), lambda i: (0, i)),
        out_shape=jax.ShapeDtypeStruct((QV_ROWS, rows), BF16),
        compiler_params=pltpu.CompilerParams(
            dimension_semantics=("parallel",), vmem_limit_bytes=VMEM_LIMIT),
        name="project_t",
    )(wmain, wkv, xn)


def _attn_pair_setup(p, ka_ref, kb_ref, km_ref, vta_ref, vtb_ref, vtm_ref):
    k = jnp.concatenate([ka_ref[...], kb_ref[...], km_ref[...]], axis=0).astype(BF16)
    vt = jnp.concatenate([vta_ref[...], vtb_ref[...], vtm_ref[...]], axis=1)
    odd = (lax.broadcasted_iota(jnp.int32, (CHUNK, GQA_GROUP * PAIR), 1) % PAIR) >= CHUNK
    first = p == 0
    return k, vt, (odd | first, first, jnp.logical_not(odd))


def _attn_pair_head(setup, h, qt_ref, sink_ref):
    k, vt, (mask_a, mask_b, mask_d) = setup
    kh = k[:, h * HEAD_DIM:(h + 1) * HEAD_DIM]
    rows = [(h * GQA_GROUP + g) * HEAD_DIM for g in range(GQA_GROUP)]
    qth = jnp.concatenate([qt_ref[r:r + HEAD_DIM, :] for r in rows], axis=1)
    st = jnp.dot(kh, qth, preferred_element_type=F32)
    st = jnp.concatenate([
        jnp.where(mask_a, NEG_INF, st[:CHUNK]),
        jnp.where(mask_b, NEG_INF, st[CHUNK:2 * CHUNK]),
        st[2 * CHUNK:3 * CHUNK],
        jnp.where(mask_d, NEG_INF, st[3 * CHUNK:4 * CHUNK]),
        st[4 * CHUNK:]], axis=0)
    sink = sink_ref[h]
    m = jnp.maximum(jnp.max(st, axis=0, keepdims=True), sink)
    e = jnp.exp(st - m).astype(BF16)
    ones = jnp.ones((ONES_ROWS, PAIR_KEYS), BF16)
    vth = jnp.concatenate([vt[h * HEAD_DIM:(h + 1) * HEAD_DIM, :], ones], axis=0)
    ot = jnp.dot(vth, e, preferred_element_type=F32)
    denom = ot[HEAD_DIM:HEAD_DIM + 1, :] + jnp.exp(sink - m)
    ot = ot[:HEAD_DIM, :] * (1.0 / denom)
    return jnp.concatenate([ot[:, g * PAIR:(g + 1) * PAIR] for g in range(GQA_GROUP)], axis=0).T


def _attend_heads(q, k, v, sink_ref, o_ref, lq):
    for h in range(N_KV_HEADS):
        kh = k[:, h * HEAD_DIM:(h + 1) * HEAD_DIM]
        vh = v[:, h * HEAD_DIM:(h + 1) * HEAD_DIM]
        heads = [h * GQA_GROUP + g for g in range(GQA_GROUP)]
        qh = jnp.concatenate([q[:, n * HEAD_DIM:(n + 1) * HEAD_DIM] for n in heads], axis=0)
        s = lax.dot_general(qh, kh, (((1,), (1,)), ((), ())), preferred_element_type=F32)
        sink = jnp.concatenate(
            [jnp.broadcast_to(sink_ref[n:n + 1, 0:1], (lq, 1)) for n in heads], axis=0)
        m = jnp.maximum(jnp.max(s, axis=-1, keepdims=True), sink)
        e = jnp.exp(s - m)
        denom = jnp.sum(e, axis=-1, keepdims=True) + jnp.exp(sink - m)
        oh = jnp.dot(e.astype(BF16), vh, preferred_element_type=F32) * (1.0 / denom)
        o_ref[:, h * GQA_GROUP * HEAD_DIM:(h + 1) * GQA_GROUP * HEAD_DIM] = jnp.concatenate(
            [oh[g * lq:(g + 1) * lq, :] for g in range(GQA_GROUP)], axis=1).astype(o_ref.dtype)


def _attn_sample_body(q_ref, ck_ref, cv_ref, kn_ref, vn_ref, km_ref, vm_ref, sink_ref, o_ref):
    k = jnp.concatenate([ck_ref[...], kn_ref[...], km_ref[...]], axis=0).astype(BF16)
    v = jnp.concatenate([cv_ref[...], vn_ref[...], vm_ref[...]], axis=0).astype(BF16)
    _attend_heads(q_ref[...], k, v, sink_ref, o_ref, q_ref.shape[0])


def _attend_sample(o16, o32, cache_k, cache_v, meta_block, sinks, n_streams, ds):
    cache_rows = cache_k.shape[1]
    new = lambda col: pl.BlockSpec((ds, KV_WIDTH), lambda s: (s, col))
    meta = lambda col: pl.BlockSpec((N_META, KV_WIDTH), lambda s: (meta_block, col))
    cache = pl.BlockSpec((None, cache_rows, KV_WIDTH), lambda s: (s, 0, 0))
    return pl.pallas_call(
        _attn_sample_body,
        grid=(n_streams,),
        in_specs=[
            pl.BlockSpec((ds, ATTN_WIDTH), lambda s: (s, COL_QA)),
            cache, cache, new(0), new(1), meta(0), meta(1),
            pl.BlockSpec((N_Q_HEADS, LANES), lambda s: (0, 0)),
        ],
        out_specs=pl.BlockSpec((ds, ATTN_WIDTH), lambda s: (s, 0)),
        out_shape=jax.ShapeDtypeStruct((n_streams * ds, ATTN_WIDTH), BF16),
        compiler_params=pltpu.CompilerParams(
            dimension_semantics=("parallel",), vmem_limit_bytes=VMEM_LIMIT),
        name="attend_sample",
    )(o16, cache_k, cache_v, o32, o32, o32, o32, sinks)


def _mlstm_body(q_ref, k_ref, v_ref, gt_ref, bias_ref, c0_ref, n0_ref, m0_ref,
                hn_ref, c_ref, n_ref, m_ref, *, blk):
    @pl.when(pl.program_id(1) == 0)
    def _():
        c_ref[...] = c0_ref[...]
        n_ref[...] = n0_ref[...]
        m_ref[...] = m0_ref[...]

    setup = _mlstm_setup(gt_ref, bias_ref, blk)
    for h in range(M_HEADS):
        hv = _mlstm_head(setup, h, q_ref, k_ref, v_ref, c_ref, n_ref, m_ref, blk, None)
        hn_ref[:, h * M_V_DIM:(h + 1) * M_V_DIM] = hv.astype(hn_ref.dtype)


def _mlstm_setup(gt_ref, bias_ref, blk):
    x = gt_ref[...] + bias_ref[...]
    lane = lax.broadcasted_iota(jnp.int32, x.shape, 1)
    log_f = jnp.minimum(x, 0.0) - jnp.log(1.0 + jnp.exp(-jnp.abs(x)))
    gates = jnp.where(lane < M_HEADS, x, log_f)
    row = lax.broadcasted_iota(jnp.int32, (blk, blk), 0)
    col = lax.broadcasted_iota(jnp.int32, (blk, blk), 1)
    causal = row >= col
    csum = jnp.dot(causal.astype(F32), gates, precision=lax.Precision.HIGHEST,
                   preferred_element_type=F32)
    z = jnp.where(lane < M_HEADS, gates, csum)
    pad = (-blk) % LANES
    zsq = z if pad == 0 else jnp.concatenate([z, jnp.zeros((pad, LANES), F32)], axis=0)
    zt = zsq.T[:, :blk]
    return z, zt, causal


def _mlstm_head(setup, h, q_ref, k_ref, v_ref, c_ref, n_ref, m_ref, blk, live):
    z, zt, causal = setup
    keep = (lambda new, old: new) if live is None else (lambda new, old: jnp.where(live, new, old))
    li_c = z[:, h:h + 1]
    b_c = z[:, M_HEADS + h:M_HEADS + h + 1]
    g_r = zt[h:h + 1, :] - zt[M_HEADS + h:M_HEADS + h + 1, :]
    m_prev = m_ref[h:h + 1, 0:1]
    d0 = jnp.where(causal, g_r, -jnp.inf)
    mm = jnp.maximum(m_prev, jnp.max(d0, axis=-1, keepdims=True))
    m_t = b_c + mm
    w = jnp.exp(d0 - mm)
    a = jnp.exp(m_prev - mm)

    qh = q_ref[:, h * M_QK_DIM:(h + 1) * M_QK_DIM]
    kh = k_ref[:, h * M_QK_DIM:(h + 1) * M_QK_DIM]
    vh = v_ref[:, h * M_V_DIM:(h + 1) * M_V_DIM]
    c_h = c_ref[h]
    n_h = n_ref[h:h + 1, :]

    qk = lax.dot_general(qh, kh, _NT, preferred_element_type=F32)
    wqk = w * (qk * MLSTM_K_SCALE)
    num = a * jnp.dot(qh, c_h.astype(BF16), preferred_element_type=F32) + jnp.dot(
        wqk.astype(BF16), vh, preferred_element_type=F32)
    den = a * jnp.sum(qh.astype(F32) * n_h, axis=-1, keepdims=True) + jnp.sum(wqk, axis=-1, keepdims=True)
    hv = num * (1.0 / jnp.maximum(jnp.abs(den), jnp.exp(-m_t)))
    hv = hv * lax.rsqrt(jnp.mean(hv * hv, axis=-1, keepdims=True) + EPS)

    m_new = m_t[blk - 1:blk, :]
    b_last = b_c[blk - 1:blk, :]
    ws = jnp.exp(b_last - b_c + li_c - m_new) * MLSTM_K_SCALE
    a_last = jnp.exp(b_last + m_prev - m_new)
    vs = (vh.astype(F32) * ws).astype(BF16)
    ktv = lax.dot_general(kh, vs, (((0,), (0,)), ((), ())), preferred_element_type=F32)
    c_ref[h] = keep(a_last * c_h + ktv, c_h)
    n_ref[h:h + 1, :] = keep(a_last * n_h + jnp.sum(kh.astype(F32) * ws, axis=0, keepdims=True), n_h)
    m_ref[h:h + 1, :] = keep(jnp.broadcast_to(m_new, (1, LANES)), m_ref[h:h + 1, :])
    return hv


def _mlstm(o16, o32, bias, c0, n0, m0, *, blk, n_streams, n_blocks, first_block, shared_init, col0):
    rb = lambda s, c: first_block + s * n_blocks + c
    st = (lambda s: 0) if shared_init else (lambda s: s)
    qk_blk = 2 * (COL_QKM - col0)
    return pl.pallas_call(
        functools.partial(_mlstm_body, blk=blk),
        grid=(n_streams, n_blocks),
        in_specs=[
            pl.BlockSpec((blk, M_QK_WIDTH), lambda s, c: (rb(s, c), qk_blk)),
            pl.BlockSpec((blk, M_QK_WIDTH), lambda s, c: (rb(s, c), qk_blk + 1)),
            pl.BlockSpec((blk, M_V_WIDTH), lambda s, c: (rb(s, c), COL_VM - col0)),
            pl.BlockSpec((blk, LANES), lambda s, c: (rb(s, c), GATE_BLOCK)),
            pl.BlockSpec((1, LANES), lambda s, c: (0, 0)),
            pl.BlockSpec((None, M_HEADS, M_QK_DIM, M_V_DIM), lambda s, c: (st(s), 0, 0, 0)),
            pl.BlockSpec((None, M_HEADS, M_QK_DIM), lambda s, c: (st(s), 0, 0)),
            pl.BlockSpec((None, M_HEADS, LANES), lambda s, c: (st(s), 0, 0)),
        ],
        out_specs=[
            pl.BlockSpec((blk, M_V_WIDTH), lambda s, c: (s * n_blocks + c, 0)),
            pl.BlockSpec((None, M_HEADS, M_QK_DIM, M_V_DIM), lambda s, c: (s, 0, 0, 0)),
            pl.BlockSpec((None, M_HEADS, M_QK_DIM), lambda s, c: (s, 0, 0)),
            pl.BlockSpec((None, M_HEADS, LANES), lambda s, c: (s, 0, 0)),
        ],
        out_shape=[
            jax.ShapeDtypeStruct((n_streams * n_blocks * blk, M_V_WIDTH), BF16),
            jax.ShapeDtypeStruct((n_streams, M_HEADS, M_QK_DIM, M_V_DIM), F32),
            jax.ShapeDtypeStruct((n_streams, M_HEADS, M_QK_DIM), F32),
            jax.ShapeDtypeStruct((n_streams, M_HEADS, LANES), F32),
        ],
        compiler_params=pltpu.CompilerParams(
            dimension_semantics=("parallel", "arbitrary"), vmem_limit_bytes=VMEM_LIMIT),
        name="mlstm_blk%d" % blk,
    )(o16, o16, o16, o32, bias, c0, n0, m0)


def _merge_body(x_ref, oa_ref, hn_ref, za_ref, om_ref, zm_ref, ga_ref, gm_ref, gmh_ref,
                wpa_ref, wpm_ref, wout_ref, gf_ref, y_ref):
    _merge_compute(x_ref, oa_ref, hn_ref, za_ref, om_ref, zm_ref, ga_ref, gm_ref, gmh_ref,
                   wpa_ref, wpm_ref, wout_ref, gf_ref, y_ref)


def _merge_compute(x_ref, oa_ref, hn_ref, za_ref, om_ref, zm_ref, ga_ref, gm_ref, gmh_ref,
                   wpa_ref, wpm_ref, wout_ref, gf_ref, y_ref):
    a_in = oa_ref[...].astype(F32) * za_ref[...].astype(F32)
    ya = jnp.dot(a_in.astype(BF16), wpa_ref[...], preferred_element_type=F32)
    m_in = (hn_ref[...].astype(F32) * gmh_ref[...]) * om_ref[...].astype(F32) * zm_ref[...].astype(F32)
    ym = jnp.dot(m_in.astype(BF16), wpm_ref[...], preferred_element_type=F32)
    merged = ga_ref[...].astype(F32) * ya + gm_ref[...].astype(F32) * ym
    xo = x_ref[...] + jnp.dot(merged.astype(BF16), wout_ref[...], preferred_element_type=F32)
    ms = jnp.mean(xo * xo, axis=-1, keepdims=True)
    y_ref[...] = (xo * lax.rsqrt(ms + EPS)) * gf_ref[...]


def _merge(x, o_att, hn, o16, g_mh, w_pa, w_pm, w_out, g_final, rows, tm, col0):
    assert rows % tm == 0
    tile = lambda col: pl.BlockSpec((tm, D_MODEL), lambda i: (i, col))
    vec = pl.BlockSpec((1, D_MODEL), lambda i: (0, 0))
    weight = pl.BlockSpec((D_MODEL, D_MODEL), lambda i: (0, 0), pipeline_mode=pl.Buffered(1))
    return pl.pallas_call(
        _merge_body,
        grid=(rows // tm,),
        in_specs=[tile(0), tile(0), tile(0), tile(COL_ZA - col0), tile(COL_OM - col0), tile(COL_ZM - col0),
                  tile(COL_GA - col0), tile(COL_GM - col0), vec, weight, weight, weight, vec],
        out_specs=tile(0),
        out_shape=jax.ShapeDtypeStruct((rows, D_MODEL), F32),
        compiler_params=pltpu.CompilerParams(
            dimension_semantics=("parallel",), vmem_limit_bytes=VMEM_LIMIT),
        name="merge",
    )(x, o_att, hn, o16, o16, o16, o16, o16, g_mh, w_pa, w_pm, w_out, g_final)


def _mixer_body(qt_ref, ka_ref, kb_ref, km_ref, vta_ref, vtb_ref, vtm_ref, sink_ref,
                q_ref, k_ref, v_ref, gt_ref, bias_ref, c0_ref, n0_ref, m0_ref,
                x_ref, za_ref, om_ref, zm_ref, ga_ref, gm_ref, gmh_ref, wpa_ref, wpm_ref, wout_ref, gf_ref,
                y_ref, c_ref, n_ref, m_ref, a_scr, m_scr, *, n_blk):
    c = pl.program_id(0)

    @pl.when(c == 0)
    def _():
        c_ref[...] = c0_ref[...]
        n_ref[...] = n0_ref[...]
        m_ref[...] = m0_ref[...]
        a_scr[...] = jnp.zeros(a_scr.shape, a_scr.dtype)
        m_scr[...] = jnp.zeros(m_scr.shape, m_scr.dtype)

    wr = c % 2
    rd = 1 - wr
    blk = jnp.minimum(c, n_blk - 1)
    live = c < n_blk
    attn = _attn_pair_setup(blk, ka_ref, kb_ref, km_ref, vta_ref, vtb_ref, vtm_ref)
    cell = _mlstm_setup(gt_ref, bias_ref, PAIR)
    a_in = a_scr[rd]
    m_in = m_scr[rd]
    width = GQA_GROUP * HEAD_DIM
    heads_per_piece = width // M_V_DIM
    merged = []
    for i in range(N_KV_HEADS):
        cols = slice(i * width, (i + 1) * width)
        ya = jnp.dot(a_in, wpa_ref[:, cols], preferred_element_type=F32)
        ym = jnp.dot(m_in, wpm_ref[:, cols], preferred_element_type=F32)
        merged.append((ga_ref[:, cols].astype(F32) * ya + gm_ref[:, cols].astype(F32) * ym).astype(BF16))

        for h in range(i * heads_per_piece, (i + 1) * heads_per_piece):
            hc = slice(h * M_V_DIM, (h + 1) * M_V_DIM)
            hv = _mlstm_head(cell, h, q_ref, k_ref, v_ref, c_ref, n_ref, m_ref, PAIR, live)
            m_scr[wr, :, hc] = ((hv * gmh_ref[:, hc]) * om_ref[:, hc].astype(F32)
                                * zm_ref[:, hc].astype(F32)).astype(BF16)
        oa = _attn_pair_head(attn, i, qt_ref, sink_ref)
        a_scr[wr, :, cols] = (oa * za_ref[:, cols].astype(F32)).astype(BF16)

    xo = x_ref[...] + jnp.dot(jnp.concatenate(merged, axis=1), wout_ref[...], preferred_element_type=F32)
    ms = jnp.mean(xo * xo, axis=-1, keepdims=True)
    y_ref[...] = (xo * lax.rsqrt(ms + EPS)) * gf_ref[...]


def _mixer(x, qvt, o32, o16, k_meta, vt_meta, sinks_t, bias, c0, n0, m0, g_mh, w_pa, w_pm, w_out, g_final,
           col0):
    seq = x.shape[0]
    assert seq % PAIR == 0
    n_blk = seq // PAIR
    cur = lambda c: jnp.minimum(c, n_blk - 1)
    prv = lambda c: jnp.maximum(cur(c) - 1, 0)
    mrg = lambda c: jnp.maximum(c - 1, 0)
    v_row = ATTN_WIDTH // KV_WIDTH
    qk_blk = 2 * (COL_QKM - col0)
    tile = lambda col: pl.BlockSpec((PAIR, D_MODEL), lambda c: (mrg(c), col))
    tile_cur = lambda col: pl.BlockSpec((PAIR, D_MODEL), lambda c: (cur(c), col))
    vec = pl.BlockSpec((1, D_MODEL), lambda c: (0, 0))
    weight = pl.BlockSpec((D_MODEL, D_MODEL), lambda c: (0, 0), pipeline_mode=pl.Buffered(1))
    state = lambda *blk: pl.BlockSpec((None,) + blk, lambda c: (0,) * (len(blk) + 1))
    return pl.pallas_call(
        functools.partial(_mixer_body, n_blk=n_blk),
        grid=(n_blk + 1,),
        in_specs=[
            pl.BlockSpec((ATTN_WIDTH, PAIR), lambda c: (0, cur(c))),
            pl.BlockSpec((PAIR, KV_WIDTH), lambda c: (prv(c), 0)),
            pl.BlockSpec((PAIR, KV_WIDTH), lambda c: (cur(c), 0)),
            pl.BlockSpec((N_META, KV_WIDTH), lambda c: (0, 0)),
            pl.BlockSpec((KV_WIDTH, PAIR), lambda c: (v_row, prv(c))),
            pl.BlockSpec((KV_WIDTH, PAIR), lambda c: (v_row, cur(c))),
            pl.BlockSpec((KV_WIDTH, N_META), lambda c: (0, 0)),
            pl.BlockSpec((N_KV_HEADS, 1, GQA_GROUP * PAIR), lambda c: (0, 0, 0)),
            pl.BlockSpec((PAIR, M_QK_WIDTH), lambda c: (cur(c), qk_blk)),
            pl.BlockSpec((PAIR, M_QK_WIDTH), lambda c: (cur(c), qk_blk + 1)),
            pl.BlockSpec((PAIR, M_V_WIDTH), lambda c: (cur(c), COL_VM - col0)),
            pl.BlockSpec((PAIR, LANES), lambda c: (cur(c), GATE_BLOCK)),
            pl.BlockSpec((1, LANES), lambda c: (0, 0)),
            state(M_HEADS, M_QK_DIM, M_V_DIM), state(M_HEADS, M_QK_DIM), state(M_HEADS, LANES),
            tile(0), tile_cur(COL_ZA - col0), tile_cur(COL_OM - col0), tile_cur(COL_ZM - col0),
            tile(COL_GA - col0), tile(COL_GM - col0), vec, weight, weight, weight, vec,
        ],
        out_specs=[
            tile(0),
            state(M_HEADS, M_QK_DIM, M_V_DIM), state(M_HEADS, M_QK_DIM), state(M_HEADS, LANES),
        ],
        out_shape=[
            jax.ShapeDtypeStruct((seq, D_MODEL), F32),
            jax.ShapeDtypeStruct((1, M_HEADS, M_QK_DIM, M_V_DIM), F32),
            jax.ShapeDtypeStruct((1, M_HEADS, M_QK_DIM), F32),
            jax.ShapeDtypeStruct((1, M_HEADS, LANES), F32),
        ],
        scratch_shapes=[pltpu.VMEM((2, PAIR, ATTN_WIDTH), BF16), pltpu.VMEM((2, PAIR, M_V_WIDTH), BF16)],
        compiler_params=pltpu.CompilerParams(
            dimension_semantics=("arbitrary",), vmem_limit_bytes=VMEM_LIMIT),
        name="mixer",
    )(qvt, o32, o32, k_meta, qvt, qvt, vt_meta, sinks_t,
      o16, o16, o16, o32, bias, c0, n0, m0,
      x, o16, o16, o16, o16, o16, g_mh, w_pa, w_pm, w_out, g_final)


def kernel(x_prompt, x_sample, cache_k, cache_v, state_C, state_n, state_m, meta_tokens, g_norm, w_in,
           b_igate, b_fgate, attn_sinks, g_mhnorm, w_pa, w_pm, w_out, g_final):
    batch, seq, _ = x_prompt.shape
    db, ds, _ = x_sample.shape
    depth = w_in.shape[0]
    assert batch == 1 and depth == 1 and ds == N_META
    cache_rows = cache_k.shape[2]

    w_t = w_in[0].T
    wkv = _prep_kv(w_t)
    wmain = _prep_main(w_t)
    g_in = g_norm[0].reshape(1, D_MODEL)
    bias = jnp.concatenate([b_igate[0], b_fgate[0], jnp.zeros((LANES - 2 * M_HEADS,), F32)]).reshape(1, LANES)
    sinks = jnp.broadcast_to(attn_sinks[0][:, None], (N_Q_HEADS, LANES))
    sinks_t = jnp.broadcast_to(attn_sinks[0].reshape(N_KV_HEADS, 1, GQA_GROUP, 1),
                               (N_KV_HEADS, 1, GQA_GROUP, PAIR)).reshape(N_KV_HEADS, 1, GQA_GROUP * PAIR)
    g_mh = g_mhnorm[0].reshape(1, M_V_WIDTH)
    g_fin = g_final.reshape(1, D_MODEL)
    wpa, wpm, wout = w_pa[0].astype(BF16), w_pm[0].astype(BF16), w_out[0].astype(BF16)

    xp = x_prompt.reshape(seq, D_MODEL)
    xs_rows = db * ds
    x_small = jnp.concatenate([x_sample.reshape(xs_rows, D_MODEL), meta_tokens.astype(F32)], axis=0)
    meta_block = xs_rows // N_META
    p_tm = min(seq, 1024)
    p32, p16, xn_p = _project(xp, g_in, wkv, wmain, tm=p_tm, col0=COL_ZA)
    s32, s16, _ = _project(x_small, g_in, wkv, wmain, tm=x_small.shape[0], col0=COL_QA)
    qvt = _project_t(wmain, wkv, xn_p, tm=p_tm)

    zeros = lambda *shape: jnp.zeros(shape, F32)
    _, c_meta, n_meta, m_meta = _mlstm(
        s16, s32, bias, zeros(1, M_HEADS, M_QK_DIM, M_V_DIM), zeros(1, M_HEADS, M_QK_DIM),
        zeros(1, M_HEADS, LANES), blk=N_META, n_streams=1, n_blocks=1, first_block=meta_block,
        shared_init=True, col0=COL_QA)

    k_meta = s32[xs_rows:, :KV_WIDTH]
    vt_meta = s32[xs_rows:, KV_WIDTH:2 * KV_WIDTH].T.astype(BF16)
    y_p, c_p, n_p, m_p = _mixer(xp, qvt, p32, p16, k_meta, vt_meta, sinks_t, bias, c_meta, n_meta, m_meta,
                                g_mh, wpa, wpm, wout, g_fin, col0=COL_ZA)

    ck = cache_k[0].reshape(db, cache_rows, KV_WIDTH)
    cv = cache_v[0].reshape(db, cache_rows, KV_WIDTH)
    oa_s = _attend_sample(s16, s32, ck, cv, meta_block, sinks, db, ds)
    m0_s = jnp.broadcast_to(state_m[0][:, :, None], (db, M_HEADS, LANES))
    hn_s, c_s, n_s, m_s = _mlstm(s16, s32, bias, state_C[0], state_n[0], m0_s, blk=ds, n_streams=db,
                                 n_blocks=1, first_block=0, shared_init=False, col0=COL_QA)
    y_s = _merge(x_small, oa_s, hn_s, s16, g_mh, wpa, wpm, wout, g_fin, rows=xs_rows, tm=xs_rows, col0=COL_QA)

    kv_shape = (1, batch, cache_rows, N_KV_HEADS, HEAD_DIM)
    k_p = p32[seq - cache_rows:, :KV_WIDTH].reshape(kv_shape)
    v_p = p32[seq - cache_rows:, KV_WIDTH:2 * KV_WIDTH].reshape(kv_shape)
    new_k = s32[:xs_rows, :KV_WIDTH].reshape(db, ds, KV_WIDTH)
    new_v = s32[:xs_rows, KV_WIDTH:2 * KV_WIDTH].reshape(db, ds, KV_WIDTH)
    skv_shape = (1, db, cache_rows, N_KV_HEADS, HEAD_DIM)
    k_s = jnp.concatenate([ck, new_k], axis=1)[:, -cache_rows:].reshape(skv_shape)
    v_s = jnp.concatenate([cv, new_v], axis=1)[:, -cache_rows:].reshape(skv_shape)

    return (y_p.reshape(batch, seq, D_MODEL), y_s.reshape(db, ds, D_MODEL),
            k_p, v_p, c_p[None], n_p[None], m_p[None, :, :, 0],
            k_s, v_s, c_s[None], n_s[None], m_s[None, :, :, 0])
```

```python
import functools
import math

import jax
import jax.numpy as jnp
import numpy as np
from jax import lax
from jax.experimental import pallas as pl
from jax.experimental.pallas import tpu as pltpu

F32 = jnp.float32
BF16 = jnp.bfloat16

D_MODEL = 2048
CHUNK = 64
N_META = 16
HEAD_DIM = 64
N_Q_HEADS = 32
N_KV_HEADS = 4
GQA_GROUP = N_Q_HEADS // N_KV_HEADS
WINDOW = 128
WIN_CHUNKS = WINDOW // CHUNK
ATTN_WIDTH = N_Q_HEADS * HEAD_DIM
KV_WIDTH = N_KV_HEADS * HEAD_DIM
M_HEADS = 8
M_QK_DIM = 128
M_V_DIM = 256
M_QK_WIDTH = M_HEADS * M_QK_DIM
M_V_WIDTH = M_HEADS * M_V_DIM
EPS = 1e-6
NEG_INF = -1e30
COL_SIZES = (ATTN_WIDTH, KV_WIDTH, KV_WIDTH, ATTN_WIDTH, M_QK_WIDTH, M_QK_WIDTH, M_V_WIDTH, M_V_WIDTH,
             M_HEADS, M_HEADS, M_V_WIDTH, D_MODEL, D_MODEL)
COL_STARTS = tuple(int(v) for v in np.cumsum((0,) + COL_SIZES[:-1]))
(SRC_QA, SRC_KA, SRC_VA, SRC_ZA, SRC_QM, SRC_KM, SRC_VM, SRC_OM, SRC_IG, SRC_FG, SRC_ZM, SRC_GA,
 SRC_GM) = COL_STARTS

LANES = 128
VMEM_LIMIT = 56 * 1024 * 1024

KV_OUT = 2 * KV_WIDTH + LANES
GATE_BLOCK = 2 * KV_WIDTH // LANES
MAIN_OUT = 8 * D_MODEL
COL_QA, COL_ZA, COL_QKM, COL_VM, COL_OM, COL_ZM, COL_GA, COL_GM = range(8)
PROJ_TN = 2048
MLSTM_K_SCALE = M_QK_DIM ** -0.5
Q_SCALE = 1.0 / math.sqrt(HEAD_DIM)
PREP_TN = 512
GATE_ROW_SHIFT = 2 * M_HEADS
QV_ROWS = ATTN_WIDTH + KV_WIDTH
PAIR = 2 * CHUNK
PAIR_KEYS = 2 * PAIR + N_META
ONES_ROWS = 16


def _sigmoid(x):
    return 0.5 * jnp.tanh(0.5 * x) + 0.5


def _silu(x):
    return x * _sigmoid(x)


def _main_weight_plan():
    groups = [(SRC_QA, ATTN_WIDTH, True), (SRC_ZA, ATTN_WIDTH, False), (SRC_QM, M_QK_WIDTH, False),
              (SRC_KM, M_QK_WIDTH, False), (SRC_VM, M_V_WIDTH, False), (SRC_OM, M_V_WIDTH, False),
              (SRC_ZM, M_V_WIDTH, False), (SRC_GA, D_MODEL, False), (SRC_GM, D_MODEL, False)]
    src, nb, shifted, is_q = [], [], [], []
    for start, width, q in groups:
        shift = start % PREP_TN
        assert shift in (0, GATE_ROW_SHIFT) and width % PREP_TN == 0
        for b in range(width // PREP_TN):
            blk = (start - shift) // PREP_TN + b
            src.append(blk)
            nb.append((blk + 1) * (PREP_TN // GATE_ROW_SHIFT))
            shifted.append(int(shift != 0))
            is_q.append(int(q))
    return tuple(np.asarray(a, np.int32) for a in (src, nb, shifted, is_q))


def _prep_main_body(src_ref, nb_ref, shifted_ref, isq_ref, wa_ref, wb_ref, o_ref):
    j = pl.program_id(0)

    @pl.when(shifted_ref[j] == 0)
    def _():
        scale = jnp.where(isq_ref[j] != 0, Q_SCALE, 1.0)
        o_ref[...] = (wa_ref[...] * scale).astype(BF16)

    @pl.when(shifted_ref[j] != 0)
    def _():
        o_ref[:PREP_TN - GATE_ROW_SHIFT, :] = wa_ref[GATE_ROW_SHIFT:, :].astype(BF16)
        o_ref[PREP_TN - GATE_ROW_SHIFT:, :] = wb_ref[...].astype(BF16)


def _prep_main(w_t):
    plan = _main_weight_plan()
    n_blocks = plan[0].shape[0]
    assert n_blocks * PREP_TN == MAIN_OUT
    return pl.pallas_call(
        _prep_main_body,
        grid_spec=pltpu.PrefetchScalarGridSpec(
            num_scalar_prefetch=4,
            grid=(n_blocks,),
            in_specs=[
                pl.BlockSpec((PREP_TN, D_MODEL), lambda j, src, nb, sh, q: (src[j], 0)),
                pl.BlockSpec((GATE_ROW_SHIFT, D_MODEL), lambda j, src, nb, sh, q: (nb[j], 0)),
            ],
            out_specs=pl.BlockSpec((PREP_TN, D_MODEL), lambda j, src, nb, sh, q: (j, 0)),
        ),
        out_shape=jax.ShapeDtypeStruct((MAIN_OUT, D_MODEL), BF16),
        compiler_params=pltpu.CompilerParams(
            dimension_semantics=("arbitrary",), vmem_limit_bytes=VMEM_LIMIT),
        name="prep_main",
    )(*[jnp.asarray(a) for a in plan], w_t, w_t)


def _prep_kv_body(kv_ref, gate_ref, o_ref):
    o_ref[:2 * KV_WIDTH, :] = kv_ref[...].astype(BF16)
    o_ref[2 * KV_WIDTH:2 * KV_WIDTH + GATE_ROW_SHIFT, :] = gate_ref[...].astype(BF16)
    o_ref[2 * KV_WIDTH + GATE_ROW_SHIFT:, :] = jnp.zeros((LANES - GATE_ROW_SHIFT, D_MODEL), BF16)


def _prep_kv(w_t):
    assert SRC_KA % (2 * KV_WIDTH) == 0 and SRC_VA == SRC_KA + KV_WIDTH
    assert SRC_IG % GATE_ROW_SHIFT == 0 and SRC_FG == SRC_IG + M_HEADS
    return pl.pallas_call(
        _prep_kv_body,
        grid=(1,),
        in_specs=[pl.BlockSpec((2 * KV_WIDTH, D_MODEL), lambda i: (SRC_KA // (2 * KV_WIDTH), 0)),
                  pl.BlockSpec((GATE_ROW_SHIFT, D_MODEL), lambda i: (SRC_IG // GATE_ROW_SHIFT, 0))],
        out_specs=pl.BlockSpec((KV_OUT, D_MODEL), lambda i: (0, 0)),
        out_shape=jax.ShapeDtypeStruct((KV_OUT, D_MODEL), BF16),
        compiler_params=pltpu.CompilerParams(vmem_limit_bytes=VMEM_LIMIT),
        name="prep_kv",
    )(w_t, w_t)


_NT = (((1,), (1,)), ((), ()))


def _proj_body(x_ref, g_ref, wkv_ref, w_ref, o32_ref, o16_ref, xn_ref, *, tile0):
    @pl.when(pl.program_id(1) == 0)
    def _():
        x = x_ref[...]
        ms = jnp.mean(x * x, axis=-1, keepdims=True)
        xn_ref[...] = ((x * lax.rsqrt(ms + EPS)) * g_ref[...]).astype(BF16)
        o32_ref[...] = lax.dot_general(xn_ref[...], wkv_ref[...], _NT, preferred_element_type=F32)

    group = (pl.program_id(1) + tile0) // (D_MODEL // PROJ_TN)
    is_sigmoid = (group == COL_OM) | (group == COL_GA) | (group == COL_GM)
    is_silu = (group == COL_ZA) | (group == COL_ZM)

    def tile(act):
        acc = lax.dot_general(xn_ref[...], w_ref[...], _NT, preferred_element_type=F32)
        o16_ref[...] = act(acc).astype(BF16)

    pl.when(is_sigmoid)(lambda: tile(_sigmoid))
    pl.when(is_silu)(lambda: tile(_silu))
    pl.when(jnp.logical_not(is_sigmoid | is_silu))(lambda: tile(lambda acc: acc))


def _project(x, g, wkv, wmain, tm, col0):
    rows = x.shape[0]
    assert rows % tm == 0
    tile0 = col0 * (D_MODEL // PROJ_TN)
    n_tiles = MAIN_OUT // PROJ_TN - tile0
    return pl.pallas_call(
        functools.partial(_proj_body, tile0=tile0),
        grid=(rows // tm, n_tiles),
        in_specs=[
            pl.BlockSpec((tm, D_MODEL), lambda i, j: (i, 0), pipeline_mode=pl.Buffered(1)),
            pl.BlockSpec((1, D_MODEL), lambda i, j: (0, 0)),
            pl.BlockSpec((KV_OUT, D_MODEL), lambda i, j: (0, 0), pipeline_mode=pl.Buffered(1)),
            pl.BlockSpec((PROJ_TN, D_MODEL), lambda i, j: (j + tile0, 0)),
        ],
        out_specs=[
            pl.BlockSpec((tm, KV_OUT), lambda i, j: (i, 0)),
            pl.BlockSpec((tm, PROJ_TN), lambda i, j: (i, j)),
            pl.BlockSpec((tm, D_MODEL), lambda i, j: (i, 0)),
        ],
        out_shape=[jax.ShapeDtypeStruct((rows, KV_OUT), F32),
                   jax.ShapeDtypeStruct((rows, n_tiles * PROJ_TN), BF16),
                   jax.ShapeDtypeStruct((rows, D_MODEL), BF16)],
        compiler_params=pltpu.CompilerParams(
            dimension_semantics=("parallel", "arbitrary"), vmem_limit_bytes=VMEM_LIMIT),
        name="project",
    )(x, g, wkv, wmain)


def _proj_t_body(wq_ref, wv_ref, xn_ref, o_ref):
    xn = xn_ref[...]
    o_ref[:ATTN_WIDTH, :] = lax.dot_general(wq_ref[...], xn, _NT, preferred_element_type=F32).astype(BF16)
    o_ref[ATTN_WIDTH:, :] = lax.dot_general(wv_ref[...], xn, _NT, preferred_element_type=F32).astype(BF16)


def _project_t(wmain, wkv, xn, tm):
    rows = xn.shape[0]
    assert rows % tm == 0
    return pl.pallas_call(
        _proj_t_body,
        grid=(rows // tm,),
        in_specs=[pl.BlockSpec((ATTN_WIDTH, D_MODEL), lambda i: (COL_QA, 0), pipeline_mode=pl.Buffered(1)),
                  pl.BlockSpec((KV_WIDTH, D_MODEL), lambda i: (1, 0), pipeline_mode=pl.Buffered(1)),
                  pl.BlockSpec((tm, D_MODEL), lambda i: (i, 0))],
        out_specs=pl.BlockSpec((QV_ROWS, tm), lambda i: (0, i)),
        out_shape=jax.ShapeDtypeStruct((QV_ROWS, rows), BF16),
        compiler_params=pltpu.CompilerParams(
            dimension_semantics=("parallel",), vmem_limit_bytes=VMEM_LIMIT),
        name="project_t",
    )(wmain, wkv, xn)


def _attn_pair_setup(p, ka_ref, kb_ref, km_ref, vta_ref, vtb_ref, vtm_ref):
    k = jnp.concatenate([ka_ref[...], kb_ref[...], km_ref[...]], axis=0).astype(BF16)
    vt = jnp.concatenate([vta_ref[...], vtb_ref[...], vtm_ref[...]], axis=1)
    odd = (lax.broadcasted_iota(jnp.int32, (CHUNK, GQA_GROUP * PAIR), 1) % PAIR) >= CHUNK
    first = p == 0
    return k, vt, (odd | first, first, jnp.logical_not(odd))


def _attn_pair_head(setup, h, qt_ref, sink_ref):
    k, vt, (mask_a, mask_b, mask_d) = setup
    kh = k[:, h * HEAD_DIM:(h + 1) * HEAD_DIM]
    rows = [(h * GQA_GROUP + g) * HEAD_DIM for g in range(GQA_GROUP)]
    qth = jnp.concatenate([qt_ref[r:r + HEAD_DIM, :] for r in rows], axis=1)
    st = jnp.dot(kh, qth, preferred_element_type=F32)
    st = jnp.concatenate([
        jnp.where(mask_a, NEG_INF, st[:CHUNK]),
        jnp.where(mask_b, NEG_INF, st[CHUNK:2 * CHUNK]),
        st[2 * CHUNK:3 * CHUNK],
        jnp.where(mask_d, NEG_INF, st[3 * CHUNK:4 * CHUNK]),
        st[4 * CHUNK:]], axis=0)
    sink = sink_ref[h]
    m = jnp.maximum(jnp.max(st, axis=0, keepdims=True), sink)
    e = jnp.exp(st - m).astype(BF16)
    ones = jnp.ones((ONES_ROWS, PAIR_KEYS), BF16)
    vth = jnp.concatenate([vt[h * HEAD_DIM:(h + 1) * HEAD_DIM, :], ones], axis=0)
    ot = jnp.dot(vth, e, preferred_element_type=F32)
    denom = ot[HEAD_DIM:HEAD_DIM + 1, :] + jnp.exp(sink - m)
    ot = ot[:HEAD_DIM, :] * (1.0 / denom)
    return jnp.concatenate([ot[:, g * PAIR:(g + 1) * PAIR] for g in range(GQA_GROUP)], axis=0).T


def _attend_heads(q, k, v, sink_ref, o_ref, lq):
    for h in range(N_KV_HEADS):
        kh = k[:, h * HEAD_DIM:(h + 1) * HEAD_DIM]
        vh = v[:, h * HEAD_DIM:(h + 1) * HEAD_DIM]
        heads = [h * GQA_GROUP + g for g in range(GQA_GROUP)]
        qh = jnp.concatenate([q[:, n * HEAD_DIM:(n + 1) * HEAD_DIM] for n in heads], axis=0)
        s = lax.dot_general(qh, kh, (((1,), (1,)), ((), ())), preferred_element_type=F32)
        sink = jnp.concatenate(
            [jnp.broadcast_to(sink_ref[n:n + 1, 0:1], (lq, 1)) for n in heads], axis=0)
        m = jnp.maximum(jnp.max(s, axis=-1, keepdims=True), sink)
        e = jnp.exp(s - m)
        denom = jnp.sum(e, axis=-1, keepdims=True) + jnp.exp(sink - m)
        oh = jnp.dot(e.astype(BF16), vh, preferred_element_type=F32) * (1.0 / denom)
        o_ref[:, h * GQA_GROUP * HEAD_DIM:(h + 1) * GQA_GROUP * HEAD_DIM] = jnp.concatenate(
            [oh[g * lq:(g + 1) * lq, :] for g in range(GQA_GROUP)], axis=1).astype(o_ref.dtype)


def _attn_sample_body(q_ref, ck_ref, cv_ref, kn_ref, vn_ref, km_ref, vm_ref, sink_ref, o_ref):
    k = jnp.concatenate([ck_ref[...], kn_ref[...], km_ref[...]], axis=0).astype(BF16)
    v = jnp.concatenate([cv_ref[...], vn_ref[...], vm_ref[...]], axis=0).astype(BF16)
    _attend_heads(q_ref[...], k, v, sink_ref, o_ref, q_ref.shape[0])


def _attend_sample(o16, o32, cache_k, cache_v, meta_block, sinks, n_streams, ds):
    cache_rows = cache_k.shape[1]
    new = lambda col: pl.BlockSpec((ds, KV_WIDTH), lambda s: (s, col))
    meta = lambda col: pl.BlockSpec((N_META, KV_WIDTH), lambda s: (meta_block, col))
    cache = pl.BlockSpec((None, cache_rows, KV_WIDTH), lambda s: (s, 0, 0))
    return pl.pallas_call(
        _attn_sample_body,
        grid=(n_streams,),
        in_specs=[
            pl.BlockSpec((ds, ATTN_WIDTH), lambda s: (s, COL_QA)),
            cache, cache, new(0), new(1), meta(0), meta(1),
            pl.BlockSpec((N_Q_HEADS, LANES), lambda s: (0, 0)),
        ],
        out_specs=pl.BlockSpec((ds, ATTN_WIDTH), lambda s: (s, 0)),
        out_shape=jax.ShapeDtypeStruct((n_streams * ds, ATTN_WIDTH), BF16),
        compiler_params=pltpu.CompilerParams(
            dimension_semantics=("parallel",), vmem_limit_bytes=VMEM_LIMIT),
        name="attend_sample",
    )(o16, cache_k, cache_v, o32, o32, o32, o32, sinks)


def _mlstm_body(q_ref, k_ref, v_ref, gt_ref, bias_ref, c0_ref, n0_ref, m0_ref,
                hn_ref, c_ref, n_ref, m_ref, *, blk):
    @pl.when(pl.program_id(1) == 0)
    def _():
        c_ref[...] = c0_ref[...]
        n_ref[...] = n0_ref[...]
        m_ref[...] = m0_ref[...]

    setup = _mlstm_setup(gt_ref, bias_ref, blk)
    for h in range(M_HEADS):
        hv = _mlstm_head(setup, h, q_ref, k_ref, v_ref, c_ref, n_ref, m_ref, blk, None)
        hn_ref[:, h * M_V_DIM:(h + 1) * M_V_DIM] = hv.astype(hn_ref.dtype)


def _mlstm_setup(gt_ref, bias_ref, blk):
    x = gt_ref[...] + bias_ref[...]
    lane = lax.broadcasted_iota(jnp.int32, x.shape, 1)
    log_f = jnp.minimum(x, 0.0) - jnp.log(1.0 + jnp.exp(-jnp.abs(x)))
    gates = jnp.where(lane < M_HEADS, x, log_f)
    row = lax.broadcasted_iota(jnp.int32, (blk, blk), 0)
    col = lax.broadcasted_iota(jnp.int32, (blk, blk), 1)
    causal = row >= col
    csum = jnp.dot(causal.astype(F32), gates, precision=lax.Precision.HIGHEST,
                   preferred_element_type=F32)
    z = jnp.where(lane < M_HEADS, gates, csum)
    pad = (-blk) % LANES
    zsq = z if pad == 0 else jnp.concatenate([z, jnp.zeros((pad, LANES), F32)], axis=0)
    zt = zsq.T[:, :blk]
    return z, zt, causal


def _mlstm_head(setup, h, q_ref, k_ref, v_ref, c_ref, n_ref, m_ref, blk, live):
    z, zt, causal = setup
    keep = (lambda new, old: new) if live is None else (lambda new, old: jnp.where(live, new, old))
    li_c = z[:, h:h + 1]
    b_c = z[:, M_HEADS + h:M_HEADS + h + 1]
    g_r = zt[h:h + 1, :] - zt[M_HEADS + h:M_HEADS + h + 1, :]
    m_prev = m_ref[h:h + 1, 0:1]
    d0 = jnp.where(causal, g_r, -jnp.inf)
    mm = jnp.maximum(m_prev, jnp.max(d0, axis=-1, keepdims=True))
    m_t = b_c + mm
    w = jnp.exp(d0 - mm)
    a = jnp.exp(m_prev - mm)

    qh = q_ref[:, h * M_QK_DIM:(h + 1) * M_QK_DIM]
    kh = k_ref[:, h * M_QK_DIM:(h + 1) * M_QK_DIM]
    vh = v_ref[:, h * M_V_DIM:(h + 1) * M_V_DIM]
    c_h = c_ref[h]
    n_h = n_ref[h:h + 1, :]

    qk = lax.dot_general(qh, kh, _NT, preferred_element_type=F32)
    wqk = w * (qk * MLSTM_K_SCALE)
    num = a * jnp.dot(qh, c_h.astype(BF16), preferred_element_type=F32) + jnp.dot(
        wqk.astype(BF16), vh, preferred_element_type=F32)
    den = a * jnp.sum(qh.astype(F32) * n_h, axis=-1, keepdims=True) + jnp.sum(wqk, axis=-1, keepdims=True)
    hv = num * (1.0 / jnp.maximum(jnp.abs(den), jnp.exp(-m_t)))
    hv = hv * lax.rsqrt(jnp.mean(hv * hv, axis=-1, keepdims=True) + EPS)

    m_new = m_t[blk - 1:blk, :]
    b_last = b_c[blk - 1:blk, :]
    ws = jnp.exp(b_last - b_c + li_c - m_new) * MLSTM_K_SCALE
    a_last = jnp.exp(b_last + m_prev - m_new)
    vs = (vh.astype(F32) * ws).astype(BF16)
    ktv = lax.dot_general(kh, vs, (((0,), (0,)), ((), ())), preferred_element_type=F32)
    c_ref[h] = keep(a_last * c_h + ktv, c_h)
    n_ref[h:h + 1, :] = keep(a_last * n_h + jnp.sum(kh.astype(F32) * ws, axis=0, keepdims=True), n_h)
    m_ref[h:h + 1, :] = keep(jnp.broadcast_to(m_new, (1, LANES)), m_ref[h:h + 1, :])
    return hv


def _mlstm(o16, o32, bias, c0, n0, m0, *, blk, n_streams, n_blocks, first_block, shared_init, col0):
    rb = lambda s, c: first_block + s * n_blocks + c
    st = (lambda s: 0) if shared_init else (lambda s: s)
    qk_blk = 2 * (COL_QKM - col0)
    return pl.pallas_call(
        functools.partial(_mlstm_body, blk=blk),
        grid=(n_streams, n_blocks),
        in_specs=[
            pl.BlockSpec((blk, M_QK_WIDTH), lambda s, c: (rb(s, c), qk_blk)),
            pl.BlockSpec((blk, M_QK_WIDTH), lambda s, c: (rb(s, c), qk_blk + 1)),
            pl.BlockSpec((blk, M_V_WIDTH), lambda s, c: (rb(s, c), COL_VM - col0)),
            pl.BlockSpec((blk, LANES), lambda s, c: (rb(s, c), GATE_BLOCK)),
            pl.BlockSpec((1, LANES), lambda s, c: (0, 0)),
            pl.BlockSpec((None, M_HEADS, M_QK_DIM, M_V_DIM), lambda s, c: (st(s), 0, 0, 0)),
            pl.BlockSpec((None, M_HEADS, M_QK_DIM), lambda s, c: (st(s), 0, 0)),
            pl.BlockSpec((None, M_HEADS, LANES), lambda s, c: (st(s), 0, 0)),
        ],
        out_specs=[
            pl.BlockSpec((blk, M_V_WIDTH), lambda s, c: (s * n_blocks + c, 0)),
            pl.BlockSpec((None, M_HEADS, M_QK_DIM, M_V_DIM), lambda s, c: (s, 0, 0, 0)),
            pl.BlockSpec((None, M_HEADS, M_QK_DIM), lambda s, c: (s, 0, 0)),
            pl.BlockSpec((None, M_HEADS, LANES), lambda s, c: (s, 0, 0)),
        ],
        out_shape=[
            jax.ShapeDtypeStruct((n_streams * n_blocks * blk, M_V_WIDTH), BF16),
            jax.ShapeDtypeStruct((n_streams, M_HEADS, M_QK_DIM, M_V_DIM), F32),
            jax.ShapeDtypeStruct((n_streams, M_HEADS, M_QK_DIM), F32),
            jax.ShapeDtypeStruct((n_streams, M_HEADS, LANES), F32),
        ],
        compiler_params=pltpu.CompilerParams(
            dimension_semantics=("parallel", "arbitrary"), vmem_limit_bytes=VMEM_LIMIT),
        name="mlstm_blk%d" % blk,
    )(o16, o16, o16, o32, bias, c0, n0, m0)


def _merge_body(x_ref, oa_ref, hn_ref, za_ref, om_ref, zm_ref, ga_ref, gm_ref, gmh_ref,
                wpa_ref, wpm_ref, wout_ref, gf_ref, y_ref):
    _merge_compute(x_ref, oa_ref, hn_ref, za_ref, om_ref, zm_ref, ga_ref, gm_ref, gmh_ref,
                   wpa_ref, wpm_ref, wout_ref, gf_ref, y_ref)


def _merge_compute(x_ref, oa_ref, hn_ref, za_ref, om_ref, zm_ref, ga_ref, gm_ref, gmh_ref,
                   wpa_ref, wpm_ref, wout_ref, gf_ref, y_ref):
    a_in = oa_ref[...].astype(F32) * za_ref[...].astype(F32)
    ya = jnp.dot(a_in.astype(BF16), wpa_ref[...], preferred_element_type=F32)
    m_in = (hn_ref[...].astype(F32) * gmh_ref[...]) * om_ref[...].astype(F32) * zm_ref[...].astype(F32)
    ym = jnp.dot(m_in.astype(BF16), wpm_ref[...], preferred_element_type=F32)
    merged = ga_ref[...].astype(F32) * ya + gm_ref[...].astype(F32) * ym
    xo = x_ref[...] + jnp.dot(merged.astype(BF16), wout_ref[...], preferred_element_type=F32)
    ms = jnp.mean(xo * xo, axis=-1, keepdims=True)
    y_ref[...] = (xo * lax.rsqrt(ms + EPS)) * gf_ref[...]


def _merge(x, o_att, hn, o16, g_mh, w_pa, w_pm, w_out, g_final, rows, tm, col0):
    assert rows % tm == 0
    tile = lambda col: pl.BlockSpec((tm, D_MODEL), lambda i: (i, col))
    vec = pl.BlockSpec((1, D_MODEL), lambda i: (0, 0))
    weight = pl.BlockSpec((D_MODEL, D_MODEL), lambda i: (0, 0), pipeline_mode=pl.Buffered(1))
    return pl.pallas_call(
        _merge_body,
        grid=(rows // tm,),
        in_specs=[tile(0), tile(0), tile(0), tile(COL_ZA - col0), tile(COL_OM - col0), tile(COL_ZM - col0),
                  tile(COL_GA - col0), tile(COL_GM - col0), vec, weight, weight, weight, vec],
        out_specs=tile(0),
        out_shape=jax.ShapeDtypeStruct((rows, D_MODEL), F32),
        compiler_params=pltpu.CompilerParams(
            dimension_semantics=("parallel",), vmem_limit_bytes=VMEM_LIMIT),
        name="merge",
    )(x, o_att, hn, o16, o16, o16, o16, o16, g_mh, w_pa, w_pm, w_out, g_final)


def _mixer_body(qt_ref, ka_ref, kb_ref, km_ref, vta_ref, vtb_ref, vtm_ref, sink_ref,
                q_ref, k_ref, v_ref, gt_ref, bias_ref, c0_ref, n0_ref, m0_ref,
                x_ref, za_ref, om_ref, zm_ref, ga_ref, gm_ref, gmh_ref, wpa_ref, wpm_ref, wout_ref, gf_ref,
                y_ref, c_ref, n_ref, m_ref, a_scr, m_scr, *, n_blk):
    c = pl.program_id(0)

    @pl.when(c == 0)
    def _():
        c_ref[...] = c0_ref[...]
        n_ref[...] = n0_ref[...]
        m_ref[...] = m0_ref[...]
        a_scr[...] = jnp.zeros(a_scr.shape, a_scr.dtype)
        m_scr[...] = jnp.zeros(m_scr.shape, m_scr.dtype)

    wr = c % 2
    rd = 1 - wr
    blk = jnp.minimum(c, n_blk - 1)
    live = c < n_blk
    attn = _attn_pair_setup(blk, ka_ref, kb_ref, km_ref, vta_ref, vtb_ref, vtm_ref)
    cell = _mlstm_setup(gt_ref, bias_ref, PAIR)
    a_in = a_scr[rd]
    m_in = m_scr[rd]
    width = GQA_GROUP * HEAD_DIM
    heads_per_piece = width // M_V_DIM
    merged = []
    for i in range(N_KV_HEADS):
        cols = slice(i * width, (i + 1) * width)
        ya = jnp.dot(a_in, wpa_ref[:, cols], preferred_element_type=F32)
        ym = jnp.dot(m_in, wpm_ref[:, cols], preferred_element_type=F32)
        merged.append((ga_ref[:, cols].astype(F32) * ya + gm_ref[:, cols].astype(F32) * ym).astype(BF16))

        for h in range(i * heads_per_piece, (i + 1) * heads_per_piece):
            hc = slice(h * M_V_DIM, (h + 1) * M_V_DIM)
            hv = _mlstm_head(cell, h, q_ref, k_ref, v_ref, c_ref, n_ref, m_ref, PAIR, live)
            m_scr[wr, :, hc] = ((hv * gmh_ref[:, hc]) * om_ref[:, hc].astype(F32)
                                * zm_ref[:, hc].astype(F32)).astype(BF16)
        oa = _attn_pair_head(attn, i, qt_ref, sink_ref)
        a_scr[wr, :, cols] = (oa * za_ref[:, cols].astype(F32)).astype(BF16)

    xo = x_ref[...] + jnp.dot(jnp.concatenate(merged, axis=1), wout_ref[...], preferred_element_type=F32)
    ms = jnp.mean(xo * xo, axis=-1, keepdims=True)
    y_ref[...] = (xo * lax.rsqrt(ms + EPS)) * gf_ref[...]


def _mixer(x, qvt, o32, o16, k_meta, vt_meta, sinks_t, bias, c0, n0, m0, g_mh, w_pa, w_pm, w_out, g_final,
           col0):
    seq = x.shape[0]
    assert seq % PAIR == 0
    n_blk = seq // PAIR
    cur = lambda c: jnp.minimum(c, n_blk - 1)
    prv = lambda c: jnp.maximum(cur(c) - 1, 0)
    mrg = lambda c: jnp.maximum(c - 1, 0)
    v_row = ATTN_WIDTH // KV_WIDTH
    qk_blk = 2 * (COL_QKM - col0)
    tile = lambda col: pl.BlockSpec((PAIR, D_MODEL), lambda c: (mrg(c), col))
    tile_cur = lambda col: pl.BlockSpec((PAIR, D_MODEL), lambda c: (cur(c), col))
    vec = pl.BlockSpec((1, D_MODEL), lambda c: (0, 0))
    weight = pl.BlockSpec((D_MODEL, D_MODEL), lambda c: (0, 0), pipeline_mode=pl.Buffered(1))
    state = lambda *blk: pl.BlockSpec((None,) + blk, lambda c: (0,) * (len(blk) + 1))
    return pl.pallas_call(
        functools.partial(_mixer_body, n_blk=n_blk),
        grid=(n_blk + 1,),
        in_specs=[
            pl.BlockSpec((ATTN_WIDTH, PAIR), lambda c: (0, cur(c))),
            pl.BlockSpec((PAIR, KV_WIDTH), lambda c: (prv(c), 0)),
            pl.BlockSpec((PAIR, KV_WIDTH), lambda c: (cur(c), 0)),
            pl.BlockSpec((N_META, KV_WIDTH), lambda c: (0, 0)),
            pl.BlockSpec((KV_WIDTH, PAIR), lambda c: (v_row, prv(c))),
            pl.BlockSpec((KV_WIDTH, PAIR), lambda c: (v_row, cur(c))),
            pl.BlockSpec((KV_WIDTH, N_META), lambda c: (0, 0)),
            pl.BlockSpec((N_KV_HEADS, 1, GQA_GROUP * PAIR), lambda c: (0, 0, 0)),
            pl.BlockSpec((PAIR, M_QK_WIDTH), lambda c: (cur(c), qk_blk)),
            pl.BlockSpec((PAIR, M_QK_WIDTH), lambda c: (cur(c), qk_blk + 1)),
            pl.BlockSpec((PAIR, M_V_WIDTH), lambda c: (cur(c), COL_VM - col0)),
            pl.BlockSpec((PAIR, LANES), lambda c: (cur(c), GATE_BLOCK)),
            pl.BlockSpec((1, LANES), lambda c: (0, 0)),
            state(M_HEADS, M_QK_DIM, M_V_DIM), state(M_HEADS, M_QK_DIM), state(M_HEADS, LANES),
            tile(0), tile_cur(COL_ZA - col0), tile_cur(COL_OM - col0), tile_cur(COL_ZM - col0),
            tile(COL_GA - col0), tile(COL_GM - col0), vec, weight, weight, weight, vec,
        ],
        out_specs=[
            tile(0),
            state(M_HEADS, M_QK_DIM, M_V_DIM), state(M_HEADS, M_QK_DIM), state(M_HEADS, LANES),
        ],
        out_shape=[
            jax.ShapeDtypeStruct((seq, D_MODEL), F32),
            jax.ShapeDtypeStruct((1, M_HEADS, M_QK_DIM, M_V_DIM), F32),
            jax.ShapeDtypeStruct((1, M_HEADS, M_QK_DIM), F32),
            jax.ShapeDtypeStruct((1, M_HEADS, LANES), F32),
        ],
        scratch_shapes=[pltpu.VMEM((2, PAIR, ATTN_WIDTH), BF16), pltpu.VMEM((2, PAIR, M_V_WIDTH), BF16)],
        compiler_params=pltpu.CompilerParams(
            dimension_semantics=("arbitrary",), vmem_limit_bytes=VMEM_LIMIT),
        name="mixer",
    )(qvt, o32, o32, k_meta, qvt, qvt, vt_meta, sinks_t,
      o16, o16, o16, o32, bias, c0, n0, m0,
      x, o16, o16, o16, o16, o16, g_mh, w_pa, w_pm, w_out, g_final)


def kernel(x_prompt, x_sample, cache_k, cache_v, state_C, state_n, state_m, meta_tokens, g_norm, w_in,
           b_igate, b_fgate, attn_sinks, g_mhnorm, w_pa, w_pm, w_out, g_final):
    batch, seq, _ = x_prompt.shape
    db, ds, _ = x_sample.shape
    depth = w_in.shape[0]
    assert batch == 1 and depth == 1 and ds == N_META
    cache_rows = cache_k.shape[2]

    w_t = w_in[0].T
    wkv = _prep_kv(w_t)
    wmain = _prep_main(w_t)
    g_in = g_norm[0].reshape(1, D_MODEL)
    bias = jnp.concatenate([b_igate[0], b_fgate[0], jnp.zeros((LANES - 2 * M_HEADS,), F32)]).reshape(1, LANES)
    sinks = jnp.broadcast_to(attn_sinks[0][:, None], (N_Q_HEADS, LANES))
    sinks_t = jnp.broadcast_to(attn_sinks[0].reshape(N_KV_HEADS, 1, GQA_GROUP, 1),
                               (N_KV_HEADS, 1, GQA_GROUP, PAIR)).reshape(N_KV_HEADS, 1, GQA_GROUP * PAIR)
    g_mh = g_mhnorm[0].reshape(1, M_V_WIDTH)
    g_fin = g_final.reshape(1, D_MODEL)
    wpa, wpm, wout = w_pa[0].astype(BF16), w_pm[0].astype(BF16), w_out[0].astype(BF16)

    xp = x_prompt.reshape(seq, D_MODEL)
    xs_rows = db * ds
    x_small = jnp.concatenate([x_sample.reshape(xs_rows, D_MODEL), meta_tokens.astype(F32)], axis=0)
    meta_block = xs_rows // N_META
    p_tm = min(seq, 1024)
    p32, p16, xn_p = _project(xp, g_in, wkv, wmain, tm=p_tm, col0=COL_ZA)
    s32, s16, _ = _project(x_small, g_in, wkv, wmain, tm=x_small.shape[0], col0=COL_QA)
    qvt = _project_t(wmain, wkv, xn_p, tm=p_tm)

    zeros = lambda *shape: jnp.zeros(shape, F32)
    _, c_meta, n_meta, m_meta = _mlstm(
        s16, s32, bias, zeros(1, M_HEADS, M_QK_DIM, M_V_DIM), zeros(1, M_HEADS, M_QK_DIM),
        zeros(1, M_HEADS, LANES), blk=N_META, n_streams=1, n_blocks=1, first_block=meta_block,
        shared_init=True, col0=COL_QA)

    k_meta = s32[xs_rows:, :KV_WIDTH]
    vt_meta = s32[xs_rows:, KV_WIDTH:2 * KV_WIDTH].T.astype(BF16)
    y_p, c_p, n_p, m_p = _mixer(xp, qvt, p32, p16, k_meta, vt_meta, sinks_t, bias, c_meta, n_meta, m_meta,
                                g_mh, wpa, wpm, wout, g_fin, col0=COL_ZA)

    ck = cache_k[0].reshape(db, cache_rows, KV_WIDTH)
    cv = cache_v[0].reshape(db, cache_rows, KV_WIDTH)
    oa_s = _attend_sample(s16, s32, ck, cv, meta_block, sinks, db, ds)
    m0_s = jnp.broadcast_to(state_m[0][:, :, None], (db, M_HEADS, LANES))
    hn_s, c_s, n_s, m_s = _mlstm(s16, s32, bias, state_C[0], state_n[0], m0_s, blk=ds, n_streams=db,
                                 n_blocks=1, first_block=0, shared_init=False, col0=COL_QA)
    y_s = _merge(x_small, oa_s, hn_s, s16, g_mh, wpa, wpm, wout, g_fin, rows=xs_rows, tm=xs_rows, col0=COL_QA)

    kv_shape = (1, batch, cache_rows, N_KV_HEADS, HEAD_DIM)
    k_p = p32[seq - cache_rows:, :KV_WIDTH].reshape(kv_shape)
    v_p = p32[seq - cache_rows:, KV_WIDTH:2 * KV_WIDTH].reshape(kv_shape)
    new_k = s32[:xs_rows, :KV_WIDTH].reshape(db, ds, KV_WIDTH)
    new_v = s32[:xs_rows, KV_WIDTH:2 * KV_WIDTH].reshape(db, ds, KV_WIDTH)
    skv_shape = (1, db, cache_rows, N_KV_HEADS, HEAD_DIM)
    k_s = jnp.concatenate([ck, new_k], axis=1)[:, -cache_rows:].reshape(skv_shape)
    v_s = jnp.concatenate([cv, new_v], axis=1)[:, -cache_rows:].reshape(skv_shape)

    return (y_p.reshape(batch, seq, D_MODEL), y_s.reshape(db, ds, D_MODEL),
            k_p, v_p, c_p[None], n_p[None], m_p[None, :, :, 0],
            k_s, v_s, c_s[None], n_s[None], m_s[None, :, :, 0])
```

```python
import functools
import math

import jax
import jax.numpy as jnp
import numpy as np
from jax import lax
from jax.experimental import pallas as pl
from jax.experimental.pallas import tpu as pltpu

F32 = jnp.float32
BF16 = jnp.bfloat16

D_MODEL = 2048
CHUNK = 64
N_META = 16
HEAD_DIM = 64
N_Q_HEADS = 32
N_KV_HEADS = 4
GQA_GROUP = N_Q_HEADS // N_KV_HEADS
WINDOW = 128
WIN_CHUNKS = WINDOW // CHUNK
ATTN_WIDTH = N_Q_HEADS * HEAD_DIM
KV_WIDTH = N_KV_HEADS * HEAD_DIM
M_HEADS = 8
M_QK_DIM = 128
M_V_DIM = 256
M_QK_WIDTH = M_HEADS * M_QK_DIM
M_V_WIDTH = M_HEADS * M_V_DIM
EPS = 1e-6
NEG_INF = -1e30
COL_SIZES = (ATTN_WIDTH, KV_WIDTH, KV_WIDTH, ATTN_WIDTH, M_QK_WIDTH, M_QK_WIDTH, M_V_WIDTH, M_V_WIDTH,
             M_HEADS, M_HEADS, M_V_WIDTH, D_MODEL, D_MODEL)
COL_STARTS = tuple(int(v) for v in np.cumsum((0,) + COL_SIZES[:-1]))
(SRC_QA, SRC_KA, SRC_VA, SRC_ZA, SRC_QM, SRC_KM, SRC_VM, SRC_OM, SRC_IG, SRC_FG, SRC_ZM, SRC_GA,
 SRC_GM) = COL_STARTS

LANES = 128
VMEM_LIMIT = 56 * 1024 * 1024

KV_OUT = 2 * KV_WIDTH + LANES
GATE_BLOCK = 2 * KV_WIDTH // LANES
MAIN_OUT = 8 * D_MODEL
COL_QA, COL_ZA, COL_QKM, COL_VM, COL_OM, COL_ZM, COL_GA, COL_GM = range(8)
PROJ_TN = 512
ROW_ALIGN = 16
MLSTM_K_SCALE = M_QK_DIM ** -0.5
Q_SCALE = 1.0 / math.sqrt(HEAD_DIM)
QV_ROWS = ATTN_WIDTH + KV_WIDTH
PAIR = 2 * CHUNK
PAIR_KEYS = 2 * PAIR + N_META
ONES_ROWS = 16


def _sigmoid(x):
    return 0.5 * jnp.tanh(0.5 * x) + 0.5


def _silu(x):
    return x * _sigmoid(x)


_NT = (((1,), (1,)), ((), ()))


def _main_row_offsets():
    groups = [(SRC_QA, ATTN_WIDTH), (SRC_ZA, ATTN_WIDTH), (SRC_QM, M_QK_WIDTH), (SRC_KM, M_QK_WIDTH),
              (SRC_VM, M_V_WIDTH), (SRC_OM, M_V_WIDTH), (SRC_ZM, M_V_WIDTH), (SRC_GA, D_MODEL),
              (SRC_GM, D_MODEL)]
    offs = []
    for start, width in groups:
        assert start % ROW_ALIGN == 0 and width % PROJ_TN == 0
        offs += [(start + b * PROJ_TN) // ROW_ALIGN for b in range(width // PROJ_TN)]
    assert len(offs) * PROJ_TN == MAIN_OUT
    return np.asarray(offs, np.int32)


def _proj_body(off_ref, x_ref, g_ref, wkv_ref, wg_ref, w_ref, o32_ref, o16_ref, xn_ref, *, tile0):
    del off_ref

    @pl.when(pl.program_id(1) == 0)
    def _():
        x = x_ref[...]
        ms = jnp.mean(x * x, axis=-1, keepdims=True)
        xn_ref[...] = ((x * lax.rsqrt(ms + EPS)) * g_ref[...]).astype(BF16)
        xn = xn_ref[...]
        o32_ref[:, :2 * KV_WIDTH] = lax.dot_general(xn, wkv_ref[...].astype(BF16), _NT,
                                                    preferred_element_type=F32)
        wg = jnp.concatenate([wg_ref[...], jnp.zeros((LANES - ROW_ALIGN, D_MODEL), F32)], axis=0)
        o32_ref[:, 2 * KV_WIDTH:] = lax.dot_general(xn, wg.astype(BF16), _NT, preferred_element_type=F32)

    group = (pl.program_id(1) + tile0) // (D_MODEL // PROJ_TN)
    is_sigmoid = (group == COL_OM) | (group == COL_GA) | (group == COL_GM)
    is_silu = (group == COL_ZA) | (group == COL_ZM)
    is_q = group == COL_QA

    def tile(act):
        acc = lax.dot_general(xn_ref[...], w_ref[...].astype(BF16), _NT, preferred_element_type=F32)
        o16_ref[...] = act(acc).astype(BF16)

    pl.when(is_sigmoid)(lambda: tile(_sigmoid))
    pl.when(is_silu)(lambda: tile(_silu))
    pl.when(is_q)(lambda: tile(lambda acc: acc * Q_SCALE))
    pl.when(jnp.logical_not(is_sigmoid | is_silu | is_q))(lambda: tile(lambda acc: acc))


def _project(x, g, w_t, tm, col0):
    rows = x.shape[0]
    assert rows % tm == 0
    assert SRC_KA % (2 * KV_WIDTH) == 0 and SRC_VA == SRC_KA + KV_WIDTH
    assert SRC_IG % ROW_ALIGN == 0 and SRC_FG == SRC_IG + M_HEADS and 2 * M_HEADS == ROW_ALIGN
    tile0 = col0 * (D_MODEL // PROJ_TN)
    n_tiles = MAIN_OUT // PROJ_TN - tile0
    return pl.pallas_call(
        functools.partial(_proj_body, tile0=tile0),
        grid_spec=pltpu.PrefetchScalarGridSpec(
            num_scalar_prefetch=1,
            grid=(rows // tm, n_tiles),
            in_specs=[
                pl.BlockSpec((tm, D_MODEL), lambda i, j, off: (i, 0)),
                pl.BlockSpec((1, D_MODEL), lambda i, j, off: (0, 0)),
                pl.BlockSpec((2 * KV_WIDTH, D_MODEL), lambda i, j, off: (SRC_KA // (2 * KV_WIDTH), 0),
                             pipeline_mode=pl.Buffered(1)),
                pl.BlockSpec((ROW_ALIGN, D_MODEL), lambda i, j, off: (SRC_IG // ROW_ALIGN, 0),
                             pipeline_mode=pl.Buffered(1)),
                pl.BlockSpec((pl.Element(PROJ_TN), pl.Element(D_MODEL)),
                             lambda i, j, off: (off[j + tile0] * ROW_ALIGN, 0)),
            ],
            out_specs=[
                pl.BlockSpec((tm, KV_OUT), lambda i, j, off: (i, 0)),
                pl.BlockSpec((tm, PROJ_TN), lambda i, j, off: (i, j)),
                pl.BlockSpec((tm, D_MODEL), lambda i, j, off: (i, 0)),
            ],
        ),
        out_shape=[jax.ShapeDtypeStruct((rows, KV_OUT), F32),
                   jax.ShapeDtypeStruct((rows, n_tiles * PROJ_TN), BF16),
                   jax.ShapeDtypeStruct((rows, D_MODEL), BF16)],
        compiler_params=pltpu.CompilerParams(
            dimension_semantics=("parallel", "arbitrary"), vmem_limit_bytes=VMEM_LIMIT),
        name="project",
    )(jnp.asarray(_main_row_offsets()), x, g, w_t, w_t, w_t)


def _proj_t_body(wq_ref, wv_ref, xn_ref, o_ref, w_scr):
    @pl.when(pl.program_id(0) == 0)
    def _():
        for r in range(0, ATTN_WIDTH, KV_WIDTH):
            w_scr[r:r + KV_WIDTH, :] = (wq_ref[r:r + KV_WIDTH, :] * Q_SCALE).astype(BF16)
        w_scr[ATTN_WIDTH:, :] = wv_ref[...].astype(BF16)

    o_ref[...] = lax.dot_general(w_scr[...], xn_ref[...], _NT, preferred_element_type=F32).astype(BF16)


def _project_t(w_t, xn, tm):
    rows = xn.shape[0]
    assert rows % tm == 0 and SRC_QA == 0 and SRC_VA % KV_WIDTH == 0
    return pl.pallas_call(
        _proj_t_body,
        grid=(rows // tm,),
        in_specs=[pl.BlockSpec((ATTN_WIDTH, D_MODEL), lambda i: (0, 0), pipeline_mode=pl.Buffered(1)),
                  pl.BlockSpec((KV_WIDTH, D_MODEL), lambda i: (SRC_VA // KV_WIDTH, 0),
                               pipeline_mode=pl.Buffered(1)),
                  pl.BlockSpec((tm, D_MODEL), lambda i: (i, 0))],
        out_specs=pl.BlockSpec((QV_ROWS, tm), lambda i: (0, i)),
        out_shape=jax.ShapeDtypeStruct((QV_ROWS, rows), BF16),
        scratch_shapes=[pltpu.VMEM((QV_ROWS, D_MODEL), BF16)],
        compiler_params=pltpu.CompilerParams(
            dimension_semantics=("arbitrary",), vmem_limit_bytes=VMEM_LIMIT),
        name="project_t",
    )(w_t, w_t, xn)


def _attn_pair_setup(p, ka_ref, kb_ref, km_ref, vta_ref, vtb_ref, vtm_ref):
    k = jnp.concatenate([ka_ref[...], kb_ref[...], km_ref[...]], axis=0).astype(BF16)
    vt = jnp.concatenate([vta_ref[...], vtb_ref[...], vtm_ref[...]], axis=1)
    odd = (lax.broadcasted_iota(jnp.int32, (CHUNK, GQA_GROUP * PAIR), 1) % PAIR) >= CHUNK
    first = p == 0
    return k, vt, (odd | first, first, jnp.logical_not(odd))


def _attn_pair_head(setup, h, qt_ref, sink_ref):
    k, vt, (mask_a, mask_b, mask_d) = setup
    kh = k[:, h * HEAD_DIM:(h + 1) * HEAD_DIM]
    rows = [(h * GQA_GROUP + g) * HEAD_DIM for g in range(GQA_GROUP)]
    qth = jnp.concatenate([qt_ref[r:r + HEAD_DIM, :] for r in rows], axis=1)
    st = jnp.dot(kh, qth, preferred_element_type=F32)
    st = jnp.concatenate([
        jnp.where(mask_a, NEG_INF, st[:CHUNK]),
        jnp.where(mask_b, NEG_INF, st[CHUNK:2 * CHUNK]),
        st[2 * CHUNK:3 * CHUNK],
        jnp.where(mask_d, NEG_INF, st[3 * CHUNK:4 * CHUNK]),
        st[4 * CHUNK:]], axis=0)
    sink = sink_ref[h]
    m = jnp.maximum(jnp.max(st, axis=0, keepdims=True), sink)
    e = jnp.exp(st - m).astype(BF16)
    ones = jnp.ones((ONES_ROWS, PAIR_KEYS), BF16)
    vth = jnp.concatenate([vt[h * HEAD_DIM:(h + 1) * HEAD_DIM, :], ones], axis=0)
    ot = jnp.dot(vth, e, preferred_element_type=F32)
    denom = ot[HEAD_DIM:HEAD_DIM + 1, :] + jnp.exp(sink - m)
    ot = ot[:HEAD_DIM, :] * (1.0 / denom)
    return jnp.concatenate([ot[:, g * PAIR:(g + 1) * PAIR] for g in range(GQA_GROUP)], axis=0).T


def _attend_heads(q, k, v, sink_ref, o_ref, lq):
    for h in range(N_KV_HEADS):
        kh = k[:, h * HEAD_DIM:(h + 1) * HEAD_DIM]
        vh = v[:, h * HEAD_DIM:(h + 1) * HEAD_DIM]
        heads = [h * GQA_GROUP + g for g in range(GQA_GROUP)]
        qh = jnp.concatenate([q[:, n * HEAD_DIM:(n + 1) * HEAD_DIM] for n in heads], axis=0)
        s = lax.dot_general(qh, kh, (((1,), (1,)), ((), ())), preferred_element_type=F32)
        sink = jnp.concatenate(
            [jnp.broadcast_to(sink_ref[n:n + 1, 0:1], (lq, 1)) for n in heads], axis=0)
        m = jnp.maximum(jnp.max(s, axis=-1, keepdims=True), sink)
        e = jnp.exp(s - m)
        denom = jnp.sum(e, axis=-1, keepdims=True) + jnp.exp(sink - m)
        oh = jnp.dot(e.astype(BF16), vh, preferred_element_type=F32) * (1.0 / denom)
        o_ref[:, h * GQA_GROUP * HEAD_DIM:(h + 1) * GQA_GROUP * HEAD_DIM] = jnp.concatenate(
            [oh[g * lq:(g + 1) * lq, :] for g in range(GQA_GROUP)], axis=1).astype(o_ref.dtype)


def _attn_sample_body(q_ref, ck_ref, cv_ref, kn_ref, vn_ref, km_ref, vm_ref, sink_ref, o_ref):
    k = jnp.concatenate([ck_ref[...], kn_ref[...], km_ref[...]], axis=0).astype(BF16)
    v = jnp.concatenate([cv_ref[...], vn_ref[...], vm_ref[...]], axis=0).astype(BF16)
    _attend_heads(q_ref[...], k, v, sink_ref, o_ref, q_ref.shape[0])


def _attend_sample(o16, o32, cache_k, cache_v, meta_block, sinks, n_streams, ds):
    cache_rows = cache_k.shape[1]
    new = lambda col: pl.BlockSpec((ds, KV_WIDTH), lambda s: (s, col))
    meta = lambda col: pl.BlockSpec((N_META, KV_WIDTH), lambda s: (meta_block, col))
    cache = pl.BlockSpec((None, cache_rows, KV_WIDTH), lambda s: (s, 0, 0))
    return pl.pallas_call(
        _attn_sample_body,
        grid=(n_streams,),
        in_specs=[
            pl.BlockSpec((ds, ATTN_WIDTH), lambda s: (s, COL_QA)),
            cache, cache, new(0), new(1), meta(0), meta(1),
            pl.BlockSpec((N_Q_HEADS, LANES), lambda s: (0, 0)),
        ],
        out_specs=pl.BlockSpec((ds, ATTN_WIDTH), lambda s: (s, 0)),
        out_shape=jax.ShapeDtypeStruct((n_streams * ds, ATTN_WIDTH), BF16),
        compiler_params=pltpu.CompilerParams(
            dimension_semantics=("parallel",), vmem_limit_bytes=VMEM_LIMIT),
        name="attend_sample",
    )(o16, cache_k, cache_v, o32, o32, o32, o32, sinks)


def _mlstm_body(q_ref, k_ref, v_ref, gt_ref, bias_ref, c0_ref, n0_ref, m0_ref,
                hn_ref, c_ref, n_ref, m_ref, *, blk):
    @pl.when(pl.program_id(1) == 0)
    def _():
        c_ref[...] = c0_ref[...]
        n_ref[...] = n0_ref[...]
        m_ref[...] = m0_ref[...]

    setup = _mlstm_setup(gt_ref, bias_ref, blk)
    for h in range(M_HEADS):
        hv = _mlstm_head(setup, h, q_ref, k_ref, v_ref, c_ref, n_ref, m_ref, blk, None)
        hn_ref[:, h * M_V_DIM:(h + 1) * M_V_DIM] = hv.astype(hn_ref.dtype)


def _mlstm_setup(gt_ref, bias_ref, blk):
    x = gt_ref[...] + bias_ref[...]
    lane = lax.broadcasted_iota(jnp.int32, x.shape, 1)
    log_f = jnp.minimum(x, 0.0) - jnp.log(1.0 + jnp.exp(-jnp.abs(x)))
    gates = jnp.where(lane < M_HEADS, x, log_f)
    row = lax.broadcasted_iota(jnp.int32, (blk, blk), 0)
    col = lax.broadcasted_iota(jnp.int32, (blk, blk), 1)
    causal = row >= col
    csum = jnp.dot(causal.astype(F32), gates, precision=lax.Precision.HIGHEST,
                   preferred_element_type=F32)
    z = jnp.where(lane < M_HEADS, gates, csum)
    pad = (-blk) % LANES
    zsq = z if pad == 0 else jnp.concatenate([z, jnp.zeros((pad, LANES), F32)], axis=0)
    zt = zsq.T[:, :blk]
    return z, zt, causal


def _mlstm_head(setup, h, q_ref, k_ref, v_ref, c_ref, n_ref, m_ref, blk, live):
    z, zt, causal = setup
    keep = (lambda new, old: new) if live is None else (lambda new, old: jnp.where(live, new, old))
    li_c = z[:, h:h + 1]
    b_c = z[:, M_HEADS + h:M_HEADS + h + 1]
    g_r = zt[h:h + 1, :] - zt[M_HEADS + h:M_HEADS + h + 1, :]
    m_prev = m_ref[h:h + 1, 0:1]
    d0 = jnp.where(causal, g_r, -jnp.inf)
    mm = jnp.maximum(m_prev, jnp.max(d0, axis=-1, keepdims=True))
    m_t = b_c + mm
    w = jnp.exp(d0 - mm)
    a = jnp.exp(m_prev - mm)

    qh = q_ref[:, h * M_QK_DIM:(h + 1) * M_QK_DIM]
    kh = k_ref[:, h * M_QK_DIM:(h + 1) * M_QK_DIM]
    vh = v_ref[:, h * M_V_DIM:(h + 1) * M_V_DIM]
    c_h = c_ref[h]
    n_h = n_ref[h:h + 1, :]

    qk = lax.dot_general(qh, kh, _NT, preferred_element_type=F32)
    wqk = w * (qk * MLSTM_K_SCALE)
    num = a * jnp.dot(qh, c_h.astype(BF16), preferred_element_type=F32) + jnp.dot(
        wqk.astype(BF16), vh, preferred_element_type=F32)
    den = a * jnp.sum(qh.astype(F32) * n_h, axis=-1, keepdims=True) + jnp.sum(wqk, axis=-1, keepdims=True)
    hv = num * (1.0 / jnp.maximum(jnp.abs(den), jnp.exp(-m_t)))
    hv = hv * lax.rsqrt(jnp.mean(hv * hv, axis=-1, keepdims=True) + EPS)

    m_new = m_t[blk - 1:blk, :]
    b_last = b_c[blk - 1:blk, :]
    ws = jnp.exp(b_last - b_c + li_c - m_new) * MLSTM_K_SCALE
    a_last = jnp.exp(b_last + m_prev - m_new)
    vs = (vh.astype(F32) * ws).astype(BF16)
    ktv = lax.dot_general(kh, vs, (((0,), (0,)), ((), ())), preferred_element_type=F32)
    c_ref[h] = keep(a_last * c_h + ktv, c_h)
    n_ref[h:h + 1, :] = keep(a_last * n_h + jnp.sum(kh.astype(F32) * ws, axis=0, keepdims=True), n_h)
    m_ref[h:h + 1, :] = keep(jnp.broadcast_to(m_new, (1, LANES)), m_ref[h:h + 1, :])
    return hv


def _mlstm(o16, o32, bias, c0, n0, m0, *, blk, n_streams, n_blocks, first_block, shared_init, col0):
    rb = lambda s, c: first_block + s * n_blocks + c
    st = (lambda s: 0) if shared_init else (lambda s: s)
    qk_blk = 2 * (COL_QKM - col0)
    return pl.pallas_call(
        functools.partial(_mlstm_body, blk=blk),
        grid=(n_streams, n_blocks),
        in_specs=[
            pl.BlockSpec((blk, M_QK_WIDTH), lambda s, c: (rb(s, c), qk_blk)),
            pl.BlockSpec((blk, M_QK_WIDTH), lambda s, c: (rb(s, c), qk_blk + 1)),
            pl.BlockSpec((blk, M_V_WIDTH), lambda s, c: (rb(s, c), COL_VM - col0)),
            pl.BlockSpec((blk, LANES), lambda s, c: (rb(s, c), GATE_BLOCK)),
            pl.BlockSpec((1, LANES), lambda s, c: (0, 0)),
            pl.BlockSpec((None, M_HEADS, M_QK_DIM, M_V_DIM), lambda s, c: (st(s), 0, 0, 0)),
            pl.BlockSpec((None, M_HEADS, M_QK_DIM), lambda s, c: (st(s), 0, 0)),
            pl.BlockSpec((None, M_HEADS, LANES), lambda s, c: (st(s), 0, 0)),
        ],
        out_specs=[
            pl.BlockSpec((blk, M_V_WIDTH), lambda s, c: (s * n_blocks + c, 0)),
            pl.BlockSpec((None, M_HEADS, M_QK_DIM, M_V_DIM), lambda s, c: (s, 0, 0, 0)),
            pl.BlockSpec((None, M_HEADS, M_QK_DIM), lambda s, c: (s, 0, 0)),
            pl.BlockSpec((None, M_HEADS, LANES), lambda s, c: (s, 0, 0)),
        ],
        out_shape=[
            jax.ShapeDtypeStruct((n_streams * n_blocks * blk, M_V_WIDTH), BF16),
            jax.ShapeDtypeStruct((n_streams, M_HEADS, M_QK_DIM, M_V_DIM), F32),
            jax.ShapeDtypeStruct((n_streams, M_HEADS, M_QK_DIM), F32),
            jax.ShapeDtypeStruct((n_streams, M_HEADS, LANES), F32),
        ],
        compiler_params=pltpu.CompilerParams(
            dimension_semantics=("parallel", "arbitrary"), vmem_limit_bytes=VMEM_LIMIT),
        name="mlstm_blk%d" % blk,
    )(o16, o16, o16, o32, bias, c0, n0, m0)


def _merge_body(x_ref, oa_ref, hn_ref, za_ref, om_ref, zm_ref, ga_ref, gm_ref, gmh_ref,
                wpa_ref, wpm_ref, wout_ref, gf_ref, y_ref):
    _merge_compute(x_ref, oa_ref, hn_ref, za_ref, om_ref, zm_ref, ga_ref, gm_ref, gmh_ref,
                   wpa_ref, wpm_ref, wout_ref, gf_ref, y_ref)


def _merge_compute(x_ref, oa_ref, hn_ref, za_ref, om_ref, zm_ref, ga_ref, gm_ref, gmh_ref,
                   wpa_ref, wpm_ref, wout_ref, gf_ref, y_ref):
    a_in = oa_ref[...].astype(F32) * za_ref[...].astype(F32)
    ya = jnp.dot(a_in.astype(BF16), wpa_ref[...], preferred_element_type=F32)
    m_in = (hn_ref[...].astype(F32) * gmh_ref[...]) * om_ref[...].astype(F32) * zm_ref[...].astype(F32)
    ym = jnp.dot(m_in.astype(BF16), wpm_ref[...], preferred_element_type=F32)
    merged = ga_ref[...].astype(F32) * ya + gm_ref[...].astype(F32) * ym
    xo = x_ref[...] + jnp.dot(merged.astype(BF16), wout_ref[...], preferred_element_type=F32)
    ms = jnp.mean(xo * xo, axis=-1, keepdims=True)
    y_ref[...] = (xo * lax.rsqrt(ms + EPS)) * gf_ref[...]


def _merge(x, o_att, hn, o16, g_mh, w_pa, w_pm, w_out, g_final, rows, tm, col0):
    assert rows % tm == 0
    tile = lambda col: pl.BlockSpec((tm, D_MODEL), lambda i: (i, col))
    vec = pl.BlockSpec((1, D_MODEL), lambda i: (0, 0))
    weight = pl.BlockSpec((D_MODEL, D_MODEL), lambda i: (0, 0), pipeline_mode=pl.Buffered(1))
    return pl.pallas_call(
        _merge_body,
        grid=(rows // tm,),
        in_specs=[tile(0), tile(0), tile(0), tile(COL_ZA - col0), tile(COL_OM - col0), tile(COL_ZM - col0),
                  tile(COL_GA - col0), tile(COL_GM - col0), vec, weight, weight, weight, vec],
        out_specs=tile(0),
        out_shape=jax.ShapeDtypeStruct((rows, D_MODEL), F32),
        compiler_params=pltpu.CompilerParams(
            dimension_semantics=("parallel",), vmem_limit_bytes=VMEM_LIMIT),
        name="merge",
    )(x, o_att, hn, o16, o16, o16, o16, o16, g_mh, w_pa, w_pm, w_out, g_final)


def _mixer_body(qt_ref, ka_ref, kb_ref, km_ref, vta_ref, vtb_ref, vtm_ref, sink_ref,
                q_ref, k_ref, v_ref, gt_ref, bias_ref, c0_ref, n0_ref, m0_ref,
                x_ref, za_ref, om_ref, zm_ref, ga_ref, gm_ref, gmh_ref, wpa_ref, wpm_ref, wout_ref, gf_ref,
                y_ref, c_ref, n_ref, m_ref, a_scr, m_scr, *, n_blk):
    c = pl.program_id(0)

    @pl.when(c == 0)
    def _():
        c_ref[...] = c0_ref[...]
        n_ref[...] = n0_ref[...]
        m_ref[...] = m0_ref[...]
        a_scr[...] = jnp.zeros(a_scr.shape, a_scr.dtype)
        m_scr[...] = jnp.zeros(m_scr.shape, m_scr.dtype)

    wr = c % 2
    rd = 1 - wr
    blk = jnp.minimum(c, n_blk - 1)
    live = c < n_blk
    attn = _attn_pair_setup(blk, ka_ref, kb_ref, km_ref, vta_ref, vtb_ref, vtm_ref)
    cell = _mlstm_setup(gt_ref, bias_ref, PAIR)
    a_in = a_scr[rd]
    m_in = m_scr[rd]
    width = GQA_GROUP * HEAD_DIM
    heads_per_piece = width // M_V_DIM
    merged = []
    for i in range(N_KV_HEADS):
        cols = slice(i * width, (i + 1) * width)
        ya = jnp.dot(a_in, wpa_ref[:, cols], preferred_element_type=F32)
        ym = jnp.dot(m_in, wpm_ref[:, cols], preferred_element_type=F32)
        merged.append((ga_ref[:, cols].astype(F32) * ya + gm_ref[:, cols].astype(F32) * ym).astype(BF16))

        for h in range(i * heads_per_piece, (i + 1) * heads_per_piece):
            hc = slice(h * M_V_DIM, (h + 1) * M_V_DIM)
            hv = _mlstm_head(cell, h, q_ref, k_ref, v_ref, c_ref, n_ref, m_ref, PAIR, live)
            m_scr[wr, :, hc] = ((hv * gmh_ref[:, hc]) * om_ref[:, hc].astype(F32)
                                * zm_ref[:, hc].astype(F32)).astype(BF16)
        oa = _attn_pair_head(attn, i, qt_ref, sink_ref)
        a_scr[wr, :, cols] = (oa * za_ref[:, cols].astype(F32)).astype(BF16)

    xo = x_ref[...] + jnp.dot(jnp.concatenate(merged, axis=1), wout_ref[...], preferred_element_type=F32)
    ms = jnp.mean(xo * xo, axis=-1, keepdims=True)
    y_ref[...] = (xo * lax.rsqrt(ms + EPS)) * gf_ref[...]


def _mixer(x, qvt, o32, o16, k_meta, vt_meta, sinks_t, bias, c0, n0, m0, g_mh, w_pa, w_pm, w_out, g_final,
           col0):
    seq = x.shape[0]
    assert seq % PAIR == 0
    n_blk = seq // PAIR
    cur = lambda c: jnp.minimum(c, n_blk - 1)
    prv = lambda c: jnp.maximum(cur(c) - 1, 0)
    mrg = lambda c: jnp.maximum(c - 1, 0)
    v_row = ATTN_WIDTH // KV_WIDTH
    qk_blk = 2 * (COL_QKM - col0)
    tile = lambda col: pl.BlockSpec((PAIR, D_MODEL), lambda c: (mrg(c), col))
    tile_cur = lambda col: pl.BlockSpec((PAIR, D_MODEL), lambda c: (cur(c), col))
    vec = pl.BlockSpec((1, D_MODEL), lambda c: (0, 0))
    weight = pl.BlockSpec((D_MODEL, D_MODEL), lambda c: (0, 0), pipeline_mode=pl.Buffered(1))
    state = lambda *blk: pl.BlockSpec((None,) + blk, lambda c: (0,) * (len(blk) + 1))
    return pl.pallas_call(
        functools.partial(_mixer_body, n_blk=n_blk),
        grid=(n_blk + 1,),
        in_specs=[
            pl.BlockSpec((ATTN_WIDTH, PAIR), lambda c: (0, cur(c))),
            pl.BlockSpec((PAIR, KV_WIDTH), lambda c: (prv(c), 0)),
            pl.BlockSpec((PAIR, KV_WIDTH), lambda c: (cur(c), 0)),
            pl.BlockSpec((N_META, KV_WIDTH), lambda c: (0, 0)),
            pl.BlockSpec((KV_WIDTH, PAIR), lambda c: (v_row, prv(c))),
            pl.BlockSpec((KV_WIDTH, PAIR), lambda c: (v_row, cur(c))),
            pl.BlockSpec((KV_WIDTH, N_META), lambda c: (0, 0)),
            pl.BlockSpec((N_KV_HEADS, 1, GQA_GROUP * PAIR), lambda c: (0, 0, 0)),
            pl.BlockSpec((PAIR, M_QK_WIDTH), lambda c: (cur(c), qk_blk)),
            pl.BlockSpec((PAIR, M_QK_WIDTH), lambda c: (cur(c), qk_blk + 1)),
            pl.BlockSpec((PAIR, M_V_WIDTH), lambda c: (cur(c), COL_VM - col0)),
            pl.BlockSpec((PAIR, LANES), lambda c: (cur(c), GATE_BLOCK)),
            pl.BlockSpec((1, LANES), lambda c: (0, 0)),
            state(M_HEADS, M_QK_DIM, M_V_DIM), state(M_HEADS, M_QK_DIM), state(M_HEADS, LANES),
            tile(0), tile_cur(COL_ZA - col0), tile_cur(COL_OM - col0), tile_cur(COL_ZM - col0),
            tile(COL_GA - col0), tile(COL_GM - col0), vec, weight, weight, weight, vec,
        ],
        out_specs=[
            tile(0),
            state(M_HEADS, M_QK_DIM, M_V_DIM), state(M_HEADS, M_QK_DIM), state(M_HEADS, LANES),
        ],
        out_shape=[
            jax.ShapeDtypeStruct((seq, D_MODEL), F32),
            jax.ShapeDtypeStruct((1, M_HEADS, M_QK_DIM, M_V_DIM), F32),
            jax.ShapeDtypeStruct((1, M_HEADS, M_QK_DIM), F32),
            jax.ShapeDtypeStruct((1, M_HEADS, LANES), F32),
        ],
        scratch_shapes=[pltpu.VMEM((2, PAIR, ATTN_WIDTH), BF16), pltpu.VMEM((2, PAIR, M_V_WIDTH), BF16)],
        compiler_params=pltpu.CompilerParams(
            dimension_semantics=("arbitrary",), vmem_limit_bytes=VMEM_LIMIT),
        name="mixer",
    )(qvt, o32, o32, k_meta, qvt, qvt, vt_meta, sinks_t,
      o16, o16, o16, o32, bias, c0, n0, m0,
      x, o16, o16, o16, o16, o16, g_mh, w_pa, w_pm, w_out, g_final)


def kernel(x_prompt, x_sample, cache_k, cache_v, state_C, state_n, state_m, meta_tokens, g_norm, w_in,
           b_igate, b_fgate, attn_sinks, g_mhnorm, w_pa, w_pm, w_out, g_final):
    batch, seq, _ = x_prompt.shape
    db, ds, _ = x_sample.shape
    depth = w_in.shape[0]
    assert batch == 1 and depth == 1 and ds == N_META
    cache_rows = cache_k.shape[2]

    w_t = w_in[0].T
    g_in = g_norm[0].reshape(1, D_MODEL)
    bias = jnp.concatenate([b_igate[0], b_fgate[0], jnp.zeros((LANES - 2 * M_HEADS,), F32)]).reshape(1, LANES)
    sinks = jnp.broadcast_to(attn_sinks[0][:, None], (N_Q_HEADS, LANES))
    sinks_t = jnp.broadcast_to(attn_sinks[0].reshape(N_KV_HEADS, 1, GQA_GROUP, 1),
                               (N_KV_HEADS, 1, GQA_GROUP, PAIR)).reshape(N_KV_HEADS, 1, GQA_GROUP * PAIR)
    g_mh = g_mhnorm[0].reshape(1, M_V_WIDTH)
    g_fin = g_final.reshape(1, D_MODEL)
    wpa, wpm, wout = w_pa[0].astype(BF16), w_pm[0].astype(BF16), w_out[0].astype(BF16)

    xp = x_prompt.reshape(seq, D_MODEL)
    xs_rows = db * ds
    x_small = jnp.concatenate([x_sample.reshape(xs_rows, D_MODEL), meta_tokens.astype(F32)], axis=0)
    meta_block = xs_rows // N_META
    p_tm = min(seq, 1024)
    p32, p16, xn_p = _project(xp, g_in, w_t, tm=p_tm, col0=COL_ZA)
    s32, s16, _ = _project(x_small, g_in, w_t, tm=x_small.shape[0], col0=COL_QA)
    qvt = _project_t(w_t, xn_p, tm=min(seq, 512))

    zeros = lambda *shape: jnp.zeros(shape, F32)
    _, c_meta, n_meta, m_meta = _mlstm(
        s16, s32, bias, zeros(1, M_HEADS, M_QK_DIM, M_V_DIM), zeros(1, M_HEADS, M_QK_DIM),
        zeros(1, M_HEADS, LANES), blk=N_META, n_streams=1, n_blocks=1, first_block=meta_block,
        shared_init=True, col0=COL_QA)

    k_meta = s32[xs_rows:, :KV_WIDTH]
    vt_meta = s32[xs_rows:, KV_WIDTH:2 * KV_WIDTH].T.astype(BF16)
    y_p, c_p, n_p, m_p = _mixer(xp, qvt, p32, p16, k_meta, vt_meta, sinks_t, bias, c_meta, n_meta, m_meta,
                                g_mh, wpa, wpm, wout, g_fin, col0=COL_ZA)

    ck = cache_k[0].reshape(db, cache_rows, KV_WIDTH)
    cv = cache_v[0].reshape(db, cache_rows, KV_WIDTH)
    oa_s = _attend_sample(s16, s32, ck, cv, meta_block, sinks, db, ds)
    m0_s = jnp.broadcast_to(state_m[0][:, :, None], (db, M_HEADS, LANES))
    hn_s, c_s, n_s, m_s = _mlstm(s16, s32, bias, state_C[0], state_n[0], m0_s, blk=ds, n_streams=db,
                                 n_blocks=1, first_block=0, shared_init=False, col0=COL_QA)
    y_s = _merge(x_small, oa_s, hn_s, s16, g_mh, wpa, wpm, wout, g_fin, rows=xs_rows, tm=xs_rows, col0=COL_QA)

    kv_shape = (1, batch, cache_rows, N_KV_HEADS, HEAD_DIM)
    k_p = p32[seq - cache_rows:, :KV_WIDTH].reshape(kv_shape)
    v_p = p32[seq - cache_rows:, KV_WIDTH:2 * KV_WIDTH].reshape(kv_shape)
    new_k = s32[:xs_rows, :KV_WIDTH].reshape(db, ds, KV_WIDTH)
    new_v = s32[:xs_rows, KV_WIDTH:2 * KV_WIDTH].reshape(db, ds, KV_WIDTH)
    skv_shape = (1, db, cache_rows, N_KV_HEADS, HEAD_DIM)
    k_s = jnp.concatenate([ck, new_k], axis=1)[:, -cache_rows:].reshape(skv_shape)
    v_s = jnp.concatenate([cv, new_v], axis=1)[:, -cache_rows:].reshape(skv_shape)

    return (y_p.reshape(batch, seq, D_MODEL), y_s.reshape(db, ds, D_MODEL),
            k_p, v_p, c_p[None], n_p[None], m_p[None, :, :, 0],
            k_s, v_s, c_s[None], n_s[None], m_s[None, :, :, 0])
```

```python
import functools
import math

import jax
import jax.numpy as jnp
import numpy as np
from jax import lax
from jax.experimental import pallas as pl
from jax.experimental.pallas import tpu as pltpu

F32 = jnp.float32
BF16 = jnp.bfloat16

D_MODEL = 2048
CHUNK = 64
N_META = 16
HEAD_DIM = 64
N_Q_HEADS = 32
N_KV_HEADS = 4
GQA_GROUP = N_Q_HEADS // N_KV_HEADS
WINDOW = 128
WIN_CHUNKS = WINDOW // CHUNK
ATTN_WIDTH = N_Q_HEADS * HEAD_DIM
KV_WIDTH = N_KV_HEADS * HEAD_DIM
M_HEADS = 8
M_QK_DIM = 128
M_V_DIM = 256
M_QK_WIDTH = M_HEADS * M_QK_DIM
M_V_WIDTH = M_HEADS * M_V_DIM
EPS = 1e-6
NEG_INF = -1e30
COL_SIZES = (ATTN_WIDTH, KV_WIDTH, KV_WIDTH, ATTN_WIDTH, M_QK_WIDTH, M_QK_WIDTH, M_V_WIDTH, M_V_WIDTH,
             M_HEADS, M_HEADS, M_V_WIDTH, D_MODEL, D_MODEL)
COL_STARTS = tuple(int(v) for v in np.cumsum((0,) + COL_SIZES[:-1]))
(SRC_QA, SRC_KA, SRC_VA, SRC_ZA, SRC_QM, SRC_KM, SRC_VM, SRC_OM, SRC_IG, SRC_FG, SRC_ZM, SRC_GA,
 SRC_GM) = COL_STARTS

LANES = 128
VMEM_LIMIT = 56 * 1024 * 1024

KV_OUT = 2 * KV_WIDTH + LANES
GATE_BLOCK = 2 * KV_WIDTH // LANES
MAIN_OUT = 8 * D_MODEL
COL_QA, COL_ZA, COL_QKM, COL_VM, COL_OM, COL_ZM, COL_GA, COL_GM = range(8)
PROJ_TN = 1024
ROW_ALIGN = 16
MLSTM_K_SCALE = M_QK_DIM ** -0.5
Q_SCALE = 1.0 / math.sqrt(HEAD_DIM)
QV_ROWS = ATTN_WIDTH + KV_WIDTH
PAIR = 2 * CHUNK
PAIR_KEYS = 2 * PAIR + N_META
ONES_ROWS = 16


def _sigmoid(x):
    return 0.5 * jnp.tanh(0.5 * x) + 0.5


def _silu(x):
    return x * _sigmoid(x)


_NT = (((1,), (1,)), ((), ()))


def _main_row_offsets():
    groups = [(SRC_QA, ATTN_WIDTH), (SRC_ZA, ATTN_WIDTH), (SRC_QM, M_QK_WIDTH), (SRC_KM, M_QK_WIDTH),
              (SRC_VM, M_V_WIDTH), (SRC_OM, M_V_WIDTH), (SRC_ZM, M_V_WIDTH), (SRC_GA, D_MODEL),
              (SRC_GM, D_MODEL)]
    offs = []
    for start, width in groups:
        assert start % ROW_ALIGN == 0 and width % PROJ_TN == 0
        offs += [(start + b * PROJ_TN) // ROW_ALIGN for b in range(width // PROJ_TN)]
    assert len(offs) * PROJ_TN == MAIN_OUT
    return np.asarray(offs, np.int32)


def _proj_body(off_ref, x_ref, g_ref, wkv_ref, wg_ref, w_ref, o32_ref, o16_ref, xn_ref, *, tile0):
    del off_ref

    @pl.when(pl.program_id(1) == 0)
    def _():
        x = x_ref[...]
        ms = jnp.mean(x * x, axis=-1, keepdims=True)
        xn_ref[...] = ((x * lax.rsqrt(ms + EPS)) * g_ref[...]).astype(BF16)
        xn = xn_ref[...]
        o32_ref[:, :2 * KV_WIDTH] = lax.dot_general(xn, wkv_ref[...], _NT, preferred_element_type=F32)
        wg = jnp.concatenate([wg_ref[...], jnp.zeros((LANES - ROW_ALIGN, D_MODEL), BF16)], axis=0)
        o32_ref[:, 2 * KV_WIDTH:] = lax.dot_general(xn, wg, _NT, preferred_element_type=F32)

    group = (pl.program_id(1) + tile0) // (D_MODEL // PROJ_TN)
    is_sigmoid = (group == COL_OM) | (group == COL_GA) | (group == COL_GM)
    is_silu = (group == COL_ZA) | (group == COL_ZM)
    is_q = group == COL_QA

    def tile(act):
        acc = lax.dot_general(xn_ref[...], w_ref[...], _NT, preferred_element_type=F32)
        o16_ref[...] = act(acc).astype(BF16)

    pl.when(is_sigmoid)(lambda: tile(_sigmoid))
    pl.when(is_silu)(lambda: tile(_silu))
    pl.when(is_q)(lambda: tile(lambda acc: acc * Q_SCALE))
    pl.when(jnp.logical_not(is_sigmoid | is_silu | is_q))(lambda: tile(lambda acc: acc))


def _project(x, g, w_t, tm, col0):
    rows = x.shape[0]
    assert rows % tm == 0
    assert SRC_KA % (2 * KV_WIDTH) == 0 and SRC_VA == SRC_KA + KV_WIDTH
    assert SRC_IG % ROW_ALIGN == 0 and SRC_FG == SRC_IG + M_HEADS and 2 * M_HEADS == ROW_ALIGN
    tile0 = col0 * (D_MODEL // PROJ_TN)
    n_tiles = MAIN_OUT // PROJ_TN - tile0
    return pl.pallas_call(
        functools.partial(_proj_body, tile0=tile0),
        grid_spec=pltpu.PrefetchScalarGridSpec(
            num_scalar_prefetch=1,
            grid=(rows // tm, n_tiles),
            in_specs=[
                pl.BlockSpec((tm, D_MODEL), lambda i, j, off: (i, 0)),
                pl.BlockSpec((1, D_MODEL), lambda i, j, off: (0, 0)),
                pl.BlockSpec((2 * KV_WIDTH, D_MODEL), lambda i, j, off: (SRC_KA // (2 * KV_WIDTH), 0),
                             pipeline_mode=pl.Buffered(1)),
                pl.BlockSpec((ROW_ALIGN, D_MODEL), lambda i, j, off: (SRC_IG // ROW_ALIGN, 0),
                             pipeline_mode=pl.Buffered(1)),
                pl.BlockSpec((pl.Element(PROJ_TN), pl.Element(D_MODEL)),
                             lambda i, j, off: (off[j + tile0] * ROW_ALIGN, 0)),
            ],
            out_specs=[
                pl.BlockSpec((tm, KV_OUT), lambda i, j, off: (i, 0)),
                pl.BlockSpec((tm, PROJ_TN), lambda i, j, off: (i, j)),
                pl.BlockSpec((tm, D_MODEL), lambda i, j, off: (i, 0)),
            ],
        ),
        out_shape=[jax.ShapeDtypeStruct((rows, KV_OUT), F32),
                   jax.ShapeDtypeStruct((rows, n_tiles * PROJ_TN), BF16),
                   jax.ShapeDtypeStruct((rows, D_MODEL), BF16)],
        compiler_params=pltpu.CompilerParams(
            dimension_semantics=("parallel", "arbitrary"), vmem_limit_bytes=VMEM_LIMIT),
        name="project",
    )(jnp.asarray(_main_row_offsets()), x, g, w_t, w_t, w_t)


def _proj_t_body(wq_ref, wv_ref, xn_ref, o_ref):
    xn = xn_ref[...]
    q_t = lax.dot_general(wq_ref[...], xn, _NT, preferred_element_type=F32) * Q_SCALE
    o_ref[:ATTN_WIDTH, :] = q_t.astype(BF16)
    o_ref[ATTN_WIDTH:, :] = lax.dot_general(wv_ref[...], xn, _NT, preferred_element_type=F32).astype(BF16)


def _project_t(w_t, xn, tm):
    rows = xn.shape[0]
    assert rows % tm == 0 and SRC_QA == 0 and SRC_VA % KV_WIDTH == 0
    return pl.pallas_call(
        _proj_t_body,
        grid=(rows // tm,),
        in_specs=[pl.BlockSpec((ATTN_WIDTH, D_MODEL), lambda i: (0, 0), pipeline_mode=pl.Buffered(1)),
                  pl.BlockSpec((KV_WIDTH, D_MODEL), lambda i: (SRC_VA // KV_WIDTH, 0),
                               pipeline_mode=pl.Buffered(1)),
                  pl.BlockSpec((tm, D_MODEL), lambda i: (i, 0))],
        out_specs=pl.BlockSpec((QV_ROWS, tm), lambda i: (0, i)),
        out_shape=jax.ShapeDtypeStruct((QV_ROWS, rows), BF16),
        compiler_params=pltpu.CompilerParams(
            dimension_semantics=("parallel",), vmem_limit_bytes=VMEM_LIMIT),
        name="project_t",
    )(w_t, w_t, xn)


def _attn_pair_setup(p, ka_ref, kb_ref, km_ref, vta_ref, vtb_ref, vtm_ref):
    k = jnp.concatenate([ka_ref[...], kb_ref[...], km_ref[...]], axis=0).astype(BF16)
    vt = jnp.concatenate([vta_ref[...], vtb_ref[...], vtm_ref[...]], axis=1)
    odd = (lax.broadcasted_iota(jnp.int32, (CHUNK, GQA_GROUP * PAIR), 1) % PAIR) >= CHUNK
    first = p == 0
    return k, vt, (odd | first, first, jnp.logical_not(odd))


def _attn_pair_head(setup, h, qt_ref, sink_ref):
    k, vt, (mask_a, mask_b, mask_d) = setup
    kh = k[:, h * HEAD_DIM:(h + 1) * HEAD_DIM]
    rows = [(h * GQA_GROUP + g) * HEAD_DIM for g in range(GQA_GROUP)]
    qth = jnp.concatenate([qt_ref[r:r + HEAD_DIM, :] for r in rows], axis=1)
    st = jnp.dot(kh, qth, preferred_element_type=F32)
    st = jnp.concatenate([
        jnp.where(mask_a, NEG_INF, st[:CHUNK]),
        jnp.where(mask_b, NEG_INF, st[CHUNK:2 * CHUNK]),
        st[2 * CHUNK:3 * CHUNK],
        jnp.where(mask_d, NEG_INF, st[3 * CHUNK:4 * CHUNK]),
        st[4 * CHUNK:]], axis=0)
    sink = sink_ref[h]
    m = jnp.maximum(jnp.max(st, axis=0, keepdims=True), sink)
    e = jnp.exp(st - m).astype(BF16)
    ones = jnp.ones((ONES_ROWS, PAIR_KEYS), BF16)
    vth = jnp.concatenate([vt[h * HEAD_DIM:(h + 1) * HEAD_DIM, :], ones], axis=0)
    ot = jnp.dot(vth, e, preferred_element_type=F32)
    denom = ot[HEAD_DIM:HEAD_DIM + 1, :] + jnp.exp(sink - m)
    ot = ot[:HEAD_DIM, :] * (1.0 / denom)
    return jnp.concatenate([ot[:, g * PAIR:(g + 1) * PAIR] for g in range(GQA_GROUP)], axis=0).T


def _attend_heads(q, k, v, sink_ref, o_ref, lq):
    for h in range(N_KV_HEADS):
        kh = k[:, h * HEAD_DIM:(h + 1) * HEAD_DIM]
        vh = v[:, h * HEAD_DIM:(h + 1) * HEAD_DIM]
        heads = [h * GQA_GROUP + g for g in range(GQA_GROUP)]
        qh = jnp.concatenate([q[:, n * HEAD_DIM:(n + 1) * HEAD_DIM] for n in heads], axis=0)
        s = lax.dot_general(qh, kh, (((1,), (1,)), ((), ())), preferred_element_type=F32)
        sink = jnp.concatenate(
            [jnp.broadcast_to(sink_ref[n:n + 1, 0:1], (lq, 1)) for n in heads], axis=0)
        m = jnp.maximum(jnp.max(s, axis=-1, keepdims=True), sink)
        e = jnp.exp(s - m)
        denom = jnp.sum(e, axis=-1, keepdims=True) + jnp.exp(sink - m)
        oh = jnp.dot(e.astype(BF16), vh, preferred_element_type=F32) * (1.0 / denom)
        o_ref[:, h * GQA_GROUP * HEAD_DIM:(h + 1) * GQA_GROUP * HEAD_DIM] = jnp.concatenate(
            [oh[g * lq:(g + 1) * lq, :] for g in range(GQA_GROUP)], axis=1).astype(o_ref.dtype)


def _attn_sample_body(q_ref, ck_ref, cv_ref, kn_ref, vn_ref, km_ref, vm_ref, sink_ref, o_ref):
    k = jnp.concatenate([ck_ref[...], kn_ref[...], km_ref[...]], axis=0).astype(BF16)
    v = jnp.concatenate([cv_ref[...], vn_ref[...], vm_ref[...]], axis=0).astype(BF16)
    _attend_heads(q_ref[...], k, v, sink_ref, o_ref, q_ref.shape[0])


def _attend_sample(o16, o32, cache_k, cache_v, meta_block, sinks, n_streams, ds):
    cache_rows = cache_k.shape[1]
    new = lambda col: pl.BlockSpec((ds, KV_WIDTH), lambda s: (s, col))
    meta = lambda col: pl.BlockSpec((N_META, KV_WIDTH), lambda s: (meta_block, col))
    cache = pl.BlockSpec((None, cache_rows, KV_WIDTH), lambda s: (s, 0, 0))
    return pl.pallas_call(
        _attn_sample_body,
        grid=(n_streams,),
        in_specs=[
            pl.BlockSpec((ds, ATTN_WIDTH), lambda s: (s, COL_QA)),
            cache, cache, new(0), new(1), meta(0), meta(1),
            pl.BlockSpec((N_Q_HEADS, LANES), lambda s: (0, 0)),
        ],
        out_specs=pl.BlockSpec((ds, ATTN_WIDTH), lambda s: (s, 0)),
        out_shape=jax.ShapeDtypeStruct((n_streams * ds, ATTN_WIDTH), BF16),
        compiler_params=pltpu.CompilerParams(
            dimension_semantics=("parallel",), vmem_limit_bytes=VMEM_LIMIT),
        name="attend_sample",
    )(o16, cache_k, cache_v, o32, o32, o32, o32, sinks)


def _mlstm_body(q_ref, k_ref, v_ref, gt_ref, bias_ref, c0_ref, n0_ref, m0_ref,
                hn_ref, c_ref, n_ref, m_ref, *, blk):
    @pl.when(pl.program_id(1) == 0)
    def _():
        c_ref[...] = c0_ref[...]
        n_ref[...] = n0_ref[...]
        m_ref[...] = m0_ref[...]

    setup = _mlstm_setup(gt_ref, bias_ref, blk)
    for h in range(M_HEADS):
        hv = _mlstm_head(setup, h, q_ref, k_ref, v_ref, c_ref, n_ref, m_ref, blk, None)
        hn_ref[:, h * M_V_DIM:(h + 1) * M_V_DIM] = hv.astype(hn_ref.dtype)


def _mlstm_setup(gt_ref, bias_ref, blk):
    x = gt_ref[...] + bias_ref[...]
    lane = lax.broadcasted_iota(jnp.int32, x.shape, 1)
    log_f = jnp.minimum(x, 0.0) - jnp.log(1.0 + jnp.exp(-jnp.abs(x)))
    gates = jnp.where(lane < M_HEADS, x, log_f)
    row = lax.broadcasted_iota(jnp.int32, (blk, blk), 0)
    col = lax.broadcasted_iota(jnp.int32, (blk, blk), 1)
    causal = row >= col
    csum = jnp.dot(causal.astype(F32), gates, precision=lax.Precision.HIGHEST,
                   preferred_element_type=F32)
    z = jnp.where(lane < M_HEADS, gates, csum)
    pad = (-blk) % LANES
    zsq = z if pad == 0 else jnp.concatenate([z, jnp.zeros((pad, LANES), F32)], axis=0)
    zt = zsq.T[:, :blk]
    return z, zt, causal


def _mlstm_head(setup, h, q_ref, k_ref, v_ref, c_ref, n_ref, m_ref, blk, live):
    z, zt, causal = setup
    keep = (lambda new, old: new) if live is None else (lambda new, old: jnp.where(live, new, old))
    li_c = z[:, h:h + 1]
    b_c = z[:, M_HEADS + h:M_HEADS + h + 1]
    g_r = zt[h:h + 1, :] - zt[M_HEADS + h:M_HEADS + h + 1, :]
    m_prev = m_ref[h:h + 1, 0:1]
    d0 = jnp.where(causal, g_r, -jnp.inf)
    mm = jnp.maximum(m_prev, jnp.max(d0, axis=-1, keepdims=True))
    m_t = b_c + mm
    w = jnp.exp(d0 - mm)
    a = jnp.exp(m_prev - mm)

    qh = q_ref[:, h * M_QK_DIM:(h + 1) * M_QK_DIM]
    kh = k_ref[:, h * M_QK_DIM:(h + 1) * M_QK_DIM]
    vh = v_ref[:, h * M_V_DIM:(h + 1) * M_V_DIM]
    c_h = c_ref[h]
    n_h = n_ref[h:h + 1, :]

    qk = lax.dot_general(qh, kh, _NT, preferred_element_type=F32)
    wqk = w * (qk * MLSTM_K_SCALE)
    num = a * jnp.dot(qh, c_h.astype(BF16), preferred_element_type=F32) + jnp.dot(
        wqk.astype(BF16), vh, preferred_element_type=F32)
    den = a * jnp.sum(qh.astype(F32) * n_h, axis=-1, keepdims=True) + jnp.sum(wqk, axis=-1, keepdims=True)
    hv = num * (1.0 / jnp.maximum(jnp.abs(den), jnp.exp(-m_t)))
    hv = hv * lax.rsqrt(jnp.mean(hv * hv, axis=-1, keepdims=True) + EPS)

    m_new = m_t[blk - 1:blk, :]
    b_last = b_c[blk - 1:blk, :]
    ws = jnp.exp(b_last - b_c + li_c - m_new) * MLSTM_K_SCALE
    a_last = jnp.exp(b_last + m_prev - m_new)
    vs = (vh.astype(F32) * ws).astype(BF16)
    ktv = lax.dot_general(kh, vs, (((0,), (0,)), ((), ())), preferred_element_type=F32)
    c_ref[h] = keep(a_last * c_h + ktv, c_h)
    n_ref[h:h + 1, :] = keep(a_last * n_h + jnp.sum(kh.astype(F32) * ws, axis=0, keepdims=True), n_h)
    m_ref[h:h + 1, :] = keep(jnp.broadcast_to(m_new, (1, LANES)), m_ref[h:h + 1, :])
    return hv


def _mlstm(o16, o32, bias, c0, n0, m0, *, blk, n_streams, n_blocks, first_block, shared_init, col0):
    rb = lambda s, c: first_block + s * n_blocks + c
    st = (lambda s: 0) if shared_init else (lambda s: s)
    qk_blk = 2 * (COL_QKM - col0)
    return pl.pallas_call(
        functools.partial(_mlstm_body, blk=blk),
        grid=(n_streams, n_blocks),
        in_specs=[
            pl.BlockSpec((blk, M_QK_WIDTH), lambda s, c: (rb(s, c), qk_blk)),
            pl.BlockSpec((blk, M_QK_WIDTH), lambda s, c: (rb(s, c), qk_blk + 1)),
            pl.BlockSpec((blk, M_V_WIDTH), lambda s, c: (rb(s, c), COL_VM - col0)),
            pl.BlockSpec((blk, LANES), lambda s, c: (rb(s, c), GATE_BLOCK)),
            pl.BlockSpec((1, LANES), lambda s, c: (0, 0)),
            pl.BlockSpec((None, M_HEADS, M_QK_DIM, M_V_DIM), lambda s, c: (st(s), 0, 0, 0)),
            pl.BlockSpec((None, M_HEADS, M_QK_DIM), lambda s, c: (st(s), 0, 0)),
            pl.BlockSpec((None, M_HEADS, LANES), lambda s, c: (st(s), 0, 0)),
        ],
        out_specs=[
            pl.BlockSpec((blk, M_V_WIDTH), lambda s, c: (s * n_blocks + c, 0)),
            pl.BlockSpec((None, M_HEADS, M_QK_DIM, M_V_DIM), lambda s, c: (s, 0, 0, 0)),
            pl.BlockSpec((None, M_HEADS, M_QK_DIM), lambda s, c: (s, 0, 0)),
            pl.BlockSpec((None, M_HEADS, LANES), lambda s, c: (s, 0, 0)),
        ],
        out_shape=[
            jax.ShapeDtypeStruct((n_streams * n_blocks * blk, M_V_WIDTH), BF16),
            jax.ShapeDtypeStruct((n_streams, M_HEADS, M_QK_DIM, M_V_DIM), F32),
            jax.ShapeDtypeStruct((n_streams, M_HEADS, M_QK_DIM), F32),
            jax.ShapeDtypeStruct((n_streams, M_HEADS, LANES), F32),
        ],
        compiler_params=pltpu.CompilerParams(
            dimension_semantics=("parallel", "arbitrary"), vmem_limit_bytes=VMEM_LIMIT),
        name="mlstm_blk%d" % blk,
    )(o16, o16, o16, o32, bias, c0, n0, m0)


def _merge_body(x_ref, oa_ref, hn_ref, za_ref, om_ref, zm_ref, ga_ref, gm_ref, gmh_ref,
                wpa_ref, wpm_ref, wout_ref, gf_ref, y_ref):
    _merge_compute(x_ref, oa_ref, hn_ref, za_ref, om_ref, zm_ref, ga_ref, gm_ref, gmh_ref,
                   wpa_ref, wpm_ref, wout_ref, gf_ref, y_ref)


def _merge_compute(x_ref, oa_ref, hn_ref, za_ref, om_ref, zm_ref, ga_ref, gm_ref, gmh_ref,
                   wpa_ref, wpm_ref, wout_ref, gf_ref, y_ref):
    a_in = oa_ref[...].astype(F32) * za_ref[...].astype(F32)
    ya = jnp.dot(a_in.astype(BF16), wpa_ref[...], preferred_element_type=F32)
    m_in = (hn_ref[...].astype(F32) * gmh_ref[...]) * om_ref[...].astype(F32) * zm_ref[...].astype(F32)
    ym = jnp.dot(m_in.astype(BF16), wpm_ref[...], preferred_element_type=F32)
    merged = ga_ref[...].astype(F32) * ya + gm_ref[...].astype(F32) * ym
    xo = x_ref[...] + jnp.dot(merged.astype(BF16), wout_ref[...], preferred_element_type=F32)
    ms = jnp.mean(xo * xo, axis=-1, keepdims=True)
    y_ref[...] = (xo * lax.rsqrt(ms + EPS)) * gf_ref[...]


def _merge(x, o_att, hn, o16, g_mh, w_pa, w_pm, w_out, g_final, rows, tm, col0):
    assert rows % tm == 0
    tile = lambda col: pl.BlockSpec((tm, D_MODEL), lambda i: (i, col))
    vec = pl.BlockSpec((1, D_MODEL), lambda i: (0, 0))
    weight = pl.BlockSpec((D_MODEL, D_MODEL), lambda i: (0, 0), pipeline_mode=pl.Buffered(1))
    return pl.pallas_call(
        _merge_body,
        grid=(rows // tm,),
        in_specs=[tile(0), tile(0), tile(0), tile(COL_ZA - col0), tile(COL_OM - col0), tile(COL_ZM - col0),
                  tile(COL_GA - col0), tile(COL_GM - col0), vec, weight, weight, weight, vec],
        out_specs=tile(0),
        out_shape=jax.ShapeDtypeStruct((rows, D_MODEL), F32),
        compiler_params=pltpu.CompilerParams(
            dimension_semantics=("parallel",), vmem_limit_bytes=VMEM_LIMIT),
        name="merge",
    )(x, o_att, hn, o16, o16, o16, o16, o16, g_mh, w_pa, w_pm, w_out, g_final)


def _mixer_body(qt_ref, ka_ref, kb_ref, km_ref, vta_ref, vtb_ref, vtm_ref, sink_ref,
                q_ref, k_ref, v_ref, gt_ref, bias_ref, c0_ref, n0_ref, m0_ref,
                x_ref, za_ref, om_ref, zm_ref, ga_ref, gm_ref, gmh_ref, wpa_ref, wpm_ref, wout_ref, gf_ref,
                y_ref, c_ref, n_ref, m_ref, a_scr, m_scr, *, n_blk):
    c = pl.program_id(0)

    @pl.when(c == 0)
    def _():
        c_ref[...] = c0_ref[...]
        n_ref[...] = n0_ref[...]
        m_ref[...] = m0_ref[...]
        a_scr[...] = jnp.zeros(a_scr.shape, a_scr.dtype)
        m_scr[...] = jnp.zeros(m_scr.shape, m_scr.dtype)

    wr = c % 2
    rd = 1 - wr
    blk = jnp.minimum(c, n_blk - 1)
    live = c < n_blk
    attn = _attn_pair_setup(blk, ka_ref, kb_ref, km_ref, vta_ref, vtb_ref, vtm_ref)
    cell = _mlstm_setup(gt_ref, bias_ref, PAIR)
    a_in = a_scr[rd]
    m_in = m_scr[rd]
    width = GQA_GROUP * HEAD_DIM
    heads_per_piece = width // M_V_DIM
    merged = []
    for i in range(N_KV_HEADS):
        cols = slice(i * width, (i + 1) * width)
        ya = jnp.dot(a_in, wpa_ref[:, cols], preferred_element_type=F32)
        ym = jnp.dot(m_in, wpm_ref[:, cols], preferred_element_type=F32)
        merged.append((ga_ref[:, cols].astype(F32) * ya + gm_ref[:, cols].astype(F32) * ym).astype(BF16))

        for h in range(i * heads_per_piece, (i + 1) * heads_per_piece):
            hc = slice(h * M_V_DIM, (h + 1) * M_V_DIM)
            hv = _mlstm_head(cell, h, q_ref, k_ref, v_ref, c_ref, n_ref, m_ref, PAIR, live)
            m_scr[wr, :, hc] = ((hv * gmh_ref[:, hc]) * om_ref[:, hc].astype(F32)
                                * zm_ref[:, hc].astype(F32)).astype(BF16)
        oa = _attn_pair_head(attn, i, qt_ref, sink_ref)
        a_scr[wr, :, cols] = (oa * za_ref[:, cols].astype(F32)).astype(BF16)

    xo = x_ref[...] + jnp.dot(jnp.concatenate(merged, axis=1), wout_ref[...], preferred_element_type=F32)
    ms = jnp.mean(xo * xo, axis=-1, keepdims=True)
    y_ref[...] = (xo * lax.rsqrt(ms + EPS)) * gf_ref[...]


def _mixer(x, qvt, o32, o16, k_meta, vt_meta, sinks_t, bias, c0, n0, m0, g_mh, w_pa, w_pm, w_out, g_final,
           col0):
    seq = x.shape[0]
    assert seq % PAIR == 0
    n_blk = seq // PAIR
    cur = lambda c: jnp.minimum(c, n_blk - 1)
    prv = lambda c: jnp.maximum(cur(c) - 1, 0)
    mrg = lambda c: jnp.maximum(c - 1, 0)
    v_row = ATTN_WIDTH // KV_WIDTH
    qk_blk = 2 * (COL_QKM - col0)
    tile = lambda col: pl.BlockSpec((PAIR, D_MODEL), lambda c: (mrg(c), col))
    tile_cur = lambda col: pl.BlockSpec((PAIR, D_MODEL), lambda c: (cur(c), col))
    vec = pl.BlockSpec((1, D_MODEL), lambda c: (0, 0))
    weight = pl.BlockSpec((D_MODEL, D_MODEL), lambda c: (0, 0), pipeline_mode=pl.Buffered(1))
    state = lambda *blk: pl.BlockSpec((None,) + blk, lambda c: (0,) * (len(blk) + 1))
    return pl.pallas_call(
        functools.partial(_mixer_body, n_blk=n_blk),
        grid=(n_blk + 1,),
        in_specs=[
            pl.BlockSpec((ATTN_WIDTH, PAIR), lambda c: (0, cur(c))),
            pl.BlockSpec((PAIR, KV_WIDTH), lambda c: (prv(c), 0)),
            pl.BlockSpec((PAIR, KV_WIDTH), lambda c: (cur(c), 0)),
            pl.BlockSpec((N_META, KV_WIDTH), lambda c: (0, 0)),
            pl.BlockSpec((KV_WIDTH, PAIR), lambda c: (v_row, prv(c))),
            pl.BlockSpec((KV_WIDTH, PAIR), lambda c: (v_row, cur(c))),
            pl.BlockSpec((KV_WIDTH, N_META), lambda c: (0, 0)),
            pl.BlockSpec((N_KV_HEADS, 1, GQA_GROUP * PAIR), lambda c: (0, 0, 0)),
            pl.BlockSpec((PAIR, M_QK_WIDTH), lambda c: (cur(c), qk_blk)),
            pl.BlockSpec((PAIR, M_QK_WIDTH), lambda c: (cur(c), qk_blk + 1)),
            pl.BlockSpec((PAIR, M_V_WIDTH), lambda c: (cur(c), COL_VM - col0)),
            pl.BlockSpec((PAIR, LANES), lambda c: (cur(c), GATE_BLOCK)),
            pl.BlockSpec((1, LANES), lambda c: (0, 0)),
            state(M_HEADS, M_QK_DIM, M_V_DIM), state(M_HEADS, M_QK_DIM), state(M_HEADS, LANES),
            tile(0), tile_cur(COL_ZA - col0), tile_cur(COL_OM - col0), tile_cur(COL_ZM - col0),
            tile(COL_GA - col0), tile(COL_GM - col0), vec, weight, weight, weight, vec,
        ],
        out_specs=[
            tile(0),
            state(M_HEADS, M_QK_DIM, M_V_DIM), state(M_HEADS, M_QK_DIM), state(M_HEADS, LANES),
        ],
        out_shape=[
            jax.ShapeDtypeStruct((seq, D_MODEL), F32),
            jax.ShapeDtypeStruct((1, M_HEADS, M_QK_DIM, M_V_DIM), F32),
            jax.ShapeDtypeStruct((1, M_HEADS, M_QK_DIM), F32),
            jax.ShapeDtypeStruct((1, M_HEADS, LANES), F32),
        ],
        scratch_shapes=[pltpu.VMEM((2, PAIR, ATTN_WIDTH), BF16), pltpu.VMEM((2, PAIR, M_V_WIDTH), BF16)],
        compiler_params=pltpu.CompilerParams(
            dimension_semantics=("arbitrary",), vmem_limit_bytes=VMEM_LIMIT),
        name="mixer",
    )(qvt, o32, o32, k_meta, qvt, qvt, vt_meta, sinks_t,
      o16, o16, o16, o32, bias, c0, n0, m0,
      x, o16, o16, o16, o16, o16, g_mh, w_pa, w_pm, w_out, g_final)


def kernel(x_prompt, x_sample, cache_k, cache_v, state_C, state_n, state_m, meta_tokens, g_norm, w_in,
           b_igate, b_fgate, attn_sinks, g_mhnorm, w_pa, w_pm, w_out, g_final):
    batch, seq, _ = x_prompt.shape
    db, ds, _ = x_sample.shape
    depth = w_in.shape[0]
    assert batch == 1 and depth == 1 and ds == N_META
    cache_rows = cache_k.shape[2]

    w_t = w_in[0].T.astype(BF16)
    g_in = g_norm[0].reshape(1, D_MODEL)
    bias = jnp.concatenate([b_igate[0], b_fgate[0], jnp.zeros((LANES - 2 * M_HEADS,), F32)]).reshape(1, LANES)
    sinks = jnp.broadcast_to(attn_sinks[0][:, None], (N_Q_HEADS, LANES))
    sinks_t = jnp.broadcast_to(attn_sinks[0].reshape(N_KV_HEADS, 1, GQA_GROUP, 1),
                               (N_KV_HEADS, 1, GQA_GROUP, PAIR)).reshape(N_KV_HEADS, 1, GQA_GROUP * PAIR)
    g_mh = g_mhnorm[0].reshape(1, M_V_WIDTH)
    g_fin = g_final.reshape(1, D_MODEL)
    wpa, wpm, wout = w_pa[0].astype(BF16), w_pm[0].astype(BF16), w_out[0].astype(BF16)

    xp = x_prompt.reshape(seq, D_MODEL)
    xs_rows = db * ds
    x_small = jnp.concatenate([x_sample.reshape(xs_rows, D_MODEL), meta_tokens.astype(F32)], axis=0)
    meta_block = xs_rows // N_META
    p_tm = min(seq, 1024)
    p32, p16, xn_p = _project(xp, g_in, w_t, tm=p_tm, col0=COL_ZA)
    s32, s16, _ = _project(x_small, g_in, w_t, tm=x_small.shape[0], col0=COL_QA)
    qvt = _project_t(w_t, xn_p, tm=p_tm)

    zeros = lambda *shape: jnp.zeros(shape, F32)
    _, c_meta, n_meta, m_meta = _mlstm(
        s16, s32, bias, zeros(1, M_HEADS, M_QK_DIM, M_V_DIM), zeros(1, M_HEADS, M_QK_DIM),
        zeros(1, M_HEADS, LANES), blk=N_META, n_streams=1, n_blocks=1, first_block=meta_block,
        shared_init=True, col0=COL_QA)

    k_meta = s32[xs_rows:, :KV_WIDTH]
    vt_meta = s32[xs_rows:, KV_WIDTH:2 * KV_WIDTH].T.astype(BF16)
    y_p, c_p, n_p, m_p = _mixer(xp, qvt, p32, p16, k_meta, vt_meta, sinks_t, bias, c_meta, n_meta, m_meta,
                                g_mh, wpa, wpm, wout, g_fin, col0=COL_ZA)

    ck = cache_k[0].reshape(db, cache_rows, KV_WIDTH)
    cv = cache_v[0].reshape(db, cache_rows, KV_WIDTH)
    oa_s = _attend_sample(s16, s32, ck, cv, meta_block, sinks, db, ds)
    m0_s = jnp.broadcast_to(state_m[0][:, :, None], (db, M_HEADS, LANES))
    hn_s, c_s, n_s, m_s = _mlstm(s16, s32, bias, state_C[0], state_n[0], m0_s, blk=ds, n_streams=db,
                                 n_blocks=1, first_block=0, shared_init=False, col0=COL_QA)
    y_s = _merge(x_small, oa_s, hn_s, s16, g_mh, wpa, wpm, wout, g_fin, rows=xs_rows, tm=xs_rows, col0=COL_QA)

    kv_shape = (1, batch, cache_rows, N_KV_HEADS, HEAD_DIM)
    k_p = p32[seq - cache_rows:, :KV_WIDTH].reshape(kv_shape)
    v_p = p32[seq - cache_rows:, KV_WIDTH:2 * KV_WIDTH].reshape(kv_shape)
    new_k = s32[:xs_rows, :KV_WIDTH].reshape(db, ds, KV_WIDTH)
    new_v = s32[:xs_rows, KV_WIDTH:2 * KV_WIDTH].reshape(db, ds, KV_WIDTH)
    skv_shape = (1, db, cache_rows, N_KV_HEADS, HEAD_DIM)
    k_s = jnp.concatenate([ck, new_k], axis=1)[:, -cache_rows:].reshape(skv_shape)
    v_s = jnp.concatenate([cv, new_v], axis=1)[:, -cache_rows:].reshape(skv_shape)

    return (y_p.reshape(batch, seq, D_MODEL), y_s.reshape(db, ds, D_MODEL),
            k_p, v_p, c_p[None], n_p[None], m_p[None, :, :, 0],
            k_s, v_s, c_s[None], n_s[None], m_s[None, :, :, 0])
```

```python
import functools
import math

import jax
import jax.numpy as jnp
import numpy as np
from jax import lax
from jax.experimental import pallas as pl
from jax.experimental.pallas import tpu as pltpu

F32 = jnp.float32
BF16 = jnp.bfloat16

D_MODEL = 2048
CHUNK = 64
N_META = 16
HEAD_DIM = 64
N_Q_HEADS = 32
N_KV_HEADS = 4
GQA_GROUP = N_Q_HEADS // N_KV_HEADS
WINDOW = 128
WIN_CHUNKS = WINDOW // CHUNK
ATTN_WIDTH = N_Q_HEADS * HEAD_DIM
KV_WIDTH = N_KV_HEADS * HEAD_DIM
M_HEADS = 8
M_QK_DIM = 128
M_V_DIM = 256
M_QK_WIDTH = M_HEADS * M_QK_DIM
M_V_WIDTH = M_HEADS * M_V_DIM
EPS = 1e-6
NEG_INF = -1e30
COL_SIZES = (ATTN_WIDTH, KV_WIDTH, KV_WIDTH, ATTN_WIDTH, M_QK_WIDTH, M_QK_WIDTH, M_V_WIDTH, M_V_WIDTH,
             M_HEADS, M_HEADS, M_V_WIDTH, D_MODEL, D_MODEL)
COL_STARTS = tuple(int(v) for v in np.cumsum((0,) + COL_SIZES[:-1]))
(SRC_QA, SRC_KA, SRC_VA, SRC_ZA, SRC_QM, SRC_KM, SRC_VM, SRC_OM, SRC_IG, SRC_FG, SRC_ZM, SRC_GA,
 SRC_GM) = COL_STARTS

LANES = 128
VMEM_LIMIT = 56 * 1024 * 1024

KV_OUT = 2 * KV_WIDTH + LANES
GATE_BLOCK = 2 * KV_WIDTH // LANES
MAIN_OUT = 8 * D_MODEL
COL_QA, COL_ZA, COL_QKM, COL_VM, COL_OM, COL_ZM, COL_GA, COL_GM = range(8)
PROJ_TN = 1024
ROW_ALIGN = 16
MLSTM_K_SCALE = M_QK_DIM ** -0.5
Q_SCALE = 1.0 / math.sqrt(HEAD_DIM)
QV_ROWS = ATTN_WIDTH + KV_WIDTH
PAIR = 2 * CHUNK
PAIR_KEYS = 2 * PAIR + N_META
ONES_ROWS = 16
MIX_SLOTS = 2


def _sigmoid(x):
    return 0.5 * jnp.tanh(0.5 * x) + 0.5


def _silu(x):
    return x * _sigmoid(x)


_NT = (((1,), (1,)), ((), ()))


def _main_row_offsets():
    groups = [(SRC_QA, ATTN_WIDTH), (SRC_ZA, ATTN_WIDTH), (SRC_QM, M_QK_WIDTH), (SRC_KM, M_QK_WIDTH),
              (SRC_VM, M_V_WIDTH), (SRC_OM, M_V_WIDTH), (SRC_ZM, M_V_WIDTH), (SRC_GA, D_MODEL),
              (SRC_GM, D_MODEL)]
    offs = []
    for start, width in groups:
        assert start % ROW_ALIGN == 0 and width % PROJ_TN == 0
        offs += [(start + b * PROJ_TN) // ROW_ALIGN for b in range(width // PROJ_TN)]
    assert len(offs) * PROJ_TN == MAIN_OUT
    return np.asarray(offs, np.int32)


def _proj_body(off_ref, x_ref, g_ref, wkv_ref, wg_ref, w_ref, o32_ref, o16_ref, xn_ref, *, tile0):
    del off_ref

    @pl.when(pl.program_id(1) == 0)
    def _():
        x = x_ref[...]
        ms = jnp.mean(x * x, axis=-1, keepdims=True)
        xn_ref[...] = ((x * lax.rsqrt(ms + EPS)) * g_ref[...]).astype(BF16)
        xn = xn_ref[...]
        o32_ref[:, :2 * KV_WIDTH] = lax.dot_general(xn, wkv_ref[...], _NT, preferred_element_type=F32)
        wg = jnp.concatenate([wg_ref[...], jnp.zeros((LANES - ROW_ALIGN, D_MODEL), BF16)], axis=0)
        o32_ref[:, 2 * KV_WIDTH:] = lax.dot_general(xn, wg, _NT, preferred_element_type=F32)

    group = (pl.program_id(1) + tile0) // (D_MODEL // PROJ_TN)
    is_sigmoid = (group == COL_OM) | (group == COL_GA) | (group == COL_GM)
    is_silu = (group == COL_ZA) | (group == COL_ZM)
    is_q = group == COL_QA

    def tile(act):
        acc = lax.dot_general(xn_ref[...], w_ref[...], _NT, preferred_element_type=F32)
        o16_ref[...] = act(acc).astype(BF16)

    pl.when(is_sigmoid)(lambda: tile(_sigmoid))
    pl.when(is_silu)(lambda: tile(_silu))
    pl.when(is_q)(lambda: tile(lambda acc: acc * Q_SCALE))
    pl.when(jnp.logical_not(is_sigmoid | is_silu | is_q))(lambda: tile(lambda acc: acc))


def _project(x, g, w_t, tm, col0):
    rows = x.shape[0]
    assert rows % tm == 0
    assert SRC_KA % (2 * KV_WIDTH) == 0 and SRC_VA == SRC_KA + KV_WIDTH
    assert SRC_IG % ROW_ALIGN == 0 and SRC_FG == SRC_IG + M_HEADS and 2 * M_HEADS == ROW_ALIGN
    tile0 = col0 * (D_MODEL // PROJ_TN)
    n_tiles = MAIN_OUT // PROJ_TN - tile0
    return pl.pallas_call(
        functools.partial(_proj_body, tile0=tile0),
        grid_spec=pltpu.PrefetchScalarGridSpec(
            num_scalar_prefetch=1,
            grid=(rows // tm, n_tiles),
            in_specs=[
                pl.BlockSpec((tm, D_MODEL), lambda i, j, off: (i, 0)),
                pl.BlockSpec((1, D_MODEL), lambda i, j, off: (0, 0)),
                pl.BlockSpec((2 * KV_WIDTH, D_MODEL), lambda i, j, off: (SRC_KA // (2 * KV_WIDTH), 0),
                             pipeline_mode=pl.Buffered(1)),
                pl.BlockSpec((ROW_ALIGN, D_MODEL), lambda i, j, off: (SRC_IG // ROW_ALIGN, 0),
                             pipeline_mode=pl.Buffered(1)),
                pl.BlockSpec((pl.Element(PROJ_TN), pl.Element(D_MODEL)),
                             lambda i, j, off: (off[j + tile0] * ROW_ALIGN, 0)),
            ],
            out_specs=[
                pl.BlockSpec((tm, KV_OUT), lambda i, j, off: (i, 0)),
                pl.BlockSpec((tm, PROJ_TN), lambda i, j, off: (i, j)),
                pl.BlockSpec((tm, D_MODEL), lambda i, j, off: (i, 0)),
            ],
        ),
        out_shape=[jax.ShapeDtypeStruct((rows, KV_OUT), F32),
                   jax.ShapeDtypeStruct((rows, n_tiles * PROJ_TN), BF16),
                   jax.ShapeDtypeStruct((rows, D_MODEL), BF16)],
        compiler_params=pltpu.CompilerParams(
            dimension_semantics=("parallel", "arbitrary"), vmem_limit_bytes=VMEM_LIMIT),
        name="project",
    )(jnp.asarray(_main_row_offsets()), x, g, w_t, w_t, w_t)


def _proj_t_body(wq_ref, wv_ref, xn_ref, o_ref):
    xn = xn_ref[...]
    q_t = lax.dot_general(wq_ref[...], xn, _NT, preferred_element_type=F32) * Q_SCALE
    o_ref[:ATTN_WIDTH, :] = q_t.astype(BF16)
    o_ref[ATTN_WIDTH:, :] = lax.dot_general(wv_ref[...], xn, _NT, preferred_element_type=F32).astype(BF16)


def _project_t(w_t, xn, tm):
    rows = xn.shape[0]
    assert rows % tm == 0 and SRC_QA == 0 and SRC_VA % KV_WIDTH == 0
    return pl.pallas_call(
        _proj_t_body,
        grid=(rows // tm,),
        in_specs=[pl.BlockSpec((ATTN_WIDTH, D_MODEL), lambda i: (0, 0), pipeline_mode=pl.Buffered(1)),
                  pl.BlockSpec((KV_WIDTH, D_MODEL), lambda i: (SRC_VA // KV_WIDTH, 0),
                               pipeline_mode=pl.Buffered(1)),
                  pl.BlockSpec((tm, D_MODEL), lambda i: (i, 0))],
        out_specs=pl.BlockSpec((QV_ROWS, tm), lambda i: (0, i)),
        out_shape=jax.ShapeDtypeStruct((QV_ROWS, rows), BF16),
        compiler_params=pltpu.CompilerParams(
            dimension_semantics=("parallel",), vmem_limit_bytes=VMEM_LIMIT),
        name="project_t",
    )(w_t, w_t, xn)


def _attn_pair_setup(p, ka_ref, kb_ref, km_ref, vta_ref, vtb_ref, vtm_ref):
    k = jnp.concatenate([ka_ref[...], kb_ref[...], km_ref[...]], axis=0).astype(BF16)
    vt = jnp.concatenate([vta_ref[...], vtb_ref[...], vtm_ref[...]], axis=1)
    odd = (lax.broadcasted_iota(jnp.int32, (CHUNK, GQA_GROUP * PAIR), 1) % PAIR) >= CHUNK
    first = p == 0
    return k, vt, (odd | first, first, jnp.logical_not(odd))


def _attn_pair_probs(setup, h, qt_ref, sink_ref):
    k, vt, (mask_a, mask_b, mask_d) = setup
    kh = k[:, h * HEAD_DIM:(h + 1) * HEAD_DIM]
    rows = [(h * GQA_GROUP + g) * HEAD_DIM for g in range(GQA_GROUP)]
    qth = jnp.concatenate([qt_ref[r:r + HEAD_DIM, :] for r in rows], axis=1)
    st = jnp.dot(kh, qth, preferred_element_type=F32)
    st = jnp.concatenate([
        jnp.where(mask_a, NEG_INF, st[:CHUNK]),
        jnp.where(mask_b, NEG_INF, st[CHUNK:2 * CHUNK]),
        st[2 * CHUNK:3 * CHUNK],
        jnp.where(mask_d, NEG_INF, st[3 * CHUNK:4 * CHUNK]),
        st[4 * CHUNK:]], axis=0)
    sink = sink_ref[h]
    m = jnp.maximum(jnp.max(st, axis=0, keepdims=True), sink)
    return jnp.exp(st - m).astype(BF16), jnp.exp(sink - m)


def _attn_pair_out(setup, h, probs):
    _, vt, _ = setup
    e, e_sink = probs
    ones = jnp.ones((ONES_ROWS, PAIR_KEYS), BF16)
    vth = jnp.concatenate([vt[h * HEAD_DIM:(h + 1) * HEAD_DIM, :], ones], axis=0)
    ot = jnp.dot(vth, e, preferred_element_type=F32)
    denom = ot[HEAD_DIM:HEAD_DIM + 1, :] + e_sink
    ot = ot[:HEAD_DIM, :] * (1.0 / denom)
    return jnp.concatenate([ot[:, g * PAIR:(g + 1) * PAIR] for g in range(GQA_GROUP)], axis=0).T


def _attend_heads(q, k, v, sink_ref, o_ref, lq):
    for h in range(N_KV_HEADS):
        kh = k[:, h * HEAD_DIM:(h + 1) * HEAD_DIM]
        vh = v[:, h * HEAD_DIM:(h + 1) * HEAD_DIM]
        heads = [h * GQA_GROUP + g for g in range(GQA_GROUP)]
        qh = jnp.concatenate([q[:, n * HEAD_DIM:(n + 1) * HEAD_DIM] for n in heads], axis=0)
        s = lax.dot_general(qh, kh, (((1,), (1,)), ((), ())), preferred_element_type=F32)
        sink = jnp.concatenate(
            [jnp.broadcast_to(sink_ref[n:n + 1, 0:1], (lq, 1)) for n in heads], axis=0)
        m = jnp.maximum(jnp.max(s, axis=-1, keepdims=True), sink)
        e = jnp.exp(s - m)
        denom = jnp.sum(e, axis=-1, keepdims=True) + jnp.exp(sink - m)
        oh = jnp.dot(e.astype(BF16), vh, preferred_element_type=F32) * (1.0 / denom)
        o_ref[:, h * GQA_GROUP * HEAD_DIM:(h + 1) * GQA_GROUP * HEAD_DIM] = jnp.concatenate(
            [oh[g * lq:(g + 1) * lq, :] for g in range(GQA_GROUP)], axis=1).astype(o_ref.dtype)


def _attn_sample_body(q_ref, ck_ref, cv_ref, kn_ref, vn_ref, km_ref, vm_ref, sink_ref, o_ref):
    k = jnp.concatenate([ck_ref[...], kn_ref[...], km_ref[...]], axis=0).astype(BF16)
    v = jnp.concatenate([cv_ref[...], vn_ref[...], vm_ref[...]], axis=0).astype(BF16)
    _attend_heads(q_ref[...], k, v, sink_ref, o_ref, q_ref.shape[0])


def _attend_sample(o16, o32, cache_k, cache_v, meta_block, sinks, n_streams, ds):
    cache_rows = cache_k.shape[1]
    new = lambda col: pl.BlockSpec((ds, KV_WIDTH), lambda s: (s, col))
    meta = lambda col: pl.BlockSpec((N_META, KV_WIDTH), lambda s: (meta_block, col))
    cache = pl.BlockSpec((None, cache_rows, KV_WIDTH), lambda s: (s, 0, 0))
    return pl.pallas_call(
        _attn_sample_body,
        grid=(n_streams,),
        in_specs=[
            pl.BlockSpec((ds, ATTN_WIDTH), lambda s: (s, COL_QA)),
            cache, cache, new(0), new(1), meta(0), meta(1),
            pl.BlockSpec((N_Q_HEADS, LANES), lambda s: (0, 0)),
        ],
        out_specs=pl.BlockSpec((ds, ATTN_WIDTH), lambda s: (s, 0)),
        out_shape=jax.ShapeDtypeStruct((n_streams * ds, ATTN_WIDTH), BF16),
        compiler_params=pltpu.CompilerParams(
            dimension_semantics=("parallel",), vmem_limit_bytes=VMEM_LIMIT),
        name="attend_sample",
    )(o16, cache_k, cache_v, o32, o32, o32, o32, sinks)


def _mlstm_body(q_ref, k_ref, v_ref, gt_ref, bias_ref, c0_ref, n0_ref, m0_ref,
                hn_ref, c_ref, n_ref, m_ref, *, blk):
    @pl.when(pl.program_id(1) == 0)
    def _():
        c_ref[...] = c0_ref[...]
        n_ref[...] = n0_ref[...]
        m_ref[...] = m0_ref[...]

    setup = _mlstm_setup(gt_ref, bias_ref, blk)
    for h in range(M_HEADS):
        hv = _mlstm_head(setup, h, q_ref, k_ref, v_ref, c_ref, n_ref, m_ref, blk, None)
        hn_ref[:, h * M_V_DIM:(h + 1) * M_V_DIM] = hv.astype(hn_ref.dtype)


def _mlstm_setup(gt_ref, bias_ref, blk):
    x = gt_ref[...] + bias_ref[...]
    lane = lax.broadcasted_iota(jnp.int32, x.shape, 1)
    log_f = jnp.minimum(x, 0.0) - jnp.log(1.0 + jnp.exp(-jnp.abs(x)))
    gates = jnp.where(lane < M_HEADS, x, log_f)
    row = lax.broadcasted_iota(jnp.int32, (blk, blk), 0)
    col = lax.broadcasted_iota(jnp.int32, (blk, blk), 1)
    causal = row >= col
    csum = jnp.dot(causal.astype(F32), gates, precision=lax.Precision.HIGHEST,
                   preferred_element_type=F32)
    z = jnp.where(lane < M_HEADS, gates, csum)
    pad = (-blk) % LANES
    zsq = z if pad == 0 else jnp.concatenate([z, jnp.zeros((pad, LANES), F32)], axis=0)
    zt = zsq.T[:, :blk]
    return z, zt, causal


def _mlstm_head(setup, h, q_ref, k_ref, v_ref, c_ref, n_ref, m_ref, blk, live):
    scores = _mlstm_scores(setup, h, q_ref, k_ref, m_ref)
    return _mlstm_finish(setup, h, scores, q_ref, k_ref, v_ref, c_ref, n_ref, m_ref, blk, live)


def _mlstm_scores(setup, h, q_ref, k_ref, m_ref):
    z, zt, causal = setup
    g_r = zt[h:h + 1, :] - zt[M_HEADS + h:M_HEADS + h + 1, :]
    m_prev = m_ref[h:h + 1, 0:1]
    d0 = jnp.where(causal, g_r, -jnp.inf)
    mm = jnp.maximum(m_prev, jnp.max(d0, axis=-1, keepdims=True))
    w = jnp.exp(d0 - mm)
    qh = q_ref[:, h * M_QK_DIM:(h + 1) * M_QK_DIM]
    kh = k_ref[:, h * M_QK_DIM:(h + 1) * M_QK_DIM]
    qk = lax.dot_general(qh, kh, _NT, preferred_element_type=F32)
    wqk = w * (qk * MLSTM_K_SCALE)
    return m_prev, mm, wqk.astype(BF16), jnp.sum(wqk, axis=-1, keepdims=True)


def _mlstm_finish(setup, h, scores, q_ref, k_ref, v_ref, c_ref, n_ref, m_ref, blk, live):
    z, _, _ = setup
    m_prev, mm, wqk, den_intra = scores
    keep = (lambda new, old: new) if live is None else (lambda new, old: jnp.where(live, new, old))
    li_c = z[:, h:h + 1]
    b_c = z[:, M_HEADS + h:M_HEADS + h + 1]
    m_t = b_c + mm
    a = jnp.exp(m_prev - mm)

    qh = q_ref[:, h * M_QK_DIM:(h + 1) * M_QK_DIM]
    kh = k_ref[:, h * M_QK_DIM:(h + 1) * M_QK_DIM]
    vh = v_ref[:, h * M_V_DIM:(h + 1) * M_V_DIM]
    c_h = c_ref[h]
    n_h = n_ref[h:h + 1, :]

    num = a * jnp.dot(qh, c_h.astype(BF16), preferred_element_type=F32) + jnp.dot(
        wqk, vh, preferred_element_type=F32)
    den = a * jnp.sum(qh.astype(F32) * n_h, axis=-1, keepdims=True) + den_intra
    hv = num * (1.0 / jnp.maximum(jnp.abs(den), jnp.exp(-m_t)))
    hv = hv * lax.rsqrt(jnp.mean(hv * hv, axis=-1, keepdims=True) + EPS)

    m_new = m_t[blk - 1:blk, :]
    b_last = b_c[blk - 1:blk, :]
    ws = jnp.exp(b_last - b_c + li_c - m_new) * MLSTM_K_SCALE
    a_last = jnp.exp(b_last + m_prev - m_new)
    vs = (vh.astype(F32) * ws).astype(BF16)
    ktv = lax.dot_general(kh, vs, (((0,), (0,)), ((), ())), preferred_element_type=F32)
    c_ref[h] = keep(a_last * c_h + ktv, c_h)
    n_ref[h:h + 1, :] = keep(a_last * n_h + jnp.sum(kh.astype(F32) * ws, axis=0, keepdims=True), n_h)
    m_ref[h:h + 1, :] = keep(jnp.broadcast_to(m_new, (1, LANES)), m_ref[h:h + 1, :])
    return hv


def _mlstm(o16, o32, bias, c0, n0, m0, *, blk, n_streams, n_blocks, first_block, shared_init, col0):
    rb = lambda s, c: first_block + s * n_blocks + c
    st = (lambda s: 0) if shared_init else (lambda s: s)
    qk_blk = 2 * (COL_QKM - col0)
    return pl.pallas_call(
        functools.partial(_mlstm_body, blk=blk),
        grid=(n_streams, n_blocks),
        in_specs=[
            pl.BlockSpec((blk, M_QK_WIDTH), lambda s, c: (rb(s, c), qk_blk)),
            pl.BlockSpec((blk, M_QK_WIDTH), lambda s, c: (rb(s, c), qk_blk + 1)),
            pl.BlockSpec((blk, M_V_WIDTH), lambda s, c: (rb(s, c), COL_VM - col0)),
            pl.BlockSpec((blk, LANES), lambda s, c: (rb(s, c), GATE_BLOCK)),
            pl.BlockSpec((1, LANES), lambda s, c: (0, 0)),
            pl.BlockSpec((None, M_HEADS, M_QK_DIM, M_V_DIM), lambda s, c: (st(s), 0, 0, 0)),
            pl.BlockSpec((None, M_HEADS, M_QK_DIM), lambda s, c: (st(s), 0, 0)),
            pl.BlockSpec((None, M_HEADS, LANES), lambda s, c: (st(s), 0, 0)),
        ],
        out_specs=[
            pl.BlockSpec((blk, M_V_WIDTH), lambda s, c: (s * n_blocks + c, 0)),
            pl.BlockSpec((None, M_HEADS, M_QK_DIM, M_V_DIM), lambda s, c: (s, 0, 0, 0)),
            pl.BlockSpec((None, M_HEADS, M_QK_DIM), lambda s, c: (s, 0, 0)),
            pl.BlockSpec((None, M_HEADS, LANES), lambda s, c: (s, 0, 0)),
        ],
        out_shape=[
            jax.ShapeDtypeStruct((n_streams * n_blocks * blk, M_V_WIDTH), BF16),
            jax.ShapeDtypeStruct((n_streams, M_HEADS, M_QK_DIM, M_V_DIM), F32),
            jax.ShapeDtypeStruct((n_streams, M_HEADS, M_QK_DIM), F32),
            jax.ShapeDtypeStruct((n_streams, M_HEADS, LANES), F32),
        ],
        compiler_params=pltpu.CompilerParams(
            dimension_semantics=("parallel", "arbitrary"), vmem_limit_bytes=VMEM_LIMIT),
        name="mlstm_blk%d" % blk,
    )(o16, o16, o16, o32, bias, c0, n0, m0)


def _merge_body(x_ref, oa_ref, hn_ref, za_ref, om_ref, zm_ref, ga_ref, gm_ref, gmh_ref,
                wpa_ref, wpm_ref, wout_ref, gf_ref, y_ref):
    _merge_compute(x_ref, oa_ref, hn_ref, za_ref, om_ref, zm_ref, ga_ref, gm_ref, gmh_ref,
                   wpa_ref, wpm_ref, wout_ref, gf_ref, y_ref)


def _merge_compute(x_ref, oa_ref, hn_ref, za_ref, om_ref, zm_ref, ga_ref, gm_ref, gmh_ref,
                   wpa_ref, wpm_ref, wout_ref, gf_ref, y_ref):
    a_in = oa_ref[...].astype(F32) * za_ref[...].astype(F32)
    ya = jnp.dot(a_in.astype(BF16), wpa_ref[...], preferred_element_type=F32)
    m_in = (hn_ref[...].astype(F32) * gmh_ref[...]) * om_ref[...].astype(F32) * zm_ref[...].astype(F32)
    ym = jnp.dot(m_in.astype(BF16), wpm_ref[...], preferred_element_type=F32)
    merged = ga_ref[...].astype(F32) * ya + gm_ref[...].astype(F32) * ym
    xo = x_ref[...] + jnp.dot(merged.astype(BF16), wout_ref[...], preferred_element_type=F32)
    ms = jnp.mean(xo * xo, axis=-1, keepdims=True)
    y_ref[...] = (xo * lax.rsqrt(ms + EPS)) * gf_ref[...]


def _merge(x, o_att, hn, o16, g_mh, w_pa, w_pm, w_out, g_final, rows, tm, col0):
    assert rows % tm == 0
    tile = lambda col: pl.BlockSpec((tm, D_MODEL), lambda i: (i, col))
    vec = pl.BlockSpec((1, D_MODEL), lambda i: (0, 0))
    weight = pl.BlockSpec((D_MODEL, D_MODEL), lambda i: (0, 0), pipeline_mode=pl.Buffered(1))
    return pl.pallas_call(
        _merge_body,
        grid=(rows // tm,),
        in_specs=[tile(0), tile(0), tile(0), tile(COL_ZA - col0), tile(COL_OM - col0), tile(COL_ZM - col0),
                  tile(COL_GA - col0), tile(COL_GM - col0), vec, weight, weight, weight, vec],
        out_specs=tile(0),
        out_shape=jax.ShapeDtypeStruct((rows, D_MODEL), F32),
        compiler_params=pltpu.CompilerParams(
            dimension_semantics=("parallel",), vmem_limit_bytes=VMEM_LIMIT),
        name="merge",
    )(x, o_att, hn, o16, o16, o16, o16, o16, g_mh, w_pa, w_pm, w_out, g_final)


def _mixer_body(qt_ref, ka_ref, kb_ref, km_ref, vta_ref, vtb_ref, vtm_ref, sink_ref,
                q_ref, k_ref, v_ref, gt_ref, bias_ref, c0_ref, n0_ref, m0_ref,
                x_ref, za_ref, om_ref, zm_ref, ga_ref, gm_ref, gmh_ref, wpa_ref, wpm_ref, wout_ref, gf_ref,
                y_ref, c_ref, n_ref, m_ref, a_scr, m_scr, *, n_blk):
    c = pl.program_id(0)

    @pl.when(c == 0)
    def _():
        c_ref[...] = c0_ref[...]
        n_ref[...] = n0_ref[...]
        m_ref[...] = m0_ref[...]
        a_scr[...] = jnp.zeros(a_scr.shape, a_scr.dtype)
        m_scr[...] = jnp.zeros(m_scr.shape, m_scr.dtype)

    wr = c % MIX_SLOTS
    rd = 1 - wr
    blk = jnp.minimum(c, n_blk - 1)
    live = c < n_blk
    width = GQA_GROUP * HEAD_DIM
    heads_per_piece = width // M_V_DIM
    a_in = a_scr[rd]
    m_in = m_scr[rd]

    def mlstm_piece(h):
        hc = slice(h * M_V_DIM, (h + 1) * M_V_DIM)
        hv = _mlstm_finish(cell, h, scores[h], q_ref, k_ref, v_ref, c_ref, n_ref, m_ref, PAIR, live)
        m_scr[wr, :, hc] = ((hv * gmh_ref[:, hc]) * om_ref[:, hc].astype(F32)
                            * zm_ref[:, hc].astype(F32)).astype(BF16)

    def attn_piece(h, probs):
        cols = slice(h * width, (h + 1) * width)
        oa = _attn_pair_out(attn, h, probs)
        a_scr[wr, :, cols] = (oa * za_ref[:, cols].astype(F32)).astype(BF16)

    attn = _attn_pair_setup(blk, ka_ref, kb_ref, km_ref, vta_ref, vtb_ref, vtm_ref)
    cell = _mlstm_setup(gt_ref, bias_ref, PAIR)
    scores = [_mlstm_scores(cell, h, q_ref, k_ref, m_ref) for h in range(M_HEADS)]
    probs = _attn_pair_probs(attn, 0, qt_ref, sink_ref)
    merged = []
    for i in range(N_KV_HEADS):
        cols = slice(i * width, (i + 1) * width)
        ya = jnp.dot(a_in, wpa_ref[:, cols], preferred_element_type=F32)
        mlstm_piece(heads_per_piece * i)
        next_probs = _attn_pair_probs(attn, i + 1, qt_ref, sink_ref) if i + 1 < N_KV_HEADS else None
        ym = jnp.dot(m_in, wpm_ref[:, cols], preferred_element_type=F32)
        mlstm_piece(heads_per_piece * i + 1)
        merged.append((ga_ref[:, cols].astype(F32) * ya + gm_ref[:, cols].astype(F32) * ym).astype(BF16))
        if i + 1 == N_KV_HEADS:
            xo = x_ref[...] + jnp.dot(jnp.concatenate(merged, axis=1), wout_ref[...],
                                      preferred_element_type=F32)
        attn_piece(i, probs)
        probs = next_probs
    ms = jnp.mean(xo * xo, axis=-1, keepdims=True)
    y_ref[...] = (xo * lax.rsqrt(ms + EPS)) * gf_ref[...]


def _mixer(x, qvt, o32, o16, k_meta, vt_meta, sinks_t, bias, c0, n0, m0, g_mh, w_pa, w_pm, w_out, g_final,
           col0):
    seq = x.shape[0]
    assert seq % PAIR == 0
    n_blk = seq // PAIR
    cur = lambda c: jnp.minimum(c, n_blk - 1)
    prv = lambda c: jnp.maximum(cur(c) - 1, 0)
    mrg = lambda c: jnp.maximum(c - 1, 0)
    v_row = ATTN_WIDTH // KV_WIDTH
    qk_blk = 2 * (COL_QKM - col0)
    tile = lambda col: pl.BlockSpec((PAIR, D_MODEL), lambda c: (mrg(c), col))
    tile_cur = lambda col: pl.BlockSpec((PAIR, D_MODEL), lambda c: (cur(c), col))
    vec = pl.BlockSpec((1, D_MODEL), lambda c: (0, 0))
    weight = pl.BlockSpec((D_MODEL, D_MODEL), lambda c: (0, 0), pipeline_mode=pl.Buffered(1))
    state = lambda *blk: pl.BlockSpec((None,) + blk, lambda c: (0,) * (len(blk) + 1))
    return pl.pallas_call(
        functools.partial(_mixer_body, n_blk=n_blk),
        grid=(n_blk + 1,),
        in_specs=[
            pl.BlockSpec((ATTN_WIDTH, PAIR), lambda c: (0, cur(c))),
            pl.BlockSpec((PAIR, KV_WIDTH), lambda c: (prv(c), 0)),
            pl.BlockSpec((PAIR, KV_WIDTH), lambda c: (cur(c), 0)),
            pl.BlockSpec((N_META, KV_WIDTH), lambda c: (0, 0)),
            pl.BlockSpec((KV_WIDTH, PAIR), lambda c: (v_row, prv(c))),
            pl.BlockSpec((KV_WIDTH, PAIR), lambda c: (v_row, cur(c))),
            pl.BlockSpec((KV_WIDTH, N_META), lambda c: (0, 0)),
            pl.BlockSpec((N_KV_HEADS, 1, GQA_GROUP * PAIR), lambda c: (0, 0, 0)),
            pl.BlockSpec((PAIR, M_QK_WIDTH), lambda c: (cur(c), qk_blk)),
            pl.BlockSpec((PAIR, M_QK_WIDTH), lambda c: (cur(c), qk_blk + 1)),
            pl.BlockSpec((PAIR, M_V_WIDTH), lambda c: (cur(c), COL_VM - col0)),
            pl.BlockSpec((PAIR, LANES), lambda c: (cur(c), GATE_BLOCK)),
            pl.BlockSpec((1, LANES), lambda c: (0, 0)),
            state(M_HEADS, M_QK_DIM, M_V_DIM), state(M_HEADS, M_QK_DIM), state(M_HEADS, LANES),
            tile(0), tile_cur(COL_ZA - col0), tile_cur(COL_OM - col0), tile_cur(COL_ZM - col0),
            tile(COL_GA - col0), tile(COL_GM - col0), vec, weight, weight, weight, vec,
        ],
        out_specs=[
            tile(0),
            state(M_HEADS, M_QK_DIM, M_V_DIM), state(M_HEADS, M_QK_DIM), state(M_HEADS, LANES),
        ],
        out_shape=[
            jax.ShapeDtypeStruct((seq, D_MODEL), F32),
            jax.ShapeDtypeStruct((1, M_HEADS, M_QK_DIM, M_V_DIM), F32),
            jax.ShapeDtypeStruct((1, M_HEADS, M_QK_DIM), F32),
            jax.ShapeDtypeStruct((1, M_HEADS, LANES), F32),
        ],
        scratch_shapes=[pltpu.VMEM((MIX_SLOTS, PAIR, ATTN_WIDTH), BF16),
                        pltpu.VMEM((MIX_SLOTS, PAIR, M_V_WIDTH), BF16)],
        compiler_params=pltpu.CompilerParams(
            dimension_semantics=("arbitrary",), vmem_limit_bytes=VMEM_LIMIT),
        name="mixer",
    )(qvt, o32, o32, k_meta, qvt, qvt, vt_meta, sinks_t,
      o16, o16, o16, o32, bias, c0, n0, m0,
      x, o16, o16, o16, o16, o16, g_mh, w_pa, w_pm, w_out, g_final)


def kernel(x_prompt, x_sample, cache_k, cache_v, state_C, state_n, state_m, meta_tokens, g_norm, w_in,
           b_igate, b_fgate, attn_sinks, g_mhnorm, w_pa, w_pm, w_out, g_final):
    batch, seq, _ = x_prompt.shape
    db, ds, _ = x_sample.shape
    depth = w_in.shape[0]
    assert batch == 1 and depth == 1 and ds == N_META
    cache_rows = cache_k.shape[2]

    w_t = w_in[0].T.astype(BF16)
    g_in = g_norm[0].reshape(1, D_MODEL)
    bias = jnp.concatenate([b_igate[0], b_fgate[0], jnp.zeros((LANES - 2 * M_HEADS,), F32)]).reshape(1, LANES)
    sinks = jnp.broadcast_to(attn_sinks[0][:, None], (N_Q_HEADS, LANES))
    sinks_t = jnp.broadcast_to(attn_sinks[0].reshape(N_KV_HEADS, 1, GQA_GROUP, 1),
                               (N_KV_HEADS, 1, GQA_GROUP, PAIR)).reshape(N_KV_HEADS, 1, GQA_GROUP * PAIR)
    g_mh = g_mhnorm[0].reshape(1, M_V_WIDTH)
    g_fin = g_final.reshape(1, D_MODEL)
    wpa, wpm, wout = w_pa[0].astype(BF16), w_pm[0].astype(BF16), w_out[0].astype(BF16)

    xp = x_prompt.reshape(seq, D_MODEL)
    xs_rows = db * ds
    x_small = jnp.concatenate([x_sample.reshape(xs_rows, D_MODEL), meta_tokens.astype(F32)], axis=0)
    meta_block = xs_rows // N_META
    p_tm = min(seq, 1024)
    p32, p16, xn_p = _project(xp, g_in, w_t, tm=p_tm, col0=COL_ZA)
    s32, s16, _ = _project(x_small, g_in, w_t, tm=x_small.shape[0], col0=COL_QA)
    qvt = _project_t(w_t, xn_p, tm=p_tm)

    zeros = lambda *shape: jnp.zeros(shape, F32)
    _, c_meta, n_meta, m_meta = _mlstm(
        s16, s32, bias, zeros(1, M_HEADS, M_QK_DIM, M_V_DIM), zeros(1, M_HEADS, M_QK_DIM),
        zeros(1, M_HEADS, LANES), blk=N_META, n_streams=1, n_blocks=1, first_block=meta_block,
        shared_init=True, col0=COL_QA)

    k_meta = s32[xs_rows:, :KV_WIDTH]
    vt_meta = s32[xs_rows:, KV_WIDTH:2 * KV_WIDTH].T.astype(BF16)
    y_p, c_p, n_p, m_p = _mixer(xp, qvt, p32, p16, k_meta, vt_meta, sinks_t, bias, c_meta, n_meta, m_meta,
                                g_mh, wpa, wpm, wout, g_fin, col0=COL_ZA)

    ck = cache_k[0].reshape(db, cache_rows, KV_WIDTH)
    cv = cache_v[0].reshape(db, cache_rows, KV_WIDTH)
    oa_s = _attend_sample(s16, s32, ck, cv, meta_block, sinks, db, ds)
    m0_s = jnp.broadcast_to(state_m[0][:, :, None], (db, M_HEADS, LANES))
    hn_s, c_s, n_s, m_s = _mlstm(s16, s32, bias, state_C[0], state_n[0], m0_s, blk=ds, n_streams=db,
                                 n_blocks=1, first_block=0, shared_init=False, col0=COL_QA)
    y_s = _merge(x_small, oa_s, hn_s, s16, g_mh, wpa, wpm, wout, g_fin, rows=xs_rows, tm=xs_rows, col0=COL_QA)

    kv_shape = (1, batch, cache_rows, N_KV_HEADS, HEAD_DIM)
    k_p = p32[seq - cache_rows:, :KV_WIDTH].reshape(kv_shape)
    v_p = p32[seq - cache_rows:, KV_WIDTH:2 * KV_WIDTH].reshape(kv_shape)
    new_k = s32[:xs_rows, :KV_WIDTH].reshape(db, ds, KV_WIDTH)
    new_v = s32[:xs_rows, KV_WIDTH:2 * KV_WIDTH].reshape(db, ds, KV_WIDTH)
    skv_shape = (1, db, cache_rows, N_KV_HEADS, HEAD_DIM)
    k_s = jnp.concatenate([ck, new_k], axis=1)[:, -cache_rows:].reshape(skv_shape)
    v_s = jnp.concatenate([cv, new_v], axis=1)[:, -cache_rows:].reshape(skv_shape)

    return (y_p.reshape(batch, seq, D_MODEL), y_s.reshape(db, ds, D_MODEL),
            k_p, v_p, c_p[None], n_p[None], m_p[None, :, :, 0],
            k_s, v_s, c_s[None], n_s[None], m_s[None, :, :, 0])
```

```python
import functools
import math

import jax
import jax.numpy as jnp
import numpy as np
from jax import lax
from jax.experimental import pallas as pl
from jax.experimental.pallas import tpu as pltpu

F32 = jnp.float32
BF16 = jnp.bfloat16

D_MODEL = 2048
CHUNK = 64
N_META = 16
HEAD_DIM = 64
N_Q_HEADS = 32
N_KV_HEADS = 4
GQA_GROUP = N_Q_HEADS // N_KV_HEADS
WINDOW = 128
WIN_CHUNKS = WINDOW // CHUNK
ATTN_WIDTH = N_Q_HEADS * HEAD_DIM
KV_WIDTH = N_KV_HEADS * HEAD_DIM
M_HEADS = 8
M_QK_DIM = 128
M_V_DIM = 256
M_QK_WIDTH = M_HEADS * M_QK_DIM
M_V_WIDTH = M_HEADS * M_V_DIM
EPS = 1e-6
NEG_INF = -1e30
COL_SIZES = (ATTN_WIDTH, KV_WIDTH, KV_WIDTH, ATTN_WIDTH, M_QK_WIDTH, M_QK_WIDTH, M_V_WIDTH, M_V_WIDTH,
             M_HEADS, M_HEADS, M_V_WIDTH, D_MODEL, D_MODEL)
COL_STARTS = tuple(int(v) for v in np.cumsum((0,) + COL_SIZES[:-1]))
(SRC_QA, SRC_KA, SRC_VA, SRC_ZA, SRC_QM, SRC_KM, SRC_VM, SRC_OM, SRC_IG, SRC_FG, SRC_ZM, SRC_GA,
 SRC_GM) = COL_STARTS

LANES = 128
VMEM_LIMIT = 56 * 1024 * 1024

KV_OUT = 2 * KV_WIDTH + LANES
GATE_BLOCK = 2 * KV_WIDTH // LANES
MAIN_OUT = 8 * D_MODEL
COL_QA, COL_ZA, COL_QKM, COL_VM, COL_OM, COL_ZM, COL_GA, COL_GM = range(8)
PROJ_TN = 1024
ROW_ALIGN = 16
MLSTM_K_SCALE = M_QK_DIM ** -0.5
Q_SCALE = 1.0 / math.sqrt(HEAD_DIM)
QV_ROWS = ATTN_WIDTH + KV_WIDTH
PAIR = 2 * CHUNK
PAIR_KEYS = 2 * PAIR + N_META
ONES_ROWS = 16
MIX_SLOTS = 2
SAMPLE_STREAMS_PER_STEP = 4
MERGE_TN = 512


def _sigmoid(x):
    return 0.5 * jnp.tanh(0.5 * x) + 0.5


def _silu(x):
    return x * _sigmoid(x)


_NT = (((1,), (1,)), ((), ()))


def _main_row_offsets():
    groups = [(SRC_QA, ATTN_WIDTH), (SRC_ZA, ATTN_WIDTH), (SRC_QM, M_QK_WIDTH), (SRC_KM, M_QK_WIDTH),
              (SRC_VM, M_V_WIDTH), (SRC_OM, M_V_WIDTH), (SRC_ZM, M_V_WIDTH), (SRC_GA, D_MODEL),
              (SRC_GM, D_MODEL)]
    offs = []
    for start, width in groups:
        assert start % ROW_ALIGN == 0 and width % PROJ_TN == 0
        offs += [(start + b * PROJ_TN) // ROW_ALIGN for b in range(width // PROJ_TN)]
    assert len(offs) * PROJ_TN == MAIN_OUT
    return np.asarray(offs, np.int32)


def _proj_body(off_ref, x_ref, g_ref, wkv_ref, wg_ref, w_ref, o32_ref, o16_ref, xn_ref, *, tile0):
    del off_ref

    @pl.when(pl.program_id(1) == 0)
    def _():
        x = x_ref[...]
        ms = jnp.mean(x * x, axis=-1, keepdims=True)
        xn_ref[...] = ((x * lax.rsqrt(ms + EPS)) * g_ref[...]).astype(BF16)
        xn = xn_ref[...]
        o32_ref[:, :2 * KV_WIDTH] = lax.dot_general(xn, wkv_ref[...], _NT, preferred_element_type=F32)
        wg = jnp.concatenate([wg_ref[...], jnp.zeros((LANES - ROW_ALIGN, D_MODEL), BF16)], axis=0)
        o32_ref[:, 2 * KV_WIDTH:] = lax.dot_general(xn, wg, _NT, preferred_element_type=F32)

    group = (pl.program_id(1) + tile0) // (D_MODEL // PROJ_TN)
    is_sigmoid = (group == COL_OM) | (group == COL_GA) | (group == COL_GM)
    is_silu = (group == COL_ZA) | (group == COL_ZM)
    is_q = group == COL_QA

    def tile(act):
        acc = lax.dot_general(xn_ref[...], w_ref[...], _NT, preferred_element_type=F32)
        o16_ref[...] = act(acc).astype(BF16)

    pl.when(is_sigmoid)(lambda: tile(_sigmoid))
    pl.when(is_silu)(lambda: tile(_silu))
    pl.when(is_q)(lambda: tile(lambda acc: acc * Q_SCALE))
    pl.when(jnp.logical_not(is_sigmoid | is_silu | is_q))(lambda: tile(lambda acc: acc))


def _project(x, g, w_t, tm, col0):
    rows = x.shape[0]
    assert rows % tm == 0
    assert SRC_KA % (2 * KV_WIDTH) == 0 and SRC_VA == SRC_KA + KV_WIDTH
    assert SRC_IG % ROW_ALIGN == 0 and SRC_FG == SRC_IG + M_HEADS and 2 * M_HEADS == ROW_ALIGN
    tile0 = col0 * (D_MODEL // PROJ_TN)
    n_tiles = MAIN_OUT // PROJ_TN - tile0
    return pl.pallas_call(
        functools.partial(_proj_body, tile0=tile0),
        grid_spec=pltpu.PrefetchScalarGridSpec(
            num_scalar_prefetch=1,
            grid=(rows // tm, n_tiles),
            in_specs=[
                pl.BlockSpec((tm, D_MODEL), lambda i, j, off: (i, 0)),
                pl.BlockSpec((1, D_MODEL), lambda i, j, off: (0, 0)),
                pl.BlockSpec((2 * KV_WIDTH, D_MODEL), lambda i, j, off: (SRC_KA // (2 * KV_WIDTH), 0),
                             pipeline_mode=pl.Buffered(1)),
                pl.BlockSpec((ROW_ALIGN, D_MODEL), lambda i, j, off: (SRC_IG // ROW_ALIGN, 0),
                             pipeline_mode=pl.Buffered(1)),
                pl.BlockSpec((pl.Element(PROJ_TN), pl.Element(D_MODEL)),
                             lambda i, j, off: (off[j + tile0] * ROW_ALIGN, 0)),
            ],
            out_specs=[
                pl.BlockSpec((tm, KV_OUT), lambda i, j, off: (i, 0)),
                pl.BlockSpec((tm, PROJ_TN), lambda i, j, off: (i, j)),
                pl.BlockSpec((tm, D_MODEL), lambda i, j, off: (i, 0)),
            ],
        ),
        out_shape=[jax.ShapeDtypeStruct((rows, KV_OUT), F32),
                   jax.ShapeDtypeStruct((rows, n_tiles * PROJ_TN), BF16),
                   jax.ShapeDtypeStruct((rows, D_MODEL), BF16)],
        compiler_params=pltpu.CompilerParams(
            dimension_semantics=("parallel", "arbitrary"), vmem_limit_bytes=VMEM_LIMIT),
        name="project",
    )(jnp.asarray(_main_row_offsets()), x, g, w_t, w_t, w_t)


def _proj_t_body(wq_ref, wv_ref, xn_ref, o_ref):
    xn = xn_ref[...]
    q_t = lax.dot_general(wq_ref[...], xn, _NT, preferred_element_type=F32) * Q_SCALE
    o_ref[:ATTN_WIDTH, :] = q_t.astype(BF16)
    o_ref[ATTN_WIDTH:, :] = lax.dot_general(wv_ref[...], xn, _NT, preferred_element_type=F32).astype(BF16)


def _project_t(w_t, xn, tm):
    rows = xn.shape[0]
    assert rows % tm == 0 and SRC_QA == 0 and SRC_VA % KV_WIDTH == 0
    return pl.pallas_call(
        _proj_t_body,
        grid=(rows // tm,),
        in_specs=[pl.BlockSpec((ATTN_WIDTH, D_MODEL), lambda i: (0, 0), pipeline_mode=pl.Buffered(1)),
                  pl.BlockSpec((KV_WIDTH, D_MODEL), lambda i: (SRC_VA // KV_WIDTH, 0),
                               pipeline_mode=pl.Buffered(1)),
                  pl.BlockSpec((tm, D_MODEL), lambda i: (i, 0))],
        out_specs=pl.BlockSpec((QV_ROWS, tm), lambda i: (0, i)),
        out_shape=jax.ShapeDtypeStruct((QV_ROWS, rows), BF16),
        compiler_params=pltpu.CompilerParams(
            dimension_semantics=("parallel",), vmem_limit_bytes=VMEM_LIMIT),
        name="project_t",
    )(w_t, w_t, xn)


def _attn_pair_setup(p, ka_ref, kb_ref, km_ref, vta_ref, vtb_ref, vtm_ref):
    k = jnp.concatenate([ka_ref[...], kb_ref[...], km_ref[...]], axis=0).astype(BF16)
    vt = jnp.concatenate([vta_ref[...], vtb_ref[...], vtm_ref[...]], axis=1)
    odd = (lax.broadcasted_iota(jnp.int32, (CHUNK, GQA_GROUP * PAIR), 1) % PAIR) >= CHUNK
    first = p == 0
    return k, vt, (odd | first, first, jnp.logical_not(odd))


def _attn_pair_probs(setup, h, qt_ref, sink_ref):
    k, vt, (mask_a, mask_b, mask_d) = setup
    kh = k[:, h * HEAD_DIM:(h + 1) * HEAD_DIM]
    rows = [(h * GQA_GROUP + g) * HEAD_DIM for g in range(GQA_GROUP)]
    qth = jnp.concatenate([qt_ref[r:r + HEAD_DIM, :] for r in rows], axis=1)
    st = jnp.dot(kh, qth, preferred_element_type=F32)
    st = jnp.concatenate([
        jnp.where(mask_a, NEG_INF, st[:CHUNK]),
        jnp.where(mask_b, NEG_INF, st[CHUNK:2 * CHUNK]),
        st[2 * CHUNK:3 * CHUNK],
        jnp.where(mask_d, NEG_INF, st[3 * CHUNK:4 * CHUNK]),
        st[4 * CHUNK:]], axis=0)
    sink = sink_ref[h]
    m = jnp.maximum(jnp.max(st, axis=0, keepdims=True), sink)
    return jnp.exp(st - m).astype(BF16), jnp.exp(sink - m)


def _attn_pair_out(setup, h, probs):
    _, vt, _ = setup
    e, e_sink = probs
    ones = jnp.ones((ONES_ROWS, PAIR_KEYS), BF16)
    vth = jnp.concatenate([vt[h * HEAD_DIM:(h + 1) * HEAD_DIM, :], ones], axis=0)
    ot = jnp.dot(vth, e, preferred_element_type=F32)
    denom = ot[HEAD_DIM:HEAD_DIM + 1, :] + e_sink
    ot = ot[:HEAD_DIM, :] * (1.0 / denom)
    return jnp.concatenate([ot[:, g * PAIR:(g + 1) * PAIR] for g in range(GQA_GROUP)], axis=0).T


def _attend_heads(q, k, v, sink_ref, o_ref, lq):
    for h in range(N_KV_HEADS):
        kh = k[:, h * HEAD_DIM:(h + 1) * HEAD_DIM]
        vh = v[:, h * HEAD_DIM:(h + 1) * HEAD_DIM]
        heads = [h * GQA_GROUP + g for g in range(GQA_GROUP)]
        qh = jnp.concatenate([q[:, n * HEAD_DIM:(n + 1) * HEAD_DIM] for n in heads], axis=0)
        s = lax.dot_general(qh, kh, (((1,), (1,)), ((), ())), preferred_element_type=F32)
        sink = jnp.concatenate(
            [jnp.broadcast_to(sink_ref[n:n + 1, 0:1], (lq, 1)) for n in heads], axis=0)
        m = jnp.maximum(jnp.max(s, axis=-1, keepdims=True), sink)
        e = jnp.exp(s - m)
        denom = jnp.sum(e, axis=-1, keepdims=True) + jnp.exp(sink - m)
        oh = jnp.dot(e.astype(BF16), vh, preferred_element_type=F32) * (1.0 / denom)
        o_ref[:, h * GQA_GROUP * HEAD_DIM:(h + 1) * GQA_GROUP * HEAD_DIM] = jnp.concatenate(
            [oh[g * lq:(g + 1) * lq, :] for g in range(GQA_GROUP)], axis=1).astype(o_ref.dtype)


def _attn_sample_body(q_ref, ck_ref, cv_ref, kn_ref, vn_ref, km_ref, vm_ref, sink_ref, o_ref, *, ds, spb):
    for j in range(spb):
        rows = pl.ds(j * ds, ds)
        k = jnp.concatenate([ck_ref[j], kn_ref[rows, :], km_ref[...]], axis=0).astype(BF16)
        v = jnp.concatenate([cv_ref[j], vn_ref[rows, :], vm_ref[...]], axis=0).astype(BF16)
        _attend_heads(q_ref[rows, :], k, v, sink_ref, o_ref.at[rows], ds)


def _attend_sample(o16, o32, cache_k, cache_v, meta_block, sinks, n_streams, ds, spb):
    assert n_streams % spb == 0
    cache_rows = cache_k.shape[1]
    new = lambda col: pl.BlockSpec((spb * ds, KV_WIDTH), lambda s: (s, col))
    meta = lambda col: pl.BlockSpec((N_META, KV_WIDTH), lambda s: (meta_block, col))
    cache = pl.BlockSpec((spb, cache_rows, KV_WIDTH), lambda s: (s, 0, 0))
    return pl.pallas_call(
        functools.partial(_attn_sample_body, ds=ds, spb=spb),
        grid=(n_streams // spb,),
        in_specs=[
            pl.BlockSpec((spb * ds, ATTN_WIDTH), lambda s: (s, COL_QA)),
            cache, cache, new(0), new(1), meta(0), meta(1),
            pl.BlockSpec((N_Q_HEADS, LANES), lambda s: (0, 0)),
        ],
        out_specs=pl.BlockSpec((spb * ds, ATTN_WIDTH), lambda s: (s, 0)),
        out_shape=jax.ShapeDtypeStruct((n_streams * ds, ATTN_WIDTH), BF16),
        compiler_params=pltpu.CompilerParams(
            dimension_semantics=("parallel",), vmem_limit_bytes=VMEM_LIMIT),
        name="attend_sample",
    )(o16, cache_k, cache_v, o32, o32, o32, o32, sinks)


def _mlstm_body(q_ref, k_ref, v_ref, gt_ref, bias_ref, c0_ref, n0_ref, m0_ref,
                hn_ref, c_ref, n_ref, m_ref, *, blk, spb):
    c_ref[...] = c0_ref[...]
    n_ref[...] = n0_ref[...]
    m_ref[...] = m0_ref[...]
    for j in range(spb):
        rows = pl.ds(j * blk, blk)
        q, k, v, hn = q_ref.at[rows], k_ref.at[rows], v_ref.at[rows], hn_ref.at[rows]
        setup = _mlstm_setup(gt_ref.at[rows], bias_ref, blk)
        for h in range(M_HEADS):
            hv = _mlstm_head(setup, h, q, k, v, c_ref.at[j], n_ref.at[j], m_ref.at[j], blk, None)
            hn[:, h * M_V_DIM:(h + 1) * M_V_DIM] = hv.astype(hn_ref.dtype)


def _mlstm_setup(gt_ref, bias_ref, blk):
    x = gt_ref[...] + bias_ref[...]
    lane = lax.broadcasted_iota(jnp.int32, x.shape, 1)
    log_f = jnp.minimum(x, 0.0) - jnp.log(1.0 + jnp.exp(-jnp.abs(x)))
    gates = jnp.where(lane < M_HEADS, x, log_f)
    row = lax.broadcasted_iota(jnp.int32, (blk, blk), 0)
    col = lax.broadcasted_iota(jnp.int32, (blk, blk), 1)
    causal = row >= col
    csum = jnp.dot(causal.astype(F32), gates, precision=lax.Precision.HIGHEST,
                   preferred_element_type=F32)
    z = jnp.where(lane < M_HEADS, gates, csum)
    pad = (-blk) % LANES
    zsq = z if pad == 0 else jnp.concatenate([z, jnp.zeros((pad, LANES), F32)], axis=0)
    zt = zsq.T[:, :blk]
    return z, zt, causal


def _mlstm_head(setup, h, q_ref, k_ref, v_ref, c_ref, n_ref, m_ref, blk, live):
    scores = _mlstm_scores(setup, h, q_ref, k_ref, m_ref)
    return _mlstm_finish(setup, h, scores, q_ref, k_ref, v_ref, c_ref, n_ref, m_ref, blk, live)


def _mlstm_scores(setup, h, q_ref, k_ref, m_ref):
    z, zt, causal = setup
    g_r = zt[h:h + 1, :] - zt[M_HEADS + h:M_HEADS + h + 1, :]
    m_prev = m_ref[h:h + 1, 0:1]
    d0 = jnp.where(causal, g_r, -jnp.inf)
    mm = jnp.maximum(m_prev, jnp.max(d0, axis=-1, keepdims=True))
    w = jnp.exp(d0 - mm)
    qh = q_ref[:, h * M_QK_DIM:(h + 1) * M_QK_DIM]
    kh = k_ref[:, h * M_QK_DIM:(h + 1) * M_QK_DIM]
    qk = lax.dot_general(qh, kh, _NT, preferred_element_type=F32)
    wqk = w * (qk * MLSTM_K_SCALE)
    return m_prev, mm, wqk.astype(BF16), jnp.sum(wqk, axis=-1, keepdims=True)


def _mlstm_finish(setup, h, scores, q_ref, k_ref, v_ref, c_ref, n_ref, m_ref, blk, live):
    z, _, _ = setup
    m_prev, mm, wqk, den_intra = scores
    keep = (lambda new, old: new) if live is None else (lambda new, old: jnp.where(live, new, old))
    li_c = z[:, h:h + 1]
    b_c = z[:, M_HEADS + h:M_HEADS + h + 1]
    m_t = b_c + mm
    a = jnp.exp(m_prev - mm)

    qh = q_ref[:, h * M_QK_DIM:(h + 1) * M_QK_DIM]
    kh = k_ref[:, h * M_QK_DIM:(h + 1) * M_QK_DIM]
    vh = v_ref[:, h * M_V_DIM:(h + 1) * M_V_DIM]
    c_h = c_ref[h]
    n_h = n_ref[h:h + 1, :]

    num = a * jnp.dot(qh, c_h.astype(BF16), preferred_element_type=F32) + jnp.dot(
        wqk, vh, preferred_element_type=F32)
    den = a * jnp.sum(qh.astype(F32) * n_h, axis=-1, keepdims=True) + den_intra
    hv = num * (1.0 / jnp.maximum(jnp.abs(den), jnp.exp(-m_t)))
    hv = hv * lax.rsqrt(jnp.mean(hv * hv, axis=-1, keepdims=True) + EPS)

    m_new = m_t[blk - 1:blk, :]
    b_last = b_c[blk - 1:blk, :]
    ws = jnp.exp(b_last - b_c + li_c - m_new) * MLSTM_K_SCALE
    a_last = jnp.exp(b_last + m_prev - m_new)
    vs = (vh.astype(F32) * ws).astype(BF16)
    ktv = lax.dot_general(kh, vs, (((0,), (0,)), ((), ())), preferred_element_type=F32)
    c_ref[h] = keep(a_last * c_h + ktv, c_h)
    n_ref[h:h + 1, :] = keep(a_last * n_h + jnp.sum(kh.astype(F32) * ws, axis=0, keepdims=True), n_h)
    m_ref[h:h + 1, :] = keep(jnp.broadcast_to(m_new, (1, LANES)), m_ref[h:h + 1, :])
    return hv


def _mlstm(o16, o32, bias, c0, n0, m0, *, blk, n_streams, spb, first_block, col0):
    assert n_streams % spb == 0 and first_block % spb == 0
    rb = lambda s: first_block // spb + s
    qk_blk = 2 * (COL_QKM - col0)
    rows = spb * blk
    return pl.pallas_call(
        functools.partial(_mlstm_body, blk=blk, spb=spb),
        grid=(n_streams // spb,),
        in_specs=[
            pl.BlockSpec((rows, M_QK_WIDTH), lambda s: (rb(s), qk_blk)),
            pl.BlockSpec((rows, M_QK_WIDTH), lambda s: (rb(s), qk_blk + 1)),
            pl.BlockSpec((rows, M_V_WIDTH), lambda s: (rb(s), COL_VM - col0)),
            pl.BlockSpec((rows, LANES), lambda s: (rb(s), GATE_BLOCK)),
            pl.BlockSpec((1, LANES), lambda s: (0, 0)),
            pl.BlockSpec((spb, M_HEADS, M_QK_DIM, M_V_DIM), lambda s: (s, 0, 0, 0)),
            pl.BlockSpec((spb, M_HEADS, M_QK_DIM), lambda s: (s, 0, 0)),
            pl.BlockSpec((spb, M_HEADS, LANES), lambda s: (s, 0, 0)),
        ],
        out_specs=[
            pl.BlockSpec((rows, M_V_WIDTH), lambda s: (s, 0)),
            pl.BlockSpec((spb, M_HEADS, M_QK_DIM, M_V_DIM), lambda s: (s, 0, 0, 0)),
            pl.BlockSpec((spb, M_HEADS, M_QK_DIM), lambda s: (s, 0, 0)),
            pl.BlockSpec((spb, M_HEADS, LANES), lambda s: (s, 0, 0)),
        ],
        out_shape=[
            jax.ShapeDtypeStruct((n_streams * blk, M_V_WIDTH), BF16),
            jax.ShapeDtypeStruct((n_streams, M_HEADS, M_QK_DIM, M_V_DIM), F32),
            jax.ShapeDtypeStruct((n_streams, M_HEADS, M_QK_DIM), F32),
            jax.ShapeDtypeStruct((n_streams, M_HEADS, LANES), F32),
        ],
        compiler_params=pltpu.CompilerParams(
            dimension_semantics=("parallel",), vmem_limit_bytes=VMEM_LIMIT),
        name="mlstm_blk%d" % blk,
    )(o16, o16, o16, o32, bias, c0, n0, m0)


def _merge_body(x_ref, oa_ref, hn_ref, za_ref, om_ref, zm_ref, ga_ref, gm_ref, gmh_ref,
                wpa_ref, wpm_ref, wout_ref, gf_ref, y_ref, a_scr, m_scr, acc_scr):
    j = pl.program_id(0)

    @pl.when(j == 0)
    def _():
        a_scr[...] = (oa_ref[...].astype(F32) * za_ref[...].astype(F32)).astype(BF16)
        m_scr[...] = ((hn_ref[...].astype(F32) * gmh_ref[...]) * om_ref[...].astype(F32)
                      * zm_ref[...].astype(F32)).astype(BF16)
        acc_scr[...] = x_ref[...]

    ya = jnp.dot(a_scr[...], wpa_ref[...], preferred_element_type=F32)
    ym = jnp.dot(m_scr[...], wpm_ref[...], preferred_element_type=F32)
    merged = ga_ref[...].astype(F32) * ya + gm_ref[...].astype(F32) * ym
    acc_scr[...] += jnp.dot(merged.astype(BF16), wout_ref[...], preferred_element_type=F32)

    @pl.when(j == pl.num_programs(0) - 1)
    def _():
        xo = acc_scr[...]
        ms = jnp.mean(xo * xo, axis=-1, keepdims=True)
        y_ref[...] = (xo * lax.rsqrt(ms + EPS)) * gf_ref[...]


def _merge(x, o_att, hn, o16, g_mh, w_pa, w_pm, w_out, g_final, rows, col0):
    n_chunks = D_MODEL // MERGE_TN
    per_group = D_MODEL // MERGE_TN
    full = lambda col: pl.BlockSpec((rows, D_MODEL), lambda j: (0, col))
    gate = lambda col: pl.BlockSpec((rows, MERGE_TN), lambda j: (0, col * per_group + j))
    vec = pl.BlockSpec((1, D_MODEL), lambda j: (0, 0))
    w_cols = pl.BlockSpec((D_MODEL, MERGE_TN), lambda j: (0, j))
    w_rows = pl.BlockSpec((MERGE_TN, D_MODEL), lambda j: (j, 0))
    return pl.pallas_call(
        _merge_body,
        grid=(n_chunks,),
        in_specs=[full(0), full(0), full(0), full(COL_ZA - col0), full(COL_OM - col0), full(COL_ZM - col0),
                  gate(COL_GA - col0), gate(COL_GM - col0), vec, w_cols, w_cols, w_rows, vec],
        out_specs=full(0),
        out_shape=jax.ShapeDtypeStruct((rows, D_MODEL), F32),
        scratch_shapes=[pltpu.VMEM((rows, D_MODEL), BF16), pltpu.VMEM((rows, D_MODEL), BF16),
                        pltpu.VMEM((rows, D_MODEL), F32)],
        compiler_params=pltpu.CompilerParams(
            dimension_semantics=("arbitrary",), vmem_limit_bytes=VMEM_LIMIT),
        name="merge",
    )(x, o_att, hn, o16, o16, o16, o16, o16, g_mh, w_pa, w_pm, w_out, g_final)


def _mixer_body(qt_ref, ka_ref, kb_ref, km_ref, vta_ref, vtb_ref, vtm_ref, sink_ref,
                q_ref, k_ref, v_ref, gt_ref, bias_ref, c0_ref, n0_ref, m0_ref,
                x_ref, za_ref, om_ref, zm_ref, ga_ref, gm_ref, gmh_ref, wpa_ref, wpm_ref, wout_ref, gf_ref,
                y_ref, c_ref, n_ref, m_ref, a_scr, m_scr, *, n_blk):
    c = pl.program_id(0)

    @pl.when(c == 0)
    def _():
        c_ref[...] = c0_ref[...]
        n_ref[...] = n0_ref[...]
        m_ref[...] = m0_ref[...]
        a_scr[...] = jnp.zeros(a_scr.shape, a_scr.dtype)
        m_scr[...] = jnp.zeros(m_scr.shape, m_scr.dtype)

    wr = c % MIX_SLOTS
    rd = 1 - wr
    blk = jnp.minimum(c, n_blk - 1)
    live = c < n_blk
    width = GQA_GROUP * HEAD_DIM
    heads_per_piece = width // M_V_DIM
    a_in = a_scr[rd]
    m_in = m_scr[rd]

    def mlstm_piece(h):
        hc = slice(h * M_V_DIM, (h + 1) * M_V_DIM)
        hv = _mlstm_finish(cell, h, scores[h], q_ref, k_ref, v_ref, c_ref, n_ref, m_ref, PAIR, live)
        m_scr[wr, :, hc] = ((hv * gmh_ref[:, hc]) * om_ref[:, hc].astype(F32)
                            * zm_ref[:, hc].astype(F32)).astype(BF16)

    def attn_piece(h, probs):
        cols = slice(h * width, (h + 1) * width)
        oa = _attn_pair_out(attn, h, probs)
        a_scr[wr, :, cols] = (oa * za_ref[:, cols].astype(F32)).astype(BF16)

    attn = _attn_pair_setup(blk, ka_ref, kb_ref, km_ref, vta_ref, vtb_ref, vtm_ref)
    cell = _mlstm_setup(gt_ref, bias_ref, PAIR)
    scores = [_mlstm_scores(cell, h, q_ref, k_ref, m_ref) for h in range(M_HEADS)]
    probs = _attn_pair_probs(attn, 0, qt_ref, sink_ref)
    merged = []
    for i in range(N_KV_HEADS):
        cols = slice(i * width, (i + 1) * width)
        ya = jnp.dot(a_in, wpa_ref[:, cols], preferred_element_type=F32)
        mlstm_piece(heads_per_piece * i)
        next_probs = _attn_pair_probs(attn, i + 1, qt_ref, sink_ref) if i + 1 < N_KV_HEADS else None
        ym = jnp.dot(m_in, wpm_ref[:, cols], preferred_element_type=F32)
        mlstm_piece(heads_per_piece * i + 1)
        merged.append((ga_ref[:, cols].astype(F32) * ya + gm_ref[:, cols].astype(F32) * ym).astype(BF16))
        if i + 1 == N_KV_HEADS:
            xo = x_ref[...] + jnp.dot(jnp.concatenate(merged, axis=1), wout_ref[...],
                                      preferred_element_type=F32)
        attn_piece(i, probs)
        probs = next_probs
    ms = jnp.mean(xo * xo, axis=-1, keepdims=True)
    y_ref[...] = (xo * lax.rsqrt(ms + EPS)) * gf_ref[...]


def _mixer(x, qvt, o32, o16, k_meta, vt_meta, sinks_t, bias, c0, n0, m0, g_mh, w_pa, w_pm, w_out, g_final,
           col0):
    seq = x.shape[0]
    assert seq % PAIR == 0
    n_blk = seq // PAIR
    cur = lambda c: jnp.minimum(c, n_blk - 1)
    prv = lambda c: jnp.maximum(cur(c) - 1, 0)
    mrg = lambda c: jnp.maximum(c - 1, 0)
    v_row = ATTN_WIDTH // KV_WIDTH
    qk_blk = 2 * (COL_QKM - col0)
    tile = lambda col: pl.BlockSpec((PAIR, D_MODEL), lambda c: (mrg(c), col))
    tile_cur = lambda col: pl.BlockSpec((PAIR, D_MODEL), lambda c: (cur(c), col))
    vec = pl.BlockSpec((1, D_MODEL), lambda c: (0, 0))
    weight = pl.BlockSpec((D_MODEL, D_MODEL), lambda c: (0, 0), pipeline_mode=pl.Buffered(1))
    state = lambda *blk: pl.BlockSpec((None,) + blk, lambda c: (0,) * (len(blk) + 1))
    return pl.pallas_call(
        functools.partial(_mixer_body, n_blk=n_blk),
        grid=(n_blk + 1,),
        in_specs=[
            pl.BlockSpec((ATTN_WIDTH, PAIR), lambda c: (0, cur(c))),
            pl.BlockSpec((PAIR, KV_WIDTH), lambda c: (prv(c), 0)),
            pl.BlockSpec((PAIR, KV_WIDTH), lambda c: (cur(c), 0)),
            pl.BlockSpec((N_META, KV_WIDTH), lambda c: (0, 0)),
            pl.BlockSpec((KV_WIDTH, PAIR), lambda c: (v_row, prv(c))),
            pl.BlockSpec((KV_WIDTH, PAIR), lambda c: (v_row, cur(c))),
            pl.BlockSpec((KV_WIDTH, N_META), lambda c: (0, 0)),
            pl.BlockSpec((N_KV_HEADS, 1, GQA_GROUP * PAIR), lambda c: (0, 0, 0)),
            pl.BlockSpec((PAIR, M_QK_WIDTH), lambda c: (cur(c), qk_blk)),
            pl.BlockSpec((PAIR, M_QK_WIDTH), lambda c: (cur(c), qk_blk + 1)),
            pl.BlockSpec((PAIR, M_V_WIDTH), lambda c: (cur(c), COL_VM - col0)),
            pl.BlockSpec((PAIR, LANES), lambda c: (cur(c), GATE_BLOCK)),
            pl.BlockSpec((1, LANES), lambda c: (0, 0)),
            state(M_HEADS, M_QK_DIM, M_V_DIM), state(M_HEADS, M_QK_DIM), state(M_HEADS, LANES),
            tile(0), tile_cur(COL_ZA - col0), tile_cur(COL_OM - col0), tile_cur(COL_ZM - col0),
            tile(COL_GA - col0), tile(COL_GM - col0), vec, weight, weight, weight, vec,
        ],
        out_specs=[
            tile(0),
            state(M_HEADS, M_QK_DIM, M_V_DIM), state(M_HEADS, M_QK_DIM), state(M_HEADS, LANES),
        ],
        out_shape=[
            jax.ShapeDtypeStruct((seq, D_MODEL), F32),
            jax.ShapeDtypeStruct((1, M_HEADS, M_QK_DIM, M_V_DIM), F32),
            jax.ShapeDtypeStruct((1, M_HEADS, M_QK_DIM), F32),
            jax.ShapeDtypeStruct((1, M_HEADS, LANES), F32),
        ],
        scratch_shapes=[pltpu.VMEM((MIX_SLOTS, PAIR, ATTN_WIDTH), BF16),
                        pltpu.VMEM((MIX_SLOTS, PAIR, M_V_WIDTH), BF16)],
        compiler_params=pltpu.CompilerParams(
            dimension_semantics=("arbitrary",), vmem_limit_bytes=VMEM_LIMIT),
        name="mixer",
    )(qvt, o32, o32, k_meta, qvt, qvt, vt_meta, sinks_t,
      o16, o16, o16, o32, bias, c0, n0, m0,
      x, o16, o16, o16, o16, o16, g_mh, w_pa, w_pm, w_out, g_final)


def kernel(x_prompt, x_sample, cache_k, cache_v, state_C, state_n, state_m, meta_tokens, g_norm, w_in,
           b_igate, b_fgate, attn_sinks, g_mhnorm, w_pa, w_pm, w_out, g_final):
    batch, seq, _ = x_prompt.shape
    db, ds, _ = x_sample.shape
    depth = w_in.shape[0]
    assert batch == 1 and depth == 1 and ds == N_META
    cache_rows = cache_k.shape[2]

    w_t = w_in[0].T.astype(BF16)
    g_in = g_norm[0].reshape(1, D_MODEL)
    bias = jnp.concatenate([b_igate[0], b_fgate[0], jnp.zeros((LANES - 2 * M_HEADS,), F32)]).reshape(1, LANES)
    sinks = jnp.broadcast_to(attn_sinks[0][:, None], (N_Q_HEADS, LANES))
    sinks_t = jnp.broadcast_to(attn_sinks[0].reshape(N_KV_HEADS, 1, GQA_GROUP, 1),
                               (N_KV_HEADS, 1, GQA_GROUP, PAIR)).reshape(N_KV_HEADS, 1, GQA_GROUP * PAIR)
    g_mh = g_mhnorm[0].reshape(1, M_V_WIDTH)
    g_fin = g_final.reshape(1, D_MODEL)
    wpa, wpm, wout = w_pa[0].astype(BF16), w_pm[0].astype(BF16), w_out[0].astype(BF16)

    xp = x_prompt.reshape(seq, D_MODEL)
    xs_rows = db * ds
    x_small = jnp.concatenate([x_sample.reshape(xs_rows, D_MODEL), meta_tokens.astype(F32)], axis=0)
    meta_block = xs_rows // N_META
    p_tm = min(seq, 1024)
    p32, p16, xn_p = _project(xp, g_in, w_t, tm=p_tm, col0=COL_ZA)
    s32, s16, _ = _project(x_small, g_in, w_t, tm=x_small.shape[0], col0=COL_QA)
    qvt = _project_t(w_t, xn_p, tm=p_tm)

    zeros = lambda *shape: jnp.zeros(shape, F32)
    _, c_meta, n_meta, m_meta = _mlstm(
        s16, s32, bias, zeros(1, M_HEADS, M_QK_DIM, M_V_DIM), zeros(1, M_HEADS, M_QK_DIM),
        zeros(1, M_HEADS, LANES), blk=N_META, n_streams=1, spb=1, first_block=meta_block, col0=COL_QA)

    k_meta = s32[xs_rows:, :KV_WIDTH]
    vt_meta = s32[xs_rows:, KV_WIDTH:2 * KV_WIDTH].T.astype(BF16)
    y_p, c_p, n_p, m_p = _mixer(xp, qvt, p32, p16, k_meta, vt_meta, sinks_t, bias, c_meta, n_meta, m_meta,
                                g_mh, wpa, wpm, wout, g_fin, col0=COL_ZA)

    ck = cache_k[0].reshape(db, cache_rows, KV_WIDTH)
    cv = cache_v[0].reshape(db, cache_rows, KV_WIDTH)
    spb = SAMPLE_STREAMS_PER_STEP if db % SAMPLE_STREAMS_PER_STEP == 0 else 1
    oa_s = _attend_sample(s16, s32, ck, cv, meta_block, sinks, db, ds, spb)
    m0_s = jnp.broadcast_to(state_m[0][:, :, None], (db, M_HEADS, LANES))
    hn_s, c_s, n_s, m_s = _mlstm(s16, s32, bias, state_C[0], state_n[0], m0_s, blk=ds, n_streams=db,
                                 spb=spb, first_block=0, col0=COL_QA)
    y_s = _merge(x_small, oa_s, hn_s, s16, g_mh, wpa, wpm, wout, g_fin, rows=xs_rows, col0=COL_QA)

    kv_shape = (1, batch, cache_rows, N_KV_HEADS, HEAD_DIM)
    k_p = p32[seq - cache_rows:, :KV_WIDTH].reshape(kv_shape)
    v_p = p32[seq - cache_rows:, KV_WIDTH:2 * KV_WIDTH].reshape(kv_shape)
    new_k = s32[:xs_rows, :KV_WIDTH].reshape(db, ds, KV_WIDTH)
    new_v = s32[:xs_rows, KV_WIDTH:2 * KV_WIDTH].reshape(db, ds, KV_WIDTH)
    skv_shape = (1, db, cache_rows, N_KV_HEADS, HEAD_DIM)
    k_s = jnp.concatenate([ck, new_k], axis=1)[:, -cache_rows:].reshape(skv_shape)
    v_s = jnp.concatenate([cv, new_v], axis=1)[:, -cache_rows:].reshape(skv_shape)

    return (y_p.reshape(batch, seq, D_MODEL), y_s.reshape(db, ds, D_MODEL),
            k_p, v_p, c_p[None], n_p[None], m_p[None, :, :, 0],
            k_s, v_s, c_s[None], n_s[None], m_s[None, :, :, 0])
```

```python
import functools
import math

import jax
import jax.numpy as jnp
import numpy as np
from jax import lax
from jax.experimental import pallas as pl
from jax.experimental.pallas import tpu as pltpu

F32 = jnp.float32
BF16 = jnp.bfloat16

D_MODEL = 2048
CHUNK = 64
N_META = 16
HEAD_DIM = 64
N_Q_HEADS = 32
N_KV_HEADS = 4
GQA_GROUP = N_Q_HEADS // N_KV_HEADS
WINDOW = 128
WIN_CHUNKS = WINDOW // CHUNK
ATTN_WIDTH = N_Q_HEADS * HEAD_DIM
KV_WIDTH = N_KV_HEADS * HEAD_DIM
M_HEADS = 8
M_QK_DIM = 128
M_V_DIM = 256
M_QK_WIDTH = M_HEADS * M_QK_DIM
M_V_WIDTH = M_HEADS * M_V_DIM
EPS = 1e-6
NEG_INF = -1e30
COL_SIZES = (ATTN_WIDTH, KV_WIDTH, KV_WIDTH, ATTN_WIDTH, M_QK_WIDTH, M_QK_WIDTH, M_V_WIDTH, M_V_WIDTH,
             M_HEADS, M_HEADS, M_V_WIDTH, D_MODEL, D_MODEL)
COL_STARTS = tuple(int(v) for v in np.cumsum((0,) + COL_SIZES[:-1]))
(SRC_QA, SRC_KA, SRC_VA, SRC_ZA, SRC_QM, SRC_KM, SRC_VM, SRC_OM, SRC_IG, SRC_FG, SRC_ZM, SRC_GA,
 SRC_GM) = COL_STARTS

LANES = 128
VMEM_LIMIT = 56 * 1024 * 1024

KV_OUT = 2 * KV_WIDTH + LANES
GATE_BLOCK = 2 * KV_WIDTH // LANES
MAIN_OUT = 8 * D_MODEL
COL_QA, COL_ZA, COL_QKM, COL_VM, COL_OM, COL_ZM, COL_GA, COL_GM = range(8)
PROJ_TN = 1024
ROW_ALIGN = 16
MLSTM_K_SCALE = M_QK_DIM ** -0.5
Q_SCALE = 1.0 / math.sqrt(HEAD_DIM)
QV_ROWS = ATTN_WIDTH + KV_WIDTH
PAIR = 2 * CHUNK
PAIR_KEYS = 2 * PAIR + N_META
ONES_ROWS = 16
MIX_SLOTS = 2
SAMPLE_STREAMS_PER_STEP = 1
MERGE_TN = 512


def _sigmoid(x):
    return 0.5 * jnp.tanh(0.5 * x) + 0.5


def _silu(x):
    return x * _sigmoid(x)


_NT = (((1,), (1,)), ((), ()))


def _main_row_offsets():
    groups = [(SRC_QA, ATTN_WIDTH), (SRC_ZA, ATTN_WIDTH), (SRC_QM, M_QK_WIDTH), (SRC_KM, M_QK_WIDTH),
              (SRC_VM, M_V_WIDTH), (SRC_OM, M_V_WIDTH), (SRC_ZM, M_V_WIDTH), (SRC_GA, D_MODEL),
              (SRC_GM, D_MODEL)]
    offs = []
    for start, width in groups:
        assert start % ROW_ALIGN == 0 and width % PROJ_TN == 0
        offs += [(start + b * PROJ_TN) // ROW_ALIGN for b in range(width // PROJ_TN)]
    assert len(offs) * PROJ_TN == MAIN_OUT
    return np.asarray(offs, np.int32)


def _proj_body(*refs, tile0, from_f32):
    if from_f32:
        _, x_ref, g_ref, wkv_ref, wg_ref, w_ref, o32_ref, o16_ref, xn_ref, wmain_out, wkv_out = refs
    else:
        x_ref, g_ref, wkv_ref, w_ref, o32_ref, o16_ref, xn_ref = refs

    @pl.when(pl.program_id(1) == 0)
    def _():
        x = x_ref[...]
        ms = jnp.mean(x * x, axis=-1, keepdims=True)
        xn_ref[...] = ((x * lax.rsqrt(ms + EPS)) * g_ref[...]).astype(BF16)
        if from_f32:
            wkv_out[:2 * KV_WIDTH, :] = wkv_ref[...].astype(BF16)
            wkv_out[2 * KV_WIDTH:2 * KV_WIDTH + ROW_ALIGN, :] = wg_ref[...].astype(BF16)
            wkv_out[2 * KV_WIDTH + ROW_ALIGN:, :] = jnp.zeros((LANES - ROW_ALIGN, D_MODEL), BF16)
            wkv = wkv_out[...]
        else:
            wkv = wkv_ref[...]
        o32_ref[...] = lax.dot_general(xn_ref[...], wkv, _NT, preferred_element_type=F32)

    group = (pl.program_id(1) + tile0) // (D_MODEL // PROJ_TN)
    is_sigmoid = (group == COL_OM) | (group == COL_GA) | (group == COL_GM)
    is_silu = (group == COL_ZA) | (group == COL_ZM)
    is_q = group == COL_QA

    def tile(act):
        w = w_ref[...].astype(BF16)
        if from_f32:
            wmain_out[...] = w
        acc = lax.dot_general(xn_ref[...], w, _NT, preferred_element_type=F32)
        o16_ref[...] = act(acc).astype(BF16)

    pl.when(is_sigmoid)(lambda: tile(_sigmoid))
    pl.when(is_silu)(lambda: tile(_silu))
    pl.when(is_q)(lambda: tile(lambda acc: acc * Q_SCALE))
    pl.when(jnp.logical_not(is_sigmoid | is_silu | is_q))(lambda: tile(lambda acc: acc))


def _project_first(x, g, w_t):
    rows = x.shape[0]
    assert SRC_KA % (2 * KV_WIDTH) == 0 and SRC_VA == SRC_KA + KV_WIDTH
    assert SRC_IG % ROW_ALIGN == 0 and SRC_FG == SRC_IG + M_HEADS and 2 * M_HEADS == ROW_ALIGN
    n_tiles = MAIN_OUT // PROJ_TN
    return pl.pallas_call(
        functools.partial(_proj_body, tile0=0, from_f32=True),
        grid_spec=pltpu.PrefetchScalarGridSpec(
            num_scalar_prefetch=1,
            grid=(1, n_tiles),
            in_specs=[
                pl.BlockSpec((rows, D_MODEL), lambda i, j, off: (0, 0)),
                pl.BlockSpec((1, D_MODEL), lambda i, j, off: (0, 0)),
                pl.BlockSpec((2 * KV_WIDTH, D_MODEL), lambda i, j, off: (SRC_KA // (2 * KV_WIDTH), 0),
                             pipeline_mode=pl.Buffered(1)),
                pl.BlockSpec((ROW_ALIGN, D_MODEL), lambda i, j, off: (SRC_IG // ROW_ALIGN, 0),
                             pipeline_mode=pl.Buffered(1)),
                pl.BlockSpec((pl.Element(PROJ_TN), pl.Element(D_MODEL)),
                             lambda i, j, off: (off[j] * ROW_ALIGN, 0)),
            ],
            out_specs=[
                pl.BlockSpec((rows, KV_OUT), lambda i, j, off: (0, 0)),
                pl.BlockSpec((rows, PROJ_TN), lambda i, j, off: (0, j)),
                pl.BlockSpec((rows, D_MODEL), lambda i, j, off: (0, 0)),
                pl.BlockSpec((PROJ_TN, D_MODEL), lambda i, j, off: (j, 0)),
                pl.BlockSpec((KV_OUT, D_MODEL), lambda i, j, off: (0, 0)),
            ],
        ),
        out_shape=[jax.ShapeDtypeStruct((rows, KV_OUT), F32),
                   jax.ShapeDtypeStruct((rows, MAIN_OUT), BF16),
                   jax.ShapeDtypeStruct((rows, D_MODEL), BF16),
                   jax.ShapeDtypeStruct((MAIN_OUT, D_MODEL), BF16),
                   jax.ShapeDtypeStruct((KV_OUT, D_MODEL), BF16)],
        compiler_params=pltpu.CompilerParams(
            dimension_semantics=("arbitrary", "arbitrary"), vmem_limit_bytes=VMEM_LIMIT),
        name="project_first",
    )(jnp.asarray(_main_row_offsets()), x, g, w_t, w_t, w_t)


def _project(x, g, wkv, wmain, tm, col0):
    rows = x.shape[0]
    assert rows % tm == 0
    tile0 = col0 * (D_MODEL // PROJ_TN)
    n_tiles = MAIN_OUT // PROJ_TN - tile0
    return pl.pallas_call(
        functools.partial(_proj_body, tile0=tile0, from_f32=False),
        grid=(rows // tm, n_tiles),
        in_specs=[
            pl.BlockSpec((tm, D_MODEL), lambda i, j: (i, 0)),
            pl.BlockSpec((1, D_MODEL), lambda i, j: (0, 0)),
            pl.BlockSpec((KV_OUT, D_MODEL), lambda i, j: (0, 0), pipeline_mode=pl.Buffered(1)),
            pl.BlockSpec((PROJ_TN, D_MODEL), lambda i, j: (j + tile0, 0)),
        ],
        out_specs=[
            pl.BlockSpec((tm, KV_OUT), lambda i, j: (i, 0)),
            pl.BlockSpec((tm, PROJ_TN), lambda i, j: (i, j)),
            pl.BlockSpec((tm, D_MODEL), lambda i, j: (i, 0)),
        ],
        out_shape=[jax.ShapeDtypeStruct((rows, KV_OUT), F32),
                   jax.ShapeDtypeStruct((rows, n_tiles * PROJ_TN), BF16),
                   jax.ShapeDtypeStruct((rows, D_MODEL), BF16)],
        compiler_params=pltpu.CompilerParams(
            dimension_semantics=("parallel", "arbitrary"), vmem_limit_bytes=VMEM_LIMIT),
        name="project",
    )(x, g, wkv, wmain)


def _proj_t_body(wq_ref, wv_ref, xn_ref, o_ref):
    xn = xn_ref[...]
    q_t = lax.dot_general(wq_ref[...], xn, _NT, preferred_element_type=F32) * Q_SCALE
    o_ref[:ATTN_WIDTH, :] = q_t.astype(BF16)
    o_ref[ATTN_WIDTH:, :] = lax.dot_general(wv_ref[...], xn, _NT, preferred_element_type=F32).astype(BF16)


def _project_t(wmain, wkv, xn, tm):
    rows = xn.shape[0]
    assert rows % tm == 0
    return pl.pallas_call(
        _proj_t_body,
        grid=(rows // tm,),
        in_specs=[pl.BlockSpec((ATTN_WIDTH, D_MODEL), lambda i: (COL_QA, 0), pipeline_mode=pl.Buffered(1)),
                  pl.BlockSpec((KV_WIDTH, D_MODEL), lambda i: (1, 0), pipeline_mode=pl.Buffered(1)),
                  pl.BlockSpec((tm, D_MODEL), lambda i: (i, 0))],
        out_specs=pl.BlockSpec((QV_ROWS, tm), lambda i: (0, i)),
        out_shape=jax.ShapeDtypeStruct((QV_ROWS, rows), BF16),
        compiler_params=pltpu.CompilerParams(
            dimension_semantics=("parallel",), vmem_limit_bytes=VMEM_LIMIT),
        name="project_t",
    )(wmain, wkv, xn)


def _attn_pair_setup(p, ka_ref, kb_ref, km_ref, vta_ref, vtb_ref, vtm_ref):
    k = jnp.concatenate([ka_ref[...], kb_ref[...], km_ref[...]], axis=0).astype(BF16)
    vt = jnp.concatenate([vta_ref[...], vtb_ref[...], vtm_ref[...]], axis=1)
    odd = (lax.broadcasted_iota(jnp.int32, (CHUNK, GQA_GROUP * PAIR), 1) % PAIR) >= CHUNK
    first = p == 0
    return k, vt, (odd | first, first, jnp.logical_not(odd))


def _attn_pair_probs(setup, h, qt_ref, sink_ref):
    k, vt, (mask_a, mask_b, mask_d) = setup
    kh = k[:, h * HEAD_DIM:(h + 1) * HEAD_DIM]
    rows = [(h * GQA_GROUP + g) * HEAD_DIM for g in range(GQA_GROUP)]
    qth = jnp.concatenate([qt_ref[r:r + HEAD_DIM, :] for r in rows], axis=1)
    st = jnp.dot(kh, qth, preferred_element_type=F32)
    st = jnp.concatenate([
        jnp.where(mask_a, NEG_INF, st[:CHUNK]),
        jnp.where(mask_b, NEG_INF, st[CHUNK:2 * CHUNK]),
        st[2 * CHUNK:3 * CHUNK],
        jnp.where(mask_d, NEG_INF, st[3 * CHUNK:4 * CHUNK]),
        st[4 * CHUNK:]], axis=0)
    sink = sink_ref[h]
    m = jnp.maximum(jnp.max(st, axis=0, keepdims=True), sink)
    return jnp.exp(st - m).astype(BF16), jnp.exp(sink - m)


def _attn_pair_out(setup, h, probs):
    _, vt, _ = setup
    e, e_sink = probs
    ones = jnp.ones((ONES_ROWS, PAIR_KEYS), BF16)
    vth = jnp.concatenate([vt[h * HEAD_DIM:(h + 1) * HEAD_DIM, :], ones], axis=0)
    ot = jnp.dot(vth, e, preferred_element_type=F32)
    denom = ot[HEAD_DIM:HEAD_DIM + 1, :] + e_sink
    ot = ot[:HEAD_DIM, :] * (1.0 / denom)
    return jnp.concatenate([ot[:, g * PAIR:(g + 1) * PAIR] for g in range(GQA_GROUP)], axis=0).T


def _attend_heads(q, k, v, sink_ref, o_ref, lq):
    for h in range(N_KV_HEADS):
        kh = k[:, h * HEAD_DIM:(h + 1) * HEAD_DIM]
        vh = v[:, h * HEAD_DIM:(h + 1) * HEAD_DIM]
        heads = [h * GQA_GROUP + g for g in range(GQA_GROUP)]
        qh = jnp.concatenate([q[:, n * HEAD_DIM:(n + 1) * HEAD_DIM] for n in heads], axis=0)
        s = lax.dot_general(qh, kh, (((1,), (1,)), ((), ())), preferred_element_type=F32)
        sink = jnp.concatenate(
            [jnp.broadcast_to(sink_ref[n:n + 1, 0:1], (lq, 1)) for n in heads], axis=0)
        m = jnp.maximum(jnp.max(s, axis=-1, keepdims=True), sink)
        e = jnp.exp(s - m)
        denom = jnp.sum(e, axis=-1, keepdims=True) + jnp.exp(sink - m)
        oh = jnp.dot(e.astype(BF16), vh, preferred_element_type=F32) * (1.0 / denom)
        o_ref[:, h * GQA_GROUP * HEAD_DIM:(h + 1) * GQA_GROUP * HEAD_DIM] = jnp.concatenate(
            [oh[g * lq:(g + 1) * lq, :] for g in range(GQA_GROUP)], axis=1).astype(o_ref.dtype)


def _attn_sample_body(q_ref, ck_ref, cv_ref, kn_ref, vn_ref, km_ref, vm_ref, sink_ref, o_ref, *, ds, spb):
    for j in range(spb):
        rows = pl.ds(j * ds, ds)
        k = jnp.concatenate([ck_ref[j], kn_ref[rows, :], km_ref[...]], axis=0).astype(BF16)
        v = jnp.concatenate([cv_ref[j], vn_ref[rows, :], vm_ref[...]], axis=0).astype(BF16)
        _attend_heads(q_ref[rows, :], k, v, sink_ref, o_ref.at[rows], ds)


def _attend_sample(o16, o32, cache_k, cache_v, meta_block, sinks, n_streams, ds, spb):
    assert n_streams % spb == 0
    cache_rows = cache_k.shape[1]
    new = lambda col: pl.BlockSpec((spb * ds, KV_WIDTH), lambda s: (s, col))
    meta = lambda col: pl.BlockSpec((N_META, KV_WIDTH), lambda s: (meta_block, col))
    cache = pl.BlockSpec((spb, cache_rows, KV_WIDTH), lambda s: (s, 0, 0))
    return pl.pallas_call(
        functools.partial(_attn_sample_body, ds=ds, spb=spb),
        grid=(n_streams // spb,),
        in_specs=[
            pl.BlockSpec((spb * ds, ATTN_WIDTH), lambda s: (s, COL_QA)),
            cache, cache, new(0), new(1), meta(0), meta(1),
            pl.BlockSpec((N_Q_HEADS, LANES), lambda s: (0, 0)),
        ],
        out_specs=pl.BlockSpec((spb * ds, ATTN_WIDTH), lambda s: (s, 0)),
        out_shape=jax.ShapeDtypeStruct((n_streams * ds, ATTN_WIDTH), BF16),
        compiler_params=pltpu.CompilerParams(
            dimension_semantics=("parallel",), vmem_limit_bytes=VMEM_LIMIT),
        name="attend_sample",
    )(o16, cache_k, cache_v, o32, o32, o32, o32, sinks)


def _mlstm_body(q_ref, k_ref, v_ref, gt_ref, bias_ref, c0_ref, n0_ref, m0_ref,
                hn_ref, c_ref, n_ref, m_ref, *, blk, spb):
    c_ref[...] = c0_ref[...]
    n_ref[...] = n0_ref[...]
    m_ref[...] = m0_ref[...]
    for j in range(spb):
        rows = pl.ds(j * blk, blk)
        q, k, v, hn = q_ref.at[rows], k_ref.at[rows], v_ref.at[rows], hn_ref.at[rows]
        setup = _mlstm_setup(gt_ref.at[rows], bias_ref, blk)
        for h in range(M_HEADS):
            hv = _mlstm_head(setup, h, q, k, v, c_ref.at[j], n_ref.at[j], m_ref.at[j], blk, None)
            hn[:, h * M_V_DIM:(h + 1) * M_V_DIM] = hv.astype(hn_ref.dtype)


def _mlstm_setup(gt_ref, bias_ref, blk):
    x = gt_ref[...] + bias_ref[...]
    lane = lax.broadcasted_iota(jnp.int32, x.shape, 1)
    log_f = jnp.minimum(x, 0.0) - jnp.log(1.0 + jnp.exp(-jnp.abs(x)))
    gates = jnp.where(lane < M_HEADS, x, log_f)
    row = lax.broadcasted_iota(jnp.int32, (blk, blk), 0)
    col = lax.broadcasted_iota(jnp.int32, (blk, blk), 1)
    causal = row >= col
    csum = jnp.dot(causal.astype(F32), gates, precision=lax.Precision.HIGHEST,
                   preferred_element_type=F32)
    z = jnp.where(lane < M_HEADS, gates, csum)
    pad = (-blk) % LANES
    zsq = z if pad == 0 else jnp.concatenate([z, jnp.zeros((pad, LANES), F32)], axis=0)
    zt = zsq.T[:, :blk]
    return z, zt, causal


def _mlstm_head(setup, h, q_ref, k_ref, v_ref, c_ref, n_ref, m_ref, blk, live):
    scores = _mlstm_scores(setup, h, q_ref, k_ref, m_ref)
    return _mlstm_finish(setup, h, scores, q_ref, k_ref, v_ref, c_ref, n_ref, m_ref, blk, live)


def _mlstm_scores(setup, h, q_ref, k_ref, m_ref):
    z, zt, causal = setup
    g_r = zt[h:h + 1, :] - zt[M_HEADS + h:M_HEADS + h + 1, :]
    m_prev = m_ref[h:h + 1, 0:1]
    d0 = jnp.where(causal, g_r, -jnp.inf)
    mm = jnp.maximum(m_prev, jnp.max(d0, axis=-1, keepdims=True))
    w = jnp.exp(d0 - mm)
    qh = q_ref[:, h * M_QK_DIM:(h + 1) * M_QK_DIM]
    kh = k_ref[:, h * M_QK_DIM:(h + 1) * M_QK_DIM]
    qk = lax.dot_general(qh, kh, _NT, preferred_element_type=F32)
    wqk = w * (qk * MLSTM_K_SCALE)
    return m_prev, mm, wqk.astype(BF16), jnp.sum(wqk, axis=-1, keepdims=True)


def _mlstm_finish(setup, h, scores, q_ref, k_ref, v_ref, c_ref, n_ref, m_ref, blk, live):
    z, _, _ = setup
    m_prev, mm, wqk, den_intra = scores
    keep = (lambda new, old: new) if live is None else (lambda new, old: jnp.where(live, new, old))
    li_c = z[:, h:h + 1]
    b_c = z[:, M_HEADS + h:M_HEADS + h + 1]
    m_t = b_c + mm
    a = jnp.exp(m_prev - mm)

    qh = q_ref[:, h * M_QK_DIM:(h + 1) * M_QK_DIM]
    kh = k_ref[:, h * M_QK_DIM:(h + 1) * M_QK_DIM]
    vh = v_ref[:, h * M_V_DIM:(h + 1) * M_V_DIM]
    c_h = c_ref[h]
    n_h = n_ref[h:h + 1, :]

    num = a * jnp.dot(qh, c_h.astype(BF16), preferred_element_type=F32) + jnp.dot(
        wqk, vh, preferred_element_type=F32)
    den = a * jnp.sum(qh.astype(F32) * n_h, axis=-1, keepdims=True) + den_intra
    hv = num * (1.0 / jnp.maximum(jnp.abs(den), jnp.exp(-m_t)))
    hv = hv * lax.rsqrt(jnp.mean(hv * hv, axis=-1, keepdims=True) + EPS)

    m_new = m_t[blk - 1:blk, :]
    b_last = b_c[blk - 1:blk, :]
    ws = jnp.exp(b_last - b_c + li_c - m_new) * MLSTM_K_SCALE
    a_last = jnp.exp(b_last + m_prev - m_new)
    vs = (vh.astype(F32) * ws).astype(BF16)
    ktv = lax.dot_general(kh, vs, (((0,), (0,)), ((), ())), preferred_element_type=F32)
    c_ref[h] = keep(a_last * c_h + ktv, c_h)
    n_ref[h:h + 1, :] = keep(a_last * n_h + jnp.sum(kh.astype(F32) * ws, axis=0, keepdims=True), n_h)
    m_ref[h:h + 1, :] = keep(jnp.broadcast_to(m_new, (1, LANES)), m_ref[h:h + 1, :])
    return hv


def _mlstm(o16, o32, bias, c0, n0, m0, *, blk, n_streams, spb, first_block, col0):
    assert n_streams % spb == 0 and first_block % spb == 0
    rb = lambda s: first_block // spb + s
    qk_blk = 2 * (COL_QKM - col0)
    rows = spb * blk
    return pl.pallas_call(
        functools.partial(_mlstm_body, blk=blk, spb=spb),
        grid=(n_streams // spb,),
        in_specs=[
            pl.BlockSpec((rows, M_QK_WIDTH), lambda s: (rb(s), qk_blk)),
            pl.BlockSpec((rows, M_QK_WIDTH), lambda s: (rb(s), qk_blk + 1)),
            pl.BlockSpec((rows, M_V_WIDTH), lambda s: (rb(s), COL_VM - col0)),
            pl.BlockSpec((rows, LANES), lambda s: (rb(s), GATE_BLOCK)),
            pl.BlockSpec((1, LANES), lambda s: (0, 0)),
            pl.BlockSpec((spb, M_HEADS, M_QK_DIM, M_V_DIM), lambda s: (s, 0, 0, 0)),
            pl.BlockSpec((spb, M_HEADS, M_QK_DIM), lambda s: (s, 0, 0)),
            pl.BlockSpec((spb, M_HEADS, LANES), lambda s: (s, 0, 0)),
        ],
        out_specs=[
            pl.BlockSpec((rows, M_V_WIDTH), lambda s: (s, 0)),
            pl.BlockSpec((spb, M_HEADS, M_QK_DIM, M_V_DIM), lambda s: (s, 0, 0, 0)),
            pl.BlockSpec((spb, M_HEADS, M_QK_DIM), lambda s: (s, 0, 0)),
            pl.BlockSpec((spb, M_HEADS, LANES), lambda s: (s, 0, 0)),
        ],
        out_shape=[
            jax.ShapeDtypeStruct((n_streams * blk, M_V_WIDTH), BF16),
            jax.ShapeDtypeStruct((n_streams, M_HEADS, M_QK_DIM, M_V_DIM), F32),
            jax.ShapeDtypeStruct((n_streams, M_HEADS, M_QK_DIM), F32),
            jax.ShapeDtypeStruct((n_streams, M_HEADS, LANES), F32),
        ],
        compiler_params=pltpu.CompilerParams(
            dimension_semantics=("parallel",), vmem_limit_bytes=VMEM_LIMIT),
        name="mlstm_blk%d" % blk,
    )(o16, o16, o16, o32, bias, c0, n0, m0)


def _merge_body(x_ref, oa_ref, hn_ref, za_ref, om_ref, zm_ref, ga_ref, gm_ref, gmh_ref,
                wpa_ref, wpm_ref, wout_ref, gf_ref, y_ref, a_scr, m_scr, acc_scr):
    j = pl.program_id(0)

    @pl.when(j == 0)
    def _():
        a_scr[...] = (oa_ref[...].astype(F32) * za_ref[...].astype(F32)).astype(BF16)
        m_scr[...] = ((hn_ref[...].astype(F32) * gmh_ref[...]) * om_ref[...].astype(F32)
                      * zm_ref[...].astype(F32)).astype(BF16)
        acc_scr[...] = x_ref[...]

    ya = jnp.dot(a_scr[...], wpa_ref[...], preferred_element_type=F32)
    ym = jnp.dot(m_scr[...], wpm_ref[...], preferred_element_type=F32)
    merged = ga_ref[...].astype(F32) * ya + gm_ref[...].astype(F32) * ym
    acc_scr[...] += jnp.dot(merged.astype(BF16), wout_ref[...], preferred_element_type=F32)

    @pl.when(j == pl.num_programs(0) - 1)
    def _():
        xo = acc_scr[...]
        ms = jnp.mean(xo * xo, axis=-1, keepdims=True)
        y_ref[...] = (xo * lax.rsqrt(ms + EPS)) * gf_ref[...]


def _merge(x, o_att, hn, o16, g_mh, w_pa, w_pm, w_out, g_final, rows, col0):
    n_chunks = D_MODEL // MERGE_TN
    per_group = D_MODEL // MERGE_TN
    full = lambda col: pl.BlockSpec((rows, D_MODEL), lambda j: (0, col))
    gate = lambda col: pl.BlockSpec((rows, MERGE_TN), lambda j: (0, col * per_group + j))
    vec = pl.BlockSpec((1, D_MODEL), lambda j: (0, 0))
    w_cols = pl.BlockSpec((D_MODEL, MERGE_TN), lambda j: (0, j))
    w_rows = pl.BlockSpec((MERGE_TN, D_MODEL), lambda j: (j, 0))
    return pl.pallas_call(
        _merge_body,
        grid=(n_chunks,),
        in_specs=[full(0), full(0), full(0), full(COL_ZA - col0), full(COL_OM - col0), full(COL_ZM - col0),
                  gate(COL_GA - col0), gate(COL_GM - col0), vec, w_cols, w_cols, w_rows, vec],
        out_specs=full(0),
        out_shape=jax.ShapeDtypeStruct((rows, D_MODEL), F32),
        scratch_shapes=[pltpu.VMEM((rows, D_MODEL), BF16), pltpu.VMEM((rows, D_MODEL), BF16),
                        pltpu.VMEM((rows, D_MODEL), F32)],
        compiler_params=pltpu.CompilerParams(
            dimension_semantics=("arbitrary",), vmem_limit_bytes=VMEM_LIMIT),
        name="merge",
    )(x, o_att, hn, o16, o16, o16, o16, o16, g_mh, w_pa, w_pm, w_out, g_final)


def _mixer_body(qt_ref, ka_ref, kb_ref, km_ref, vta_ref, vtb_ref, vtm_ref, sink_ref,
                q_ref, k_ref, v_ref, gt_ref, bias_ref, c0_ref, n0_ref, m0_ref,
                x_ref, za_ref, om_ref, zm_ref, ga_ref, gm_ref, gmh_ref, wpa_ref, wpm_ref, wout_ref, gf_ref,
                y_ref, c_ref, n_ref, m_ref, a_scr, m_scr, *, n_blk):
    c = pl.program_id(0)

    @pl.when(c == 0)
    def _():
        c_ref[...] = c0_ref[...]
        n_ref[...] = n0_ref[...]
        m_ref[...] = m0_ref[...]
        a_scr[...] = jnp.zeros(a_scr.shape, a_scr.dtype)
        m_scr[...] = jnp.zeros(m_scr.shape, m_scr.dtype)

    wr = c % MIX_SLOTS
    rd = 1 - wr
    blk = jnp.minimum(c, n_blk - 1)
    live = c < n_blk
    width = GQA_GROUP * HEAD_DIM
    heads_per_piece = width // M_V_DIM
    a_in = a_scr[rd]
    m_in = m_scr[rd]

    def mlstm_piece(h):
        hc = slice(h * M_V_DIM, (h + 1) * M_V_DIM)
        hv = _mlstm_finish(cell, h, scores[h], q_ref, k_ref, v_ref, c_ref, n_ref, m_ref, PAIR, live)
        m_scr[wr, :, hc] = ((hv * gmh_ref[:, hc]) * om_ref[:, hc].astype(F32)
                            * zm_ref[:, hc].astype(F32)).astype(BF16)

    def attn_piece(h, probs):
        cols = slice(h * width, (h + 1) * width)
        oa = _attn_pair_out(attn, h, probs)
        a_scr[wr, :, cols] = (oa * za_ref[:, cols].astype(F32)).astype(BF16)

    attn = _attn_pair_setup(blk, ka_ref, kb_ref, km_ref, vta_ref, vtb_ref, vtm_ref)
    cell = _mlstm_setup(gt_ref, bias_ref, PAIR)
    scores = [_mlstm_scores(cell, h, q_ref, k_ref, m_ref) for h in range(M_HEADS)]
    probs = _attn_pair_probs(attn, 0, qt_ref, sink_ref)
    merged = []
    for i in range(N_KV_HEADS):
        cols = slice(i * width, (i + 1) * width)
        ya = jnp.dot(a_in, wpa_ref[:, cols], preferred_element_type=F32)
        mlstm_piece(heads_per_piece * i)
        next_probs = _attn_pair_probs(attn, i + 1, qt_ref, sink_ref) if i + 1 < N_KV_HEADS else None
        ym = jnp.dot(m_in, wpm_ref[:, cols], preferred_element_type=F32)
        mlstm_piece(heads_per_piece * i + 1)
        merged.append((ga_ref[:, cols].astype(F32) * ya + gm_ref[:, cols].astype(F32) * ym).astype(BF16))
        if i + 1 == N_KV_HEADS:
            xo = x_ref[...] + jnp.dot(jnp.concatenate(merged, axis=1), wout_ref[...],
                                      preferred_element_type=F32)
        attn_piece(i, probs)
        probs = next_probs
    ms = jnp.mean(xo * xo, axis=-1, keepdims=True)
    y_ref[...] = (xo * lax.rsqrt(ms + EPS)) * gf_ref[...]


def _mixer(x, qvt, o32, o16, k_meta, vt_meta, sinks_t, bias, c0, n0, m0, g_mh, w_pa, w_pm, w_out, g_final,
           col0):
    seq = x.shape[0]
    assert seq % PAIR == 0
    n_blk = seq // PAIR
    cur = lambda c: jnp.minimum(c, n_blk - 1)
    prv = lambda c: jnp.maximum(cur(c) - 1, 0)
    mrg = lambda c: jnp.maximum(c - 1, 0)
    v_row = ATTN_WIDTH // KV_WIDTH
    qk_blk = 2 * (COL_QKM - col0)
    tile = lambda col: pl.BlockSpec((PAIR, D_MODEL), lambda c: (mrg(c), col))
    tile_cur = lambda col: pl.BlockSpec((PAIR, D_MODEL), lambda c: (cur(c), col))
    vec = pl.BlockSpec((1, D_MODEL), lambda c: (0, 0))
    weight = pl.BlockSpec((D_MODEL, D_MODEL), lambda c: (0, 0), pipeline_mode=pl.Buffered(1))
    state = lambda *blk: pl.BlockSpec((None,) + blk, lambda c: (0,) * (len(blk) + 1))
    return pl.pallas_call(
        functools.partial(_mixer_body, n_blk=n_blk),
        grid=(n_blk + 1,),
        in_specs=[
            pl.BlockSpec((ATTN_WIDTH, PAIR), lambda c: (0, cur(c))),
            pl.BlockSpec((PAIR, KV_WIDTH), lambda c: (prv(c), 0)),
            pl.BlockSpec((PAIR, KV_WIDTH), lambda c: (cur(c), 0)),
            pl.BlockSpec((N_META, KV_WIDTH), lambda c: (0, 0)),
            pl.BlockSpec((KV_WIDTH, PAIR), lambda c: (v_row, prv(c))),
            pl.BlockSpec((KV_WIDTH, PAIR), lambda c: (v_row, cur(c))),
            pl.BlockSpec((KV_WIDTH, N_META), lambda c: (0, 0)),
            pl.BlockSpec((N_KV_HEADS, 1, GQA_GROUP * PAIR), lambda c: (0, 0, 0)),
            pl.BlockSpec((PAIR, M_QK_WIDTH), lambda c: (cur(c), qk_blk)),
            pl.BlockSpec((PAIR, M_QK_WIDTH), lambda c: (cur(c), qk_blk + 1)),
            pl.BlockSpec((PAIR, M_V_WIDTH), lambda c: (cur(c), COL_VM - col0)),
            pl.BlockSpec((PAIR, LANES), lambda c: (cur(c), GATE_BLOCK)),
            pl.BlockSpec((1, LANES), lambda c: (0, 0)),
            state(M_HEADS, M_QK_DIM, M_V_DIM), state(M_HEADS, M_QK_DIM), state(M_HEADS, LANES),
            tile(0), tile_cur(COL_ZA - col0), tile_cur(COL_OM - col0), tile_cur(COL_ZM - col0),
            tile(COL_GA - col0), tile(COL_GM - col0), vec, weight, weight, weight, vec,
        ],
        out_specs=[
            tile(0),
            state(M_HEADS, M_QK_DIM, M_V_DIM), state(M_HEADS, M_QK_DIM), state(M_HEADS, LANES),
        ],
        out_shape=[
            jax.ShapeDtypeStruct((seq, D_MODEL), F32),
            jax.ShapeDtypeStruct((1, M_HEADS, M_QK_DIM, M_V_DIM), F32),
            jax.ShapeDtypeStruct((1, M_HEADS, M_QK_DIM), F32),
            jax.ShapeDtypeStruct((1, M_HEADS, LANES), F32),
        ],
        scratch_shapes=[pltpu.VMEM((MIX_SLOTS, PAIR, ATTN_WIDTH), BF16),
                        pltpu.VMEM((MIX_SLOTS, PAIR, M_V_WIDTH), BF16)],
        compiler_params=pltpu.CompilerParams(
            dimension_semantics=("arbitrary",), vmem_limit_bytes=VMEM_LIMIT),
        name="mixer",
    )(qvt, o32, o32, k_meta, qvt, qvt, vt_meta, sinks_t,
      o16, o16, o16, o32, bias, c0, n0, m0,
      x, o16, o16, o16, o16, o16, g_mh, w_pa, w_pm, w_out, g_final)


def kernel(x_prompt, x_sample, cache_k, cache_v, state_C, state_n, state_m, meta_tokens, g_norm, w_in,
           b_igate, b_fgate, attn_sinks, g_mhnorm, w_pa, w_pm, w_out, g_final):
    batch, seq, _ = x_prompt.shape
    db, ds, _ = x_sample.shape
    depth = w_in.shape[0]
    assert batch == 1 and depth == 1 and ds == N_META
    cache_rows = cache_k.shape[2]

    w_t = w_in[0].T
    g_in = g_norm[0].reshape(1, D_MODEL)
    bias = jnp.concatenate([b_igate[0], b_fgate[0], jnp.zeros((LANES - 2 * M_HEADS,), F32)]).reshape(1, LANES)
    sinks = jnp.broadcast_to(attn_sinks[0][:, None], (N_Q_HEADS, LANES))
    sinks_t = jnp.broadcast_to(attn_sinks[0].reshape(N_KV_HEADS, 1, GQA_GROUP, 1),
                               (N_KV_HEADS, 1, GQA_GROUP, PAIR)).reshape(N_KV_HEADS, 1, GQA_GROUP * PAIR)
    g_mh = g_mhnorm[0].reshape(1, M_V_WIDTH)
    g_fin = g_final.reshape(1, D_MODEL)
    wpa, wpm, wout = w_pa[0].astype(BF16), w_pm[0].astype(BF16), w_out[0].astype(BF16)

    xp = x_prompt.reshape(seq, D_MODEL)
    xs_rows = db * ds
    x_small = jnp.concatenate([x_sample.reshape(xs_rows, D_MODEL), meta_tokens.astype(F32)], axis=0)
    meta_block = xs_rows // N_META
    p_tm = min(seq, 1024)
    s32, s16, _, wmain, wkv = _project_first(x_small, g_in, w_t)
    p32, p16, xn_p = _project(xp, g_in, wkv, wmain, tm=p_tm, col0=COL_ZA)
    qvt = _project_t(wmain, wkv, xn_p, tm=p_tm)

    zeros = lambda *shape: jnp.zeros(shape, F32)
    _, c_meta, n_meta, m_meta = _mlstm(
        s16, s32, bias, zeros(1, M_HEADS, M_QK_DIM, M_V_DIM), zeros(1, M_HEADS, M_QK_DIM),
        zeros(1, M_HEADS, LANES), blk=N_META, n_streams=1, spb=1, first_block=meta_block, col0=COL_QA)

    k_meta = s32[xs_rows:, :KV_WIDTH]
    vt_meta = s32[xs_rows:, KV_WIDTH:2 * KV_WIDTH].T.astype(BF16)
    y_p, c_p, n_p, m_p = _mixer(xp, qvt, p32, p16, k_meta, vt_meta, sinks_t, bias, c_meta, n_meta, m_meta,
                                g_mh, wpa, wpm, wout, g_fin, col0=COL_ZA)

    ck = cache_k[0].reshape(db, cache_rows, KV_WIDTH)
    cv = cache_v[0].reshape(db, cache_rows, KV_WIDTH)
    spb = SAMPLE_STREAMS_PER_STEP if db % SAMPLE_STREAMS_PER_STEP == 0 else 1
    oa_s = _attend_sample(s16, s32, ck, cv, meta_block, sinks, db, ds, spb)
    m0_s = jnp.broadcast_to(state_m[0][:, :, None], (db, M_HEADS, LANES))
    hn_s, c_s, n_s, m_s = _mlstm(s16, s32, bias, state_C[0], state_n[0], m0_s, blk=ds, n_streams=db,
                                 spb=spb, first_block=0, col0=COL_QA)
    y_s = _merge(x_small, oa_s, hn_s, s16, g_mh, wpa, wpm, wout, g_fin, rows=xs_rows, col0=COL_QA)

    kv_shape = (1, batch, cache_rows, N_KV_HEADS, HEAD_DIM)
    k_p = p32[seq - cache_rows:, :KV_WIDTH].reshape(kv_shape)
    v_p = p32[seq - cache_rows:, KV_WIDTH:2 * KV_WIDTH].reshape(kv_shape)
    new_k = s32[:xs_rows, :KV_WIDTH].reshape(db, ds, KV_WIDTH)
    new_v = s32[:xs_rows, KV_WIDTH:2 * KV_WIDTH].reshape(db, ds, KV_WIDTH)
    skv_shape = (1, db, cache_rows, N_KV_HEADS, HEAD_DIM)
    k_s = jnp.concatenate([ck, new_k], axis=1)[:, -cache_rows:].reshape(skv_shape)
    v_s = jnp.concatenate([cv, new_v], axis=1)[:, -cache_rows:].reshape(skv_shape)

    return (y_p.reshape(batch, seq, D_MODEL), y_s.reshape(db, ds, D_MODEL),
            k_p, v_p, c_p[None], n_p[None], m_p[None, :, :, 0],
            k_s, v_s, c_s[None], n_s[None], m_s[None, :, :, 0])
```

```python
import functools
import math

import jax
import jax.numpy as jnp
import numpy as np
from jax import lax
from jax.experimental import pallas as pl
from jax.experimental.pallas import tpu as pltpu

F32 = jnp.float32
BF16 = jnp.bfloat16

D_MODEL = 2048
CHUNK = 64
N_META = 16
HEAD_DIM = 64
N_Q_HEADS = 32
N_KV_HEADS = 4
GQA_GROUP = N_Q_HEADS // N_KV_HEADS
WINDOW = 128
WIN_CHUNKS = WINDOW // CHUNK
ATTN_WIDTH = N_Q_HEADS * HEAD_DIM
KV_WIDTH = N_KV_HEADS * HEAD_DIM
M_HEADS = 8
M_QK_DIM = 128
M_V_DIM = 256
M_QK_WIDTH = M_HEADS * M_QK_DIM
M_V_WIDTH = M_HEADS * M_V_DIM
EPS = 1e-6
NEG_INF = -1e30
COL_SIZES = (ATTN_WIDTH, KV_WIDTH, KV_WIDTH, ATTN_WIDTH, M_QK_WIDTH, M_QK_WIDTH, M_V_WIDTH, M_V_WIDTH,
             M_HEADS, M_HEADS, M_V_WIDTH, D_MODEL, D_MODEL)
COL_STARTS = tuple(int(v) for v in np.cumsum((0,) + COL_SIZES[:-1]))
(SRC_QA, SRC_KA, SRC_VA, SRC_ZA, SRC_QM, SRC_KM, SRC_VM, SRC_OM, SRC_IG, SRC_FG, SRC_ZM, SRC_GA,
 SRC_GM) = COL_STARTS

LANES = 128
VMEM_LIMIT = 56 * 1024 * 1024

KV_OUT = 2 * KV_WIDTH + LANES
GATE_BLOCK = 2 * KV_WIDTH // LANES
MAIN_OUT = 8 * D_MODEL
COL_QA, COL_ZA, COL_QKM, COL_VM, COL_OM, COL_ZM, COL_GA, COL_GM = range(8)
PROJ_TN = 1024
ROW_ALIGN = 16
MLSTM_K_SCALE = M_QK_DIM ** -0.5
Q_SCALE = 1.0 / math.sqrt(HEAD_DIM)
QV_ROWS = ATTN_WIDTH + KV_WIDTH
PAIR = 2 * CHUNK
PAIR_KEYS = 2 * PAIR + N_META
ONES_ROWS = 16
MIX_SLOTS = 2
SAMPLE_STREAMS_PER_STEP = 1
MERGE_TN = 512


def _sigmoid(x):
    return 0.5 * jnp.tanh(0.5 * x) + 0.5


def _silu(x):
    return x * _sigmoid(x)


_NT = (((1,), (1,)), ((), ()))


def _main_row_offsets():
    groups = [(SRC_QA, ATTN_WIDTH), (SRC_ZA, ATTN_WIDTH), (SRC_QM, M_QK_WIDTH), (SRC_KM, M_QK_WIDTH),
              (SRC_VM, M_V_WIDTH), (SRC_OM, M_V_WIDTH), (SRC_ZM, M_V_WIDTH), (SRC_GA, D_MODEL),
              (SRC_GM, D_MODEL)]
    offs = []
    for start, width in groups:
        assert start % ROW_ALIGN == 0 and width % PROJ_TN == 0
        offs += [(start + b * PROJ_TN) // ROW_ALIGN for b in range(width // PROJ_TN)]
    assert len(offs) * PROJ_TN == MAIN_OUT
    return np.asarray(offs, np.int32)


def _proj_body(*refs, tile0, from_f32):
    if from_f32:
        _, x_ref, g_ref, wkv_ref, wg_ref, w_ref, o32_ref, o16_ref, xn_ref, wmain_out, wkv_out = refs
    else:
        x_ref, g_ref, wkv_ref, w_ref, o32_ref, o16_ref, xn_ref = refs

    @pl.when(pl.program_id(1) == 0)
    def _():
        x = x_ref[...]
        ms = jnp.mean(x * x, axis=-1, keepdims=True)
        xn_ref[...] = ((x * lax.rsqrt(ms + EPS)) * g_ref[...]).astype(BF16)
        if from_f32:
            wkv_out[:2 * KV_WIDTH, :] = wkv_ref[...].astype(BF16)
            wkv_out[2 * KV_WIDTH:2 * KV_WIDTH + ROW_ALIGN, :] = wg_ref[...].astype(BF16)
            wkv_out[2 * KV_WIDTH + ROW_ALIGN:, :] = jnp.zeros((LANES - ROW_ALIGN, D_MODEL), BF16)
            wkv = wkv_out[...]
        else:
            wkv = wkv_ref[...]
        o32_ref[...] = lax.dot_general(xn_ref[...], wkv, _NT, preferred_element_type=F32)

    group = (pl.program_id(1) + tile0) // (D_MODEL // PROJ_TN)
    is_sigmoid = (group == COL_OM) | (group == COL_GA) | (group == COL_GM)
    is_silu = (group == COL_ZA) | (group == COL_ZM)
    is_q = group == COL_QA

    def tile(act):
        w = w_ref[...].astype(BF16)
        if from_f32:
            wmain_out[...] = w
        acc = lax.dot_general(xn_ref[...], w, _NT, preferred_element_type=F32)
        o16_ref[...] = act(acc).astype(BF16)

    pl.when(is_sigmoid)(lambda: tile(_sigmoid))
    pl.when(is_silu)(lambda: tile(_silu))
    pl.when(is_q)(lambda: tile(lambda acc: acc * Q_SCALE))
    pl.when(jnp.logical_not(is_sigmoid | is_silu | is_q))(lambda: tile(lambda acc: acc))


def _project_first(x, g, w_t):
    rows = x.shape[0]
    assert SRC_KA % (2 * KV_WIDTH) == 0 and SRC_VA == SRC_KA + KV_WIDTH
    assert SRC_IG % ROW_ALIGN == 0 and SRC_FG == SRC_IG + M_HEADS and 2 * M_HEADS == ROW_ALIGN
    n_tiles = MAIN_OUT // PROJ_TN
    return pl.pallas_call(
        functools.partial(_proj_body, tile0=0, from_f32=True),
        grid_spec=pltpu.PrefetchScalarGridSpec(
            num_scalar_prefetch=1,
            grid=(1, n_tiles),
            in_specs=[
                pl.BlockSpec((rows, D_MODEL), lambda i, j, off: (0, 0)),
                pl.BlockSpec((1, D_MODEL), lambda i, j, off: (0, 0)),
                pl.BlockSpec((2 * KV_WIDTH, D_MODEL), lambda i, j, off: (SRC_KA // (2 * KV_WIDTH), 0),
                             pipeline_mode=pl.Buffered(1)),
                pl.BlockSpec((ROW_ALIGN, D_MODEL), lambda i, j, off: (SRC_IG // ROW_ALIGN, 0),
                             pipeline_mode=pl.Buffered(1)),
                pl.BlockSpec((pl.Element(PROJ_TN), pl.Element(D_MODEL)),
                             lambda i, j, off: (off[j] * ROW_ALIGN, 0)),
            ],
            out_specs=[
                pl.BlockSpec((rows, KV_OUT), lambda i, j, off: (0, 0)),
                pl.BlockSpec((rows, PROJ_TN), lambda i, j, off: (0, j)),
                pl.BlockSpec((rows, D_MODEL), lambda i, j, off: (0, 0)),
                pl.BlockSpec((PROJ_TN, D_MODEL), lambda i, j, off: (j, 0)),
                pl.BlockSpec((KV_OUT, D_MODEL), lambda i, j, off: (0, 0)),
            ],
        ),
        out_shape=[jax.ShapeDtypeStruct((rows, KV_OUT), F32),
                   jax.ShapeDtypeStruct((rows, MAIN_OUT), BF16),
                   jax.ShapeDtypeStruct((rows, D_MODEL), BF16),
                   jax.ShapeDtypeStruct((MAIN_OUT, D_MODEL), BF16),
                   jax.ShapeDtypeStruct((KV_OUT, D_MODEL), BF16)],
        compiler_params=pltpu.CompilerParams(
            dimension_semantics=("arbitrary", "arbitrary"), vmem_limit_bytes=VMEM_LIMIT),
        name="project_first",
    )(jnp.asarray(_main_row_offsets()), x, g, w_t, w_t, w_t)


def _project(x, g, wkv, wmain, tm, col0):
    rows = x.shape[0]
    assert rows % tm == 0
    tile0 = col0 * (D_MODEL // PROJ_TN)
    n_tiles = MAIN_OUT // PROJ_TN - tile0
    return pl.pallas_call(
        functools.partial(_proj_body, tile0=tile0, from_f32=False),
        grid=(rows // tm, n_tiles),
        in_specs=[
            pl.BlockSpec((tm, D_MODEL), lambda i, j: (i, 0)),
            pl.BlockSpec((1, D_MODEL), lambda i, j: (0, 0)),
            pl.BlockSpec((KV_OUT, D_MODEL), lambda i, j: (0, 0), pipeline_mode=pl.Buffered(1)),
            pl.BlockSpec((PROJ_TN, D_MODEL), lambda i, j: (j + tile0, 0)),
        ],
        out_specs=[
            pl.BlockSpec((tm, KV_OUT), lambda i, j: (i, 0)),
            pl.BlockSpec((tm, PROJ_TN), lambda i, j: (i, j)),
            pl.BlockSpec((tm, D_MODEL), lambda i, j: (i, 0)),
        ],
        out_shape=[jax.ShapeDtypeStruct((rows, KV_OUT), F32),
                   jax.ShapeDtypeStruct((rows, n_tiles * PROJ_TN), BF16),
                   jax.ShapeDtypeStruct((rows, D_MODEL), BF16)],
        compiler_params=pltpu.CompilerParams(
            dimension_semantics=("parallel", "arbitrary"), vmem_limit_bytes=VMEM_LIMIT),
        name="project",
    )(x, g, wkv, wmain)


def _proj_t_body(wq_ref, wv_ref, xn_ref, o_ref):
    xn = xn_ref[...]
    q_t = lax.dot_general(wq_ref[...], xn, _NT, preferred_element_type=F32) * Q_SCALE
    o_ref[:ATTN_WIDTH, :] = q_t.astype(BF16)
    o_ref[ATTN_WIDTH:, :] = lax.dot_general(wv_ref[...], xn, _NT, preferred_element_type=F32).astype(BF16)


def _project_t(wmain, wkv, xn, tm):
    rows = xn.shape[0]
    assert rows % tm == 0
    return pl.pallas_call(
        _proj_t_body,
        grid=(rows // tm,),
        in_specs=[pl.BlockSpec((ATTN_WIDTH, D_MODEL), lambda i: (COL_QA, 0), pipeline_mode=pl.Buffered(1)),
                  pl.BlockSpec((KV_WIDTH, D_MODEL), lambda i: (1, 0), pipeline_mode=pl.Buffered(1)),
                  pl.BlockSpec((tm, D_MODEL), lambda i: (i, 0))],
        out_specs=pl.BlockSpec((QV_ROWS, tm), lambda i: (0, i)),
        out_shape=jax.ShapeDtypeStruct((QV_ROWS, rows), BF16),
        compiler_params=pltpu.CompilerParams(
            dimension_semantics=("parallel",), vmem_limit_bytes=VMEM_LIMIT),
        name="project_t",
    )(wmain, wkv, xn)


def _attn_pair_setup(p, ka_ref, kb_ref, km_ref, vta_ref, vtb_ref, vtm_ref):
    k = jnp.concatenate([ka_ref[...], kb_ref[...], km_ref[...]], axis=0).astype(BF16)
    vt = jnp.concatenate([vta_ref[...], vtb_ref[...], vtm_ref[...]], axis=1)
    odd = (lax.broadcasted_iota(jnp.int32, (CHUNK, GQA_GROUP * PAIR), 1) % PAIR) >= CHUNK
    first = p == 0
    return k, vt, (odd | first, first, jnp.logical_not(odd))


def _attn_pair_probs(setup, h, qt_ref, sink_ref):
    k, vt, (mask_a, mask_b, mask_d) = setup
    kh = k[:, h * HEAD_DIM:(h + 1) * HEAD_DIM]
    rows = [(h * GQA_GROUP + g) * HEAD_DIM for g in range(GQA_GROUP)]
    qth = jnp.concatenate([qt_ref[r:r + HEAD_DIM, :] for r in rows], axis=1)
    st = jnp.dot(kh, qth, preferred_element_type=F32)
    st = jnp.concatenate([
        jnp.where(mask_a, NEG_INF, st[:CHUNK]),
        jnp.where(mask_b, NEG_INF, st[CHUNK:2 * CHUNK]),
        st[2 * CHUNK:3 * CHUNK],
        jnp.where(mask_d, NEG_INF, st[3 * CHUNK:4 * CHUNK]),
        st[4 * CHUNK:]], axis=0)
    sink = sink_ref[h]
    m = jnp.maximum(jnp.max(st, axis=0, keepdims=True), sink)
    return jnp.exp(st - m).astype(BF16), jnp.exp(sink - m)


def _attn_pair_out(setup, h, probs):
    _, vt, _ = setup
    e, e_sink = probs
    ones = jnp.ones((ONES_ROWS, PAIR_KEYS), BF16)
    vth = jnp.concatenate([vt[h * HEAD_DIM:(h + 1) * HEAD_DIM, :], ones], axis=0)
    ot = jnp.dot(vth, e, preferred_element_type=F32)
    denom = ot[HEAD_DIM:HEAD_DIM + 1, :] + e_sink
    ot = ot[:HEAD_DIM, :] * (1.0 / denom)
    return jnp.concatenate([ot[:, g * PAIR:(g + 1) * PAIR] for g in range(GQA_GROUP)], axis=0).T


def _attend_heads(q, k, v, sink_ref, o_ref, lq):
    probs = []
    for h in range(N_KV_HEADS):
        kh = k[:, h * HEAD_DIM:(h + 1) * HEAD_DIM]
        heads = [h * GQA_GROUP + g for g in range(GQA_GROUP)]
        qh = jnp.concatenate([q[:, n * HEAD_DIM:(n + 1) * HEAD_DIM] for n in heads], axis=0)
        s = lax.dot_general(qh, kh, _NT, preferred_element_type=F32)
        sink = jnp.concatenate(
            [jnp.broadcast_to(sink_ref[n:n + 1, 0:1], (lq, 1)) for n in heads], axis=0)
        m = jnp.maximum(jnp.max(s, axis=-1, keepdims=True), sink)
        e = jnp.exp(s - m)
        probs.append((e.astype(BF16), jnp.sum(e, axis=-1, keepdims=True) + jnp.exp(sink - m)))
    for h in range(N_KV_HEADS):
        vh = v[:, h * HEAD_DIM:(h + 1) * HEAD_DIM]
        e, denom = probs[h]
        oh = jnp.dot(e, vh, preferred_element_type=F32) * (1.0 / denom)
        o_ref[:, h * GQA_GROUP * HEAD_DIM:(h + 1) * GQA_GROUP * HEAD_DIM] = jnp.concatenate(
            [oh[g * lq:(g + 1) * lq, :] for g in range(GQA_GROUP)], axis=1).astype(o_ref.dtype)


def _attn_sample_body(q_ref, ck_ref, cv_ref, kn_ref, vn_ref, km_ref, vm_ref, sink_ref, o_ref, *, ds, spb):
    for j in range(spb):
        rows = pl.ds(j * ds, ds)
        k = jnp.concatenate([ck_ref[j], kn_ref[rows, :], km_ref[...]], axis=0).astype(BF16)
        v = jnp.concatenate([cv_ref[j], vn_ref[rows, :], vm_ref[...]], axis=0).astype(BF16)
        _attend_heads(q_ref[rows, :], k, v, sink_ref, o_ref.at[rows], ds)


def _attend_sample(o16, o32, cache_k, cache_v, meta_block, sinks, n_streams, ds, spb):
    assert n_streams % spb == 0
    cache_rows = cache_k.shape[1]
    new = lambda col: pl.BlockSpec((spb * ds, KV_WIDTH), lambda s: (s, col))
    meta = lambda col: pl.BlockSpec((N_META, KV_WIDTH), lambda s: (meta_block, col))
    cache = pl.BlockSpec((spb, cache_rows, KV_WIDTH), lambda s: (s, 0, 0))
    return pl.pallas_call(
        functools.partial(_attn_sample_body, ds=ds, spb=spb),
        grid=(n_streams // spb,),
        in_specs=[
            pl.BlockSpec((spb * ds, ATTN_WIDTH), lambda s: (s, COL_QA)),
            cache, cache, new(0), new(1), meta(0), meta(1),
            pl.BlockSpec((N_Q_HEADS, LANES), lambda s: (0, 0)),
        ],
        out_specs=pl.BlockSpec((spb * ds, ATTN_WIDTH), lambda s: (s, 0)),
        out_shape=jax.ShapeDtypeStruct((n_streams * ds, ATTN_WIDTH), BF16),
        compiler_params=pltpu.CompilerParams(
            dimension_semantics=("parallel",), vmem_limit_bytes=VMEM_LIMIT),
        name="attend_sample",
    )(o16, cache_k, cache_v, o32, o32, o32, o32, sinks)


def _mlstm_body(q_ref, k_ref, v_ref, gt_ref, bias_ref, c0_ref, n0_ref, m0_ref,
                hn_ref, c_ref, n_ref, m_ref, *, blk, spb):
    c_ref[...] = c0_ref[...]
    n_ref[...] = n0_ref[...]
    m_ref[...] = m0_ref[...]
    for j in range(spb):
        rows = pl.ds(j * blk, blk)
        q, k, v, hn = q_ref.at[rows], k_ref.at[rows], v_ref.at[rows], hn_ref.at[rows]
        c, n, m = c_ref.at[j], n_ref.at[j], m_ref.at[j]
        setup = _mlstm_setup(gt_ref.at[rows], bias_ref, blk)
        scores = [_mlstm_scores(setup, h, q, k, m) for h in range(M_HEADS)]
        for h in range(M_HEADS):
            hv = _mlstm_out(setup, h, scores[h], q, v, c, n)
            hn[:, h * M_V_DIM:(h + 1) * M_V_DIM] = hv.astype(hn_ref.dtype)
        for h in range(M_HEADS):
            _mlstm_state(setup, h, scores[h], k, v, c, n, m, blk, None)


def _mlstm_setup(gt_ref, bias_ref, blk):
    x = gt_ref[...] + bias_ref[...]
    lane = lax.broadcasted_iota(jnp.int32, x.shape, 1)
    log_f = jnp.minimum(x, 0.0) - jnp.log(1.0 + jnp.exp(-jnp.abs(x)))
    gates = jnp.where(lane < M_HEADS, x, log_f)
    row = lax.broadcasted_iota(jnp.int32, (blk, blk), 0)
    col = lax.broadcasted_iota(jnp.int32, (blk, blk), 1)
    causal = row >= col
    csum = jnp.dot(causal.astype(F32), gates, precision=lax.Precision.HIGHEST,
                   preferred_element_type=F32)
    z = jnp.where(lane < M_HEADS, gates, csum)
    pad = (-blk) % LANES
    zsq = z if pad == 0 else jnp.concatenate([z, jnp.zeros((pad, LANES), F32)], axis=0)
    zt = zsq.T[:, :blk]
    return z, zt, causal


def _mlstm_scores(setup, h, q_ref, k_ref, m_ref):
    z, zt, causal = setup
    g_r = zt[h:h + 1, :] - zt[M_HEADS + h:M_HEADS + h + 1, :]
    m_prev = m_ref[h:h + 1, 0:1]
    d0 = jnp.where(causal, g_r, -jnp.inf)
    mm = jnp.maximum(m_prev, jnp.max(d0, axis=-1, keepdims=True))
    w = jnp.exp(d0 - mm)
    qh = q_ref[:, h * M_QK_DIM:(h + 1) * M_QK_DIM]
    kh = k_ref[:, h * M_QK_DIM:(h + 1) * M_QK_DIM]
    qk = lax.dot_general(qh, kh, _NT, preferred_element_type=F32)
    wqk = w * (qk * MLSTM_K_SCALE)
    return m_prev, mm, wqk.astype(BF16), jnp.sum(wqk, axis=-1, keepdims=True)


def _mlstm_finish(setup, h, scores, q_ref, k_ref, v_ref, c_ref, n_ref, m_ref, blk, live):
    hv = _mlstm_out(setup, h, scores, q_ref, v_ref, c_ref, n_ref)
    _mlstm_state(setup, h, scores, k_ref, v_ref, c_ref, n_ref, m_ref, blk, live)
    return hv


def _mlstm_out(setup, h, scores, q_ref, v_ref, c_ref, n_ref):
    z, _, _ = setup
    m_prev, mm, wqk, den_intra = scores
    m_t = z[:, M_HEADS + h:M_HEADS + h + 1] + mm
    a = jnp.exp(m_prev - mm)
    qh = q_ref[:, h * M_QK_DIM:(h + 1) * M_QK_DIM]
    vh = v_ref[:, h * M_V_DIM:(h + 1) * M_V_DIM]
    num = a * jnp.dot(qh, c_ref[h].astype(BF16), preferred_element_type=F32) + jnp.dot(
        wqk, vh, preferred_element_type=F32)
    den = a * jnp.sum(qh.astype(F32) * n_ref[h:h + 1, :], axis=-1, keepdims=True) + den_intra
    hv = num * (1.0 / jnp.maximum(jnp.abs(den), jnp.exp(-m_t)))
    return hv * lax.rsqrt(jnp.mean(hv * hv, axis=-1, keepdims=True) + EPS)


def _mlstm_state(setup, h, scores, k_ref, v_ref, c_ref, n_ref, m_ref, blk, live):
    z, _, _ = setup
    m_prev, mm, _, _ = scores
    keep = (lambda new, old: new) if live is None else (lambda new, old: jnp.where(live, new, old))
    li_c = z[:, h:h + 1]
    b_c = z[:, M_HEADS + h:M_HEADS + h + 1]
    m_t = b_c + mm
    kh = k_ref[:, h * M_QK_DIM:(h + 1) * M_QK_DIM]
    vh = v_ref[:, h * M_V_DIM:(h + 1) * M_V_DIM]
    c_h = c_ref[h]
    n_h = n_ref[h:h + 1, :]

    m_new = m_t[blk - 1:blk, :]
    b_last = b_c[blk - 1:blk, :]
    ws = jnp.exp(b_last - b_c + li_c - m_new) * MLSTM_K_SCALE
    a_last = jnp.exp(b_last + m_prev - m_new)
    vs = (vh.astype(F32) * ws).astype(BF16)
    ktv = lax.dot_general(kh, vs, (((0,), (0,)), ((), ())), preferred_element_type=F32)
    c_ref[h] = keep(a_last * c_h + ktv, c_h)
    n_ref[h:h + 1, :] = keep(a_last * n_h + jnp.sum(kh.astype(F32) * ws, axis=0, keepdims=True), n_h)
    m_ref[h:h + 1, :] = keep(jnp.broadcast_to(m_new, (1, LANES)), m_ref[h:h + 1, :])


def _mlstm(o16, o32, bias, c0, n0, m0, *, blk, n_streams, spb, first_block, col0):
    assert n_streams % spb == 0 and first_block % spb == 0
    rb = lambda s: first_block // spb + s
    qk_blk = 2 * (COL_QKM - col0)
    rows = spb * blk
    return pl.pallas_call(
        functools.partial(_mlstm_body, blk=blk, spb=spb),
        grid=(n_streams // spb,),
        in_specs=[
            pl.BlockSpec((rows, M_QK_WIDTH), lambda s: (rb(s), qk_blk)),
            pl.BlockSpec((rows, M_QK_WIDTH), lambda s: (rb(s), qk_blk + 1)),
            pl.BlockSpec((rows, M_V_WIDTH), lambda s: (rb(s), COL_VM - col0)),
            pl.BlockSpec((rows, LANES), lambda s: (rb(s), GATE_BLOCK)),
            pl.BlockSpec((1, LANES), lambda s: (0, 0)),
            pl.BlockSpec((spb, M_HEADS, M_QK_DIM, M_V_DIM), lambda s: (s, 0, 0, 0)),
            pl.BlockSpec((spb, M_HEADS, M_QK_DIM), lambda s: (s, 0, 0)),
            pl.BlockSpec((spb, M_HEADS, LANES), lambda s: (s, 0, 0)),
        ],
        out_specs=[
            pl.BlockSpec((rows, M_V_WIDTH), lambda s: (s, 0)),
            pl.BlockSpec((spb, M_HEADS, M_QK_DIM, M_V_DIM), lambda s: (s, 0, 0, 0)),
            pl.BlockSpec((spb, M_HEADS, M_QK_DIM), lambda s: (s, 0, 0)),
            pl.BlockSpec((spb, M_HEADS, LANES), lambda s: (s, 0, 0)),
        ],
        out_shape=[
            jax.ShapeDtypeStruct((n_streams * blk, M_V_WIDTH), BF16),
            jax.ShapeDtypeStruct((n_streams, M_HEADS, M_QK_DIM, M_V_DIM), F32),
            jax.ShapeDtypeStruct((n_streams, M_HEADS, M_QK_DIM), F32),
            jax.ShapeDtypeStruct((n_streams, M_HEADS, LANES), F32),
        ],
        compiler_params=pltpu.CompilerParams(
            dimension_semantics=("parallel",), vmem_limit_bytes=VMEM_LIMIT),
        name="mlstm_blk%d" % blk,
    )(o16, o16, o16, o32, bias, c0, n0, m0)


def _merge_body(x_ref, oa_ref, hn_ref, za_ref, om_ref, zm_ref, ga_ref, gm_ref, gmh_ref,
                wpa_ref, wpm_ref, wout_ref, gf_ref, y_ref, a_scr, m_scr, acc_scr):
    j = pl.program_id(0)

    @pl.when(j == 0)
    def _():
        a_scr[...] = (oa_ref[...].astype(F32) * za_ref[...].astype(F32)).astype(BF16)
        m_scr[...] = ((hn_ref[...].astype(F32) * gmh_ref[...]) * om_ref[...].astype(F32)
                      * zm_ref[...].astype(F32)).astype(BF16)
        acc_scr[...] = x_ref[...]

    ya = jnp.dot(a_scr[...], wpa_ref[...], preferred_element_type=F32)
    ym = jnp.dot(m_scr[...], wpm_ref[...], preferred_element_type=F32)
    merged = ga_ref[...].astype(F32) * ya + gm_ref[...].astype(F32) * ym
    acc_scr[...] += jnp.dot(merged.astype(BF16), wout_ref[...], preferred_element_type=F32)

    @pl.when(j == pl.num_programs(0) - 1)
    def _():
        xo = acc_scr[...]
        ms = jnp.mean(xo * xo, axis=-1, keepdims=True)
        y_ref[...] = (xo * lax.rsqrt(ms + EPS)) * gf_ref[...]


def _merge(x, o_att, hn, o16, g_mh, w_pa, w_pm, w_out, g_final, rows, col0):
    n_chunks = D_MODEL // MERGE_TN
    per_group = D_MODEL // MERGE_TN
    full = lambda col: pl.BlockSpec((rows, D_MODEL), lambda j: (0, col))
    gate = lambda col: pl.BlockSpec((rows, MERGE_TN), lambda j: (0, col * per_group + j))
    vec = pl.BlockSpec((1, D_MODEL), lambda j: (0, 0))
    w_cols = pl.BlockSpec((D_MODEL, MERGE_TN), lambda j: (0, j))
    w_rows = pl.BlockSpec((MERGE_TN, D_MODEL), lambda j: (j, 0))
    return pl.pallas_call(
        _merge_body,
        grid=(n_chunks,),
        in_specs=[full(0), full(0), full(0), full(COL_ZA - col0), full(COL_OM - col0), full(COL_ZM - col0),
                  gate(COL_GA - col0), gate(COL_GM - col0), vec, w_cols, w_cols, w_rows, vec],
        out_specs=full(0),
        out_shape=jax.ShapeDtypeStruct((rows, D_MODEL), F32),
        scratch_shapes=[pltpu.VMEM((rows, D_MODEL), BF16), pltpu.VMEM((rows, D_MODEL), BF16),
                        pltpu.VMEM((rows, D_MODEL), F32)],
        compiler_params=pltpu.CompilerParams(
            dimension_semantics=("arbitrary",), vmem_limit_bytes=VMEM_LIMIT),
        name="merge",
    )(x, o_att, hn, o16, o16, o16, o16, o16, g_mh, w_pa, w_pm, w_out, g_final)


def _mixer_body(qt_ref, ka_ref, kb_ref, km_ref, vta_ref, vtb_ref, vtm_ref, sink_ref,
                q_ref, k_ref, v_ref, gt_ref, bias_ref, c0_ref, n0_ref, m0_ref,
                x_ref, za_ref, om_ref, zm_ref, ga_ref, gm_ref, gmh_ref, wpa_ref, wpm_ref, wout_ref, gf_ref,
                y_ref, c_ref, n_ref, m_ref, a_scr, m_scr, *, n_blk):
    c = pl.program_id(0)

    @pl.when(c == 0)
    def _():
        c_ref[...] = c0_ref[...]
        n_ref[...] = n0_ref[...]
        m_ref[...] = m0_ref[...]
        a_scr[...] = jnp.zeros(a_scr.shape, a_scr.dtype)
        m_scr[...] = jnp.zeros(m_scr.shape, m_scr.dtype)

    wr = c % MIX_SLOTS
    rd = 1 - wr
    blk = jnp.minimum(c, n_blk - 1)
    live = c < n_blk
    width = GQA_GROUP * HEAD_DIM
    heads_per_piece = width // M_V_DIM
    a_in = a_scr[rd]
    m_in = m_scr[rd]

    def mlstm_piece(h):
        hc = slice(h * M_V_DIM, (h + 1) * M_V_DIM)
        hv = _mlstm_finish(cell, h, scores[h], q_ref, k_ref, v_ref, c_ref, n_ref, m_ref, PAIR, live)
        m_scr[wr, :, hc] = ((hv * gmh_ref[:, hc]) * om_ref[:, hc].astype(F32)
                            * zm_ref[:, hc].astype(F32)).astype(BF16)

    def attn_piece(h, probs):
        cols = slice(h * width, (h + 1) * width)
        oa = _attn_pair_out(attn, h, probs)
        a_scr[wr, :, cols] = (oa * za_ref[:, cols].astype(F32)).astype(BF16)

    attn = _attn_pair_setup(blk, ka_ref, kb_ref, km_ref, vta_ref, vtb_ref, vtm_ref)
    cell = _mlstm_setup(gt_ref, bias_ref, PAIR)
    scores = [_mlstm_scores(cell, h, q_ref, k_ref, m_ref) for h in range(M_HEADS)]
    probs = _attn_pair_probs(attn, 0, qt_ref, sink_ref)
    merged = []
    for i in range(N_KV_HEADS):
        cols = slice(i * width, (i + 1) * width)
        ya = jnp.dot(a_in, wpa_ref[:, cols], preferred_element_type=F32)
        mlstm_piece(heads_per_piece * i)
        next_probs = _attn_pair_probs(attn, i + 1, qt_ref, sink_ref) if i + 1 < N_KV_HEADS else None
        ym = jnp.dot(m_in, wpm_ref[:, cols], preferred_element_type=F32)
        mlstm_piece(heads_per_piece * i + 1)
        merged.append((ga_ref[:, cols].astype(F32) * ya + gm_ref[:, cols].astype(F32) * ym).astype(BF16))
        if i + 1 == N_KV_HEADS:
            xo = x_ref[...] + jnp.dot(jnp.concatenate(merged, axis=1), wout_ref[...],
                                      preferred_element_type=F32)
        attn_piece(i, probs)
        probs = next_probs
    ms = jnp.mean(xo * xo, axis=-1, keepdims=True)
    y_ref[...] = (xo * lax.rsqrt(ms + EPS)) * gf_ref[...]


def _mixer(x, qvt, o32, o16, k_meta, vt_meta, sinks_t, bias, c0, n0, m0, g_mh, w_pa, w_pm, w_out, g_final,
           col0):
    seq = x.shape[0]
    assert seq % PAIR == 0
    n_blk = seq // PAIR
    cur = lambda c: jnp.minimum(c, n_blk - 1)
    prv = lambda c: jnp.maximum(cur(c) - 1, 0)
    mrg = lambda c: jnp.maximum(c - 1, 0)
    v_row = ATTN_WIDTH // KV_WIDTH
    qk_blk = 2 * (COL_QKM - col0)
    tile = lambda col: pl.BlockSpec((PAIR, D_MODEL), lambda c: (mrg(c), col))
    tile_cur = lambda col: pl.BlockSpec((PAIR, D_MODEL), lambda c: (cur(c), col))
    vec = pl.BlockSpec((1, D_MODEL), lambda c: (0, 0))
    weight = pl.BlockSpec((D_MODEL, D_MODEL), lambda c: (0, 0), pipeline_mode=pl.Buffered(1))
    state = lambda *blk: pl.BlockSpec((None,) + blk, lambda c: (0,) * (len(blk) + 1))
    return pl.pallas_call(
        functools.partial(_mixer_body, n_blk=n_blk),
        grid=(n_blk + 1,),
        in_specs=[
            pl.BlockSpec((ATTN_WIDTH, PAIR), lambda c: (0, cur(c))),
            pl.BlockSpec((PAIR, KV_WIDTH), lambda c: (prv(c), 0)),
            pl.BlockSpec((PAIR, KV_WIDTH), lambda c: (cur(c), 0)),
            pl.BlockSpec((N_META, KV_WIDTH), lambda c: (0, 0)),
            pl.BlockSpec((KV_WIDTH, PAIR), lambda c: (v_row, prv(c))),
            pl.BlockSpec((KV_WIDTH, PAIR), lambda c: (v_row, cur(c))),
            pl.BlockSpec((KV_WIDTH, N_META), lambda c: (0, 0)),
            pl.BlockSpec((N_KV_HEADS, 1, GQA_GROUP * PAIR), lambda c: (0, 0, 0)),
            pl.BlockSpec((PAIR, M_QK_WIDTH), lambda c: (cur(c), qk_blk)),
            pl.BlockSpec((PAIR, M_QK_WIDTH), lambda c: (cur(c), qk_blk + 1)),
            pl.BlockSpec((PAIR, M_V_WIDTH), lambda c: (cur(c), COL_VM - col0)),
            pl.BlockSpec((PAIR, LANES), lambda c: (cur(c), GATE_BLOCK)),
            pl.BlockSpec((1, LANES), lambda c: (0, 0)),
            state(M_HEADS, M_QK_DIM, M_V_DIM), state(M_HEADS, M_QK_DIM), state(M_HEADS, LANES),
            tile(0), tile_cur(COL_ZA - col0), tile_cur(COL_OM - col0), tile_cur(COL_ZM - col0),
            tile(COL_GA - col0), tile(COL_GM - col0), vec, weight, weight, weight, vec,
        ],
        out_specs=[
            tile(0),
            state(M_HEADS, M_QK_DIM, M_V_DIM), state(M_HEADS, M_QK_DIM), state(M_HEADS, LANES),
        ],
        out_shape=[
            jax.ShapeDtypeStruct((seq, D_MODEL), F32),
            jax.ShapeDtypeStruct((1, M_HEADS, M_QK_DIM, M_V_DIM), F32),
            jax.ShapeDtypeStruct((1, M_HEADS, M_QK_DIM), F32),
            jax.ShapeDtypeStruct((1, M_HEADS, LANES), F32),
        ],
        scratch_shapes=[pltpu.VMEM((MIX_SLOTS, PAIR, ATTN_WIDTH), BF16),
                        pltpu.VMEM((MIX_SLOTS, PAIR, M_V_WIDTH), BF16)],
        compiler_params=pltpu.CompilerParams(
            dimension_semantics=("arbitrary",), vmem_limit_bytes=VMEM_LIMIT),
        name="mixer",
    )(qvt, o32, o32, k_meta, qvt, qvt, vt_meta, sinks_t,
      o16, o16, o16, o32, bias, c0, n0, m0,
      x, o16, o16, o16, o16, o16, g_mh, w_pa, w_pm, w_out, g_final)


def kernel(x_prompt, x_sample, cache_k, cache_v, state_C, state_n, state_m, meta_tokens, g_norm, w_in,
           b_igate, b_fgate, attn_sinks, g_mhnorm, w_pa, w_pm, w_out, g_final):
    batch, seq, _ = x_prompt.shape
    db, ds, _ = x_sample.shape
    depth = w_in.shape[0]
    assert batch == 1 and depth == 1 and ds == N_META
    cache_rows = cache_k.shape[2]

    w_t = w_in[0].T
    g_in = g_norm[0].reshape(1, D_MODEL)
    bias = jnp.concatenate([b_igate[0], b_fgate[0], jnp.zeros((LANES - 2 * M_HEADS,), F32)]).reshape(1, LANES)
    sinks = jnp.broadcast_to(attn_sinks[0][:, None], (N_Q_HEADS, LANES))
    sinks_t = jnp.broadcast_to(attn_sinks[0].reshape(N_KV_HEADS, 1, GQA_GROUP, 1),
                               (N_KV_HEADS, 1, GQA_GROUP, PAIR)).reshape(N_KV_HEADS, 1, GQA_GROUP * PAIR)
    g_mh = g_mhnorm[0].reshape(1, M_V_WIDTH)
    g_fin = g_final.reshape(1, D_MODEL)
    wpa, wpm, wout = w_pa[0].astype(BF16), w_pm[0].astype(BF16), w_out[0].astype(BF16)

    xp = x_prompt.reshape(seq, D_MODEL)
    xs_rows = db * ds
    x_small = jnp.concatenate([x_sample.reshape(xs_rows, D_MODEL), meta_tokens.astype(F32)], axis=0)
    meta_block = xs_rows // N_META
    p_tm = min(seq, 1024)
    s32, s16, _, wmain, wkv = _project_first(x_small, g_in, w_t)
    p32, p16, xn_p = _project(xp, g_in, wkv, wmain, tm=p_tm, col0=COL_ZA)
    qvt = _project_t(wmain, wkv, xn_p, tm=p_tm)

    zeros = lambda *shape: jnp.zeros(shape, F32)
    _, c_meta, n_meta, m_meta = _mlstm(
        s16, s32, bias, zeros(1, M_HEADS, M_QK_DIM, M_V_DIM), zeros(1, M_HEADS, M_QK_DIM),
        zeros(1, M_HEADS, LANES), blk=N_META, n_streams=1, spb=1, first_block=meta_block, col0=COL_QA)

    k_meta = s32[xs_rows:, :KV_WIDTH]
    vt_meta = s32[xs_rows:, KV_WIDTH:2 * KV_WIDTH].T.astype(BF16)
    y_p, c_p, n_p, m_p = _mixer(xp, qvt, p32, p16, k_meta, vt_meta, sinks_t, bias, c_meta, n_meta, m_meta,
                                g_mh, wpa, wpm, wout, g_fin, col0=COL_ZA)

    ck = cache_k[0].reshape(db, cache_rows, KV_WIDTH)
    cv = cache_v[0].reshape(db, cache_rows, KV_WIDTH)
    spb = SAMPLE_STREAMS_PER_STEP if db % SAMPLE_STREAMS_PER_STEP == 0 else 1
    oa_s = _attend_sample(s16, s32, ck, cv, meta_block, sinks, db, ds, spb)
    m0_s = jnp.broadcast_to(state_m[0][:, :, None], (db, M_HEADS, LANES))
    hn_s, c_s, n_s, m_s = _mlstm(s16, s32, bias, state_C[0], state_n[0], m0_s, blk=ds, n_streams=db,
                                 spb=spb, first_block=0, col0=COL_QA)
    y_s = _merge(x_small, oa_s, hn_s, s16, g_mh, wpa, wpm, wout, g_fin, rows=xs_rows, col0=COL_QA)

    kv_shape = (1, batch, cache_rows, N_KV_HEADS, HEAD_DIM)
    k_p = p32[seq - cache_rows:, :KV_WIDTH].reshape(kv_shape)
    v_p = p32[seq - cache_rows:, KV_WIDTH:2 * KV_WIDTH].reshape(kv_shape)
    new_k = s32[:xs_rows, :KV_WIDTH].reshape(db, ds, KV_WIDTH)
    new_v = s32[:xs_rows, KV_WIDTH:2 * KV_WIDTH].reshape(db, ds, KV_WIDTH)
    skv_shape = (1, db, cache_rows, N_KV_HEADS, HEAD_DIM)
    k_s = jnp.concatenate([ck, new_k], axis=1)[:, -cache_rows:].reshape(skv_shape)
    v_s = jnp.concatenate([cv, new_v], axis=1)[:, -cache_rows:].reshape(skv_shape)

    return (y_p.reshape(batch, seq, D_MODEL), y_s.reshape(db, ds, D_MODEL),
            k_p, v_p, c_p[None], n_p[None], m_p[None, :, :, 0],
            k_s, v_s, c_s[None], n_s[None], m_s[None, :, :, 0])
```

```python
import functools
import math

import jax
import jax.numpy as jnp
import numpy as np
from jax import lax
from jax.experimental import pallas as pl
from jax.experimental.pallas import tpu as pltpu

F32 = jnp.float32
BF16 = jnp.bfloat16

D_MODEL = 2048
CHUNK = 64
N_META = 16
HEAD_DIM = 64
N_Q_HEADS = 32
N_KV_HEADS = 4
GQA_GROUP = N_Q_HEADS // N_KV_HEADS
WINDOW = 128
WIN_CHUNKS = WINDOW // CHUNK
ATTN_WIDTH = N_Q_HEADS * HEAD_DIM
KV_WIDTH = N_KV_HEADS * HEAD_DIM
M_HEADS = 8
M_QK_DIM = 128
M_V_DIM = 256
M_QK_WIDTH = M_HEADS * M_QK_DIM
M_V_WIDTH = M_HEADS * M_V_DIM
EPS = 1e-6
NEG_INF = -1e30
COL_SIZES = (ATTN_WIDTH, KV_WIDTH, KV_WIDTH, ATTN_WIDTH, M_QK_WIDTH, M_QK_WIDTH, M_V_WIDTH, M_V_WIDTH,
             M_HEADS, M_HEADS, M_V_WIDTH, D_MODEL, D_MODEL)
COL_STARTS = tuple(int(v) for v in np.cumsum((0,) + COL_SIZES[:-1]))
(SRC_QA, SRC_KA, SRC_VA, SRC_ZA, SRC_QM, SRC_KM, SRC_VM, SRC_OM, SRC_IG, SRC_FG, SRC_ZM, SRC_GA,
 SRC_GM) = COL_STARTS

LANES = 128
VMEM_LIMIT = 56 * 1024 * 1024

KV_OUT = 2 * KV_WIDTH + LANES
GATE_BLOCK = 2 * KV_WIDTH // LANES
MAIN_OUT = 8 * D_MODEL
COL_QA, COL_ZA, COL_QKM, COL_VM, COL_OM, COL_ZM, COL_GA, COL_GM = range(8)
PROJ_TN = 1024
ROW_ALIGN = 16
MLSTM_K_SCALE = M_QK_DIM ** -0.5
Q_SCALE = 1.0 / math.sqrt(HEAD_DIM)
QV_ROWS = ATTN_WIDTH + KV_WIDTH
PAIR = 2 * CHUNK
PAIR_KEYS = 2 * PAIR + N_META
ONES_ROWS = 16
MIX_SLOTS = 2
MERGE_TN = 512


def _sigmoid(x):
    return 0.5 * jnp.tanh(0.5 * x) + 0.5


def _silu(x):
    return x * _sigmoid(x)


_NT = (((1,), (1,)), ((), ()))


def _main_row_offsets():
    groups = [(SRC_QA, ATTN_WIDTH), (SRC_ZA, ATTN_WIDTH), (SRC_QM, M_QK_WIDTH), (SRC_KM, M_QK_WIDTH),
              (SRC_VM, M_V_WIDTH), (SRC_OM, M_V_WIDTH), (SRC_ZM, M_V_WIDTH), (SRC_GA, D_MODEL),
              (SRC_GM, D_MODEL)]
    offs = []
    for start, width in groups:
        assert start % ROW_ALIGN == 0 and width % PROJ_TN == 0
        offs += [(start + b * PROJ_TN) // ROW_ALIGN for b in range(width // PROJ_TN)]
    assert len(offs) * PROJ_TN == MAIN_OUT
    return np.asarray(offs, np.int32)


def _proj_body(*refs, tile0, from_f32):
    if from_f32:
        _, x_ref, g_ref, wkv_ref, wg_ref, w_ref, o32_ref, o16_ref, xn_ref, wmain_out, wkv_out = refs
    else:
        x_ref, g_ref, wkv_ref, w_ref, o32_ref, o16_ref, xn_ref = refs

    @pl.when(pl.program_id(1) == 0)
    def _():
        x = x_ref[...]
        ms = jnp.mean(x * x, axis=-1, keepdims=True)
        xn_ref[...] = ((x * lax.rsqrt(ms + EPS)) * g_ref[...]).astype(BF16)
        if from_f32:
            wkv_out[:2 * KV_WIDTH, :] = wkv_ref[...].astype(BF16)
            wkv_out[2 * KV_WIDTH:2 * KV_WIDTH + ROW_ALIGN, :] = wg_ref[...].astype(BF16)
            wkv_out[2 * KV_WIDTH + ROW_ALIGN:, :] = jnp.zeros((LANES - ROW_ALIGN, D_MODEL), BF16)
            wkv = wkv_out[...]
        else:
            wkv = wkv_ref[...]
        o32_ref[...] = lax.dot_general(xn_ref[...], wkv, _NT, preferred_element_type=F32)

    group = (pl.program_id(1) + tile0) // (D_MODEL // PROJ_TN)
    is_sigmoid = (group == COL_OM) | (group == COL_GA) | (group == COL_GM)
    is_silu = (group == COL_ZA) | (group == COL_ZM)
    is_q = group == COL_QA

    def tile(act):
        w = w_ref[...].astype(BF16)
        if from_f32:
            wmain_out[...] = w
        acc = lax.dot_general(xn_ref[...], w, _NT, preferred_element_type=F32)
        o16_ref[...] = act(acc).astype(BF16)

    pl.when(is_sigmoid)(lambda: tile(_sigmoid))
    pl.when(is_silu)(lambda: tile(_silu))
    pl.when(is_q)(lambda: tile(lambda acc: acc * Q_SCALE))
    pl.when(jnp.logical_not(is_sigmoid | is_silu | is_q))(lambda: tile(lambda acc: acc))


def _project_first(x, g, w_t):
    rows = x.shape[0]
    assert SRC_KA % (2 * KV_WIDTH) == 0 and SRC_VA == SRC_KA + KV_WIDTH
    assert SRC_IG % ROW_ALIGN == 0 and SRC_FG == SRC_IG + M_HEADS and 2 * M_HEADS == ROW_ALIGN
    n_tiles = MAIN_OUT // PROJ_TN
    return pl.pallas_call(
        functools.partial(_proj_body, tile0=0, from_f32=True),
        grid_spec=pltpu.PrefetchScalarGridSpec(
            num_scalar_prefetch=1,
            grid=(1, n_tiles),
            in_specs=[
                pl.BlockSpec((rows, D_MODEL), lambda i, j, off: (0, 0)),
                pl.BlockSpec((1, D_MODEL), lambda i, j, off: (0, 0)),
                pl.BlockSpec((2 * KV_WIDTH, D_MODEL), lambda i, j, off: (SRC_KA // (2 * KV_WIDTH), 0),
                             pipeline_mode=pl.Buffered(1)),
                pl.BlockSpec((ROW_ALIGN, D_MODEL), lambda i, j, off: (SRC_IG // ROW_ALIGN, 0),
                             pipeline_mode=pl.Buffered(1)),
                pl.BlockSpec((pl.Element(PROJ_TN), pl.Element(D_MODEL)),
                             lambda i, j, off: (off[j] * ROW_ALIGN, 0)),
            ],
            out_specs=[
                pl.BlockSpec((rows, KV_OUT), lambda i, j, off: (0, 0)),
                pl.BlockSpec((rows, PROJ_TN), lambda i, j, off: (0, j)),
                pl.BlockSpec((rows, D_MODEL), lambda i, j, off: (0, 0)),
                pl.BlockSpec((PROJ_TN, D_MODEL), lambda i, j, off: (j, 0)),
                pl.BlockSpec((KV_OUT, D_MODEL), lambda i, j, off: (0, 0)),
            ],
        ),
        out_shape=[jax.ShapeDtypeStruct((rows, KV_OUT), F32),
                   jax.ShapeDtypeStruct((rows, MAIN_OUT), BF16),
                   jax.ShapeDtypeStruct((rows, D_MODEL), BF16),
                   jax.ShapeDtypeStruct((MAIN_OUT, D_MODEL), BF16),
                   jax.ShapeDtypeStruct((KV_OUT, D_MODEL), BF16)],
        compiler_params=pltpu.CompilerParams(
            dimension_semantics=("arbitrary", "arbitrary"), vmem_limit_bytes=VMEM_LIMIT),
        name="project_first",
    )(jnp.asarray(_main_row_offsets()), x, g, w_t, w_t, w_t)


def _project(x, g, wkv, wmain, tm, col0):
    rows = x.shape[0]
    assert rows % tm == 0
    tile0 = col0 * (D_MODEL // PROJ_TN)
    n_tiles = MAIN_OUT // PROJ_TN - tile0
    return pl.pallas_call(
        functools.partial(_proj_body, tile0=tile0, from_f32=False),
        grid=(rows // tm, n_tiles),
        in_specs=[
            pl.BlockSpec((tm, D_MODEL), lambda i, j: (i, 0)),
            pl.BlockSpec((1, D_MODEL), lambda i, j: (0, 0)),
            pl.BlockSpec((KV_OUT, D_MODEL), lambda i, j: (0, 0), pipeline_mode=pl.Buffered(1)),
            pl.BlockSpec((PROJ_TN, D_MODEL), lambda i, j: (j + tile0, 0)),
        ],
        out_specs=[
            pl.BlockSpec((tm, KV_OUT), lambda i, j: (i, 0)),
            pl.BlockSpec((tm, PROJ_TN), lambda i, j: (i, j)),
            pl.BlockSpec((tm, D_MODEL), lambda i, j: (i, 0)),
        ],
        out_shape=[jax.ShapeDtypeStruct((rows, KV_OUT), F32),
                   jax.ShapeDtypeStruct((rows, n_tiles * PROJ_TN), BF16),
                   jax.ShapeDtypeStruct((rows, D_MODEL), BF16)],
        compiler_params=pltpu.CompilerParams(
            dimension_semantics=("parallel", "arbitrary"), vmem_limit_bytes=VMEM_LIMIT),
        name="project",
    )(x, g, wkv, wmain)


def _proj_t_body(wq_ref, wv_ref, xn_ref, o_ref):
    xn = xn_ref[...]
    q_t = lax.dot_general(wq_ref[...], xn, _NT, preferred_element_type=F32) * Q_SCALE
    o_ref[:ATTN_WIDTH, :] = q_t.astype(BF16)
    o_ref[ATTN_WIDTH:, :] = lax.dot_general(wv_ref[...], xn, _NT, preferred_element_type=F32).astype(BF16)


def _project_t(wmain, wkv, xn, tm):
    rows = xn.shape[0]
    assert rows % tm == 0
    return pl.pallas_call(
        _proj_t_body,
        grid=(rows // tm,),
        in_specs=[pl.BlockSpec((ATTN_WIDTH, D_MODEL), lambda i: (COL_QA, 0), pipeline_mode=pl.Buffered(1)),
                  pl.BlockSpec((KV_WIDTH, D_MODEL), lambda i: (1, 0), pipeline_mode=pl.Buffered(1)),
                  pl.BlockSpec((tm, D_MODEL), lambda i: (i, 0))],
        out_specs=pl.BlockSpec((QV_ROWS, tm), lambda i: (0, i)),
        out_shape=jax.ShapeDtypeStruct((QV_ROWS, rows), BF16),
        compiler_params=pltpu.CompilerParams(
            dimension_semantics=("parallel",), vmem_limit_bytes=VMEM_LIMIT),
        name="project_t",
    )(wmain, wkv, xn)


def _attn_pair_setup(p, ka_ref, kb_ref, km_ref, vta_ref, vtb_ref, vtm_ref):
    k = jnp.concatenate([ka_ref[...], kb_ref[...], km_ref[...]], axis=0).astype(BF16)
    vt = jnp.concatenate([vta_ref[...], vtb_ref[...], vtm_ref[...]], axis=1)
    odd = (lax.broadcasted_iota(jnp.int32, (CHUNK, GQA_GROUP * PAIR), 1) % PAIR) >= CHUNK
    first = p == 0
    return k, vt, (odd | first, first, jnp.logical_not(odd))


def _attn_pair_probs(setup, h, qt_ref, sink_ref):
    k, vt, (mask_a, mask_b, mask_d) = setup
    kh = k[:, h * HEAD_DIM:(h + 1) * HEAD_DIM]
    rows = [(h * GQA_GROUP + g) * HEAD_DIM for g in range(GQA_GROUP)]
    qth = jnp.concatenate([qt_ref[r:r + HEAD_DIM, :] for r in rows], axis=1)
    st = jnp.dot(kh, qth, preferred_element_type=F32)
    st = jnp.concatenate([
        jnp.where(mask_a, NEG_INF, st[:CHUNK]),
        jnp.where(mask_b, NEG_INF, st[CHUNK:2 * CHUNK]),
        st[2 * CHUNK:3 * CHUNK],
        jnp.where(mask_d, NEG_INF, st[3 * CHUNK:4 * CHUNK]),
        st[4 * CHUNK:]], axis=0)
    sink = sink_ref[h]
    m = jnp.maximum(jnp.max(st, axis=0, keepdims=True), sink)
    return jnp.exp(st - m).astype(BF16), jnp.exp(sink - m)


def _attn_pair_out(setup, h, probs):
    _, vt, _ = setup
    e, e_sink = probs
    ones = jnp.ones((ONES_ROWS, PAIR_KEYS), BF16)
    vth = jnp.concatenate([vt[h * HEAD_DIM:(h + 1) * HEAD_DIM, :], ones], axis=0)
    ot = jnp.dot(vth, e, preferred_element_type=F32)
    denom = ot[HEAD_DIM:HEAD_DIM + 1, :] + e_sink
    ot = ot[:HEAD_DIM, :] * (1.0 / denom)
    return jnp.concatenate([ot[:, g * PAIR:(g + 1) * PAIR] for g in range(GQA_GROUP)], axis=0).T


def _attend_probs(q, k, sink_ref, lq):
    probs = []
    for h in range(N_KV_HEADS):
        kh = k[:, h * HEAD_DIM:(h + 1) * HEAD_DIM]
        heads = [h * GQA_GROUP + g for g in range(GQA_GROUP)]
        qh = jnp.concatenate([q[:, n * HEAD_DIM:(n + 1) * HEAD_DIM] for n in heads], axis=0)
        s = lax.dot_general(qh, kh, _NT, preferred_element_type=F32)
        sink = jnp.concatenate(
            [jnp.broadcast_to(sink_ref[n:n + 1, 0:1], (lq, 1)) for n in heads], axis=0)
        m = jnp.maximum(jnp.max(s, axis=-1, keepdims=True), sink)
        e = jnp.exp(s - m)
        probs.append((e.astype(BF16), jnp.sum(e, axis=-1, keepdims=True) + jnp.exp(sink - m)))
    return probs


def _attend_values(probs, v, o_ref, lq):
    for h in range(N_KV_HEADS):
        vh = v[:, h * HEAD_DIM:(h + 1) * HEAD_DIM]
        e, denom = probs[h]
        oh = jnp.dot(e, vh, preferred_element_type=F32) * (1.0 / denom)
        o_ref[:, h * GQA_GROUP * HEAD_DIM:(h + 1) * GQA_GROUP * HEAD_DIM] = jnp.concatenate(
            [oh[g * lq:(g + 1) * lq, :] for g in range(GQA_GROUP)], axis=1).astype(o_ref.dtype)


def _attn_sample_body(q_ref, ck_ref, cv_ref, kn_ref, vn_ref, km_ref, vm_ref, sink_ref, o_ref):
    lq = q_ref.shape[0]
    k = jnp.concatenate([ck_ref[...], kn_ref[...], km_ref[...]], axis=0).astype(BF16)
    probs = _attend_probs(q_ref[...], k, sink_ref, lq)
    v = jnp.concatenate([cv_ref[...], vn_ref[...], vm_ref[...]], axis=0).astype(BF16)
    _attend_values(probs, v, o_ref, lq)


def _attend_sample(o16, o32, cache_k, cache_v, meta_block, sinks, n_streams, ds):
    cache_rows = cache_k.shape[1]
    new = lambda col: pl.BlockSpec((ds, KV_WIDTH), lambda s: (s, col))
    meta = lambda col: pl.BlockSpec((N_META, KV_WIDTH), lambda s: (meta_block, col))
    cache = pl.BlockSpec((None, cache_rows, KV_WIDTH), lambda s: (s, 0, 0))
    return pl.pallas_call(
        _attn_sample_body,
        grid=(n_streams,),
        in_specs=[
            pl.BlockSpec((ds, ATTN_WIDTH), lambda s: (s, COL_QA)),
            cache, cache, new(0), new(1), meta(0), meta(1),
            pl.BlockSpec((N_Q_HEADS, LANES), lambda s: (0, 0)),
        ],
        out_specs=pl.BlockSpec((ds, ATTN_WIDTH), lambda s: (s, 0)),
        out_shape=jax.ShapeDtypeStruct((n_streams * ds, ATTN_WIDTH), BF16),
        compiler_params=pltpu.CompilerParams(
            dimension_semantics=("parallel",), vmem_limit_bytes=VMEM_LIMIT),
        name="attend_sample",
    )(o16, cache_k, cache_v, o32, o32, o32, o32, sinks)


def _mlstm_body(q_ref, k_ref, v_ref, gt_ref, bias_ref, c0_ref, n0_ref, m0_ref,
                hn_ref, c_ref, n_ref, m_ref, *, blk):
    c_ref[...] = c0_ref[...]
    n_ref[...] = n0_ref[...]
    m_ref[...] = m0_ref[...]
    setup = _mlstm_setup(gt_ref, bias_ref, blk)
    scores = [_mlstm_scores(setup, h, q_ref, k_ref, m_ref) for h in range(M_HEADS)]
    for h in range(M_HEADS):
        hv = _mlstm_out(setup, h, scores[h], q_ref, v_ref, c_ref, n_ref)
        hn_ref[:, h * M_V_DIM:(h + 1) * M_V_DIM] = hv.astype(hn_ref.dtype)
    for h in range(M_HEADS):
        _mlstm_state(setup, h, scores[h], k_ref, v_ref, c_ref, n_ref, m_ref, blk, None)


def _mlstm_setup(gt_ref, bias_ref, blk):
    x = gt_ref[...] + bias_ref[...]
    lane = lax.broadcasted_iota(jnp.int32, x.shape, 1)
    log_f = jnp.minimum(x, 0.0) - jnp.log(1.0 + jnp.exp(-jnp.abs(x)))
    gates = jnp.where(lane < M_HEADS, x, log_f)
    row = lax.broadcasted_iota(jnp.int32, (blk, blk), 0)
    col = lax.broadcasted_iota(jnp.int32, (blk, blk), 1)
    causal = row >= col
    hi = gates.astype(BF16)
    rest = gates - hi.astype(F32)
    mid = rest.astype(BF16)
    lo = (rest - mid.astype(F32)).astype(BF16)
    parts = jnp.dot(causal.astype(BF16), jnp.concatenate([hi, mid, lo], axis=1), preferred_element_type=F32)
    csum = parts[:, :LANES] + parts[:, LANES:2 * LANES] + parts[:, 2 * LANES:]
    z = jnp.where(lane < M_HEADS, gates, csum)
    pad = (-blk) % LANES
    zsq = z if pad == 0 else jnp.concatenate([z, jnp.zeros((pad, LANES), F32)], axis=0)
    zt = zsq.T[:, :blk]
    return z, zt, causal


def _mlstm_scores(setup, h, q_ref, k_ref, m_ref):
    z, zt, causal = setup
    g_r = zt[h:h + 1, :] - zt[M_HEADS + h:M_HEADS + h + 1, :]
    m_prev = m_ref[h:h + 1, 0:1]
    d0 = jnp.where(causal, g_r, -jnp.inf)
    mm = jnp.maximum(m_prev, jnp.max(d0, axis=-1, keepdims=True))
    w = jnp.exp(d0 - mm)
    qh = q_ref[:, h * M_QK_DIM:(h + 1) * M_QK_DIM]
    kh = k_ref[:, h * M_QK_DIM:(h + 1) * M_QK_DIM]
    qk = lax.dot_general(qh, kh, _NT, preferred_element_type=F32)
    wqk = w * (qk * MLSTM_K_SCALE)
    return m_prev, mm, wqk.astype(BF16), jnp.sum(wqk, axis=-1, keepdims=True)


def _mlstm_finish(setup, h, scores, q_ref, k_ref, v_ref, c_ref, n_ref, m_ref, blk, live):
    hv = _mlstm_out(setup, h, scores, q_ref, v_ref, c_ref, n_ref)
    _mlstm_state(setup, h, scores, k_ref, v_ref, c_ref, n_ref, m_ref, blk, live)
    return hv


def _mlstm_out(setup, h, scores, q_ref, v_ref, c_ref, n_ref):
    z, _, _ = setup
    m_prev, mm, wqk, den_intra = scores
    m_t = z[:, M_HEADS + h:M_HEADS + h + 1] + mm
    a = jnp.exp(m_prev - mm)
    qh = q_ref[:, h * M_QK_DIM:(h + 1) * M_QK_DIM]
    vh = v_ref[:, h * M_V_DIM:(h + 1) * M_V_DIM]
    num = a * jnp.dot(qh, c_ref[h].astype(BF16), preferred_element_type=F32) + jnp.dot(
        wqk, vh, preferred_element_type=F32)
    den = a * jnp.sum(qh.astype(F32) * n_ref[h:h + 1, :], axis=-1, keepdims=True) + den_intra
    r = 1.0 / jnp.maximum(jnp.abs(den), jnp.exp(-m_t))
    rms_h = r * jnp.sqrt(jnp.mean(num * num, axis=-1, keepdims=True))
    return num * (r * lax.rsqrt(rms_h * rms_h + EPS))


def _mlstm_state(setup, h, scores, k_ref, v_ref, c_ref, n_ref, m_ref, blk, live):
    z, _, _ = setup
    m_prev, mm, _, _ = scores
    keep = (lambda new, old: new) if live is None else (lambda new, old: jnp.where(live, new, old))
    li_c = z[:, h:h + 1]
    b_c = z[:, M_HEADS + h:M_HEADS + h + 1]
    m_t = b_c + mm
    kh = k_ref[:, h * M_QK_DIM:(h + 1) * M_QK_DIM]
    vh = v_ref[:, h * M_V_DIM:(h + 1) * M_V_DIM]
    c_h = c_ref[h]
    n_h = n_ref[h:h + 1, :]

    m_new = m_t[blk - 1:blk, :]
    b_last = b_c[blk - 1:blk, :]
    ws = jnp.exp(b_last - b_c + li_c - m_new) * MLSTM_K_SCALE
    a_last = jnp.exp(b_last + m_prev - m_new)
    vs = (vh.astype(F32) * ws).astype(BF16)
    ktv = lax.dot_general(kh, vs, (((0,), (0,)), ((), ())), preferred_element_type=F32)
    c_ref[h] = keep(a_last * c_h + ktv, c_h)
    n_ref[h:h + 1, :] = keep(a_last * n_h + jnp.sum(kh.astype(F32) * ws, axis=0, keepdims=True), n_h)
    m_ref[h:h + 1, :] = keep(jnp.broadcast_to(m_new, (1, LANES)), m_ref[h:h + 1, :])


def _mlstm(o16, o32, bias, c0, n0, m0, *, blk, n_streams, first_block, col0):
    rb = lambda s: first_block + s
    qk_blk = 2 * (COL_QKM - col0)
    state = lambda *shape: pl.BlockSpec((None,) + shape, lambda s: (s,) + (0,) * len(shape))
    return pl.pallas_call(
        functools.partial(_mlstm_body, blk=blk),
        grid=(n_streams,),
        in_specs=[
            pl.BlockSpec((blk, M_QK_WIDTH), lambda s: (rb(s), qk_blk)),
            pl.BlockSpec((blk, M_QK_WIDTH), lambda s: (rb(s), qk_blk + 1)),
            pl.BlockSpec((blk, M_V_WIDTH), lambda s: (rb(s), COL_VM - col0)),
            pl.BlockSpec((blk, LANES), lambda s: (rb(s), GATE_BLOCK)),
            pl.BlockSpec((1, LANES), lambda s: (0, 0)),
            state(M_HEADS, M_QK_DIM, M_V_DIM), state(M_HEADS, M_QK_DIM), state(M_HEADS, LANES),
        ],
        out_specs=[
            pl.BlockSpec((blk, M_V_WIDTH), lambda s: (s, 0)),
            state(M_HEADS, M_QK_DIM, M_V_DIM), state(M_HEADS, M_QK_DIM), state(M_HEADS, LANES),
        ],
        out_shape=[
            jax.ShapeDtypeStruct((n_streams * blk, M_V_WIDTH), BF16),
            jax.ShapeDtypeStruct((n_streams, M_HEADS, M_QK_DIM, M_V_DIM), F32),
            jax.ShapeDtypeStruct((n_streams, M_HEADS, M_QK_DIM), F32),
            jax.ShapeDtypeStruct((n_streams, M_HEADS, LANES), F32),
        ],
        compiler_params=pltpu.CompilerParams(
            dimension_semantics=("parallel",), vmem_limit_bytes=VMEM_LIMIT),
        name="mlstm_blk%d" % blk,
    )(o16, o16, o16, o32, bias, c0, n0, m0)


def _merge_body(x_ref, oa_ref, hn_ref, za_ref, om_ref, zm_ref, ga_ref, gm_ref, gmh_ref,
                wpa_ref, wpm_ref, wout_ref, gf_ref, y_ref, a_scr, m_scr, acc_scr):
    j = pl.program_id(0)

    @pl.when(j == 0)
    def _():
        a_scr[...] = (oa_ref[...].astype(F32) * za_ref[...].astype(F32)).astype(BF16)
        m_scr[...] = ((hn_ref[...].astype(F32) * gmh_ref[...]) * om_ref[...].astype(F32)
                      * zm_ref[...].astype(F32)).astype(BF16)
        acc_scr[...] = x_ref[...]

    ya = jnp.dot(a_scr[...], wpa_ref[...], preferred_element_type=F32)
    ym = jnp.dot(m_scr[...], wpm_ref[...], preferred_element_type=F32)
    merged = ga_ref[...].astype(F32) * ya + gm_ref[...].astype(F32) * ym
    acc_scr[...] += jnp.dot(merged.astype(BF16), wout_ref[...], preferred_element_type=F32)

    @pl.when(j == pl.num_programs(0) - 1)
    def _():
        xo = acc_scr[...]
        ms = jnp.mean(xo * xo, axis=-1, keepdims=True)
        y_ref[...] = (xo * lax.rsqrt(ms + EPS)) * gf_ref[...]


def _merge(x, o_att, hn, o16, g_mh, w_pa, w_pm, w_out, g_final, rows, col0):
    n_chunks = D_MODEL // MERGE_TN
    per_group = D_MODEL // MERGE_TN
    full = lambda col: pl.BlockSpec((rows, D_MODEL), lambda j: (0, col))
    gate = lambda col: pl.BlockSpec((rows, MERGE_TN), lambda j: (0, col * per_group + j))
    vec = pl.BlockSpec((1, D_MODEL), lambda j: (0, 0))
    w_cols = pl.BlockSpec((D_MODEL, MERGE_TN), lambda j: (0, j))
    w_rows = pl.BlockSpec((MERGE_TN, D_MODEL), lambda j: (j, 0))
    return pl.pallas_call(
        _merge_body,
        grid=(n_chunks,),
        in_specs=[full(0), full(0), full(0), full(COL_ZA - col0), full(COL_OM - col0), full(COL_ZM - col0),
                  gate(COL_GA - col0), gate(COL_GM - col0), vec, w_cols, w_cols, w_rows, vec],
        out_specs=full(0),
        out_shape=jax.ShapeDtypeStruct((rows, D_MODEL), F32),
        scratch_shapes=[pltpu.VMEM((rows, D_MODEL), BF16), pltpu.VMEM((rows, D_MODEL), BF16),
                        pltpu.VMEM((rows, D_MODEL), F32)],
        compiler_params=pltpu.CompilerParams(
            dimension_semantics=("arbitrary",), vmem_limit_bytes=VMEM_LIMIT),
        name="merge",
    )(x, o_att, hn, o16, o16, o16, o16, o16, g_mh, w_pa, w_pm, w_out, g_final)


def _mixer_body(qt_ref, ka_ref, kb_ref, km_ref, vta_ref, vtb_ref, vtm_ref, sink_ref,
                q_ref, k_ref, v_ref, gt_ref, bias_ref, c0_ref, n0_ref, m0_ref,
                x_ref, za_ref, om_ref, zm_ref, ga_ref, gm_ref, gmh_ref, wpa_ref, wpm_ref, wout_ref, gf_ref,
                y_ref, c_ref, n_ref, m_ref, a_scr, m_scr, *, n_blk):
    c = pl.program_id(0)

    @pl.when(c == 0)
    def _():
        c_ref[...] = c0_ref[...]
        n_ref[...] = n0_ref[...]
        m_ref[...] = m0_ref[...]
        a_scr[...] = jnp.zeros(a_scr.shape, a_scr.dtype)
        m_scr[...] = jnp.zeros(m_scr.shape, m_scr.dtype)

    wr = c % MIX_SLOTS
    rd = 1 - wr
    blk = jnp.minimum(c, n_blk - 1)
    live = c < n_blk
    width = GQA_GROUP * HEAD_DIM
    heads_per_piece = width // M_V_DIM
    a_in = a_scr[rd]
    m_in = m_scr[rd]

    def mlstm_piece(h):
        hc = slice(h * M_V_DIM, (h + 1) * M_V_DIM)
        hv = _mlstm_finish(cell, h, scores[h], q_ref, k_ref, v_ref, c_ref, n_ref, m_ref, PAIR, live)
        m_scr[wr, :, hc] = ((hv * gmh_ref[:, hc]) * om_ref[:, hc].astype(F32)
                            * zm_ref[:, hc].astype(F32)).astype(BF16)

    def attn_piece(h, probs):
        cols = slice(h * width, (h + 1) * width)
        oa = _attn_pair_out(attn, h, probs)
        a_scr[wr, :, cols] = (oa * za_ref[:, cols].astype(F32)).astype(BF16)

    attn = _attn_pair_setup(blk, ka_ref, kb_ref, km_ref, vta_ref, vtb_ref, vtm_ref)
    cell = _mlstm_setup(gt_ref, bias_ref, PAIR)
    scores = [_mlstm_scores(cell, h, q_ref, k_ref, m_ref) for h in range(M_HEADS)]
    probs = _attn_pair_probs(attn, 0, qt_ref, sink_ref)
    merged = []
    for i in range(N_KV_HEADS):
        cols = slice(i * width, (i + 1) * width)
        ya = jnp.dot(a_in, wpa_ref[:, cols], preferred_element_type=F32)
        mlstm_piece(heads_per_piece * i)
        next_probs = _attn_pair_probs(attn, i + 1, qt_ref, sink_ref) if i + 1 < N_KV_HEADS else None
        ym = jnp.dot(m_in, wpm_ref[:, cols], preferred_element_type=F32)
        mlstm_piece(heads_per_piece * i + 1)
        merged.append((ga_ref[:, cols].astype(F32) * ya + gm_ref[:, cols].astype(F32) * ym).astype(BF16))
        if i + 1 == N_KV_HEADS:
            xo = x_ref[...] + jnp.dot(jnp.concatenate(merged, axis=1), wout_ref[...],
                                      preferred_element_type=F32)
        attn_piece(i, probs)
        probs = next_probs
    ms = jnp.mean(xo * xo, axis=-1, keepdims=True)
    y_ref[...] = (xo * lax.rsqrt(ms + EPS)) * gf_ref[...]


def _mixer(x, qvt, o32, o16, k_meta, vt_meta, sinks_t, bias, c0, n0, m0, g_mh, w_pa, w_pm, w_out, g_final,
           col0):
    seq = x.shape[0]
    assert seq % PAIR == 0
    n_blk = seq // PAIR
    cur = lambda c: jnp.minimum(c, n_blk - 1)
    prv = lambda c: jnp.maximum(cur(c) - 1, 0)
    mrg = lambda c: jnp.maximum(c - 1, 0)
    v_row = ATTN_WIDTH // KV_WIDTH
    qk_blk = 2 * (COL_QKM - col0)
    tile = lambda col: pl.BlockSpec((PAIR, D_MODEL), lambda c: (mrg(c), col))
    tile_cur = lambda col: pl.BlockSpec((PAIR, D_MODEL), lambda c: (cur(c), col))
    vec = pl.BlockSpec((1, D_MODEL), lambda c: (0, 0))
    weight = pl.BlockSpec((D_MODEL, D_MODEL), lambda c: (0, 0), pipeline_mode=pl.Buffered(1))
    state = lambda *blk: pl.BlockSpec((None,) + blk, lambda c: (0,) * (len(blk) + 1))
    return pl.pallas_call(
        functools.partial(_mixer_body, n_blk=n_blk),
        grid=(n_blk + 1,),
        in_specs=[
            pl.BlockSpec((ATTN_WIDTH, PAIR), lambda c: (0, cur(c))),
            pl.BlockSpec((PAIR, KV_WIDTH), lambda c: (prv(c), 0)),
            pl.BlockSpec((PAIR, KV_WIDTH), lambda c: (cur(c), 0)),
            pl.BlockSpec((N_META, KV_WIDTH), lambda c: (0, 0)),
            pl.BlockSpec((KV_WIDTH, PAIR), lambda c: (v_row, prv(c))),
            pl.BlockSpec((KV_WIDTH, PAIR), lambda c: (v_row, cur(c))),
            pl.BlockSpec((KV_WIDTH, N_META), lambda c: (0, 0)),
            pl.BlockSpec((N_KV_HEADS, 1, GQA_GROUP * PAIR), lambda c: (0, 0, 0)),
            pl.BlockSpec((PAIR, M_QK_WIDTH), lambda c: (cur(c), qk_blk)),
            pl.BlockSpec((PAIR, M_QK_WIDTH), lambda c: (cur(c), qk_blk + 1)),
            pl.BlockSpec((PAIR, M_V_WIDTH), lambda c: (cur(c), COL_VM - col0)),
            pl.BlockSpec((PAIR, LANES), lambda c: (cur(c), GATE_BLOCK)),
            pl.BlockSpec((1, LANES), lambda c: (0, 0)),
            state(M_HEADS, M_QK_DIM, M_V_DIM), state(M_HEADS, M_QK_DIM), state(M_HEADS, LANES),
            tile(0), tile_cur(COL_ZA - col0), tile_cur(COL_OM - col0), tile_cur(COL_ZM - col0),
            tile(COL_GA - col0), tile(COL_GM - col0), vec, weight, weight, weight, vec,
        ],
        out_specs=[
            tile(0),
            state(M_HEADS, M_QK_DIM, M_V_DIM), state(M_HEADS, M_QK_DIM), state(M_HEADS, LANES),
        ],
        out_shape=[
            jax.ShapeDtypeStruct((seq, D_MODEL), F32),
            jax.ShapeDtypeStruct((1, M_HEADS, M_QK_DIM, M_V_DIM), F32),
            jax.ShapeDtypeStruct((1, M_HEADS, M_QK_DIM), F32),
            jax.ShapeDtypeStruct((1, M_HEADS, LANES), F32),
        ],
        scratch_shapes=[pltpu.VMEM((MIX_SLOTS, PAIR, ATTN_WIDTH), BF16),
                        pltpu.VMEM((MIX_SLOTS, PAIR, M_V_WIDTH), BF16)],
        compiler_params=pltpu.CompilerParams(
            dimension_semantics=("arbitrary",), vmem_limit_bytes=VMEM_LIMIT),
        name="mixer",
    )(qvt, o32, o32, k_meta, qvt, qvt, vt_meta, sinks_t,
      o16, o16, o16, o32, bias, c0, n0, m0,
      x, o16, o16, o16, o16, o16, g_mh, w_pa, w_pm, w_out, g_final)


def kernel(x_prompt, x_sample, cache_k, cache_v, state_C, state_n, state_m, meta_tokens, g_norm, w_in,
           b_igate, b_fgate, attn_sinks, g_mhnorm, w_pa, w_pm, w_out, g_final):
    batch, seq, _ = x_prompt.shape
    db, ds, _ = x_sample.shape
    depth = w_in.shape[0]
    assert batch == 1 and depth == 1 and ds == N_META
    cache_rows = cache_k.shape[2]

    w_t = w_in[0].T
    g_in = g_norm[0].reshape(1, D_MODEL)
    bias = jnp.concatenate([b_igate[0], b_fgate[0], jnp.zeros((LANES - 2 * M_HEADS,), F32)]).reshape(1, LANES)
    sinks = jnp.broadcast_to(attn_sinks[0][:, None], (N_Q_HEADS, LANES))
    sinks_t = jnp.broadcast_to(attn_sinks[0].reshape(N_KV_HEADS, 1, GQA_GROUP, 1),
                               (N_KV_HEADS, 1, GQA_GROUP, PAIR)).reshape(N_KV_HEADS, 1, GQA_GROUP * PAIR)
    g_mh = g_mhnorm[0].reshape(1, M_V_WIDTH)
    g_fin = g_final.reshape(1, D_MODEL)
    wpa, wpm, wout = w_pa[0].astype(BF16), w_pm[0].astype(BF16), w_out[0].astype(BF16)

    xp = x_prompt.reshape(seq, D_MODEL)
    xs_rows = db * ds
    x_small = jnp.concatenate([x_sample.reshape(xs_rows, D_MODEL), meta_tokens.astype(F32)], axis=0)
    meta_block = xs_rows // N_META
    p_tm = min(seq, 1024)
    s32, s16, _, wmain, wkv = _project_first(x_small, g_in, w_t)
    p32, p16, xn_p = _project(xp, g_in, wkv, wmain, tm=p_tm, col0=COL_ZA)
    qvt = _project_t(wmain, wkv, xn_p, tm=p_tm)

    zeros = lambda *shape: jnp.zeros(shape, F32)
    _, c_meta, n_meta, m_meta = _mlstm(
        s16, s32, bias, zeros(1, M_HEADS, M_QK_DIM, M_V_DIM), zeros(1, M_HEADS, M_QK_DIM),
        zeros(1, M_HEADS, LANES), blk=N_META, n_streams=1, first_block=meta_block, col0=COL_QA)

    k_meta = s32[xs_rows:, :KV_WIDTH]
    vt_meta = s32[xs_rows:, KV_WIDTH:2 * KV_WIDTH].T.astype(BF16)
    y_p, c_p, n_p, m_p = _mixer(xp, qvt, p32, p16, k_meta, vt_meta, sinks_t, bias, c_meta, n_meta, m_meta,
                                g_mh, wpa, wpm, wout, g_fin, col0=COL_ZA)

    ck = cache_k[0].reshape(db, cache_rows, KV_WIDTH)
    cv = cache_v[0].reshape(db, cache_rows, KV_WIDTH)
    oa_s = _attend_sample(s16, s32, ck, cv, meta_block, sinks, db, ds)
    m0_s = jnp.broadcast_to(state_m[0][:, :, None], (db, M_HEADS, LANES))
    hn_s, c_s, n_s, m_s = _mlstm(s16, s32, bias, state_C[0], state_n[0], m0_s, blk=ds, n_streams=db,
                                 first_block=0, col0=COL_QA)
    y_s = _merge(x_small, oa_s, hn_s, s16, g_mh, wpa, wpm, wout, g_fin, rows=xs_rows, col0=COL_QA)

    kv_shape = (1, batch, cache_rows, N_KV_HEADS, HEAD_DIM)
    k_p = p32[seq - cache_rows:, :KV_WIDTH].reshape(kv_shape)
    v_p = p32[seq - cache_rows:, KV_WIDTH:2 * KV_WIDTH].reshape(kv_shape)
    new_k = s32[:xs_rows, :KV_WIDTH].reshape(db, ds, KV_WIDTH)
    new_v = s32[:xs_rows, KV_WIDTH:2 * KV_WIDTH].reshape(db, ds, KV_WIDTH)
    skv_shape = (1, db, cache_rows, N_KV_HEADS, HEAD_DIM)
    k_s = jnp.concatenate([ck, new_k], axis=1)[:, -cache_rows:].reshape(skv_shape)
    v_s = jnp.concatenate([cv, new_v], axis=1)[:, -cache_rows:].reshape(skv_shape)

    return (y_p.reshape(batch, seq, D_MODEL), y_s.reshape(db, ds, D_MODEL),
            k_p, v_p, c_p[None], n_p[None], m_p[None, :, :, 0],
            k_s, v_s, c_s[None], n_s[None], m_s[None, :, :, 0])
```

```python
import functools
import math

import jax
import jax.numpy as jnp
import numpy as np
from jax import lax
from jax.experimental import pallas as pl
from jax.experimental.pallas import tpu as pltpu

F32 = jnp.float32
BF16 = jnp.bfloat16

D_MODEL = 2048
CHUNK = 64
N_META = 16
HEAD_DIM = 64
N_Q_HEADS = 32
N_KV_HEADS = 4
GQA_GROUP = N_Q_HEADS // N_KV_HEADS
WINDOW = 128
WIN_CHUNKS = WINDOW // CHUNK
ATTN_WIDTH = N_Q_HEADS * HEAD_DIM
KV_WIDTH = N_KV_HEADS * HEAD_DIM
M_HEADS = 8
M_QK_DIM = 128
M_V_DIM = 256
M_QK_WIDTH = M_HEADS * M_QK_DIM
M_V_WIDTH = M_HEADS * M_V_DIM
EPS = 1e-6
NEG_INF = -1e30
COL_SIZES = (ATTN_WIDTH, KV_WIDTH, KV_WIDTH, ATTN_WIDTH, M_QK_WIDTH, M_QK_WIDTH, M_V_WIDTH, M_V_WIDTH,
             M_HEADS, M_HEADS, M_V_WIDTH, D_MODEL, D_MODEL)
COL_STARTS = tuple(int(v) for v in np.cumsum((0,) + COL_SIZES[:-1]))
(SRC_QA, SRC_KA, SRC_VA, SRC_ZA, SRC_QM, SRC_KM, SRC_VM, SRC_OM, SRC_IG, SRC_FG, SRC_ZM, SRC_GA,
 SRC_GM) = COL_STARTS

LANES = 128
VMEM_LIMIT = 56 * 1024 * 1024

KV_OUT = 2 * KV_WIDTH + LANES
GATE_BLOCK = 2 * KV_WIDTH // LANES
MAIN_OUT = 8 * D_MODEL
COL_QA, COL_ZA, COL_QKM, COL_VM, COL_OM, COL_ZM, COL_GA, COL_GM = range(8)
PROJ_TN = 1024
ROW_ALIGN = 16
MLSTM_K_SCALE = M_QK_DIM ** -0.5
Q_SCALE = 1.0 / math.sqrt(HEAD_DIM)
QV_ROWS = ATTN_WIDTH + KV_WIDTH
PAIR = 2 * CHUNK
PAIR_KEYS = 2 * PAIR + N_META
ONES_ROWS = 16
MIX_SLOTS = 2
MERGE_TN = 512
NORM_SLABS = 4


def _sigmoid(x):
    return 0.5 * jnp.tanh(0.5 * x) + 0.5


def _silu(x):
    return x * _sigmoid(x)


_NT = (((1,), (1,)), ((), ()))


def _main_row_offsets():
    groups = [(SRC_QA, ATTN_WIDTH), (SRC_ZA, ATTN_WIDTH), (SRC_QM, M_QK_WIDTH), (SRC_KM, M_QK_WIDTH),
              (SRC_VM, M_V_WIDTH), (SRC_OM, M_V_WIDTH), (SRC_ZM, M_V_WIDTH), (SRC_GA, D_MODEL),
              (SRC_GM, D_MODEL)]
    offs = []
    for start, width in groups:
        assert start % ROW_ALIGN == 0 and width % PROJ_TN == 0
        offs += [(start + b * PROJ_TN) // ROW_ALIGN for b in range(width // PROJ_TN)]
    assert len(offs) * PROJ_TN == MAIN_OUT
    return np.asarray(offs, np.int32)


def _proj_body(*refs, tile0, from_f32):
    if from_f32:
        _, x_ref, g_ref, wkv_ref, wg_ref, w_ref, o32_ref, o16_ref, xn_ref, wmain_out, wkv_out = refs
    else:
        x_ref, g_ref, wkv_ref, w_ref, o32_ref, o16_ref, xn_ref = refs

    @pl.when(pl.program_id(1) == 0)
    def _():
        if from_f32:
            wkv_out[:2 * KV_WIDTH, :] = wkv_ref[...].astype(BF16)
            wkv_out[2 * KV_WIDTH:2 * KV_WIDTH + ROW_ALIGN, :] = wg_ref[...].astype(BF16)
            wkv_out[2 * KV_WIDTH + ROW_ALIGN:, :] = jnp.zeros((LANES - ROW_ALIGN, D_MODEL), BF16)
            wkv = wkv_out[...]
        else:
            wkv = wkv_ref[...]
        tm = x_ref.shape[0]
        slab = tm // NORM_SLABS if tm % (NORM_SLABS * ROW_ALIGN) == 0 else tm
        for r in range(0, tm, slab):
            x = x_ref[r:r + slab, :]
            ms = jnp.mean(x * x, axis=-1, keepdims=True)
            xn = ((x * lax.rsqrt(ms + EPS)) * g_ref[...]).astype(BF16)
            xn_ref[r:r + slab, :] = xn
            o32_ref[r:r + slab, :] = lax.dot_general(xn, wkv, _NT, preferred_element_type=F32)

    group = (pl.program_id(1) + tile0) // (D_MODEL // PROJ_TN)
    is_sigmoid = (group == COL_OM) | (group == COL_GA) | (group == COL_GM)
    is_silu = (group == COL_ZA) | (group == COL_ZM)
    is_q = group == COL_QA

    def tile(act):
        w = w_ref[...].astype(BF16)
        if from_f32:
            wmain_out[...] = w
        acc = lax.dot_general(xn_ref[...], w, _NT, preferred_element_type=F32)
        o16_ref[...] = act(acc).astype(BF16)

    pl.when(is_sigmoid)(lambda: tile(_sigmoid))
    pl.when(is_silu)(lambda: tile(_silu))
    pl.when(is_q)(lambda: tile(lambda acc: acc * Q_SCALE))
    pl.when(jnp.logical_not(is_sigmoid | is_silu | is_q))(lambda: tile(lambda acc: acc))


def _project_first(x, g, w_t):
    rows = x.shape[0]
    assert SRC_KA % (2 * KV_WIDTH) == 0 and SRC_VA == SRC_KA + KV_WIDTH
    assert SRC_IG % ROW_ALIGN == 0 and SRC_FG == SRC_IG + M_HEADS and 2 * M_HEADS == ROW_ALIGN
    n_tiles = MAIN_OUT // PROJ_TN
    return pl.pallas_call(
        functools.partial(_proj_body, tile0=0, from_f32=True),
        grid_spec=pltpu.PrefetchScalarGridSpec(
            num_scalar_prefetch=1,
            grid=(1, n_tiles),
            in_specs=[
                pl.BlockSpec((rows, D_MODEL), lambda i, j, off: (0, 0)),
                pl.BlockSpec((1, D_MODEL), lambda i, j, off: (0, 0)),
                pl.BlockSpec((2 * KV_WIDTH, D_MODEL), lambda i, j, off: (SRC_KA // (2 * KV_WIDTH), 0),
                             pipeline_mode=pl.Buffered(1)),
                pl.BlockSpec((ROW_ALIGN, D_MODEL), lambda i, j, off: (SRC_IG // ROW_ALIGN, 0),
                             pipeline_mode=pl.Buffered(1)),
                pl.BlockSpec((pl.Element(PROJ_TN), pl.Element(D_MODEL)),
                             lambda i, j, off: (off[j] * ROW_ALIGN, 0)),
            ],
            out_specs=[
                pl.BlockSpec((rows, KV_OUT), lambda i, j, off: (0, 0)),
                pl.BlockSpec((rows, PROJ_TN), lambda i, j, off: (0, j)),
                pl.BlockSpec((rows, D_MODEL), lambda i, j, off: (0, 0)),
                pl.BlockSpec((PROJ_TN, D_MODEL), lambda i, j, off: (j, 0)),
                pl.BlockSpec((KV_OUT, D_MODEL), lambda i, j, off: (0, 0)),
            ],
        ),
        out_shape=[jax.ShapeDtypeStruct((rows, KV_OUT), F32),
                   jax.ShapeDtypeStruct((rows, MAIN_OUT), BF16),
                   jax.ShapeDtypeStruct((rows, D_MODEL), BF16),
                   jax.ShapeDtypeStruct((MAIN_OUT, D_MODEL), BF16),
                   jax.ShapeDtypeStruct((KV_OUT, D_MODEL), BF16)],
        compiler_params=pltpu.CompilerParams(
            dimension_semantics=("arbitrary", "arbitrary"), vmem_limit_bytes=VMEM_LIMIT),
        name="project_first",
    )(jnp.asarray(_main_row_offsets()), x, g, w_t, w_t, w_t)


def _project(x, g, wkv, wmain, tm, col0):
    rows = x.shape[0]
    assert rows % tm == 0
    tile0 = col0 * (D_MODEL // PROJ_TN)
    n_tiles = MAIN_OUT // PROJ_TN - tile0
    return pl.pallas_call(
        functools.partial(_proj_body, tile0=tile0, from_f32=False),
        grid=(rows // tm, n_tiles),
        in_specs=[
            pl.BlockSpec((tm, D_MODEL), lambda i, j: (i, 0)),
            pl.BlockSpec((1, D_MODEL), lambda i, j: (0, 0)),
            pl.BlockSpec((KV_OUT, D_MODEL), lambda i, j: (0, 0), pipeline_mode=pl.Buffered(1)),
            pl.BlockSpec((PROJ_TN, D_MODEL), lambda i, j: (j + tile0, 0)),
        ],
        out_specs=[
            pl.BlockSpec((tm, KV_OUT), lambda i, j: (i, 0)),
            pl.BlockSpec((tm, PROJ_TN), lambda i, j: (i, j)),
            pl.BlockSpec((tm, D_MODEL), lambda i, j: (i, 0)),
        ],
        out_shape=[jax.ShapeDtypeStruct((rows, KV_OUT), F32),
                   jax.ShapeDtypeStruct((rows, n_tiles * PROJ_TN), BF16),
                   jax.ShapeDtypeStruct((rows, D_MODEL), BF16)],
        compiler_params=pltpu.CompilerParams(
            dimension_semantics=("parallel", "arbitrary"), vmem_limit_bytes=VMEM_LIMIT),
        name="project",
    )(x, g, wkv, wmain)


def _proj_t_body(wq_ref, wv_ref, xn_ref, o_ref):
    xn = xn_ref[...]
    q_t = lax.dot_general(wq_ref[...], xn, _NT, preferred_element_type=F32) * Q_SCALE
    o_ref[:ATTN_WIDTH, :] = q_t.astype(BF16)
    o_ref[ATTN_WIDTH:, :] = lax.dot_general(wv_ref[...], xn, _NT, preferred_element_type=F32).astype(BF16)


def _project_t(wmain, wkv, xn, tm):
    rows = xn.shape[0]
    assert rows % tm == 0
    return pl.pallas_call(
        _proj_t_body,
        grid=(rows // tm,),
        in_specs=[pl.BlockSpec((ATTN_WIDTH, D_MODEL), lambda i: (COL_QA, 0), pipeline_mode=pl.Buffered(1)),
                  pl.BlockSpec((KV_WIDTH, D_MODEL), lambda i: (1, 0), pipeline_mode=pl.Buffered(1)),
                  pl.BlockSpec((tm, D_MODEL), lambda i: (i, 0))],
        out_specs=pl.BlockSpec((QV_ROWS, tm), lambda i: (0, i)),
        out_shape=jax.ShapeDtypeStruct((QV_ROWS, rows), BF16),
        compiler_params=pltpu.CompilerParams(
            dimension_semantics=("parallel",), vmem_limit_bytes=VMEM_LIMIT),
        name="project_t",
    )(wmain, wkv, xn)


def _attn_pair_setup(p, ka_ref, kb_ref, km_ref, vta_ref, vtb_ref, vtm_ref):
    k = jnp.concatenate([ka_ref[...], kb_ref[...], km_ref[...]], axis=0).astype(BF16)
    vt = jnp.concatenate([vta_ref[...], vtb_ref[...], vtm_ref[...]], axis=1)
    odd = (lax.broadcasted_iota(jnp.int32, (CHUNK, GQA_GROUP * PAIR), 1) % PAIR) >= CHUNK
    first = p == 0
    return k, vt, (odd | first, first, jnp.logical_not(odd))


def _attn_pair_probs(setup, h, qt_ref, sink_ref):
    k, vt, (mask_a, mask_b, mask_d) = setup
    kh = k[:, h * HEAD_DIM:(h + 1) * HEAD_DIM]
    rows = [(h * GQA_GROUP + g) * HEAD_DIM for g in range(GQA_GROUP)]
    qth = jnp.concatenate([qt_ref[r:r + HEAD_DIM, :] for r in rows], axis=1)
    st = jnp.dot(kh, qth, preferred_element_type=F32)
    st = jnp.concatenate([
        jnp.where(mask_a, NEG_INF, st[:CHUNK]),
        jnp.where(mask_b, NEG_INF, st[CHUNK:2 * CHUNK]),
        st[2 * CHUNK:3 * CHUNK],
        jnp.where(mask_d, NEG_INF, st[3 * CHUNK:4 * CHUNK]),
        st[4 * CHUNK:]], axis=0)
    sink = sink_ref[h]
    m = jnp.maximum(jnp.max(st, axis=0, keepdims=True), sink)
    return jnp.exp(st - m).astype(BF16), jnp.exp(sink - m)


def _attn_pair_out(setup, h, probs):
    _, vt, _ = setup
    e, e_sink = probs
    ones = jnp.ones((ONES_ROWS, PAIR_KEYS), BF16)
    vth = jnp.concatenate([vt[h * HEAD_DIM:(h + 1) * HEAD_DIM, :], ones], axis=0)
    ot = jnp.dot(vth, e, preferred_element_type=F32)
    denom = ot[HEAD_DIM:HEAD_DIM + 1, :] + e_sink
    ot = ot[:HEAD_DIM, :] * (1.0 / denom)
    return jnp.concatenate([ot[:, g * PAIR:(g + 1) * PAIR] for g in range(GQA_GROUP)], axis=0).T


def _attend_probs(q, k, sink_ref, lq):
    probs = []
    for h in range(N_KV_HEADS):
        kh = k[:, h * HEAD_DIM:(h + 1) * HEAD_DIM]
        heads = [h * GQA_GROUP + g for g in range(GQA_GROUP)]
        qh = jnp.concatenate([q[:, n * HEAD_DIM:(n + 1) * HEAD_DIM] for n in heads], axis=0)
        s = lax.dot_general(qh, kh, _NT, preferred_element_type=F32)
        sink = jnp.concatenate(
            [jnp.broadcast_to(sink_ref[n:n + 1, 0:1], (lq, 1)) for n in heads], axis=0)
        m = jnp.maximum(jnp.max(s, axis=-1, keepdims=True), sink)
        e = jnp.exp(s - m)
        probs.append((e.astype(BF16), jnp.sum(e, axis=-1, keepdims=True) + jnp.exp(sink - m)))
    return probs


def _attend_values(probs, v, o_ref, lq):
    for h in range(N_KV_HEADS):
        vh = v[:, h * HEAD_DIM:(h + 1) * HEAD_DIM]
        e, denom = probs[h]
        oh = jnp.dot(e, vh, preferred_element_type=F32) * (1.0 / denom)
        o_ref[:, h * GQA_GROUP * HEAD_DIM:(h + 1) * GQA_GROUP * HEAD_DIM] = jnp.concatenate(
            [oh[g * lq:(g + 1) * lq, :] for g in range(GQA_GROUP)], axis=1).astype(o_ref.dtype)


def _attn_sample_body(q_ref, ck_ref, cv_ref, kn_ref, vn_ref, km_ref, vm_ref, sink_ref, o_ref):
    lq = q_ref.shape[0]
    k = jnp.concatenate([ck_ref[...], kn_ref[...], km_ref[...]], axis=0).astype(BF16)
    probs = _attend_probs(q_ref[...], k, sink_ref, lq)
    v = jnp.concatenate([cv_ref[...], vn_ref[...], vm_ref[...]], axis=0).astype(BF16)
    _attend_values(probs, v, o_ref, lq)


def _attend_sample(o16, o32, cache_k, cache_v, meta_block, sinks, n_streams, ds):
    cache_rows = cache_k.shape[1]
    new = lambda col: pl.BlockSpec((ds, KV_WIDTH), lambda s: (s, col))
    meta = lambda col: pl.BlockSpec((N_META, KV_WIDTH), lambda s: (meta_block, col))
    cache = pl.BlockSpec((None, cache_rows, KV_WIDTH), lambda s: (s, 0, 0))
    return pl.pallas_call(
        _attn_sample_body,
        grid=(n_streams,),
        in_specs=[
            pl.BlockSpec((ds, ATTN_WIDTH), lambda s: (s, COL_QA)),
            cache, cache, new(0), new(1), meta(0), meta(1),
            pl.BlockSpec((N_Q_HEADS, LANES), lambda s: (0, 0)),
        ],
        out_specs=pl.BlockSpec((ds, ATTN_WIDTH), lambda s: (s, 0)),
        out_shape=jax.ShapeDtypeStruct((n_streams * ds, ATTN_WIDTH), BF16),
        compiler_params=pltpu.CompilerParams(
            dimension_semantics=("parallel",), vmem_limit_bytes=VMEM_LIMIT),
        name="attend_sample",
    )(o16, cache_k, cache_v, o32, o32, o32, o32, sinks)


def _mlstm_body(q_ref, k_ref, v_ref, gt_ref, bias_ref, c0_ref, n0_ref, m0_ref,
                hn_ref, c_ref, n_ref, m_ref, *, blk):
    c_ref[...] = c0_ref[...]
    n_ref[...] = n0_ref[...]
    m_ref[...] = m0_ref[...]
    setup = _mlstm_setup(gt_ref, bias_ref, blk)
    scores = [_mlstm_scores(setup, h, q_ref, k_ref, m_ref) for h in range(M_HEADS)]
    for h in range(M_HEADS):
        hv = _mlstm_out(setup, h, scores[h], q_ref, v_ref, c_ref, n_ref)
        hn_ref[:, h * M_V_DIM:(h + 1) * M_V_DIM] = hv.astype(hn_ref.dtype)
    for h in range(M_HEADS):
        _mlstm_state(setup, h, scores[h], k_ref, v_ref, c_ref, n_ref, m_ref, blk, None)


def _mlstm_setup(gt_ref, bias_ref, blk):
    x = gt_ref[...] + bias_ref[...]
    lane = lax.broadcasted_iota(jnp.int32, x.shape, 1)
    log_f = jnp.minimum(x, 0.0) - jnp.log(1.0 + jnp.exp(-jnp.abs(x)))
    gates = jnp.where(lane < M_HEADS, x, log_f)
    row = lax.broadcasted_iota(jnp.int32, (blk, blk), 0)
    col = lax.broadcasted_iota(jnp.int32, (blk, blk), 1)
    causal = row >= col
    hi = gates.astype(BF16)
    rest = gates - hi.astype(F32)
    mid = rest.astype(BF16)
    lo = (rest - mid.astype(F32)).astype(BF16)
    parts = jnp.dot(causal.astype(BF16), jnp.concatenate([hi, mid, lo], axis=1), preferred_element_type=F32)
    csum = parts[:, :LANES] + parts[:, LANES:2 * LANES] + parts[:, 2 * LANES:]
    z = jnp.where(lane < M_HEADS, gates, csum)
    pad = (-blk) % LANES
    zsq = z if pad == 0 else jnp.concatenate([z, jnp.zeros((pad, LANES), F32)], axis=0)
    zt = zsq.T[:, :blk]
    return z, zt, causal


def _mlstm_scores(setup, h, q_ref, k_ref, m_ref):
    z, zt, causal = setup
    g_r = zt[h:h + 1, :] - zt[M_HEADS + h:M_HEADS + h + 1, :]
    m_prev = m_ref[h:h + 1, 0:1]
    d0 = jnp.where(causal, g_r, -jnp.inf)
    mm = jnp.maximum(m_prev, jnp.max(d0, axis=-1, keepdims=True))
    w = jnp.exp(d0 - mm)
    qh = q_ref[:, h * M_QK_DIM:(h + 1) * M_QK_DIM]
    kh = k_ref[:, h * M_QK_DIM:(h + 1) * M_QK_DIM]
    qk = lax.dot_general(qh, kh, _NT, preferred_element_type=F32)
    wqk = w * (qk * MLSTM_K_SCALE)
    return m_prev, mm, wqk.astype(BF16), jnp.sum(wqk, axis=-1, keepdims=True)


def _mlstm_finish(setup, h, scores, q_ref, k_ref, v_ref, c_ref, n_ref, m_ref, blk, live):
    hv = _mlstm_out(setup, h, scores, q_ref, v_ref, c_ref, n_ref)
    _mlstm_state(setup, h, scores, k_ref, v_ref, c_ref, n_ref, m_ref, blk, live)
    return hv


def _mlstm_out(setup, h, scores, q_ref, v_ref, c_ref, n_ref):
    z, _, _ = setup
    m_prev, mm, wqk, den_intra = scores
    m_t = z[:, M_HEADS + h:M_HEADS + h + 1] + mm
    a = jnp.exp(m_prev - mm)
    qh = q_ref[:, h * M_QK_DIM:(h + 1) * M_QK_DIM]
    vh = v_ref[:, h * M_V_DIM:(h + 1) * M_V_DIM]
    num = a * jnp.dot(qh, c_ref[h].astype(BF16), preferred_element_type=F32) + jnp.dot(
        wqk, vh, preferred_element_type=F32)
    den = a * jnp.sum(qh.astype(F32) * n_ref[h:h + 1, :], axis=-1, keepdims=True) + den_intra
    r = 1.0 / jnp.maximum(jnp.abs(den), jnp.exp(-m_t))
    rms_h = r * jnp.sqrt(jnp.mean(num * num, axis=-1, keepdims=True))
    return num * (r * lax.rsqrt(rms_h * rms_h + EPS))


def _mlstm_state(setup, h, scores, k_ref, v_ref, c_ref, n_ref, m_ref, blk, live):
    z, _, _ = setup
    m_prev, mm, _, _ = scores
    keep = (lambda new, old: new) if live is None else (lambda new, old: jnp.where(live, new, old))
    li_c = z[:, h:h + 1]
    b_c = z[:, M_HEADS + h:M_HEADS + h + 1]
    m_t = b_c + mm
    kh = k_ref[:, h * M_QK_DIM:(h + 1) * M_QK_DIM]
    vh = v_ref[:, h * M_V_DIM:(h + 1) * M_V_DIM]
    c_h = c_ref[h]
    n_h = n_ref[h:h + 1, :]

    m_new = m_t[blk - 1:blk, :]
    b_last = b_c[blk - 1:blk, :]
    ws = jnp.exp(b_last - b_c + li_c - m_new) * MLSTM_K_SCALE
    a_last = jnp.exp(b_last + m_prev - m_new)
    vs = (vh.astype(F32) * ws).astype(BF16)
    ktv = lax.dot_general(kh, vs, (((0,), (0,)), ((), ())), preferred_element_type=F32)
    c_ref[h] = keep(a_last * c_h + ktv, c_h)
    n_ref[h:h + 1, :] = keep(a_last * n_h + jnp.sum(kh.astype(F32) * ws, axis=0, keepdims=True), n_h)
    m_ref[h:h + 1, :] = keep(jnp.broadcast_to(m_new, (1, LANES)), m_ref[h:h + 1, :])


def _mlstm(o16, o32, bias, c0, n0, m0, *, blk, n_streams, first_block, col0):
    rb = lambda s: first_block + s
    qk_blk = 2 * (COL_QKM - col0)
    state = lambda *shape: pl.BlockSpec((None,) + shape, lambda s: (s,) + (0,) * len(shape))
    return pl.pallas_call(
        functools.partial(_mlstm_body, blk=blk),
        grid=(n_streams,),
        in_specs=[
            pl.BlockSpec((blk, M_QK_WIDTH), lambda s: (rb(s), qk_blk)),
            pl.BlockSpec((blk, M_QK_WIDTH), lambda s: (rb(s), qk_blk + 1)),
            pl.BlockSpec((blk, M_V_WIDTH), lambda s: (rb(s), COL_VM - col0)),
            pl.BlockSpec((blk, LANES), lambda s: (rb(s), GATE_BLOCK)),
            pl.BlockSpec((1, LANES), lambda s: (0, 0)),
            state(M_HEADS, M_QK_DIM, M_V_DIM), state(M_HEADS, M_QK_DIM), state(M_HEADS, LANES),
        ],
        out_specs=[
            pl.BlockSpec((blk, M_V_WIDTH), lambda s: (s, 0)),
            state(M_HEADS, M_QK_DIM, M_V_DIM), state(M_HEADS, M_QK_DIM), state(M_HEADS, LANES),
        ],
        out_shape=[
            jax.ShapeDtypeStruct((n_streams * blk, M_V_WIDTH), BF16),
            jax.ShapeDtypeStruct((n_streams, M_HEADS, M_QK_DIM, M_V_DIM), F32),
            jax.ShapeDtypeStruct((n_streams, M_HEADS, M_QK_DIM), F32),
            jax.ShapeDtypeStruct((n_streams, M_HEADS, LANES), F32),
        ],
        compiler_params=pltpu.CompilerParams(
            dimension_semantics=("parallel",), vmem_limit_bytes=VMEM_LIMIT),
        name="mlstm_blk%d" % blk,
    )(o16, o16, o16, o32, bias, c0, n0, m0)


def _merge_body(x_ref, oa_ref, hn_ref, za_ref, om_ref, zm_ref, ga_ref, gm_ref, gmh_ref,
                wpa_ref, wpm_ref, wout_ref, gf_ref, y_ref, wpa_out, wpm_out, wout_out, a_scr, m_scr, acc_scr):
    j = pl.program_id(0)

    @pl.when(j == 0)
    def _():
        a_scr[...] = (oa_ref[...].astype(F32) * za_ref[...].astype(F32)).astype(BF16)
        m_scr[...] = ((hn_ref[...].astype(F32) * gmh_ref[...]) * om_ref[...].astype(F32)
                      * zm_ref[...].astype(F32)).astype(BF16)
        acc_scr[...] = x_ref[...]

    wpa_out[...] = wpa_ref[...].astype(BF16)
    wpm_out[...] = wpm_ref[...].astype(BF16)
    wout_out[...] = wout_ref[...].astype(BF16)
    ya = jnp.dot(a_scr[...], wpa_out[...], preferred_element_type=F32)
    ym = jnp.dot(m_scr[...], wpm_out[...], preferred_element_type=F32)
    merged = ga_ref[...].astype(F32) * ya + gm_ref[...].astype(F32) * ym
    acc_scr[...] += jnp.dot(merged.astype(BF16), wout_out[...], preferred_element_type=F32)

    @pl.when(j == pl.num_programs(0) - 1)
    def _():
        xo = acc_scr[...]
        ms = jnp.mean(xo * xo, axis=-1, keepdims=True)
        y_ref[...] = (xo * lax.rsqrt(ms + EPS)) * gf_ref[...]


def _merge(x, o_att, hn, o16, g_mh, w_pa, w_pm, w_out, g_final, rows, col0):
    n_chunks = D_MODEL // MERGE_TN
    per_group = D_MODEL // MERGE_TN
    full = lambda col: pl.BlockSpec((rows, D_MODEL), lambda j: (0, col))
    gate = lambda col: pl.BlockSpec((rows, MERGE_TN), lambda j: (0, col * per_group + j))
    vec = pl.BlockSpec((1, D_MODEL), lambda j: (0, 0))
    w_cols = pl.BlockSpec((D_MODEL, MERGE_TN), lambda j: (0, j))
    w_rows = pl.BlockSpec((MERGE_TN, D_MODEL), lambda j: (j, 0))
    return pl.pallas_call(
        _merge_body,
        grid=(n_chunks,),
        in_specs=[full(0), full(0), full(0), full(COL_ZA - col0), full(COL_OM - col0), full(COL_ZM - col0),
                  gate(COL_GA - col0), gate(COL_GM - col0), vec, w_cols, w_cols, w_rows, vec],
        out_specs=[full(0), w_cols, w_cols, w_rows],
        out_shape=[jax.ShapeDtypeStruct((rows, D_MODEL), F32)]
        + [jax.ShapeDtypeStruct((D_MODEL, D_MODEL), BF16)] * 3,
        scratch_shapes=[pltpu.VMEM((rows, D_MODEL), BF16), pltpu.VMEM((rows, D_MODEL), BF16),
                        pltpu.VMEM((rows, D_MODEL), F32)],
        compiler_params=pltpu.CompilerParams(
            dimension_semantics=("arbitrary",), vmem_limit_bytes=VMEM_LIMIT),
        name="merge",
    )(x, o_att, hn, o16, o16, o16, o16, o16, g_mh, w_pa, w_pm, w_out, g_final)


def _mixer_body(qt_ref, ka_ref, kb_ref, km_ref, vta_ref, vtb_ref, vtm_ref, sink_ref,
                q_ref, k_ref, v_ref, gt_ref, bias_ref, c0_ref, n0_ref, m0_ref,
                x_ref, za_ref, om_ref, zm_ref, ga_ref, gm_ref, gmh_ref, wpa_ref, wpm_ref, wout_ref, gf_ref,
                y_ref, c_ref, n_ref, m_ref, a_scr, m_scr, *, n_blk):
    c = pl.program_id(0)

    @pl.when(c == 0)
    def _():
        c_ref[...] = c0_ref[...]
        n_ref[...] = n0_ref[...]
        m_ref[...] = m0_ref[...]
        a_scr[...] = jnp.zeros(a_scr.shape, a_scr.dtype)
        m_scr[...] = jnp.zeros(m_scr.shape, m_scr.dtype)

    wr = c % MIX_SLOTS
    rd = 1 - wr
    blk = jnp.minimum(c, n_blk - 1)
    live = c < n_blk
    width = GQA_GROUP * HEAD_DIM
    heads_per_piece = width // M_V_DIM
    a_in = a_scr[rd]
    m_in = m_scr[rd]

    def mlstm_piece(h):
        hc = slice(h * M_V_DIM, (h + 1) * M_V_DIM)
        hv = _mlstm_finish(cell, h, scores[h], q_ref, k_ref, v_ref, c_ref, n_ref, m_ref, PAIR, live)
        m_scr[wr, :, hc] = ((hv * gmh_ref[:, hc]) * om_ref[:, hc].astype(F32)
                            * zm_ref[:, hc].astype(F32)).astype(BF16)

    def attn_piece(h, probs):
        cols = slice(h * width, (h + 1) * width)
        oa = _attn_pair_out(attn, h, probs)
        a_scr[wr, :, cols] = (oa * za_ref[:, cols].astype(F32)).astype(BF16)

    attn = _attn_pair_setup(blk, ka_ref, kb_ref, km_ref, vta_ref, vtb_ref, vtm_ref)
    cell = _mlstm_setup(gt_ref, bias_ref, PAIR)
    scores = [_mlstm_scores(cell, h, q_ref, k_ref, m_ref) for h in range(M_HEADS)]
    probs = _attn_pair_probs(attn, 0, qt_ref, sink_ref)
    merged = []
    for i in range(N_KV_HEADS):
        cols = slice(i * width, (i + 1) * width)
        ya = jnp.dot(a_in, wpa_ref[:, cols], preferred_element_type=F32)
        mlstm_piece(heads_per_piece * i)
        next_probs = _attn_pair_probs(attn, i + 1, qt_ref, sink_ref) if i + 1 < N_KV_HEADS else None
        ym = jnp.dot(m_in, wpm_ref[:, cols], preferred_element_type=F32)
        mlstm_piece(heads_per_piece * i + 1)
        merged.append((ga_ref[:, cols].astype(F32) * ya + gm_ref[:, cols].astype(F32) * ym).astype(BF16))
        if i + 1 == N_KV_HEADS:
            xo = x_ref[...] + jnp.dot(jnp.concatenate(merged, axis=1), wout_ref[...],
                                      preferred_element_type=F32)
        attn_piece(i, probs)
        probs = next_probs
    ms = jnp.mean(xo * xo, axis=-1, keepdims=True)
    y_ref[...] = (xo * lax.rsqrt(ms + EPS)) * gf_ref[...]


def _mixer(x, qvt, o32, o16, k_meta, vt_meta, sinks_t, bias, c0, n0, m0, g_mh, w_pa, w_pm, w_out, g_final,
           col0):
    seq = x.shape[0]
    assert seq % PAIR == 0
    n_blk = seq // PAIR
    cur = lambda c: jnp.minimum(c, n_blk - 1)
    prv = lambda c: jnp.maximum(cur(c) - 1, 0)
    mrg = lambda c: jnp.maximum(c - 1, 0)
    v_row = ATTN_WIDTH // KV_WIDTH
    qk_blk = 2 * (COL_QKM - col0)
    tile = lambda col: pl.BlockSpec((PAIR, D_MODEL), lambda c: (mrg(c), col))
    tile_cur = lambda col: pl.BlockSpec((PAIR, D_MODEL), lambda c: (cur(c), col))
    vec = pl.BlockSpec((1, D_MODEL), lambda c: (0, 0))
    weight = pl.BlockSpec((D_MODEL, D_MODEL), lambda c: (0, 0), pipeline_mode=pl.Buffered(1))
    state = lambda *blk: pl.BlockSpec((None,) + blk, lambda c: (0,) * (len(blk) + 1))
    return pl.pallas_call(
        functools.partial(_mixer_body, n_blk=n_blk),
        grid=(n_blk + 1,),
        in_specs=[
            pl.BlockSpec((ATTN_WIDTH, PAIR), lambda c: (0, cur(c))),
            pl.BlockSpec((PAIR, KV_WIDTH), lambda c: (prv(c), 0)),
            pl.BlockSpec((PAIR, KV_WIDTH), lambda c: (cur(c), 0)),
            pl.BlockSpec((N_META, KV_WIDTH), lambda c: (0, 0)),
            pl.BlockSpec((KV_WIDTH, PAIR), lambda c: (v_row, prv(c))),
            pl.BlockSpec((KV_WIDTH, PAIR), lambda c: (v_row, cur(c))),
            pl.BlockSpec((KV_WIDTH, N_META), lambda c: (0, 0)),
            pl.BlockSpec((N_KV_HEADS, 1, GQA_GROUP * PAIR), lambda c: (0, 0, 0)),
            pl.BlockSpec((PAIR, M_QK_WIDTH), lambda c: (cur(c), qk_blk)),
            pl.BlockSpec((PAIR, M_QK_WIDTH), lambda c: (cur(c), qk_blk + 1)),
            pl.BlockSpec((PAIR, M_V_WIDTH), lambda c: (cur(c), COL_VM - col0)),
            pl.BlockSpec((PAIR, LANES), lambda c: (cur(c), GATE_BLOCK)),
            pl.BlockSpec((1, LANES), lambda c: (0, 0)),
            state(M_HEADS, M_QK_DIM, M_V_DIM), state(M_HEADS, M_QK_DIM), state(M_HEADS, LANES),
            tile(0), tile_cur(COL_ZA - col0), tile_cur(COL_OM - col0), tile_cur(COL_ZM - col0),
            tile(COL_GA - col0), tile(COL_GM - col0), vec, weight, weight, weight, vec,
        ],
        out_specs=[
            tile(0),
            state(M_HEADS, M_QK_DIM, M_V_DIM), state(M_HEADS, M_QK_DIM), state(M_HEADS, LANES),
        ],
        out_shape=[
            jax.ShapeDtypeStruct((seq, D_MODEL), F32),
            jax.ShapeDtypeStruct((1, M_HEADS, M_QK_DIM, M_V_DIM), F32),
            jax.ShapeDtypeStruct((1, M_HEADS, M_QK_DIM), F32),
            jax.ShapeDtypeStruct((1, M_HEADS, LANES), F32),
        ],
        scratch_shapes=[pltpu.VMEM((MIX_SLOTS, PAIR, ATTN_WIDTH), BF16),
                        pltpu.VMEM((MIX_SLOTS, PAIR, M_V_WIDTH), BF16)],
        compiler_params=pltpu.CompilerParams(
            dimension_semantics=("arbitrary",), vmem_limit_bytes=VMEM_LIMIT),
        name="mixer",
    )(qvt, o32, o32, k_meta, qvt, qvt, vt_meta, sinks_t,
      o16, o16, o16, o32, bias, c0, n0, m0,
      x, o16, o16, o16, o16, o16, g_mh, w_pa, w_pm, w_out, g_final)


def kernel(x_prompt, x_sample, cache_k, cache_v, state_C, state_n, state_m, meta_tokens, g_norm, w_in,
           b_igate, b_fgate, attn_sinks, g_mhnorm, w_pa, w_pm, w_out, g_final):
    batch, seq, _ = x_prompt.shape
    db, ds, _ = x_sample.shape
    depth = w_in.shape[0]
    assert batch == 1 and depth == 1 and ds == N_META
    cache_rows = cache_k.shape[2]

    w_t = w_in[0].T
    g_in = g_norm[0].reshape(1, D_MODEL)
    bias = jnp.concatenate([b_igate[0], b_fgate[0], jnp.zeros((LANES - 2 * M_HEADS,), F32)]).reshape(1, LANES)
    sinks = jnp.broadcast_to(attn_sinks[0][:, None], (N_Q_HEADS, LANES))
    sinks_t = jnp.broadcast_to(attn_sinks[0].reshape(N_KV_HEADS, 1, GQA_GROUP, 1),
                               (N_KV_HEADS, 1, GQA_GROUP, PAIR)).reshape(N_KV_HEADS, 1, GQA_GROUP * PAIR)
    g_mh = g_mhnorm[0].reshape(1, M_V_WIDTH)
    g_fin = g_final.reshape(1, D_MODEL)

    xp = x_prompt.reshape(seq, D_MODEL)
    xs_rows = db * ds
    x_small = jnp.concatenate([x_sample.reshape(xs_rows, D_MODEL), meta_tokens.astype(F32)], axis=0)
    meta_block = xs_rows // N_META
    p_tm = min(seq, 1024)
    s32, s16, _, wmain, wkv = _project_first(x_small, g_in, w_t)
    p32, p16, xn_p = _project(xp, g_in, wkv, wmain, tm=p_tm, col0=COL_ZA)
    qvt = _project_t(wmain, wkv, xn_p, tm=p_tm)

    zeros = lambda *shape: jnp.zeros(shape, F32)
    _, c_meta, n_meta, m_meta = _mlstm(
        s16, s32, bias, zeros(1, M_HEADS, M_QK_DIM, M_V_DIM), zeros(1, M_HEADS, M_QK_DIM),
        zeros(1, M_HEADS, LANES), blk=N_META, n_streams=1, first_block=meta_block, col0=COL_QA)

    ck = cache_k[0].reshape(db, cache_rows, KV_WIDTH)
    cv = cache_v[0].reshape(db, cache_rows, KV_WIDTH)
    oa_s = _attend_sample(s16, s32, ck, cv, meta_block, sinks, db, ds)
    m0_s = jnp.broadcast_to(state_m[0][:, :, None], (db, M_HEADS, LANES))
    hn_s, c_s, n_s, m_s = _mlstm(s16, s32, bias, state_C[0], state_n[0], m0_s, blk=ds, n_streams=db,
                                 first_block=0, col0=COL_QA)
    y_s, wpa, wpm, wout = _merge(x_small, oa_s, hn_s, s16, g_mh, w_pa[0], w_pm[0], w_out[0], g_fin,
                                 rows=xs_rows, col0=COL_QA)

    k_meta = s32[xs_rows:, :KV_WIDTH]
    vt_meta = s32[xs_rows:, KV_WIDTH:2 * KV_WIDTH].T.astype(BF16)
    y_p, c_p, n_p, m_p = _mixer(xp, qvt, p32, p16, k_meta, vt_meta, sinks_t, bias, c_meta, n_meta, m_meta,
                                g_mh, wpa, wpm, wout, g_fin, col0=COL_ZA)

    kv_shape = (1, batch, cache_rows, N_KV_HEADS, HEAD_DIM)
    k_p = p32[seq - cache_rows:, :KV_WIDTH].reshape(kv_shape)
    v_p = p32[seq - cache_rows:, KV_WIDTH:2 * KV_WIDTH].reshape(kv_shape)
    new_k = s32[:xs_rows, :KV_WIDTH].reshape(db, ds, KV_WIDTH)
    new_v = s32[:xs_rows, KV_WIDTH:2 * KV_WIDTH].reshape(db, ds, KV_WIDTH)
    skv_shape = (1, db, cache_rows, N_KV_HEADS, HEAD_DIM)
    k_s = jnp.concatenate([ck, new_k], axis=1)[:, -cache_rows:].reshape(skv_shape)
    v_s = jnp.concatenate([cv, new_v], axis=1)[:, -cache_rows:].reshape(skv_shape)

    return (y_p.reshape(batch, seq, D_MODEL), y_s.reshape(db, ds, D_MODEL),
            k_p, v_p, c_p[None], n_p[None], m_p[None, :, :, 0],
            k_s, v_s, c_s[None], n_s[None], m_s[None, :, :, 0])
```

```python
import functools
import math

import jax
import jax.numpy as jnp
import numpy as np
from jax import lax
from jax.experimental import pallas as pl
from jax.experimental.pallas import tpu as pltpu

F32 = jnp.float32
BF16 = jnp.bfloat16

D_MODEL = 2048
CHUNK = 64
N_META = 16
HEAD_DIM = 64
N_Q_HEADS = 32
N_KV_HEADS = 4
GQA_GROUP = N_Q_HEADS // N_KV_HEADS
WINDOW = 128
WIN_CHUNKS = WINDOW // CHUNK
ATTN_WIDTH = N_Q_HEADS * HEAD_DIM
KV_WIDTH = N_KV_HEADS * HEAD_DIM
M_HEADS = 8
M_QK_DIM = 128
M_V_DIM = 256
M_QK_WIDTH = M_HEADS * M_QK_DIM
M_V_WIDTH = M_HEADS * M_V_DIM
EPS = 1e-6
NEG_INF = -1e30
COL_SIZES = (ATTN_WIDTH, KV_WIDTH, KV_WIDTH, ATTN_WIDTH, M_QK_WIDTH, M_QK_WIDTH, M_V_WIDTH, M_V_WIDTH,
             M_HEADS, M_HEADS, M_V_WIDTH, D_MODEL, D_MODEL)
COL_STARTS = tuple(int(v) for v in np.cumsum((0,) + COL_SIZES[:-1]))
(SRC_QA, SRC_KA, SRC_VA, SRC_ZA, SRC_QM, SRC_KM, SRC_VM, SRC_OM, SRC_IG, SRC_FG, SRC_ZM, SRC_GA,
 SRC_GM) = COL_STARTS

LANES = 128
VMEM_LIMIT = 56 * 1024 * 1024

KV_OUT = 2 * KV_WIDTH + LANES
GATE_BLOCK = 2 * KV_WIDTH // LANES
MAIN_OUT = 8 * D_MODEL
COL_QA, COL_ZA, COL_QKM, COL_VM, COL_OM, COL_ZM, COL_GA, COL_GM = range(8)
PROJ_TN = 1024
ROW_ALIGN = 16
MLSTM_K_SCALE = M_QK_DIM ** -0.5
Q_SCALE = 1.0 / math.sqrt(HEAD_DIM)
QV_ROWS = ATTN_WIDTH + KV_WIDTH
PAIR = 2 * CHUNK
PAIR_KEYS = 2 * PAIR + N_META
ONES_ROWS = 16
MIX_SLOTS = 2
MERGE_TN = 256
NORM_SLABS = 4


def _sigmoid(x):
    return 0.5 * jnp.tanh(0.5 * x) + 0.5


def _silu(x):
    return x * _sigmoid(x)


_NT = (((1,), (1,)), ((), ()))


def _main_row_offsets():
    groups = [(SRC_QA, ATTN_WIDTH), (SRC_ZA, ATTN_WIDTH), (SRC_QM, M_QK_WIDTH), (SRC_KM, M_QK_WIDTH),
              (SRC_VM, M_V_WIDTH), (SRC_OM, M_V_WIDTH), (SRC_ZM, M_V_WIDTH), (SRC_GA, D_MODEL),
              (SRC_GM, D_MODEL)]
    offs = []
    for start, width in groups:
        assert start % ROW_ALIGN == 0 and width % PROJ_TN == 0
        offs += [(start + b * PROJ_TN) // ROW_ALIGN for b in range(width // PROJ_TN)]
    assert len(offs) * PROJ_TN == MAIN_OUT
    return np.asarray(offs, np.int32)


def _proj_body(*refs, tile0, from_f32):
    if from_f32:
        _, x_ref, g_ref, wkv_ref, wg_ref, w_ref, o32_ref, o16_ref, xn_ref, wmain_out, wkv_out = refs
    else:
        x_ref, g_ref, wkv_ref, w_ref, o32_ref, o16_ref, xn_ref = refs

    @pl.when(pl.program_id(1) == 0)
    def _():
        if from_f32:
            wkv_out[:2 * KV_WIDTH, :] = wkv_ref[...].astype(BF16)
            wkv_out[2 * KV_WIDTH:2 * KV_WIDTH + ROW_ALIGN, :] = wg_ref[...].astype(BF16)
            wkv_out[2 * KV_WIDTH + ROW_ALIGN:, :] = jnp.zeros((LANES - ROW_ALIGN, D_MODEL), BF16)
            wkv = wkv_out[...]
        else:
            wkv = wkv_ref[...]
        tm = x_ref.shape[0]
        slab = tm // NORM_SLABS if tm % (NORM_SLABS * ROW_ALIGN) == 0 else tm
        for r in range(0, tm, slab):
            x = x_ref[r:r + slab, :]
            ms = jnp.mean(x * x, axis=-1, keepdims=True)
            xn = ((x * lax.rsqrt(ms + EPS)) * g_ref[...]).astype(BF16)
            xn_ref[r:r + slab, :] = xn
            o32_ref[r:r + slab, :] = lax.dot_general(xn, wkv, _NT, preferred_element_type=F32)

    group = (pl.program_id(1) + tile0) // (D_MODEL // PROJ_TN)
    is_sigmoid = (group == COL_OM) | (group == COL_GA) | (group == COL_GM)
    is_silu = (group == COL_ZA) | (group == COL_ZM)
    is_q = group == COL_QA

    def tile(act):
        w = w_ref[...].astype(BF16)
        if from_f32:
            wmain_out[...] = w
        acc = lax.dot_general(xn_ref[...], w, _NT, preferred_element_type=F32)
        o16_ref[...] = act(acc).astype(BF16)

    pl.when(is_sigmoid)(lambda: tile(_sigmoid))
    pl.when(is_silu)(lambda: tile(_silu))
    pl.when(is_q)(lambda: tile(lambda acc: acc * Q_SCALE))
    pl.when(jnp.logical_not(is_sigmoid | is_silu | is_q))(lambda: tile(lambda acc: acc))


def _project_first(x, g, w_t):
    rows = x.shape[0]
    assert SRC_KA % (2 * KV_WIDTH) == 0 and SRC_VA == SRC_KA + KV_WIDTH
    assert SRC_IG % ROW_ALIGN == 0 and SRC_FG == SRC_IG + M_HEADS and 2 * M_HEADS == ROW_ALIGN
    n_tiles = MAIN_OUT // PROJ_TN
    return pl.pallas_call(
        functools.partial(_proj_body, tile0=0, from_f32=True),
        grid_spec=pltpu.PrefetchScalarGridSpec(
            num_scalar_prefetch=1,
            grid=(1, n_tiles),
            in_specs=[
                pl.BlockSpec((rows, D_MODEL), lambda i, j, off: (0, 0)),
                pl.BlockSpec((1, D_MODEL), lambda i, j, off: (0, 0)),
                pl.BlockSpec((2 * KV_WIDTH, D_MODEL), lambda i, j, off: (SRC_KA // (2 * KV_WIDTH), 0),
                             pipeline_mode=pl.Buffered(1)),
                pl.BlockSpec((ROW_ALIGN, D_MODEL), lambda i, j, off: (SRC_IG // ROW_ALIGN, 0),
                             pipeline_mode=pl.Buffered(1)),
                pl.BlockSpec((pl.Element(PROJ_TN), pl.Element(D_MODEL)),
                             lambda i, j, off: (off[j] * ROW_ALIGN, 0)),
            ],
            out_specs=[
                pl.BlockSpec((rows, KV_OUT), lambda i, j, off: (0, 0)),
                pl.BlockSpec((rows, PROJ_TN), lambda i, j, off: (0, j)),
                pl.BlockSpec((rows, D_MODEL), lambda i, j, off: (0, 0)),
                pl.BlockSpec((PROJ_TN, D_MODEL), lambda i, j, off: (j, 0)),
                pl.BlockSpec((KV_OUT, D_MODEL), lambda i, j, off: (0, 0)),
            ],
        ),
        out_shape=[jax.ShapeDtypeStruct((rows, KV_OUT), F32),
                   jax.ShapeDtypeStruct((rows, MAIN_OUT), BF16),
                   jax.ShapeDtypeStruct((rows, D_MODEL), BF16),
                   jax.ShapeDtypeStruct((MAIN_OUT, D_MODEL), BF16),
                   jax.ShapeDtypeStruct((KV_OUT, D_MODEL), BF16)],
        compiler_params=pltpu.CompilerParams(
            dimension_semantics=("arbitrary", "arbitrary"), vmem_limit_bytes=VMEM_LIMIT),
        name="project_first",
    )(jnp.asarray(_main_row_offsets()), x, g, w_t, w_t, w_t)


def _project(x, g, wkv, wmain, tm, col0):
    rows = x.shape[0]
    assert rows % tm == 0
    tile0 = col0 * (D_MODEL // PROJ_TN)
    n_tiles = MAIN_OUT // PROJ_TN - tile0
    return pl.pallas_call(
        functools.partial(_proj_body, tile0=tile0, from_f32=False),
        grid=(rows // tm, n_tiles),
        in_specs=[
            pl.BlockSpec((tm, D_MODEL), lambda i, j: (i, 0)),
            pl.BlockSpec((1, D_MODEL), lambda i, j: (0, 0)),
            pl.BlockSpec((KV_OUT, D_MODEL), lambda i, j: (0, 0), pipeline_mode=pl.Buffered(1)),
            pl.BlockSpec((PROJ_TN, D_MODEL), lambda i, j: (j + tile0, 0)),
        ],
        out_specs=[
            pl.BlockSpec((tm, KV_OUT), lambda i, j: (i, 0)),
            pl.BlockSpec((tm, PROJ_TN), lambda i, j: (i, j)),
            pl.BlockSpec((tm, D_MODEL), lambda i, j: (i, 0)),
        ],
        out_shape=[jax.ShapeDtypeStruct((rows, KV_OUT), F32),
                   jax.ShapeDtypeStruct((rows, n_tiles * PROJ_TN), BF16),
                   jax.ShapeDtypeStruct((rows, D_MODEL), BF16)],
        compiler_params=pltpu.CompilerParams(
            dimension_semantics=("parallel", "arbitrary"), vmem_limit_bytes=VMEM_LIMIT),
        name="project",
    )(x, g, wkv, wmain)


def _proj_t_body(wq_ref, wv_ref, xn_ref, o_ref):
    xn = xn_ref[...]
    q_t = lax.dot_general(wq_ref[...], xn, _NT, preferred_element_type=F32) * Q_SCALE
    o_ref[:ATTN_WIDTH, :] = q_t.astype(BF16)
    o_ref[ATTN_WIDTH:, :] = lax.dot_general(wv_ref[...], xn, _NT, preferred_element_type=F32).astype(BF16)


def _project_t(wmain, wkv, xn, tm):
    rows = xn.shape[0]
    assert rows % tm == 0
    return pl.pallas_call(
        _proj_t_body,
        grid=(rows // tm,),
        in_specs=[pl.BlockSpec((ATTN_WIDTH, D_MODEL), lambda i: (COL_QA, 0), pipeline_mode=pl.Buffered(1)),
                  pl.BlockSpec((KV_WIDTH, D_MODEL), lambda i: (1, 0), pipeline_mode=pl.Buffered(1)),
                  pl.BlockSpec((tm, D_MODEL), lambda i: (i, 0))],
        out_specs=pl.BlockSpec((QV_ROWS, tm), lambda i: (0, i)),
        out_shape=jax.ShapeDtypeStruct((QV_ROWS, rows), BF16),
        compiler_params=pltpu.CompilerParams(
            dimension_semantics=("parallel",), vmem_limit_bytes=VMEM_LIMIT),
        name="project_t",
    )(wmain, wkv, xn)


def _attn_pair_setup(p, ka_ref, kb_ref, km_ref, vta_ref, vtb_ref, vtm_ref):
    k = jnp.concatenate([ka_ref[...], kb_ref[...], km_ref[...]], axis=0).astype(BF16)
    vt = jnp.concatenate([vta_ref[...], vtb_ref[...], vtm_ref[...]], axis=1)
    odd = (lax.broadcasted_iota(jnp.int32, (CHUNK, GQA_GROUP * PAIR), 1) % PAIR) >= CHUNK
    first = p == 0
    return k, vt, (odd | first, first, jnp.logical_not(odd))


def _attn_pair_probs(setup, h, qt_ref, sink_ref):
    k, vt, (mask_a, mask_b, mask_d) = setup
    kh = k[:, h * HEAD_DIM:(h + 1) * HEAD_DIM]
    rows = [(h * GQA_GROUP + g) * HEAD_DIM for g in range(GQA_GROUP)]
    qth = jnp.concatenate([qt_ref[r:r + HEAD_DIM, :] for r in rows], axis=1)
    st = jnp.dot(kh, qth, preferred_element_type=F32)
    st = jnp.concatenate([
        jnp.where(mask_a, NEG_INF, st[:CHUNK]),
        jnp.where(mask_b, NEG_INF, st[CHUNK:2 * CHUNK]),
        st[2 * CHUNK:3 * CHUNK],
        jnp.where(mask_d, NEG_INF, st[3 * CHUNK:4 * CHUNK]),
        st[4 * CHUNK:]], axis=0)
    sink = sink_ref[h]
    m = jnp.maximum(jnp.max(st, axis=0, keepdims=True), sink)
    return jnp.exp(st - m).astype(BF16), jnp.exp(sink - m)


def _attn_pair_out(setup, h, probs):
    _, vt, _ = setup
    e, e_sink = probs
    ones = jnp.ones((ONES_ROWS, PAIR_KEYS), BF16)
    vth = jnp.concatenate([vt[h * HEAD_DIM:(h + 1) * HEAD_DIM, :], ones], axis=0)
    ot = jnp.dot(vth, e, preferred_element_type=F32)
    denom = ot[HEAD_DIM:HEAD_DIM + 1, :] + e_sink
    ot = ot[:HEAD_DIM, :] * (1.0 / denom)
    return jnp.concatenate([ot[:, g * PAIR:(g + 1) * PAIR] for g in range(GQA_GROUP)], axis=0).T


def _attend_probs(q, k, sink_ref, lq):
    probs = []
    for h in range(N_KV_HEADS):
        kh = k[:, h * HEAD_DIM:(h + 1) * HEAD_DIM]
        heads = [h * GQA_GROUP + g for g in range(GQA_GROUP)]
        qh = jnp.concatenate([q[:, n * HEAD_DIM:(n + 1) * HEAD_DIM] for n in heads], axis=0)
        s = lax.dot_general(qh, kh, _NT, preferred_element_type=F32)
        sink = jnp.concatenate(
            [jnp.broadcast_to(sink_ref[n:n + 1, 0:1], (lq, 1)) for n in heads], axis=0)
        m = jnp.maximum(jnp.max(s, axis=-1, keepdims=True), sink)
        e = jnp.exp(s - m)
        probs.append((e.astype(BF16), jnp.sum(e, axis=-1, keepdims=True) + jnp.exp(sink - m)))
    return probs


def _attend_values(probs, v, o_ref, lq):
    for h in range(N_KV_HEADS):
        vh = v[:, h * HEAD_DIM:(h + 1) * HEAD_DIM]
        e, denom = probs[h]
        oh = jnp.dot(e, vh, preferred_element_type=F32) * (1.0 / denom)
        o_ref[:, h * GQA_GROUP * HEAD_DIM:(h + 1) * GQA_GROUP * HEAD_DIM] = jnp.concatenate(
            [oh[g * lq:(g + 1) * lq, :] for g in range(GQA_GROUP)], axis=1).astype(o_ref.dtype)


def _attn_sample_body(q_ref, ck_ref, cv_ref, kn_ref, vn_ref, km_ref, vm_ref, sink_ref, o_ref):
    lq = q_ref.shape[0]
    k = jnp.concatenate([ck_ref[...], kn_ref[...], km_ref[...]], axis=0).astype(BF16)
    probs = _attend_probs(q_ref[...], k, sink_ref, lq)
    v = jnp.concatenate([cv_ref[...], vn_ref[...], vm_ref[...]], axis=0).astype(BF16)
    _attend_values(probs, v, o_ref, lq)


def _attend_sample(o16, o32, cache_k, cache_v, meta_block, sinks, n_streams, ds):
    cache_rows = cache_k.shape[1]
    new = lambda col: pl.BlockSpec((ds, KV_WIDTH), lambda s: (s, col))
    meta = lambda col: pl.BlockSpec((N_META, KV_WIDTH), lambda s: (meta_block, col))
    cache = pl.BlockSpec((None, cache_rows, KV_WIDTH), lambda s: (s, 0, 0))
    return pl.pallas_call(
        _attn_sample_body,
        grid=(n_streams,),
        in_specs=[
            pl.BlockSpec((ds, ATTN_WIDTH), lambda s: (s, COL_QA)),
            cache, cache, new(0), new(1), meta(0), meta(1),
            pl.BlockSpec((N_Q_HEADS, LANES), lambda s: (0, 0)),
        ],
        out_specs=pl.BlockSpec((ds, ATTN_WIDTH), lambda s: (s, 0)),
        out_shape=jax.ShapeDtypeStruct((n_streams * ds, ATTN_WIDTH), BF16),
        compiler_params=pltpu.CompilerParams(
            dimension_semantics=("parallel",), vmem_limit_bytes=VMEM_LIMIT),
        name="attend_sample",
    )(o16, cache_k, cache_v, o32, o32, o32, o32, sinks)


def _mlstm_body(q_ref, k_ref, v_ref, gt_ref, bias_ref, c0_ref, n0_ref, m0_ref,
                hn_ref, c_ref, n_ref, m_ref, *, blk):
    setup = _mlstm_setup(gt_ref, bias_ref, blk)
    scores = [_mlstm_scores(setup, h, q_ref, k_ref, m0_ref) for h in range(M_HEADS)]
    for h in range(M_HEADS):
        hv = _mlstm_out(setup, h, scores[h], q_ref, v_ref, c0_ref, n0_ref)
        hn_ref[:, h * M_V_DIM:(h + 1) * M_V_DIM] = hv.astype(hn_ref.dtype)
    for h in range(M_HEADS):
        _mlstm_state(setup, h, scores[h], k_ref, v_ref, (c0_ref, n0_ref, m0_ref), (c_ref, n_ref, m_ref),
                     blk, None)


def _mlstm_setup(gt_ref, bias_ref, blk):
    x = gt_ref[...] + bias_ref[...]
    lane = lax.broadcasted_iota(jnp.int32, x.shape, 1)
    log_f = jnp.minimum(x, 0.0) - jnp.log(1.0 + jnp.exp(-jnp.abs(x)))
    gates = jnp.where(lane < M_HEADS, x, log_f)
    row = lax.broadcasted_iota(jnp.int32, (blk, blk), 0)
    col = lax.broadcasted_iota(jnp.int32, (blk, blk), 1)
    causal = row >= col
    hi = gates.astype(BF16)
    rest = gates - hi.astype(F32)
    mid = rest.astype(BF16)
    lo = (rest - mid.astype(F32)).astype(BF16)
    parts = jnp.dot(causal.astype(BF16), jnp.concatenate([hi, mid, lo], axis=1), preferred_element_type=F32)
    csum = parts[:, :LANES] + parts[:, LANES:2 * LANES] + parts[:, 2 * LANES:]
    z = jnp.where(lane < M_HEADS, gates, csum)
    pad = (-blk) % LANES
    zsq = z if pad == 0 else jnp.concatenate([z, jnp.zeros((pad, LANES), F32)], axis=0)
    zt = zsq.T[:, :blk]
    return z, zt, causal


def _mlstm_scores(setup, h, q_ref, k_ref, m_ref):
    z, zt, causal = setup
    g_r = zt[h:h + 1, :] - zt[M_HEADS + h:M_HEADS + h + 1, :]
    m_prev = m_ref[h:h + 1, 0:1]
    d0 = jnp.where(causal, g_r, -jnp.inf)
    mm = jnp.maximum(m_prev, jnp.max(d0, axis=-1, keepdims=True))
    w = jnp.exp(d0 - mm)
    qh = q_ref[:, h * M_QK_DIM:(h + 1) * M_QK_DIM]
    kh = k_ref[:, h * M_QK_DIM:(h + 1) * M_QK_DIM]
    qk = lax.dot_general(qh, kh, _NT, preferred_element_type=F32)
    wqk = w * (qk * MLSTM_K_SCALE)
    return m_prev, mm, wqk.astype(BF16), jnp.sum(wqk, axis=-1, keepdims=True)


def _mlstm_finish(setup, h, scores, q_ref, k_ref, v_ref, c_ref, n_ref, m_ref, blk, live):
    hv = _mlstm_out(setup, h, scores, q_ref, v_ref, c_ref, n_ref)
    state = (c_ref, n_ref, m_ref)
    _mlstm_state(setup, h, scores, k_ref, v_ref, state, state, blk, live)
    return hv


def _mlstm_out(setup, h, scores, q_ref, v_ref, c_ref, n_ref):
    z, _, _ = setup
    m_prev, mm, wqk, den_intra = scores
    m_t = z[:, M_HEADS + h:M_HEADS + h + 1] + mm
    a = jnp.exp(m_prev - mm)
    qh = q_ref[:, h * M_QK_DIM:(h + 1) * M_QK_DIM]
    vh = v_ref[:, h * M_V_DIM:(h + 1) * M_V_DIM]
    num = a * jnp.dot(qh, c_ref[h].astype(BF16), preferred_element_type=F32) + jnp.dot(
        wqk, vh, preferred_element_type=F32)
    den = a * jnp.sum(qh.astype(F32) * n_ref[h:h + 1, :], axis=-1, keepdims=True) + den_intra
    r = 1.0 / jnp.maximum(jnp.abs(den), jnp.exp(-m_t))
    rms_h = r * jnp.sqrt(jnp.mean(num * num, axis=-1, keepdims=True))
    return num * (r * lax.rsqrt(rms_h * rms_h + EPS))


def _mlstm_state(setup, h, scores, k_ref, v_ref, state_in, state_out, blk, live):
    c_ref, n_ref, m_in = state_in
    c_out, n_out, m_out = state_out
    z, _, _ = setup
    m_prev, mm, _, _ = scores
    keep = (lambda new, old: new) if live is None else (lambda new, old: jnp.where(live, new, old))
    li_c = z[:, h:h + 1]
    b_c = z[:, M_HEADS + h:M_HEADS + h + 1]
    m_t = b_c + mm
    kh = k_ref[:, h * M_QK_DIM:(h + 1) * M_QK_DIM]
    vh = v_ref[:, h * M_V_DIM:(h + 1) * M_V_DIM]
    c_h = c_ref[h]
    n_h = n_ref[h:h + 1, :]

    m_new = m_t[blk - 1:blk, :]
    b_last = b_c[blk - 1:blk, :]
    ws = jnp.exp(b_last - b_c + li_c - m_new) * MLSTM_K_SCALE
    a_last = jnp.exp(b_last + m_prev - m_new)
    vs = (vh.astype(F32) * ws).astype(BF16)
    ktv = lax.dot_general(kh, vs, (((0,), (0,)), ((), ())), preferred_element_type=F32)
    c_out[h] = keep(a_last * c_h + ktv, c_h)
    n_out[h:h + 1, :] = keep(a_last * n_h + jnp.sum(kh.astype(F32) * ws, axis=0, keepdims=True), n_h)
    m_out[h:h + 1, :] = keep(jnp.broadcast_to(m_new, (1, LANES)), m_in[h:h + 1, :])


def _mlstm(o16, o32, bias, c0, n0, m0, *, blk, n_streams, first_block, col0):
    rb = lambda s: first_block + s
    qk_blk = 2 * (COL_QKM - col0)
    state = lambda *shape: pl.BlockSpec((None,) + shape, lambda s: (s,) + (0,) * len(shape))
    return pl.pallas_call(
        functools.partial(_mlstm_body, blk=blk),
        grid=(n_streams,),
        in_specs=[
            pl.BlockSpec((blk, M_QK_WIDTH), lambda s: (rb(s), qk_blk)),
            pl.BlockSpec((blk, M_QK_WIDTH), lambda s: (rb(s), qk_blk + 1)),
            pl.BlockSpec((blk, M_V_WIDTH), lambda s: (rb(s), COL_VM - col0)),
            pl.BlockSpec((blk, LANES), lambda s: (rb(s), GATE_BLOCK)),
            pl.BlockSpec((1, LANES), lambda s: (0, 0)),
            state(M_HEADS, M_QK_DIM, M_V_DIM), state(M_HEADS, M_QK_DIM), state(M_HEADS, LANES),
        ],
        out_specs=[
            pl.BlockSpec((blk, M_V_WIDTH), lambda s: (s, 0)),
            state(M_HEADS, M_QK_DIM, M_V_DIM), state(M_HEADS, M_QK_DIM), state(M_HEADS, LANES),
        ],
        out_shape=[
            jax.ShapeDtypeStruct((n_streams * blk, M_V_WIDTH), BF16),
            jax.ShapeDtypeStruct((n_streams, M_HEADS, M_QK_DIM, M_V_DIM), F32),
            jax.ShapeDtypeStruct((n_streams, M_HEADS, M_QK_DIM), F32),
            jax.ShapeDtypeStruct((n_streams, M_HEADS, LANES), F32),
        ],
        compiler_params=pltpu.CompilerParams(
            dimension_semantics=("parallel",), vmem_limit_bytes=VMEM_LIMIT),
        name="mlstm_blk%d" % blk,
    )(o16, o16, o16, o32, bias, c0, n0, m0)


def _merge_body(x_ref, oa_ref, hn_ref, za_ref, om_ref, zm_ref, ga_ref, gm_ref, gmh_ref,
                wpa_ref, wpm_ref, wout_ref, gf_ref, y_ref, wpa_out, wpm_out, wout_out, a_scr, m_scr, acc_scr):
    j = pl.program_id(0)

    @pl.when(j == 0)
    def _():
        a_scr[...] = (oa_ref[...].astype(F32) * za_ref[...].astype(F32)).astype(BF16)
        m_scr[...] = ((hn_ref[...].astype(F32) * gmh_ref[...]) * om_ref[...].astype(F32)
                      * zm_ref[...].astype(F32)).astype(BF16)
        acc_scr[...] = x_ref[...]

    wpa_out[...] = wpa_ref[...].astype(BF16)
    wpm_out[...] = wpm_ref[...].astype(BF16)
    wout_out[...] = wout_ref[...].astype(BF16)
    ya = jnp.dot(a_scr[...], wpa_out[...], preferred_element_type=F32)
    ym = jnp.dot(m_scr[...], wpm_out[...], preferred_element_type=F32)
    merged = ga_ref[...].astype(F32) * ya + gm_ref[...].astype(F32) * ym
    acc_scr[...] += jnp.dot(merged.astype(BF16), wout_out[...], preferred_element_type=F32)

    @pl.when(j == pl.num_programs(0) - 1)
    def _():
        xo = acc_scr[...]
        ms = jnp.mean(xo * xo, axis=-1, keepdims=True)
        y_ref[...] = (xo * lax.rsqrt(ms + EPS)) * gf_ref[...]


def _merge(x, o_att, hn, o16, g_mh, w_pa, w_pm, w_out, g_final, rows, col0):
    n_chunks = D_MODEL // MERGE_TN
    per_group = D_MODEL // MERGE_TN
    full = lambda col: pl.BlockSpec((rows, D_MODEL), lambda j: (0, col))
    gate = lambda col: pl.BlockSpec((rows, MERGE_TN), lambda j: (0, col * per_group + j))
    vec = pl.BlockSpec((1, D_MODEL), lambda j: (0, 0))
    w_cols = pl.BlockSpec((D_MODEL, MERGE_TN), lambda j: (0, j))
    w_rows = pl.BlockSpec((MERGE_TN, D_MODEL), lambda j: (j, 0))
    return pl.pallas_call(
        _merge_body,
        grid=(n_chunks,),
        in_specs=[full(0), full(0), full(0), full(COL_ZA - col0), full(COL_OM - col0), full(COL_ZM - col0),
                  gate(COL_GA - col0), gate(COL_GM - col0), vec, w_cols, w_cols, w_rows, vec],
        out_specs=[full(0), w_cols, w_cols, w_rows],
        out_shape=[jax.ShapeDtypeStruct((rows, D_MODEL), F32)]
        + [jax.ShapeDtypeStruct((D_MODEL, D_MODEL), BF16)] * 3,
        scratch_shapes=[pltpu.VMEM((rows, D_MODEL), BF16), pltpu.VMEM((rows, D_MODEL), BF16),
                        pltpu.VMEM((rows, D_MODEL), F32)],
        compiler_params=pltpu.CompilerParams(
            dimension_semantics=("arbitrary",), vmem_limit_bytes=VMEM_LIMIT),
        name="merge",
    )(x, o_att, hn, o16, o16, o16, o16, o16, g_mh, w_pa, w_pm, w_out, g_final)


def _mixer_body(qt_ref, ka_ref, kb_ref, km_ref, vta_ref, vtb_ref, vtm_ref, sink_ref,
                q_ref, k_ref, v_ref, gt_ref, bias_ref, c0_ref, n0_ref, m0_ref,
                x_ref, za_ref, om_ref, zm_ref, ga_ref, gm_ref, gmh_ref, wpa_ref, wpm_ref, wout_ref, gf_ref,
                y_ref, c_ref, n_ref, m_ref, a_scr, m_scr, *, n_blk):
    c = pl.program_id(0)

    @pl.when(c == 0)
    def _():
        c_ref[...] = c0_ref[...]
        n_ref[...] = n0_ref[...]
        m_ref[...] = m0_ref[...]
        a_scr[...] = jnp.zeros(a_scr.shape, a_scr.dtype)
        m_scr[...] = jnp.zeros(m_scr.shape, m_scr.dtype)

    wr = c % MIX_SLOTS
    rd = 1 - wr
    blk = jnp.minimum(c, n_blk - 1)
    live = c < n_blk
    width = GQA_GROUP * HEAD_DIM
    heads_per_piece = width // M_V_DIM
    a_in = a_scr[rd]
    m_in = m_scr[rd]

    def mlstm_piece(h):
        hc = slice(h * M_V_DIM, (h + 1) * M_V_DIM)
        hv = _mlstm_finish(cell, h, scores[h], q_ref, k_ref, v_ref, c_ref, n_ref, m_ref, PAIR, live)
        m_scr[wr, :, hc] = ((hv * gmh_ref[:, hc]) * om_ref[:, hc].astype(F32)
                            * zm_ref[:, hc].astype(F32)).astype(BF16)

    def attn_piece(h, probs):
        cols = slice(h * width, (h + 1) * width)
        oa = _attn_pair_out(attn, h, probs)
        a_scr[wr, :, cols] = (oa * za_ref[:, cols].astype(F32)).astype(BF16)

    attn = _attn_pair_setup(blk, ka_ref, kb_ref, km_ref, vta_ref, vtb_ref, vtm_ref)
    cell = _mlstm_setup(gt_ref, bias_ref, PAIR)
    scores = [_mlstm_scores(cell, h, q_ref, k_ref, m_ref) for h in range(M_HEADS)]
    probs = _attn_pair_probs(attn, 0, qt_ref, sink_ref)
    merged = []
    for i in range(N_KV_HEADS):
        cols = slice(i * width, (i + 1) * width)
        ya = jnp.dot(a_in, wpa_ref[:, cols], preferred_element_type=F32)
        mlstm_piece(heads_per_piece * i)
        next_probs = _attn_pair_probs(attn, i + 1, qt_ref, sink_ref) if i + 1 < N_KV_HEADS else None
        ym = jnp.dot(m_in, wpm_ref[:, cols], preferred_element_type=F32)
        mlstm_piece(heads_per_piece * i + 1)
        merged.append((ga_ref[:, cols].astype(F32) * ya + gm_ref[:, cols].astype(F32) * ym).astype(BF16))
        if i + 1 == N_KV_HEADS:
            xo = x_ref[...] + jnp.dot(jnp.concatenate(merged, axis=1), wout_ref[...],
                                      preferred_element_type=F32)
        attn_piece(i, probs)
        probs = next_probs
    ms = jnp.mean(xo * xo, axis=-1, keepdims=True)
    y_ref[...] = (xo * lax.rsqrt(ms + EPS)) * gf_ref[...]


def _mixer(x, qvt, o32, o16, k_meta, vt_meta, sinks_t, bias, c0, n0, m0, g_mh, w_pa, w_pm, w_out, g_final,
           col0):
    seq = x.shape[0]
    assert seq % PAIR == 0
    n_blk = seq // PAIR
    cur = lambda c: jnp.minimum(c, n_blk - 1)
    prv = lambda c: jnp.maximum(cur(c) - 1, 0)
    mrg = lambda c: jnp.maximum(c - 1, 0)
    v_row = ATTN_WIDTH // KV_WIDTH
    qk_blk = 2 * (COL_QKM - col0)
    tile = lambda col: pl.BlockSpec((PAIR, D_MODEL), lambda c: (mrg(c), col))
    tile_cur = lambda col: pl.BlockSpec((PAIR, D_MODEL), lambda c: (cur(c), col))
    vec = pl.BlockSpec((1, D_MODEL), lambda c: (0, 0))
    weight = pl.BlockSpec((D_MODEL, D_MODEL), lambda c: (0, 0), pipeline_mode=pl.Buffered(1))
    state = lambda *blk: pl.BlockSpec((None,) + blk, lambda c: (0,) * (len(blk) + 1))
    return pl.pallas_call(
        functools.partial(_mixer_body, n_blk=n_blk),
        grid=(n_blk + 1,),
        in_specs=[
            pl.BlockSpec((ATTN_WIDTH, PAIR), lambda c: (0, cur(c))),
            pl.BlockSpec((PAIR, KV_WIDTH), lambda c: (prv(c), 0)),
            pl.BlockSpec((PAIR, KV_WIDTH), lambda c: (cur(c), 0)),
            pl.BlockSpec((N_META, KV_WIDTH), lambda c: (0, 0)),
            pl.BlockSpec((KV_WIDTH, PAIR), lambda c: (v_row, prv(c))),
            pl.BlockSpec((KV_WIDTH, PAIR), lambda c: (v_row, cur(c))),
            pl.BlockSpec((KV_WIDTH, N_META), lambda c: (0, 0)),
            pl.BlockSpec((N_KV_HEADS, 1, GQA_GROUP * PAIR), lambda c: (0, 0, 0)),
            pl.BlockSpec((PAIR, M_QK_WIDTH), lambda c: (cur(c), qk_blk)),
            pl.BlockSpec((PAIR, M_QK_WIDTH), lambda c: (cur(c), qk_blk + 1)),
            pl.BlockSpec((PAIR, M_V_WIDTH), lambda c: (cur(c), COL_VM - col0)),
            pl.BlockSpec((PAIR, LANES), lambda c: (cur(c), GATE_BLOCK)),
            pl.BlockSpec((1, LANES), lambda c: (0, 0)),
            state(M_HEADS, M_QK_DIM, M_V_DIM), state(M_HEADS, M_QK_DIM), state(M_HEADS, LANES),
            tile(0), tile_cur(COL_ZA - col0), tile_cur(COL_OM - col0), tile_cur(COL_ZM - col0),
            tile(COL_GA - col0), tile(COL_GM - col0), vec, weight, weight, weight, vec,
        ],
        out_specs=[
            tile(0),
            state(M_HEADS, M_QK_DIM, M_V_DIM), state(M_HEADS, M_QK_DIM), state(M_HEADS, LANES),
        ],
        out_shape=[
            jax.ShapeDtypeStruct((seq, D_MODEL), F32),
            jax.ShapeDtypeStruct((1, M_HEADS, M_QK_DIM, M_V_DIM), F32),
            jax.ShapeDtypeStruct((1, M_HEADS, M_QK_DIM), F32),
            jax.ShapeDtypeStruct((1, M_HEADS, LANES), F32),
        ],
        scratch_shapes=[pltpu.VMEM((MIX_SLOTS, PAIR, ATTN_WIDTH), BF16),
                        pltpu.VMEM((MIX_SLOTS, PAIR, M_V_WIDTH), BF16)],
        compiler_params=pltpu.CompilerParams(
            dimension_semantics=("arbitrary",), vmem_limit_bytes=VMEM_LIMIT),
        name="mixer",
    )(qvt, o32, o32, k_meta, qvt, qvt, vt_meta, sinks_t,
      o16, o16, o16, o32, bias, c0, n0, m0,
      x, o16, o16, o16, o16, o16, g_mh, w_pa, w_pm, w_out, g_final)


def kernel(x_prompt, x_sample, cache_k, cache_v, state_C, state_n, state_m, meta_tokens, g_norm, w_in,
           b_igate, b_fgate, attn_sinks, g_mhnorm, w_pa, w_pm, w_out, g_final):
    batch, seq, _ = x_prompt.shape
    db, ds, _ = x_sample.shape
    depth = w_in.shape[0]
    assert batch == 1 and depth == 1 and ds == N_META
    cache_rows = cache_k.shape[2]

    w_t = w_in[0].T
    g_in = g_norm[0].reshape(1, D_MODEL)
    bias = jnp.concatenate([b_igate[0], b_fgate[0], jnp.zeros((LANES - 2 * M_HEADS,), F32)]).reshape(1, LANES)
    sinks = jnp.broadcast_to(attn_sinks[0][:, None], (N_Q_HEADS, LANES))
    sinks_t = jnp.broadcast_to(attn_sinks[0].reshape(N_KV_HEADS, 1, GQA_GROUP, 1),
                               (N_KV_HEADS, 1, GQA_GROUP, PAIR)).reshape(N_KV_HEADS, 1, GQA_GROUP * PAIR)
    g_mh = g_mhnorm[0].reshape(1, M_V_WIDTH)
    g_fin = g_final.reshape(1, D_MODEL)

    xp = x_prompt.reshape(seq, D_MODEL)
    xs_rows = db * ds
    x_small = jnp.concatenate([x_sample.reshape(xs_rows, D_MODEL), meta_tokens.astype(F32)], axis=0)
    meta_block = xs_rows // N_META
    p_tm = min(seq, 1024)
    s32, s16, _, wmain, wkv = _project_first(x_small, g_in, w_t)
    p32, p16, xn_p = _project(xp, g_in, wkv, wmain, tm=p_tm, col0=COL_ZA)
    qvt = _project_t(wmain, wkv, xn_p, tm=p_tm)

    zeros = lambda *shape: jnp.zeros(shape, F32)
    _, c_meta, n_meta, m_meta = _mlstm(
        s16, s32, bias, zeros(1, M_HEADS, M_QK_DIM, M_V_DIM), zeros(1, M_HEADS, M_QK_DIM),
        zeros(1, M_HEADS, LANES), blk=N_META, n_streams=1, first_block=meta_block, col0=COL_QA)

    ck = cache_k[0].reshape(db, cache_rows, KV_WIDTH)
    cv = cache_v[0].reshape(db, cache_rows, KV_WIDTH)
    oa_s = _attend_sample(s16, s32, ck, cv, meta_block, sinks, db, ds)
    m0_s = jnp.broadcast_to(state_m[0][:, :, None], (db, M_HEADS, LANES))
    hn_s, c_s, n_s, m_s = _mlstm(s16, s32, bias, state_C[0], state_n[0], m0_s, blk=ds, n_streams=db,
                                 first_block=0, col0=COL_QA)
    y_s, wpa, wpm, wout = _merge(x_small, oa_s, hn_s, s16, g_mh, w_pa[0], w_pm[0], w_out[0], g_fin,
                                 rows=xs_rows, col0=COL_QA)

    k_meta = s32[xs_rows:, :KV_WIDTH]
    vt_meta = s32[xs_rows:, KV_WIDTH:2 * KV_WIDTH].T.astype(BF16)
    y_p, c_p, n_p, m_p = _mixer(xp, qvt, p32, p16, k_meta, vt_meta, sinks_t, bias, c_meta, n_meta, m_meta,
                                g_mh, wpa, wpm, wout, g_fin, col0=COL_ZA)

    kv_shape = (1, batch, cache_rows, N_KV_HEADS, HEAD_DIM)
    k_p = p32[seq - cache_rows:, :KV_WIDTH].reshape(kv_shape)
    v_p = p32[seq - cache_rows:, KV_WIDTH:2 * KV_WIDTH].reshape(kv_shape)
    new_k = s32[:xs_rows, :KV_WIDTH].reshape(db, ds, KV_WIDTH)
    new_v = s32[:xs_rows, KV_WIDTH:2 * KV_WIDTH].reshape(db, ds, KV_WIDTH)
    skv_shape = (1, db, cache_rows, N_KV_HEADS, HEAD_DIM)
    k_s = jnp.concatenate([ck, new_k], axis=1)[:, -cache_rows:].reshape(skv_shape)
    v_s = jnp.concatenate([cv, new_v], axis=1)[:, -cache_rows:].reshape(skv_shape)

    return (y_p.reshape(batch, seq, D_MODEL), y_s.reshape(db, ds, D_MODEL),
            k_p, v_p, c_p[None], n_p[None], m_p[None, :, :, 0],
            k_s, v_s, c_s[None], n_s[None], m_s[None, :, :, 0])
```

```python
import functools
import math

import jax
import jax.numpy as jnp
import numpy as np
from jax import lax
from jax.experimental import pallas as pl
from jax.experimental.pallas import tpu as pltpu

F32 = jnp.float32
BF16 = jnp.bfloat16

D_MODEL = 2048
CHUNK = 64
N_META = 16
HEAD_DIM = 64
N_Q_HEADS = 32
N_KV_HEADS = 4
GQA_GROUP = N_Q_HEADS // N_KV_HEADS
WINDOW = 128
WIN_CHUNKS = WINDOW // CHUNK
ATTN_WIDTH = N_Q_HEADS * HEAD_DIM
KV_WIDTH = N_KV_HEADS * HEAD_DIM
M_HEADS = 8
M_QK_DIM = 128
M_V_DIM = 256
M_QK_WIDTH = M_HEADS * M_QK_DIM
M_V_WIDTH = M_HEADS * M_V_DIM
EPS = 1e-6
NEG_INF = -1e30
COL_SIZES = (ATTN_WIDTH, KV_WIDTH, KV_WIDTH, ATTN_WIDTH, M_QK_WIDTH, M_QK_WIDTH, M_V_WIDTH, M_V_WIDTH,
             M_HEADS, M_HEADS, M_V_WIDTH, D_MODEL, D_MODEL)
COL_STARTS = tuple(int(v) for v in np.cumsum((0,) + COL_SIZES[:-1]))
(SRC_QA, SRC_KA, SRC_VA, SRC_ZA, SRC_QM, SRC_KM, SRC_VM, SRC_OM, SRC_IG, SRC_FG, SRC_ZM, SRC_GA,
 SRC_GM) = COL_STARTS

LANES = 128
VMEM_LIMIT = 56 * 1024 * 1024

KV_OUT = 2 * KV_WIDTH + LANES
GATE_BLOCK = 2 * KV_WIDTH // LANES
MAIN_OUT = 8 * D_MODEL
COL_QA, COL_ZA, COL_QKM, COL_VM, COL_OM, COL_ZM, COL_GA, COL_GM = range(8)
PROJ_TN = 1024
ROW_ALIGN = 16
MLSTM_K_SCALE = M_QK_DIM ** -0.5
Q_SCALE = 1.0 / math.sqrt(HEAD_DIM)
QV_ROWS = ATTN_WIDTH + KV_WIDTH
PAIR = 2 * CHUNK
PAIR_KEYS = 2 * PAIR + N_META
ONES_ROWS = 16
MIX_SLOTS = 2
MERGE_TN = 256
NORM_SLABS = 4


def _sigmoid(x):
    return 0.5 * jnp.tanh(0.5 * x) + 0.5


def _silu(x):
    return x * _sigmoid(x)


_NT = (((1,), (1,)), ((), ()))


def _main_row_offsets():
    groups = [(SRC_QA, ATTN_WIDTH), (SRC_ZA, ATTN_WIDTH), (SRC_QM, M_QK_WIDTH), (SRC_KM, M_QK_WIDTH),
              (SRC_VM, M_V_WIDTH), (SRC_OM, M_V_WIDTH), (SRC_ZM, M_V_WIDTH), (SRC_GA, D_MODEL),
              (SRC_GM, D_MODEL)]
    offs = []
    for start, width in groups:
        assert start % ROW_ALIGN == 0 and width % PROJ_TN == 0
        offs += [(start + b * PROJ_TN) // ROW_ALIGN for b in range(width // PROJ_TN)]
    assert len(offs) * PROJ_TN == MAIN_OUT
    return np.asarray(offs, np.int32)


def _proj_body(*refs, tile0, from_f32):
    if from_f32:
        _, x_ref, g_ref, wkv_ref, wg_ref, w_ref, o32_ref, o16_ref, xn_ref, wmain_out, wkv_out = refs
    else:
        x_ref, g_ref, wkv_ref, w_ref, o32_ref, o16_ref, xn_ref = refs

    @pl.when(pl.program_id(1) == 0)
    def _():
        if from_f32:
            wkv_out[:2 * KV_WIDTH, :] = wkv_ref[...].astype(BF16)
            wkv_out[2 * KV_WIDTH:2 * KV_WIDTH + ROW_ALIGN, :] = wg_ref[...].astype(BF16)
            wkv_out[2 * KV_WIDTH + ROW_ALIGN:, :] = jnp.zeros((LANES - ROW_ALIGN, D_MODEL), BF16)
            wkv = wkv_out[...]
        else:
            wkv = wkv_ref[...]
        tm = x_ref.shape[0]
        slab = tm // NORM_SLABS if tm % (NORM_SLABS * ROW_ALIGN) == 0 else tm
        for r in range(0, tm, slab):
            x = x_ref[r:r + slab, :]
            ms = jnp.mean(x * x, axis=-1, keepdims=True)
            xn = ((x * lax.rsqrt(ms + EPS)) * g_ref[...]).astype(BF16)
            xn_ref[r:r + slab, :] = xn
            o32_ref[r:r + slab, :] = lax.dot_general(xn, wkv, _NT, preferred_element_type=F32)

    group = (pl.program_id(1) + tile0) // (D_MODEL // PROJ_TN)
    is_sigmoid = (group == COL_OM) | (group == COL_GA) | (group == COL_GM)
    is_silu = (group == COL_ZA) | (group == COL_ZM)
    is_q = group == COL_QA

    def tile(act):
        w = w_ref[...].astype(BF16)
        if from_f32:
            wmain_out[...] = w
        acc = lax.dot_general(xn_ref[...], w, _NT, preferred_element_type=F32)
        o16_ref[...] = act(acc).astype(BF16)

    pl.when(is_sigmoid)(lambda: tile(_sigmoid))
    pl.when(is_silu)(lambda: tile(_silu))
    pl.when(is_q)(lambda: tile(lambda acc: acc * Q_SCALE))
    pl.when(jnp.logical_not(is_sigmoid | is_silu | is_q))(lambda: tile(lambda acc: acc))


def _project_first(x, g, w_t):
    rows = x.shape[0]
    assert SRC_KA % (2 * KV_WIDTH) == 0 and SRC_VA == SRC_KA + KV_WIDTH
    assert SRC_IG % ROW_ALIGN == 0 and SRC_FG == SRC_IG + M_HEADS and 2 * M_HEADS == ROW_ALIGN
    n_tiles = MAIN_OUT // PROJ_TN
    return pl.pallas_call(
        functools.partial(_proj_body, tile0=0, from_f32=True),
        grid_spec=pltpu.PrefetchScalarGridSpec(
            num_scalar_prefetch=1,
            grid=(1, n_tiles),
            in_specs=[
                pl.BlockSpec((rows, D_MODEL), lambda i, j, off: (0, 0)),
                pl.BlockSpec((1, D_MODEL), lambda i, j, off: (0, 0)),
                pl.BlockSpec((2 * KV_WIDTH, D_MODEL), lambda i, j, off: (SRC_KA // (2 * KV_WIDTH), 0),
                             pipeline_mode=pl.Buffered(1)),
                pl.BlockSpec((ROW_ALIGN, D_MODEL), lambda i, j, off: (SRC_IG // ROW_ALIGN, 0),
                             pipeline_mode=pl.Buffered(1)),
                pl.BlockSpec((pl.Element(PROJ_TN), pl.Element(D_MODEL)),
                             lambda i, j, off: (off[j] * ROW_ALIGN, 0)),
            ],
            out_specs=[
                pl.BlockSpec((rows, KV_OUT), lambda i, j, off: (0, 0)),
                pl.BlockSpec((rows, PROJ_TN), lambda i, j, off: (0, j)),
                pl.BlockSpec((rows, D_MODEL), lambda i, j, off: (0, 0)),
                pl.BlockSpec((PROJ_TN, D_MODEL), lambda i, j, off: (j, 0)),
                pl.BlockSpec((KV_OUT, D_MODEL), lambda i, j, off: (0, 0)),
            ],
        ),
        out_shape=[jax.ShapeDtypeStruct((rows, KV_OUT), F32),
                   jax.ShapeDtypeStruct((rows, MAIN_OUT), BF16),
                   jax.ShapeDtypeStruct((rows, D_MODEL), BF16),
                   jax.ShapeDtypeStruct((MAIN_OUT, D_MODEL), BF16),
                   jax.ShapeDtypeStruct((KV_OUT, D_MODEL), BF16)],
        compiler_params=pltpu.CompilerParams(
            dimension_semantics=("arbitrary", "arbitrary"), vmem_limit_bytes=VMEM_LIMIT),
        name="project_first",
    )(jnp.asarray(_main_row_offsets()), x, g, w_t, w_t, w_t)


def _project(x, g, wkv, wmain, tm, col0):
    rows = x.shape[0]
    assert rows % tm == 0
    tile0 = col0 * (D_MODEL // PROJ_TN)
    n_tiles = MAIN_OUT // PROJ_TN - tile0
    return pl.pallas_call(
        functools.partial(_proj_body, tile0=tile0, from_f32=False),
        grid=(rows // tm, n_tiles),
        in_specs=[
            pl.BlockSpec((tm, D_MODEL), lambda i, j: (i, 0)),
            pl.BlockSpec((1, D_MODEL), lambda i, j: (0, 0)),
            pl.BlockSpec((KV_OUT, D_MODEL), lambda i, j: (0, 0), pipeline_mode=pl.Buffered(1)),
            pl.BlockSpec((PROJ_TN, D_MODEL), lambda i, j: (j + tile0, 0)),
        ],
        out_specs=[
            pl.BlockSpec((tm, KV_OUT), lambda i, j: (i, 0)),
            pl.BlockSpec((tm, PROJ_TN), lambda i, j: (i, j)),
            pl.BlockSpec((tm, D_MODEL), lambda i, j: (i, 0)),
        ],
        out_shape=[jax.ShapeDtypeStruct((rows, KV_OUT), F32),
                   jax.ShapeDtypeStruct((rows, n_tiles * PROJ_TN), BF16),
                   jax.ShapeDtypeStruct((rows, D_MODEL), BF16)],
        compiler_params=pltpu.CompilerParams(
            dimension_semantics=("parallel", "arbitrary"), vmem_limit_bytes=VMEM_LIMIT),
        name="project",
    )(x, g, wkv, wmain)


def _proj_t_body(wq_ref, wv_ref, xn_ref, o_ref):
    xn = xn_ref[...]
    q_t = (lax.dot_general(wq_ref[...], xn, _NT, preferred_element_type=F32) * Q_SCALE).astype(BF16)
    v_t = lax.dot_general(wv_ref[...], xn, _NT, preferred_element_type=F32).astype(BF16)
    for p in range(o_ref.shape[0]):
        o_ref[p, :ATTN_WIDTH, :] = q_t[:, p * PAIR:(p + 1) * PAIR]
        o_ref[p, ATTN_WIDTH:, :] = v_t[:, p * PAIR:(p + 1) * PAIR]


def _project_t(wmain, wkv, xn, tm):
    rows = xn.shape[0]
    assert rows % tm == 0 and tm % PAIR == 0
    return pl.pallas_call(
        _proj_t_body,
        grid=(rows // tm,),
        in_specs=[pl.BlockSpec((ATTN_WIDTH, D_MODEL), lambda i: (COL_QA, 0), pipeline_mode=pl.Buffered(1)),
                  pl.BlockSpec((KV_WIDTH, D_MODEL), lambda i: (1, 0), pipeline_mode=pl.Buffered(1)),
                  pl.BlockSpec((tm, D_MODEL), lambda i: (i, 0))],
        out_specs=pl.BlockSpec((tm // PAIR, QV_ROWS, PAIR), lambda i: (i, 0, 0)),
        out_shape=jax.ShapeDtypeStruct((rows // PAIR, QV_ROWS, PAIR), BF16),
        compiler_params=pltpu.CompilerParams(
            dimension_semantics=("parallel",), vmem_limit_bytes=VMEM_LIMIT),
        name="project_t",
    )(wmain, wkv, xn)


def _attn_pair_setup(p, ka_ref, kb_ref, km_ref, vta_ref, vtb_ref, vtm_ref):
    k = jnp.concatenate([ka_ref[...], kb_ref[...], km_ref[...]], axis=0).astype(BF16)
    vt = jnp.concatenate([vta_ref[...], vtb_ref[...], vtm_ref[...]], axis=1)
    odd = (lax.broadcasted_iota(jnp.int32, (CHUNK, GQA_GROUP * PAIR), 1) % PAIR) >= CHUNK
    first = p == 0
    return k, vt, (odd | first, first, jnp.logical_not(odd))


def _attn_pair_probs(setup, h, qt_ref, sink_ref):
    k, vt, (mask_a, mask_b, mask_d) = setup
    kh = k[:, h * HEAD_DIM:(h + 1) * HEAD_DIM]
    rows = [(h * GQA_GROUP + g) * HEAD_DIM for g in range(GQA_GROUP)]
    qth = jnp.concatenate([qt_ref[r:r + HEAD_DIM, :] for r in rows], axis=1)
    st = jnp.dot(kh, qth, preferred_element_type=F32)
    st = jnp.concatenate([
        jnp.where(mask_a, NEG_INF, st[:CHUNK]),
        jnp.where(mask_b, NEG_INF, st[CHUNK:2 * CHUNK]),
        st[2 * CHUNK:3 * CHUNK],
        jnp.where(mask_d, NEG_INF, st[3 * CHUNK:4 * CHUNK]),
        st[4 * CHUNK:]], axis=0)
    sink = sink_ref[h]
    m = jnp.maximum(jnp.max(st, axis=0, keepdims=True), sink)
    return jnp.exp(st - m).astype(BF16), jnp.exp(sink - m)


def _attn_pair_out(setup, h, probs):
    _, vt, _ = setup
    e, e_sink = probs
    ones = jnp.ones((ONES_ROWS, PAIR_KEYS), BF16)
    vth = jnp.concatenate([vt[h * HEAD_DIM:(h + 1) * HEAD_DIM, :], ones], axis=0)
    ot = jnp.dot(vth, e, preferred_element_type=F32)
    denom = ot[HEAD_DIM:HEAD_DIM + 1, :] + e_sink
    ot = ot[:HEAD_DIM, :] * (1.0 / denom)
    return jnp.concatenate([ot[:, g * PAIR:(g + 1) * PAIR] for g in range(GQA_GROUP)], axis=0).T


def _attend_probs(q, k, sink_ref, lq):
    probs = []
    for h in range(N_KV_HEADS):
        kh = k[:, h * HEAD_DIM:(h + 1) * HEAD_DIM]
        heads = [h * GQA_GROUP + g for g in range(GQA_GROUP)]
        qh = jnp.concatenate([q[:, n * HEAD_DIM:(n + 1) * HEAD_DIM] for n in heads], axis=0)
        s = lax.dot_general(qh, kh, _NT, preferred_element_type=F32)
        sink = jnp.concatenate(
            [jnp.broadcast_to(sink_ref[n:n + 1, 0:1], (lq, 1)) for n in heads], axis=0)
        m = jnp.maximum(jnp.max(s, axis=-1, keepdims=True), sink)
        e = jnp.exp(s - m)
        probs.append((e.astype(BF16), jnp.sum(e, axis=-1, keepdims=True) + jnp.exp(sink - m)))
    return probs


def _attend_values(probs, v, o_ref, lq):
    for h in range(N_KV_HEADS):
        vh = v[:, h * HEAD_DIM:(h + 1) * HEAD_DIM]
        e, denom = probs[h]
        oh = jnp.dot(e, vh, preferred_element_type=F32) * (1.0 / denom)
        o_ref[:, h * GQA_GROUP * HEAD_DIM:(h + 1) * GQA_GROUP * HEAD_DIM] = jnp.concatenate(
            [oh[g * lq:(g + 1) * lq, :] for g in range(GQA_GROUP)], axis=1).astype(o_ref.dtype)


def _attn_sample_body(q_ref, ck_ref, cv_ref, kn_ref, vn_ref, km_ref, vm_ref, sink_ref, o_ref):
    lq = q_ref.shape[0]
    k = jnp.concatenate([ck_ref[...], kn_ref[...], km_ref[...]], axis=0).astype(BF16)
    probs = _attend_probs(q_ref[...], k, sink_ref, lq)
    v = jnp.concatenate([cv_ref[...], vn_ref[...], vm_ref[...]], axis=0).astype(BF16)
    _attend_values(probs, v, o_ref, lq)


def _attend_sample(o16, o32, cache_k, cache_v, meta_block, sinks, n_streams, ds):
    cache_rows = cache_k.shape[1]
    new = lambda col: pl.BlockSpec((ds, KV_WIDTH), lambda s: (s, col))
    meta = lambda col: pl.BlockSpec((N_META, KV_WIDTH), lambda s: (meta_block, col))
    cache = pl.BlockSpec((None, cache_rows, KV_WIDTH), lambda s: (s, 0, 0))
    return pl.pallas_call(
        _attn_sample_body,
        grid=(n_streams,),
        in_specs=[
            pl.BlockSpec((ds, ATTN_WIDTH), lambda s: (s, COL_QA)),
            cache, cache, new(0), new(1), meta(0), meta(1),
            pl.BlockSpec((N_Q_HEADS, LANES), lambda s: (0, 0)),
        ],
        out_specs=pl.BlockSpec((ds, ATTN_WIDTH), lambda s: (s, 0)),
        out_shape=jax.ShapeDtypeStruct((n_streams * ds, ATTN_WIDTH), BF16),
        compiler_params=pltpu.CompilerParams(
            dimension_semantics=("parallel",), vmem_limit_bytes=VMEM_LIMIT),
        name="attend_sample",
    )(o16, cache_k, cache_v, o32, o32, o32, o32, sinks)


def _mlstm_body(q_ref, k_ref, v_ref, gt_ref, bias_ref, c0_ref, n0_ref, m0_ref,
                hn_ref, c_ref, n_ref, m_ref, *, blk):
    setup = _mlstm_setup(gt_ref, bias_ref, blk)
    scores = [_mlstm_scores(setup, h, q_ref, k_ref, m0_ref) for h in range(M_HEADS)]
    for h in range(M_HEADS):
        hv = _mlstm_out(setup, h, scores[h], q_ref, v_ref, c0_ref, n0_ref)
        hn_ref[:, h * M_V_DIM:(h + 1) * M_V_DIM] = hv.astype(hn_ref.dtype)
    for h in range(M_HEADS):
        _mlstm_state(setup, h, scores[h], k_ref, v_ref, (c0_ref, n0_ref, m0_ref), (c_ref, n_ref, m_ref),
                     blk, None)


def _mlstm_setup(gt_ref, bias_ref, blk):
    x = gt_ref[...] + bias_ref[...]
    lane = lax.broadcasted_iota(jnp.int32, x.shape, 1)
    log_f = jnp.minimum(x, 0.0) - jnp.log(1.0 + jnp.exp(-jnp.abs(x)))
    gates = jnp.where(lane < M_HEADS, x, log_f)
    row = lax.broadcasted_iota(jnp.int32, (blk, blk), 0)
    col = lax.broadcasted_iota(jnp.int32, (blk, blk), 1)
    causal = row >= col
    hi = gates.astype(BF16)
    rest = gates - hi.astype(F32)
    mid = rest.astype(BF16)
    lo = (rest - mid.astype(F32)).astype(BF16)
    parts = jnp.dot(causal.astype(BF16), jnp.concatenate([hi, mid, lo], axis=1), preferred_element_type=F32)
    csum = parts[:, :LANES] + parts[:, LANES:2 * LANES] + parts[:, 2 * LANES:]
    z = jnp.where(lane < M_HEADS, gates, csum)
    pad = (-blk) % LANES
    zsq = z if pad == 0 else jnp.concatenate([z, jnp.zeros((pad, LANES), F32)], axis=0)
    zt = zsq.T[:, :blk]
    return z, zt, causal


def _mlstm_scores(setup, h, q_ref, k_ref, m_ref):
    z, zt, causal = setup
    g_r = zt[h:h + 1, :] - zt[M_HEADS + h:M_HEADS + h + 1, :]
    m_prev = m_ref[h:h + 1, 0:1]
    d0 = jnp.where(causal, g_r, -jnp.inf)
    mm = jnp.maximum(m_prev, jnp.max(d0, axis=-1, keepdims=True))
    w = jnp.exp(d0 - mm)
    qh = q_ref[:, h * M_QK_DIM:(h + 1) * M_QK_DIM]
    kh = k_ref[:, h * M_QK_DIM:(h + 1) * M_QK_DIM]
    qk = lax.dot_general(qh, kh, _NT, preferred_element_type=F32)
    wqk = w * (qk * MLSTM_K_SCALE)
    return m_prev, mm, wqk.astype(BF16), jnp.sum(wqk, axis=-1, keepdims=True)


def _mlstm_finish(setup, h, scores, q_ref, k_ref, v_ref, c_ref, n_ref, m_ref, blk, live):
    hv = _mlstm_out(setup, h, scores, q_ref, v_ref, c_ref, n_ref)
    state = (c_ref, n_ref, m_ref)
    _mlstm_state(setup, h, scores, k_ref, v_ref, state, state, blk, live)
    return hv


def _mlstm_out(setup, h, scores, q_ref, v_ref, c_ref, n_ref):
    z, _, _ = setup
    m_prev, mm, wqk, den_intra = scores
    m_t = z[:, M_HEADS + h:M_HEADS + h + 1] + mm
    a = jnp.exp(m_prev - mm)
    qh = q_ref[:, h * M_QK_DIM:(h + 1) * M_QK_DIM]
    vh = v_ref[:, h * M_V_DIM:(h + 1) * M_V_DIM]
    num = a * jnp.dot(qh, c_ref[h].astype(BF16), preferred_element_type=F32) + jnp.dot(
        wqk, vh, preferred_element_type=F32)
    den = a * jnp.sum(qh.astype(F32) * n_ref[h:h + 1, :], axis=-1, keepdims=True) + den_intra
    r = 1.0 / jnp.maximum(jnp.abs(den), jnp.exp(-m_t))
    rms_h = r * jnp.sqrt(jnp.mean(num * num, axis=-1, keepdims=True))
    return num * (r * lax.rsqrt(rms_h * rms_h + EPS))


def _mlstm_state(setup, h, scores, k_ref, v_ref, state_in, state_out, blk, live):
    c_ref, n_ref, m_in = state_in
    c_out, n_out, m_out = state_out
    z, _, _ = setup
    m_prev, mm, _, _ = scores
    keep = (lambda new, old: new) if live is None else (lambda new, old: jnp.where(live, new, old))
    li_c = z[:, h:h + 1]
    b_c = z[:, M_HEADS + h:M_HEADS + h + 1]
    m_t = b_c + mm
    kh = k_ref[:, h * M_QK_DIM:(h + 1) * M_QK_DIM]
    vh = v_ref[:, h * M_V_DIM:(h + 1) * M_V_DIM]
    c_h = c_ref[h]
    n_h = n_ref[h:h + 1, :]

    m_new = m_t[blk - 1:blk, :]
    b_last = b_c[blk - 1:blk, :]
    ws = jnp.exp(b_last - b_c + li_c - m_new) * MLSTM_K_SCALE
    a_last = jnp.exp(b_last + m_prev - m_new)
    vs = (vh.astype(F32) * ws).astype(BF16)
    ktv = lax.dot_general(kh, vs, (((0,), (0,)), ((), ())), preferred_element_type=F32)
    c_out[h] = keep(a_last * c_h + ktv, c_h)
    n_out[h:h + 1, :] = keep(a_last * n_h + jnp.sum(kh.astype(F32) * ws, axis=0, keepdims=True), n_h)
    m_out[h:h + 1, :] = keep(jnp.broadcast_to(m_new, (1, LANES)), m_in[h:h + 1, :])


def _mlstm(o16, o32, bias, c0, n0, m0, *, blk, n_streams, first_block, col0):
    rb = lambda s: first_block + s
    qk_blk = 2 * (COL_QKM - col0)
    state = lambda *shape: pl.BlockSpec((None,) + shape, lambda s: (s,) + (0,) * len(shape))
    return pl.pallas_call(
        functools.partial(_mlstm_body, blk=blk),
        grid=(n_streams,),
        in_specs=[
            pl.BlockSpec((blk, M_QK_WIDTH), lambda s: (rb(s), qk_blk)),
            pl.BlockSpec((blk, M_QK_WIDTH), lambda s: (rb(s), qk_blk + 1)),
            pl.BlockSpec((blk, M_V_WIDTH), lambda s: (rb(s), COL_VM - col0)),
            pl.BlockSpec((blk, LANES), lambda s: (rb(s), GATE_BLOCK)),
            pl.BlockSpec((1, LANES), lambda s: (0, 0)),
            state(M_HEADS, M_QK_DIM, M_V_DIM), state(M_HEADS, M_QK_DIM), state(M_HEADS, LANES),
        ],
        out_specs=[
            pl.BlockSpec((blk, M_V_WIDTH), lambda s: (s, 0)),
            state(M_HEADS, M_QK_DIM, M_V_DIM), state(M_HEADS, M_QK_DIM), state(M_HEADS, LANES),
        ],
        out_shape=[
            jax.ShapeDtypeStruct((n_streams * blk, M_V_WIDTH), BF16),
            jax.ShapeDtypeStruct((n_streams, M_HEADS, M_QK_DIM, M_V_DIM), F32),
            jax.ShapeDtypeStruct((n_streams, M_HEADS, M_QK_DIM), F32),
            jax.ShapeDtypeStruct((n_streams, M_HEADS, LANES), F32),
        ],
        compiler_params=pltpu.CompilerParams(
            dimension_semantics=("parallel",), vmem_limit_bytes=VMEM_LIMIT),
        name="mlstm_blk%d" % blk,
    )(o16, o16, o16, o32, bias, c0, n0, m0)


def _merge_body(x_ref, oa_ref, hn_ref, za_ref, om_ref, zm_ref, ga_ref, gm_ref, gmh_ref,
                wpa_ref, wpm_ref, wout_ref, gf_ref, y_ref, wpa_out, wpm_out, wout_out, a_scr, m_scr, acc_scr):
    j = pl.program_id(0)

    @pl.when(j == 0)
    def _():
        a_scr[...] = (oa_ref[...].astype(F32) * za_ref[...].astype(F32)).astype(BF16)
        m_scr[...] = ((hn_ref[...].astype(F32) * gmh_ref[...]) * om_ref[...].astype(F32)
                      * zm_ref[...].astype(F32)).astype(BF16)
        acc_scr[...] = x_ref[...]

    wpa_out[...] = wpa_ref[...].astype(BF16)
    wpm_out[...] = wpm_ref[...].astype(BF16)
    wout_out[...] = wout_ref[...].astype(BF16)
    ya = jnp.dot(a_scr[...], wpa_out[...], preferred_element_type=F32)
    ym = jnp.dot(m_scr[...], wpm_out[...], preferred_element_type=F32)
    merged = ga_ref[...].astype(F32) * ya + gm_ref[...].astype(F32) * ym
    acc_scr[...] += jnp.dot(merged.astype(BF16), wout_out[...], preferred_element_type=F32)

    @pl.when(j == pl.num_programs(0) - 1)
    def _():
        xo = acc_scr[...]
        ms = jnp.mean(xo * xo, axis=-1, keepdims=True)
        y_ref[...] = (xo * lax.rsqrt(ms + EPS)) * gf_ref[...]


def _merge(x, o_att, hn, o16, g_mh, w_pa, w_pm, w_out, g_final, rows, col0):
    n_chunks = D_MODEL // MERGE_TN
    per_group = D_MODEL // MERGE_TN
    full = lambda col: pl.BlockSpec((rows, D_MODEL), lambda j: (0, col))
    gate = lambda col: pl.BlockSpec((rows, MERGE_TN), lambda j: (0, col * per_group + j))
    vec = pl.BlockSpec((1, D_MODEL), lambda j: (0, 0))
    w_cols = pl.BlockSpec((D_MODEL, MERGE_TN), lambda j: (0, j))
    w_rows = pl.BlockSpec((MERGE_TN, D_MODEL), lambda j: (j, 0))
    return pl.pallas_call(
        _merge_body,
        grid=(n_chunks,),
        in_specs=[full(0), full(0), full(0), full(COL_ZA - col0), full(COL_OM - col0), full(COL_ZM - col0),
                  gate(COL_GA - col0), gate(COL_GM - col0), vec, w_cols, w_cols, w_rows, vec],
        out_specs=[full(0), w_cols, w_cols, w_rows],
        out_shape=[jax.ShapeDtypeStruct((rows, D_MODEL), F32)]
        + [jax.ShapeDtypeStruct((D_MODEL, D_MODEL), BF16)] * 3,
        scratch_shapes=[pltpu.VMEM((rows, D_MODEL), BF16), pltpu.VMEM((rows, D_MODEL), BF16),
                        pltpu.VMEM((rows, D_MODEL), F32)],
        compiler_params=pltpu.CompilerParams(
            dimension_semantics=("arbitrary",), vmem_limit_bytes=VMEM_LIMIT),
        name="merge",
    )(x, o_att, hn, o16, o16, o16, o16, o16, g_mh, w_pa, w_pm, w_out, g_final)


def _mixer_body(qt_ref, ka_ref, kb_ref, km_ref, vta_ref, vtb_ref, vtm_ref, sink_ref,
                q_ref, k_ref, v_ref, gt_ref, bias_ref, c0_ref, n0_ref, m0_ref,
                x_ref, za_ref, om_ref, zm_ref, ga_ref, gm_ref, gmh_ref, wpa_ref, wpm_ref, wout_ref, gf_ref,
                y_ref, c_ref, n_ref, m_ref, a_scr, m_scr, *, n_blk):
    c = pl.program_id(0)

    @pl.when(c == 0)
    def _():
        c_ref[...] = c0_ref[...]
        n_ref[...] = n0_ref[...]
        m_ref[...] = m0_ref[...]
        a_scr[...] = jnp.zeros(a_scr.shape, a_scr.dtype)
        m_scr[...] = jnp.zeros(m_scr.shape, m_scr.dtype)

    wr = c % MIX_SLOTS
    rd = 1 - wr
    blk = jnp.minimum(c, n_blk - 1)
    live = c < n_blk
    width = GQA_GROUP * HEAD_DIM
    heads_per_piece = width // M_V_DIM
    a_in = a_scr[rd]
    m_in = m_scr[rd]

    def mlstm_piece(h):
        hc = slice(h * M_V_DIM, (h + 1) * M_V_DIM)
        hv = _mlstm_finish(cell, h, scores[h], q_ref, k_ref, v_ref, c_ref, n_ref, m_ref, PAIR, live)
        m_scr[wr, :, hc] = ((hv * gmh_ref[:, hc]) * om_ref[:, hc].astype(F32)
                            * zm_ref[:, hc].astype(F32)).astype(BF16)

    def attn_piece(h, probs):
        cols = slice(h * width, (h + 1) * width)
        oa = _attn_pair_out(attn, h, probs)
        a_scr[wr, :, cols] = (oa * za_ref[:, cols].astype(F32)).astype(BF16)

    attn = _attn_pair_setup(blk, ka_ref, kb_ref, km_ref, vta_ref, vtb_ref, vtm_ref)
    cell = _mlstm_setup(gt_ref, bias_ref, PAIR)
    scores = [_mlstm_scores(cell, h, q_ref, k_ref, m_ref) for h in range(M_HEADS)]
    probs = _attn_pair_probs(attn, 0, qt_ref, sink_ref)
    merged = []
    for i in range(N_KV_HEADS):
        cols = slice(i * width, (i + 1) * width)
        ya = jnp.dot(a_in, wpa_ref[:, cols], preferred_element_type=F32)
        mlstm_piece(heads_per_piece * i)
        next_probs = _attn_pair_probs(attn, i + 1, qt_ref, sink_ref) if i + 1 < N_KV_HEADS else None
        ym = jnp.dot(m_in, wpm_ref[:, cols], preferred_element_type=F32)
        mlstm_piece(heads_per_piece * i + 1)
        merged.append((ga_ref[:, cols].astype(F32) * ya + gm_ref[:, cols].astype(F32) * ym).astype(BF16))
        if i + 1 == N_KV_HEADS:
            xo = x_ref[...] + jnp.dot(jnp.concatenate(merged, axis=1), wout_ref[...],
                                      preferred_element_type=F32)
        attn_piece(i, probs)
        probs = next_probs
    ms = jnp.mean(xo * xo, axis=-1, keepdims=True)
    y_ref[...] = (xo * lax.rsqrt(ms + EPS)) * gf_ref[...]


def _mixer(x, qvt, o32, o16, k_meta, vt_meta, sinks_t, bias, c0, n0, m0, g_mh, w_pa, w_pm, w_out, g_final,
           col0):
    seq = x.shape[0]
    assert seq % PAIR == 0
    n_blk = seq // PAIR
    cur = lambda c: jnp.minimum(c, n_blk - 1)
    prv = lambda c: jnp.maximum(cur(c) - 1, 0)
    mrg = lambda c: jnp.maximum(c - 1, 0)
    v_row = ATTN_WIDTH // KV_WIDTH
    qk_blk = 2 * (COL_QKM - col0)
    tile = lambda col: pl.BlockSpec((PAIR, D_MODEL), lambda c: (mrg(c), col))
    tile_cur = lambda col: pl.BlockSpec((PAIR, D_MODEL), lambda c: (cur(c), col))
    vec = pl.BlockSpec((1, D_MODEL), lambda c: (0, 0))
    weight = pl.BlockSpec((D_MODEL, D_MODEL), lambda c: (0, 0), pipeline_mode=pl.Buffered(1))
    state = lambda *blk: pl.BlockSpec((None,) + blk, lambda c: (0,) * (len(blk) + 1))
    return pl.pallas_call(
        functools.partial(_mixer_body, n_blk=n_blk),
        grid=(n_blk + 1,),
        in_specs=[
            pl.BlockSpec((None, ATTN_WIDTH, PAIR), lambda c: (cur(c), 0, 0)),
            pl.BlockSpec((PAIR, KV_WIDTH), lambda c: (prv(c), 0)),
            pl.BlockSpec((PAIR, KV_WIDTH), lambda c: (cur(c), 0)),
            pl.BlockSpec((N_META, KV_WIDTH), lambda c: (0, 0)),
            pl.BlockSpec((None, KV_WIDTH, PAIR), lambda c: (prv(c), v_row, 0)),
            pl.BlockSpec((None, KV_WIDTH, PAIR), lambda c: (cur(c), v_row, 0)),
            pl.BlockSpec((KV_WIDTH, N_META), lambda c: (0, 0)),
            pl.BlockSpec((N_KV_HEADS, 1, GQA_GROUP * PAIR), lambda c: (0, 0, 0)),
            pl.BlockSpec((PAIR, M_QK_WIDTH), lambda c: (cur(c), qk_blk)),
            pl.BlockSpec((PAIR, M_QK_WIDTH), lambda c: (cur(c), qk_blk + 1)),
            pl.BlockSpec((PAIR, M_V_WIDTH), lambda c: (cur(c), COL_VM - col0)),
            pl.BlockSpec((PAIR, LANES), lambda c: (cur(c), GATE_BLOCK)),
            pl.BlockSpec((1, LANES), lambda c: (0, 0)),
            state(M_HEADS, M_QK_DIM, M_V_DIM), state(M_HEADS, M_QK_DIM), state(M_HEADS, LANES),
            tile(0), tile_cur(COL_ZA - col0), tile_cur(COL_OM - col0), tile_cur(COL_ZM - col0),
            tile(COL_GA - col0), tile(COL_GM - col0), vec, weight, weight, weight, vec,
        ],
        out_specs=[
            tile(0),
            state(M_HEADS, M_QK_DIM, M_V_DIM), state(M_HEADS, M_QK_DIM), state(M_HEADS, LANES),
        ],
        out_shape=[
            jax.ShapeDtypeStruct((seq, D_MODEL), F32),
            jax.ShapeDtypeStruct((1, M_HEADS, M_QK_DIM, M_V_DIM), F32),
            jax.ShapeDtypeStruct((1, M_HEADS, M_QK_DIM), F32),
            jax.ShapeDtypeStruct((1, M_HEADS, LANES), F32),
        ],
        scratch_shapes=[pltpu.VMEM((MIX_SLOTS, PAIR, ATTN_WIDTH), BF16),
                        pltpu.VMEM((MIX_SLOTS, PAIR, M_V_WIDTH), BF16)],
        compiler_params=pltpu.CompilerParams(
            dimension_semantics=("arbitrary",), vmem_limit_bytes=VMEM_LIMIT),
        name="mixer",
    )(qvt, o32, o32, k_meta, qvt, qvt, vt_meta, sinks_t,
      o16, o16, o16, o32, bias, c0, n0, m0,
      x, o16, o16, o16, o16, o16, g_mh, w_pa, w_pm, w_out, g_final)


def kernel(x_prompt, x_sample, cache_k, cache_v, state_C, state_n, state_m, meta_tokens, g_norm, w_in,
           b_igate, b_fgate, attn_sinks, g_mhnorm, w_pa, w_pm, w_out, g_final):
    batch, seq, _ = x_prompt.shape
    db, ds, _ = x_sample.shape
    depth = w_in.shape[0]
    assert batch == 1 and depth == 1 and ds == N_META
    cache_rows = cache_k.shape[2]

    w_t = w_in[0].T
    g_in = g_norm[0].reshape(1, D_MODEL)
    bias = jnp.concatenate([b_igate[0], b_fgate[0], jnp.zeros((LANES - 2 * M_HEADS,), F32)]).reshape(1, LANES)
    sinks = jnp.broadcast_to(attn_sinks[0][:, None], (N_Q_HEADS, LANES))
    sinks_t = jnp.broadcast_to(attn_sinks[0].reshape(N_KV_HEADS, 1, GQA_GROUP, 1),
                               (N_KV_HEADS, 1, GQA_GROUP, PAIR)).reshape(N_KV_HEADS, 1, GQA_GROUP * PAIR)
    g_mh = g_mhnorm[0].reshape(1, M_V_WIDTH)
    g_fin = g_final.reshape(1, D_MODEL)

    xp = x_prompt.reshape(seq, D_MODEL)
    xs_rows = db * ds
    x_small = jnp.concatenate([x_sample.reshape(xs_rows, D_MODEL), meta_tokens.astype(F32)], axis=0)
    meta_block = xs_rows // N_META
    p_tm = min(seq, 1024)
    s32, s16, _, wmain, wkv = _project_first(x_small, g_in, w_t)
    p32, p16, xn_p = _project(xp, g_in, wkv, wmain, tm=p_tm, col0=COL_ZA)
    qvt = _project_t(wmain, wkv, xn_p, tm=p_tm)

    zeros = lambda *shape: jnp.zeros(shape, F32)
    _, c_meta, n_meta, m_meta = _mlstm(
        s16, s32, bias, zeros(1, M_HEADS, M_QK_DIM, M_V_DIM), zeros(1, M_HEADS, M_QK_DIM),
        zeros(1, M_HEADS, LANES), blk=N_META, n_streams=1, first_block=meta_block, col0=COL_QA)

    ck = cache_k[0].reshape(db, cache_rows, KV_WIDTH)
    cv = cache_v[0].reshape(db, cache_rows, KV_WIDTH)
    oa_s = _attend_sample(s16, s32, ck, cv, meta_block, sinks, db, ds)
    m0_s = jnp.broadcast_to(state_m[0][:, :, None], (db, M_HEADS, LANES))
    hn_s, c_s, n_s, m_s = _mlstm(s16, s32, bias, state_C[0], state_n[0], m0_s, blk=ds, n_streams=db,
                                 first_block=0, col0=COL_QA)
    y_s, wpa, wpm, wout = _merge(x_small, oa_s, hn_s, s16, g_mh, w_pa[0], w_pm[0], w_out[0], g_fin,
                                 rows=xs_rows, col0=COL_QA)

    k_meta = s32[xs_rows:, :KV_WIDTH]
    vt_meta = s32[xs_rows:, KV_WIDTH:2 * KV_WIDTH].T.astype(BF16)
    y_p, c_p, n_p, m_p = _mixer(xp, qvt, p32, p16, k_meta, vt_meta, sinks_t, bias, c_meta, n_meta, m_meta,
                                g_mh, wpa, wpm, wout, g_fin, col0=COL_ZA)

    kv_shape = (1, batch, cache_rows, N_KV_HEADS, HEAD_DIM)
    k_p = p32[seq - cache_rows:, :KV_WIDTH].reshape(kv_shape)
    v_p = p32[seq - cache_rows:, KV_WIDTH:2 * KV_WIDTH].reshape(kv_shape)
    new_k = s32[:xs_rows, :KV_WIDTH].reshape(db, ds, KV_WIDTH)
    new_v = s32[:xs_rows, KV_WIDTH:2 * KV_WIDTH].reshape(db, ds, KV_WIDTH)
    skv_shape = (1, db, cache_rows, N_KV_HEADS, HEAD_DIM)
    k_s = jnp.concatenate([ck, new_k], axis=1)[:, -cache_rows:].reshape(skv_shape)
    v_s = jnp.concatenate([cv, new_v], axis=1)[:, -cache_rows:].reshape(skv_shape)

    return (y_p.reshape(batch, seq, D_MODEL), y_s.reshape(db, ds, D_MODEL),
            k_p, v_p, c_p[None], n_p[None], m_p[None, :, :, 0],
            k_s, v_s, c_s[None], n_s[None], m_s[None, :, :, 0])
```

```python
import functools
import math

import jax
import jax.numpy as jnp
import numpy as np
from jax import lax
from jax.experimental import pallas as pl
from jax.experimental.pallas import tpu as pltpu

F32 = jnp.float32
BF16 = jnp.bfloat16

D_MODEL = 2048
CHUNK = 64
N_META = 16
HEAD_DIM = 64
N_Q_HEADS = 32
N_KV_HEADS = 4
GQA_GROUP = N_Q_HEADS // N_KV_HEADS
ATTN_WIDTH = N_Q_HEADS * HEAD_DIM
KV_WIDTH = N_KV_HEADS * HEAD_DIM
M_HEADS = 8
M_QK_DIM = 128
M_V_DIM = 256
M_QK_WIDTH = M_HEADS * M_QK_DIM
M_V_WIDTH = M_HEADS * M_V_DIM
EPS = 1e-6
NEG_INF = -1e30
COL_SIZES = (ATTN_WIDTH, KV_WIDTH, KV_WIDTH, ATTN_WIDTH, M_QK_WIDTH, M_QK_WIDTH, M_V_WIDTH, M_V_WIDTH,
             M_HEADS, M_HEADS, M_V_WIDTH, D_MODEL, D_MODEL)
COL_STARTS = tuple(int(v) for v in np.cumsum((0,) + COL_SIZES[:-1]))
(SRC_QA, SRC_KA, SRC_VA, SRC_ZA, SRC_QM, SRC_KM, SRC_VM, SRC_OM, SRC_IG, SRC_FG, SRC_ZM, SRC_GA,
 SRC_GM) = COL_STARTS

LANES = 128
VMEM_LIMIT = 56 * 1024 * 1024

KV_OUT = 2 * KV_WIDTH + LANES
GATE_BLOCK = 2 * KV_WIDTH // LANES
MAIN_OUT = 8 * D_MODEL
COL_QA, COL_ZA, COL_QKM, COL_VM, COL_OM, COL_ZM, COL_GA, COL_GM = range(8)
PROJ_TN = 1024
ROW_ALIGN = 16
MLSTM_K_SCALE = M_QK_DIM ** -0.5
Q_SCALE = 1.0 / math.sqrt(HEAD_DIM)
QV_ROWS = ATTN_WIDTH + KV_WIDTH
PAIR = 2 * CHUNK
PAIR_KEYS = 2 * PAIR + N_META
ONES_ROWS = 16
MIX_SLOTS = 2
MERGE_TN = 256
NORM_SLABS = 4


def _sigmoid(x):
    return 0.5 * jnp.tanh(0.5 * x) + 0.5


def _silu(x):
    return x * _sigmoid(x)


_NT = (((1,), (1,)), ((), ()))


def _main_row_offsets():
    groups = [(SRC_QA, ATTN_WIDTH), (SRC_ZA, ATTN_WIDTH), (SRC_QM, M_QK_WIDTH), (SRC_KM, M_QK_WIDTH),
              (SRC_VM, M_V_WIDTH), (SRC_OM, M_V_WIDTH), (SRC_ZM, M_V_WIDTH), (SRC_GA, D_MODEL),
              (SRC_GM, D_MODEL)]
    offs = []
    for start, width in groups:
        assert start % ROW_ALIGN == 0 and width % PROJ_TN == 0
        offs += [(start + b * PROJ_TN) // ROW_ALIGN for b in range(width // PROJ_TN)]
    assert len(offs) * PROJ_TN == MAIN_OUT
    return np.asarray(offs, np.int32)


def _proj_body(*refs, tile0, from_f32):
    if from_f32:
        _, x_ref, g_ref, wkv_ref, wg_ref, w_ref, o32_ref, o16_ref, xn_ref, wmain_out, wkv_out = refs
    else:
        x_ref, g_ref, wkv_ref, w_ref, o32_ref, o16_ref, xn_ref = refs

    @pl.when(pl.program_id(1) == 0)
    def _():
        if from_f32:
            wkv_out[:2 * KV_WIDTH, :] = wkv_ref[...].astype(BF16)
            wkv_out[2 * KV_WIDTH:2 * KV_WIDTH + ROW_ALIGN, :] = wg_ref[...].astype(BF16)
            wkv_out[2 * KV_WIDTH + ROW_ALIGN:, :] = jnp.zeros((LANES - ROW_ALIGN, D_MODEL), BF16)
            wkv = wkv_out[...]
        else:
            wkv = wkv_ref[...]
        tm = x_ref.shape[0]
        slab = tm // NORM_SLABS if tm % (NORM_SLABS * ROW_ALIGN) == 0 else tm
        for r in range(0, tm, slab):
            x = x_ref[r:r + slab, :]
            ms = jnp.mean(x * x, axis=-1, keepdims=True)
            xn = ((x * lax.rsqrt(ms + EPS)) * g_ref[...]).astype(BF16)
            xn_ref[r:r + slab, :] = xn
            o32_ref[r:r + slab, :] = lax.dot_general(xn, wkv, _NT, preferred_element_type=F32)

    group = (pl.program_id(1) + tile0) // (D_MODEL // PROJ_TN)
    is_sigmoid = (group == COL_OM) | (group == COL_GA) | (group == COL_GM)
    is_silu = (group == COL_ZA) | (group == COL_ZM)
    is_q = group == COL_QA

    def tile(act):
        w = w_ref[...].astype(BF16)
        if from_f32:
            wmain_out[...] = w
        acc = lax.dot_general(xn_ref[...], w, _NT, preferred_element_type=F32)
        o16_ref[...] = act(acc).astype(BF16)

    pl.when(is_sigmoid)(lambda: tile(_sigmoid))
    pl.when(is_silu)(lambda: tile(_silu))
    pl.when(is_q)(lambda: tile(lambda acc: acc * Q_SCALE))
    pl.when(jnp.logical_not(is_sigmoid | is_silu | is_q))(lambda: tile(lambda acc: acc))


def _project_first(x, g, w_t):
    rows = x.shape[0]
    assert SRC_KA % (2 * KV_WIDTH) == 0 and SRC_VA == SRC_KA + KV_WIDTH
    assert SRC_IG % ROW_ALIGN == 0 and SRC_FG == SRC_IG + M_HEADS and 2 * M_HEADS == ROW_ALIGN
    n_tiles = MAIN_OUT // PROJ_TN
    return pl.pallas_call(
        functools.partial(_proj_body, tile0=0, from_f32=True),
        grid_spec=pltpu.PrefetchScalarGridSpec(
            num_scalar_prefetch=1,
            grid=(1, n_tiles),
            in_specs=[
                pl.BlockSpec((rows, D_MODEL), lambda i, j, off: (0, 0)),
                pl.BlockSpec((1, D_MODEL), lambda i, j, off: (0, 0)),
                pl.BlockSpec((2 * KV_WIDTH, D_MODEL), lambda i, j, off: (SRC_KA // (2 * KV_WIDTH), 0),
                             pipeline_mode=pl.Buffered(1)),
                pl.BlockSpec((ROW_ALIGN, D_MODEL), lambda i, j, off: (SRC_IG // ROW_ALIGN, 0),
                             pipeline_mode=pl.Buffered(1)),
                pl.BlockSpec((pl.Element(PROJ_TN), pl.Element(D_MODEL)),
                             lambda i, j, off: (off[j] * ROW_ALIGN, 0)),
            ],
            out_specs=[
                pl.BlockSpec((rows, KV_OUT), lambda i, j, off: (0, 0)),
                pl.BlockSpec((rows, PROJ_TN), lambda i, j, off: (0, j)),
                pl.BlockSpec((rows, D_MODEL), lambda i, j, off: (0, 0)),
                pl.BlockSpec((PROJ_TN, D_MODEL), lambda i, j, off: (j, 0)),
                pl.BlockSpec((KV_OUT, D_MODEL), lambda i, j, off: (0, 0)),
            ],
        ),
        out_shape=[jax.ShapeDtypeStruct((rows, KV_OUT), F32),
                   jax.ShapeDtypeStruct((rows, MAIN_OUT), BF16),
                   jax.ShapeDtypeStruct((rows, D_MODEL), BF16),
                   jax.ShapeDtypeStruct((MAIN_OUT, D_MODEL), BF16),
                   jax.ShapeDtypeStruct((KV_OUT, D_MODEL), BF16)],
        compiler_params=pltpu.CompilerParams(
            dimension_semantics=("arbitrary", "arbitrary"), vmem_limit_bytes=VMEM_LIMIT),
        name="project_first",
    )(jnp.asarray(_main_row_offsets()), x, g, w_t, w_t, w_t)


def _project(x, g, wkv, wmain, tm, col0):
    rows = x.shape[0]
    assert rows % tm == 0
    tile0 = col0 * (D_MODEL // PROJ_TN)
    n_tiles = MAIN_OUT // PROJ_TN - tile0
    return pl.pallas_call(
        functools.partial(_proj_body, tile0=tile0, from_f32=False),
        grid=(rows // tm, n_tiles),
        in_specs=[
            pl.BlockSpec((tm, D_MODEL), lambda i, j: (i, 0)),
            pl.BlockSpec((1, D_MODEL), lambda i, j: (0, 0)),
            pl.BlockSpec((KV_OUT, D_MODEL), lambda i, j: (0, 0), pipeline_mode=pl.Buffered(1)),
            pl.BlockSpec((PROJ_TN, D_MODEL), lambda i, j: (j + tile0, 0)),
        ],
        out_specs=[
            pl.BlockSpec((tm, KV_OUT), lambda i, j: (i, 0)),
            pl.BlockSpec((tm, PROJ_TN), lambda i, j: (i, j)),
            pl.BlockSpec((tm, D_MODEL), lambda i, j: (i, 0)),
        ],
        out_shape=[jax.ShapeDtypeStruct((rows, KV_OUT), F32),
                   jax.ShapeDtypeStruct((rows, n_tiles * PROJ_TN), BF16),
                   jax.ShapeDtypeStruct((rows, D_MODEL), BF16)],
        compiler_params=pltpu.CompilerParams(
            dimension_semantics=("parallel", "arbitrary"), vmem_limit_bytes=VMEM_LIMIT),
        name="project",
    )(x, g, wkv, wmain)


def _proj_t_body(wq_ref, wv_ref, xn_ref, o_ref):
    xn = xn_ref[...]
    q_t = (lax.dot_general(wq_ref[...], xn, _NT, preferred_element_type=F32) * Q_SCALE).astype(BF16)
    v_t = lax.dot_general(wv_ref[...], xn, _NT, preferred_element_type=F32).astype(BF16)
    for p in range(o_ref.shape[0]):
        o_ref[p, :ATTN_WIDTH, :] = q_t[:, p * PAIR:(p + 1) * PAIR]
        o_ref[p, ATTN_WIDTH:, :] = v_t[:, p * PAIR:(p + 1) * PAIR]


def _project_t(wmain, wkv, xn, tm):
    rows = xn.shape[0]
    assert rows % tm == 0 and tm % PAIR == 0
    return pl.pallas_call(
        _proj_t_body,
        grid=(rows // tm,),
        in_specs=[pl.BlockSpec((ATTN_WIDTH, D_MODEL), lambda i: (COL_QA, 0), pipeline_mode=pl.Buffered(1)),
                  pl.BlockSpec((KV_WIDTH, D_MODEL), lambda i: (1, 0), pipeline_mode=pl.Buffered(1)),
                  pl.BlockSpec((tm, D_MODEL), lambda i: (i, 0))],
        out_specs=pl.BlockSpec((tm // PAIR, QV_ROWS, PAIR), lambda i: (i, 0, 0)),
        out_shape=jax.ShapeDtypeStruct((rows // PAIR, QV_ROWS, PAIR), BF16),
        compiler_params=pltpu.CompilerParams(
            dimension_semantics=("parallel",), vmem_limit_bytes=VMEM_LIMIT),
        name="project_t",
    )(wmain, wkv, xn)


def _attn_pair_setup(p, ka_ref, kb_ref, km_ref, vta_ref, vtb_ref, vtm_ref):
    ka, kb = ka_ref[...], kb_ref[...]
    lane = lax.broadcasted_iota(jnp.int32, (CHUNK, LANES), 1)
    k_ad = []
    for t in range(KV_WIDTH // LANES):
        a_tile = ka[:CHUNK, t * LANES:(t + 1) * LANES]
        d_swapped = pltpu.roll(kb[CHUNK:, t * LANES:(t + 1) * LANES], HEAD_DIM, axis=1)
        k_ad.append(jnp.where(lane < HEAD_DIM, a_tile, d_swapped).astype(BF16))
        k_ad.append(jnp.where(lane < HEAD_DIM, d_swapped, a_tile).astype(BF16))
    k_rest = jnp.concatenate([kb[:CHUNK], ka[CHUNK:], km_ref[...]], axis=0).astype(BF16)
    vta, vtb = vta_ref[...], vtb_ref[...]
    first_half = lax.broadcasted_iota(jnp.int32, vta.shape, 1) < CHUNK
    vt = jnp.concatenate([jnp.where(first_half, vta, vtb), jnp.where(first_half, vtb, vta), vtm_ref[...]],
                         axis=1)
    even = (lax.broadcasted_iota(jnp.int32, (CHUNK, GQA_GROUP * PAIR), 1) % PAIR) < CHUNK
    return k_ad, k_rest, vt, (even, p == 0)


def _attn_pair_probs(setup, h, qt_ref, sink_ref):
    k_ad, k_rest, _, (even, first) = setup
    rows = [(h * GQA_GROUP + g) * HEAD_DIM for g in range(GQA_GROUP)]
    qth = jnp.concatenate([qt_ref[r:r + HEAD_DIM, :] for r in rows], axis=1)
    even_b = even.astype(BF16)
    q_even, q_odd = qth * even_b, qth * (1 - even_b)
    q2 = jnp.concatenate([q_even, q_odd] if h % 2 == 0 else [q_odd, q_even], axis=0)
    s_ad = jnp.dot(k_ad[h], q2, preferred_element_type=F32)
    s_rest = jnp.dot(k_rest[:, h * HEAD_DIM:(h + 1) * HEAD_DIM], qth, preferred_element_type=F32)
    st = jnp.concatenate([
        jnp.where(first & even, NEG_INF, s_ad),
        s_rest[:CHUNK],
        jnp.where(first, NEG_INF, s_rest[CHUNK:2 * CHUNK]),
        s_rest[2 * CHUNK:]], axis=0)
    sink = sink_ref[h]
    m = jnp.maximum(jnp.max(st, axis=0, keepdims=True), sink)
    e = jnp.exp(st - m)
    even_f = even.astype(F32)
    e = jnp.concatenate([e[:CHUNK] * even_f, e[:CHUNK] * (1.0 - even_f), e[CHUNK:]], axis=0)
    return e.astype(BF16), jnp.exp(sink - m)


def _attn_pair_out(setup, h, probs):
    _, _, vt, _ = setup
    e, e_sink = probs
    ones = jnp.ones((ONES_ROWS, PAIR_KEYS), BF16)
    vth = jnp.concatenate([vt[h * HEAD_DIM:(h + 1) * HEAD_DIM, :], ones], axis=0)
    ot = jnp.dot(vth, e, preferred_element_type=F32)
    denom = ot[HEAD_DIM:HEAD_DIM + 1, :] + e_sink
    ot = ot[:HEAD_DIM, :] * (1.0 / denom)
    return jnp.concatenate([ot[:, g * PAIR:(g + 1) * PAIR] for g in range(GQA_GROUP)], axis=0).T


def _attend_probs(q, k, sink_ref, lq):
    probs = []
    for h in range(N_KV_HEADS):
        kh = k[:, h * HEAD_DIM:(h + 1) * HEAD_DIM]
        heads = [h * GQA_GROUP + g for g in range(GQA_GROUP)]
        qh = jnp.concatenate([q[:, n * HEAD_DIM:(n + 1) * HEAD_DIM] for n in heads], axis=0)
        s = lax.dot_general(qh, kh, _NT, preferred_element_type=F32)
        sink = jnp.concatenate(
            [jnp.broadcast_to(sink_ref[n:n + 1, 0:1], (lq, 1)) for n in heads], axis=0)
        m = jnp.maximum(jnp.max(s, axis=-1, keepdims=True), sink)
        e = jnp.exp(s - m)
        probs.append((e.astype(BF16), jnp.sum(e, axis=-1, keepdims=True) + jnp.exp(sink - m)))
    return probs


def _attend_values(probs, v, o_ref, lq):
    for h in range(N_KV_HEADS):
        vh = v[:, h * HEAD_DIM:(h + 1) * HEAD_DIM]
        e, denom = probs[h]
        oh = jnp.dot(e, vh, preferred_element_type=F32) * (1.0 / denom)
        o_ref[:, h * GQA_GROUP * HEAD_DIM:(h + 1) * GQA_GROUP * HEAD_DIM] = jnp.concatenate(
            [oh[g * lq:(g + 1) * lq, :] for g in range(GQA_GROUP)], axis=1).astype(o_ref.dtype)


def _attn_sample_body(q_ref, ck_ref, cv_ref, kn_ref, vn_ref, km_ref, vm_ref, sink_ref, o_ref):
    lq = q_ref.shape[0]
    k = jnp.concatenate([ck_ref[...], kn_ref[...], km_ref[...]], axis=0).astype(BF16)
    probs = _attend_probs(q_ref[...], k, sink_ref, lq)
    v = jnp.concatenate([cv_ref[...], vn_ref[...], vm_ref[...]], axis=0).astype(BF16)
    _attend_values(probs, v, o_ref, lq)


def _attend_sample(o16, o32, cache_k, cache_v, meta_block, sinks, n_streams, ds):
    cache_rows = cache_k.shape[1]
    new = lambda col: pl.BlockSpec((ds, KV_WIDTH), lambda s: (s, col))
    meta = lambda col: pl.BlockSpec((N_META, KV_WIDTH), lambda s: (meta_block, col))
    cache = pl.BlockSpec((None, cache_rows, KV_WIDTH), lambda s: (s, 0, 0))
    return pl.pallas_call(
        _attn_sample_body,
        grid=(n_streams,),
        in_specs=[
            pl.BlockSpec((ds, ATTN_WIDTH), lambda s: (s, COL_QA)),
            cache, cache, new(0), new(1), meta(0), meta(1),
            pl.BlockSpec((N_Q_HEADS, LANES), lambda s: (0, 0)),
        ],
        out_specs=pl.BlockSpec((ds, ATTN_WIDTH), lambda s: (s, 0)),
        out_shape=jax.ShapeDtypeStruct((n_streams * ds, ATTN_WIDTH), BF16),
        compiler_params=pltpu.CompilerParams(
            dimension_semantics=("parallel",), vmem_limit_bytes=VMEM_LIMIT),
        name="attend_sample",
    )(o16, cache_k, cache_v, o32, o32, o32, o32, sinks)


def _mlstm_body(q_ref, k_ref, v_ref, gt_ref, bias_ref, c0_ref, n0_ref, m0_ref,
                hn_ref, c_ref, n_ref, m_ref, *, blk):
    setup = _mlstm_setup(gt_ref, bias_ref, blk)
    scores = [_mlstm_scores(setup, h, q_ref, k_ref, m0_ref) for h in range(M_HEADS)]
    for h in range(M_HEADS):
        hv = _mlstm_out(setup, h, scores[h], q_ref, v_ref, c0_ref, n0_ref)
        hn_ref[:, h * M_V_DIM:(h + 1) * M_V_DIM] = hv.astype(hn_ref.dtype)
    for h in range(M_HEADS):
        _mlstm_state(setup, h, scores[h], k_ref, v_ref, (c0_ref, n0_ref, m0_ref), (c_ref, n_ref, m_ref),
                     blk, None)


def _mlstm_setup(gt_ref, bias_ref, blk):
    x = gt_ref[...] + bias_ref[...]
    lane = lax.broadcasted_iota(jnp.int32, x.shape, 1)
    log_f = jnp.minimum(x, 0.0) - jnp.log(1.0 + jnp.exp(-jnp.abs(x)))
    gates = jnp.where(lane < M_HEADS, x, log_f)
    row = lax.broadcasted_iota(jnp.int32, (blk, blk), 0)
    col = lax.broadcasted_iota(jnp.int32, (blk, blk), 1)
    causal = row >= col
    hi = gates.astype(BF16)
    rest = gates - hi.astype(F32)
    mid = rest.astype(BF16)
    lo = (rest - mid.astype(F32)).astype(BF16)
    parts = jnp.dot(causal.astype(BF16), jnp.concatenate([hi, mid, lo], axis=1), preferred_element_type=F32)
    csum = parts[:, :LANES] + parts[:, LANES:2 * LANES] + parts[:, 2 * LANES:]
    z = jnp.where(lane < M_HEADS, gates, csum)
    pad = (-blk) % LANES
    zsq = z if pad == 0 else jnp.concatenate([z, jnp.zeros((pad, LANES), F32)], axis=0)
    zt = zsq.T[:, :blk]
    return z, zt, causal


def _mlstm_scores(setup, h, q_ref, k_ref, m_ref):
    z, zt, causal = setup
    g_r = zt[h:h + 1, :] - zt[M_HEADS + h:M_HEADS + h + 1, :]
    m_prev = m_ref[h:h + 1, 0:1]
    d0 = jnp.where(causal, g_r, -jnp.inf)
    mm = jnp.maximum(m_prev, jnp.max(d0, axis=-1, keepdims=True))
    w = jnp.exp(d0 - mm)
    qh = q_ref[:, h * M_QK_DIM:(h + 1) * M_QK_DIM]
    kh = k_ref[:, h * M_QK_DIM:(h + 1) * M_QK_DIM]
    qk = lax.dot_general(qh, kh, _NT, preferred_element_type=F32)
    wqk = w * (qk * MLSTM_K_SCALE)
    return m_prev, mm, wqk.astype(BF16), jnp.sum(wqk, axis=-1, keepdims=True)


def _mlstm_finish(setup, h, scores, q_ref, k_ref, v_ref, c_ref, n_ref, m_ref, blk, live):
    hv = _mlstm_out(setup, h, scores, q_ref, v_ref, c_ref, n_ref)
    state = (c_ref, n_ref, m_ref)
    _mlstm_state(setup, h, scores, k_ref, v_ref, state, state, blk, live)
    return hv


def _mlstm_out(setup, h, scores, q_ref, v_ref, c_ref, n_ref):
    z, _, _ = setup
    m_prev, mm, wqk, den_intra = scores
    m_t = z[:, M_HEADS + h:M_HEADS + h + 1] + mm
    a = jnp.exp(m_prev - mm)
    qh = q_ref[:, h * M_QK_DIM:(h + 1) * M_QK_DIM]
    vh = v_ref[:, h * M_V_DIM:(h + 1) * M_V_DIM]
    num = a * jnp.dot(qh, c_ref[h].astype(BF16), preferred_element_type=F32) + jnp.dot(
        wqk, vh, preferred_element_type=F32)
    den = a * jnp.sum(qh.astype(F32) * n_ref[h:h + 1, :], axis=-1, keepdims=True) + den_intra
    r = 1.0 / jnp.maximum(jnp.abs(den), jnp.exp(-m_t))
    rms_h = r * jnp.sqrt(jnp.mean(num * num, axis=-1, keepdims=True))
    return num * (r * lax.rsqrt(rms_h * rms_h + EPS))


def _mlstm_state(setup, h, scores, k_ref, v_ref, state_in, state_out, blk, live):
    c_ref, n_ref, m_in = state_in
    c_out, n_out, m_out = state_out
    z, _, _ = setup
    m_prev, mm, _, _ = scores
    keep = (lambda new, old: new) if live is None else (lambda new, old: jnp.where(live, new, old))
    li_c = z[:, h:h + 1]
    b_c = z[:, M_HEADS + h:M_HEADS + h + 1]
    m_t = b_c + mm
    kh = k_ref[:, h * M_QK_DIM:(h + 1) * M_QK_DIM]
    vh = v_ref[:, h * M_V_DIM:(h + 1) * M_V_DIM]
    c_h = c_ref[h]
    n_h = n_ref[h:h + 1, :]

    m_new = m_t[blk - 1:blk, :]
    b_last = b_c[blk - 1:blk, :]
    ws = jnp.exp(b_last - b_c + li_c - m_new) * MLSTM_K_SCALE
    a_last = jnp.exp(b_last + m_prev - m_new)
    vs = (vh.astype(F32) * ws).astype(BF16)
    ktv = lax.dot_general(kh, vs, (((0,), (0,)), ((), ())), preferred_element_type=F32)
    c_out[h] = keep(a_last * c_h + ktv, c_h)
    n_out[h:h + 1, :] = keep(a_last * n_h + jnp.sum(kh.astype(F32) * ws, axis=0, keepdims=True), n_h)
    m_out[h:h + 1, :] = keep(jnp.broadcast_to(m_new, (1, LANES)), m_in[h:h + 1, :])


def _mlstm(o16, o32, bias, c0, n0, m0, *, blk, n_streams, first_block, col0):
    rb = lambda s: first_block + s
    qk_blk = 2 * (COL_QKM - col0)
    state = lambda *shape: pl.BlockSpec((None,) + shape, lambda s: (s,) + (0,) * len(shape))
    return pl.pallas_call(
        functools.partial(_mlstm_body, blk=blk),
        grid=(n_streams,),
        in_specs=[
            pl.BlockSpec((blk, M_QK_WIDTH), lambda s: (rb(s), qk_blk)),
            pl.BlockSpec((blk, M_QK_WIDTH), lambda s: (rb(s), qk_blk + 1)),
            pl.BlockSpec((blk, M_V_WIDTH), lambda s: (rb(s), COL_VM - col0)),
            pl.BlockSpec((blk, LANES), lambda s: (rb(s), GATE_BLOCK)),
            pl.BlockSpec((1, LANES), lambda s: (0, 0)),
            state(M_HEADS, M_QK_DIM, M_V_DIM), state(M_HEADS, M_QK_DIM), state(M_HEADS, LANES),
        ],
        out_specs=[
            pl.BlockSpec((blk, M_V_WIDTH), lambda s: (s, 0)),
            state(M_HEADS, M_QK_DIM, M_V_DIM), state(M_HEADS, M_QK_DIM), state(M_HEADS, LANES),
        ],
        out_shape=[
            jax.ShapeDtypeStruct((n_streams * blk, M_V_WIDTH), BF16),
            jax.ShapeDtypeStruct((n_streams, M_HEADS, M_QK_DIM, M_V_DIM), F32),
            jax.ShapeDtypeStruct((n_streams, M_HEADS, M_QK_DIM), F32),
            jax.ShapeDtypeStruct((n_streams, M_HEADS, LANES), F32),
        ],
        compiler_params=pltpu.CompilerParams(
            dimension_semantics=("parallel",), vmem_limit_bytes=VMEM_LIMIT),
        name="mlstm_blk%d" % blk,
    )(o16, o16, o16, o32, bias, c0, n0, m0)


def _merge_body(x_ref, oa_ref, hn_ref, za_ref, om_ref, zm_ref, ga_ref, gm_ref, gmh_ref,
                wpa_ref, wpm_ref, wout_ref, gf_ref, y_ref, wpa_out, wpm_out, wout_out, a_scr, m_scr, acc_scr):
    j = pl.program_id(0)

    @pl.when(j == 0)
    def _():
        a_scr[...] = (oa_ref[...].astype(F32) * za_ref[...].astype(F32)).astype(BF16)
        m_scr[...] = ((hn_ref[...].astype(F32) * gmh_ref[...]) * om_ref[...].astype(F32)
                      * zm_ref[...].astype(F32)).astype(BF16)
        acc_scr[...] = x_ref[...]

    wpa_out[...] = wpa_ref[...].astype(BF16)
    wpm_out[...] = wpm_ref[...].astype(BF16)
    wout_out[...] = wout_ref[...].astype(BF16)
    ya = jnp.dot(a_scr[...], wpa_out[...], preferred_element_type=F32)
    ym = jnp.dot(m_scr[...], wpm_out[...], preferred_element_type=F32)
    merged = ga_ref[...].astype(F32) * ya + gm_ref[...].astype(F32) * ym
    acc_scr[...] += jnp.dot(merged.astype(BF16), wout_out[...], preferred_element_type=F32)

    @pl.when(j == pl.num_programs(0) - 1)
    def _():
        xo = acc_scr[...]
        ms = jnp.mean(xo * xo, axis=-1, keepdims=True)
        y_ref[...] = (xo * lax.rsqrt(ms + EPS)) * gf_ref[...]


def _merge(x, o_att, hn, o16, g_mh, w_pa, w_pm, w_out, g_final, rows, col0):
    n_chunks = D_MODEL // MERGE_TN
    per_group = D_MODEL // MERGE_TN
    full = lambda col: pl.BlockSpec((rows, D_MODEL), lambda j: (0, col))
    gate = lambda col: pl.BlockSpec((rows, MERGE_TN), lambda j: (0, col * per_group + j))
    vec = pl.BlockSpec((1, D_MODEL), lambda j: (0, 0))
    w_cols = pl.BlockSpec((D_MODEL, MERGE_TN), lambda j: (0, j))
    w_rows = pl.BlockSpec((MERGE_TN, D_MODEL), lambda j: (j, 0))
    return pl.pallas_call(
        _merge_body,
        grid=(n_chunks,),
        in_specs=[full(0), full(0), full(0), full(COL_ZA - col0), full(COL_OM - col0), full(COL_ZM - col0),
                  gate(COL_GA - col0), gate(COL_GM - col0), vec, w_cols, w_cols, w_rows, vec],
        out_specs=[full(0), w_cols, w_cols, w_rows],
        out_shape=[jax.ShapeDtypeStruct((rows, D_MODEL), F32)]
        + [jax.ShapeDtypeStruct((D_MODEL, D_MODEL), BF16)] * 3,
        scratch_shapes=[pltpu.VMEM((rows, D_MODEL), BF16), pltpu.VMEM((rows, D_MODEL), BF16),
                        pltpu.VMEM((rows, D_MODEL), F32)],
        compiler_params=pltpu.CompilerParams(
            dimension_semantics=("arbitrary",), vmem_limit_bytes=VMEM_LIMIT),
        name="merge",
    )(x, o_att, hn, o16, o16, o16, o16, o16, g_mh, w_pa, w_pm, w_out, g_final)


def _mixer_body(qt_ref, ka_ref, kb_ref, km_ref, vta_ref, vtb_ref, vtm_ref, sink_ref,
                q_ref, k_ref, v_ref, gt_ref, bias_ref, c0_ref, n0_ref, m0_ref,
                x_ref, za_ref, om_ref, zm_ref, ga_ref, gm_ref, gmh_ref, wpa_ref, wpm_ref, wout_ref, gf_ref,
                y_ref, c_ref, n_ref, m_ref, a_scr, m_scr, *, n_blk):
    c = pl.program_id(0)

    @pl.when(c == 0)
    def _():
        c_ref[...] = c0_ref[...]
        n_ref[...] = n0_ref[...]
        m_ref[...] = m0_ref[...]
        a_scr[...] = jnp.zeros(a_scr.shape, a_scr.dtype)
        m_scr[...] = jnp.zeros(m_scr.shape, m_scr.dtype)

    wr = c % MIX_SLOTS
    rd = 1 - wr
    blk = jnp.minimum(c, n_blk - 1)
    live = c < n_blk
    width = GQA_GROUP * HEAD_DIM
    heads_per_piece = width // M_V_DIM
    a_in = a_scr[rd]
    m_in = m_scr[rd]

    def mlstm_piece(h):
        hc = slice(h * M_V_DIM, (h + 1) * M_V_DIM)
        hv = _mlstm_finish(cell, h, scores[h], q_ref, k_ref, v_ref, c_ref, n_ref, m_ref, PAIR, live)
        m_scr[wr, :, hc] = ((hv * gmh_ref[:, hc]) * om_ref[:, hc].astype(F32)
                            * zm_ref[:, hc].astype(F32)).astype(BF16)

    def attn_piece(h, probs):
        cols = slice(h * width, (h + 1) * width)
        oa = _attn_pair_out(attn, h, probs)
        a_scr[wr, :, cols] = (oa * za_ref[:, cols].astype(F32)).astype(BF16)

    attn = _attn_pair_setup(blk, ka_ref, kb_ref, km_ref, vta_ref, vtb_ref, vtm_ref)
    cell = _mlstm_setup(gt_ref, bias_ref, PAIR)
    scores = [_mlstm_scores(cell, h, q_ref, k_ref, m_ref) for h in range(M_HEADS)]
    probs = _attn_pair_probs(attn, 0, qt_ref, sink_ref)
    merged = []
    mlstm_heads = ((0, 1, 2), (3, 4, 5), (6, 7), ())
    for i in range(N_KV_HEADS):
        last = i + 1 == N_KV_HEADS
        cols = slice(i * width, (i + 1) * width)
        ya = jnp.dot(a_in, wpa_ref[:, cols], preferred_element_type=F32)
        for h in mlstm_heads[i][:2]:
            mlstm_piece(h)
        next_probs = None if last else _attn_pair_probs(attn, i + 1, qt_ref, sink_ref)
        ym = jnp.dot(m_in, wpm_ref[:, cols], preferred_element_type=F32)
        for h in mlstm_heads[i][2:]:
            mlstm_piece(h)
        merged.append((ga_ref[:, cols].astype(F32) * ya + gm_ref[:, cols].astype(F32) * ym).astype(BF16))
        if last:
            xo = x_ref[...] + jnp.dot(jnp.concatenate(merged, axis=1), wout_ref[...],
                                      preferred_element_type=F32)
        attn_piece(i, probs)
        probs = next_probs
    ms = jnp.mean(xo * xo, axis=-1, keepdims=True)
    y_ref[...] = (xo * lax.rsqrt(ms + EPS)) * gf_ref[...]


def _mixer(x, qvt, o32, o16, k_meta, vt_meta, sinks_t, bias, c0, n0, m0, g_mh, w_pa, w_pm, w_out, g_final,
           col0):
    seq = x.shape[0]
    assert seq % PAIR == 0
    n_blk = seq // PAIR
    cur = lambda c: jnp.minimum(c, n_blk - 1)
    prv = lambda c: jnp.maximum(cur(c) - 1, 0)
    mrg = lambda c: jnp.maximum(c - 1, 0)
    v_row = ATTN_WIDTH // KV_WIDTH
    qk_blk = 2 * (COL_QKM - col0)
    tile = lambda col: pl.BlockSpec((PAIR, D_MODEL), lambda c: (mrg(c), col))
    tile_cur = lambda col: pl.BlockSpec((PAIR, D_MODEL), lambda c: (cur(c), col))
    vec = pl.BlockSpec((1, D_MODEL), lambda c: (0, 0))
    weight = pl.BlockSpec((D_MODEL, D_MODEL), lambda c: (0, 0), pipeline_mode=pl.Buffered(1))
    state = lambda *blk: pl.BlockSpec((None,) + blk, lambda c: (0,) * (len(blk) + 1))
    return pl.pallas_call(
        functools.partial(_mixer_body, n_blk=n_blk),
        grid=(n_blk + 1,),
        in_specs=[
            pl.BlockSpec((None, ATTN_WIDTH, PAIR), lambda c: (cur(c), 0, 0)),
            pl.BlockSpec((PAIR, KV_WIDTH), lambda c: (prv(c), 0)),
            pl.BlockSpec((PAIR, KV_WIDTH), lambda c: (cur(c), 0)),
            pl.BlockSpec((N_META, KV_WIDTH), lambda c: (0, 0)),
            pl.BlockSpec((None, KV_WIDTH, PAIR), lambda c: (prv(c), v_row, 0)),
            pl.BlockSpec((None, KV_WIDTH, PAIR), lambda c: (cur(c), v_row, 0)),
            pl.BlockSpec((KV_WIDTH, N_META), lambda c: (0, 0)),
            pl.BlockSpec((N_KV_HEADS, 1, GQA_GROUP * PAIR), lambda c: (0, 0, 0)),
            pl.BlockSpec((PAIR, M_QK_WIDTH), lambda c: (cur(c), qk_blk)),
            pl.BlockSpec((PAIR, M_QK_WIDTH), lambda c: (cur(c), qk_blk + 1)),
            pl.BlockSpec((PAIR, M_V_WIDTH), lambda c: (cur(c), COL_VM - col0)),
            pl.BlockSpec((PAIR, LANES), lambda c: (cur(c), GATE_BLOCK)),
            pl.BlockSpec((1, LANES), lambda c: (0, 0)),
            state(M_HEADS, M_QK_DIM, M_V_DIM), state(M_HEADS, M_QK_DIM), state(M_HEADS, LANES),
            tile(0), tile_cur(COL_ZA - col0), tile_cur(COL_OM - col0), tile_cur(COL_ZM - col0),
            tile(COL_GA - col0), tile(COL_GM - col0), vec, weight, weight, weight, vec,
        ],
        out_specs=[
            tile(0),
            state(M_HEADS, M_QK_DIM, M_V_DIM), state(M_HEADS, M_QK_DIM), state(M_HEADS, LANES),
        ],
        out_shape=[
            jax.ShapeDtypeStruct((seq, D_MODEL), F32),
            jax.ShapeDtypeStruct((1, M_HEADS, M_QK_DIM, M_V_DIM), F32),
            jax.ShapeDtypeStruct((1, M_HEADS, M_QK_DIM), F32),
            jax.ShapeDtypeStruct((1, M_HEADS, LANES), F32),
        ],
        scratch_shapes=[pltpu.VMEM((MIX_SLOTS, PAIR, ATTN_WIDTH), BF16),
                        pltpu.VMEM((MIX_SLOTS, PAIR, M_V_WIDTH), BF16)],
        compiler_params=pltpu.CompilerParams(
            dimension_semantics=("arbitrary",), vmem_limit_bytes=VMEM_LIMIT),
        name="mixer",
    )(qvt, o32, o32, k_meta, qvt, qvt, vt_meta, sinks_t,
      o16, o16, o16, o32, bias, c0, n0, m0,
      x, o16, o16, o16, o16, o16, g_mh, w_pa, w_pm, w_out, g_final)


def kernel(x_prompt, x_sample, cache_k, cache_v, state_C, state_n, state_m, meta_tokens, g_norm, w_in,
           b_igate, b_fgate, attn_sinks, g_mhnorm, w_pa, w_pm, w_out, g_final):
    batch, seq, _ = x_prompt.shape
    db, ds, _ = x_sample.shape
    depth = w_in.shape[0]
    assert batch == 1 and depth == 1 and ds == N_META
    cache_rows = cache_k.shape[2]

    w_t = w_in[0].T
    g_in = g_norm[0].reshape(1, D_MODEL)
    bias = jnp.concatenate([b_igate[0], b_fgate[0], jnp.zeros((LANES - 2 * M_HEADS,), F32)]).reshape(1, LANES)
    sinks = jnp.broadcast_to(attn_sinks[0][:, None], (N_Q_HEADS, LANES))
    sinks_t = jnp.broadcast_to(attn_sinks[0].reshape(N_KV_HEADS, 1, GQA_GROUP, 1),
                               (N_KV_HEADS, 1, GQA_GROUP, PAIR)).reshape(N_KV_HEADS, 1, GQA_GROUP * PAIR)
    g_mh = g_mhnorm[0].reshape(1, M_V_WIDTH)
    g_fin = g_final.reshape(1, D_MODEL)

    xp = x_prompt.reshape(seq, D_MODEL)
    xs_rows = db * ds
    x_small = jnp.concatenate([x_sample.reshape(xs_rows, D_MODEL), meta_tokens.astype(F32)], axis=0)
    meta_block = xs_rows // N_META
    p_tm = min(seq, 1024)
    s32, s16, _, wmain, wkv = _project_first(x_small, g_in, w_t)
    p32, p16, xn_p = _project(xp, g_in, wkv, wmain, tm=p_tm, col0=COL_ZA)
    qvt = _project_t(wmain, wkv, xn_p, tm=p_tm)

    zeros = lambda *shape: jnp.zeros(shape, F32)
    _, c_meta, n_meta, m_meta = _mlstm(
        s16, s32, bias, zeros(1, M_HEADS, M_QK_DIM, M_V_DIM), zeros(1, M_HEADS, M_QK_DIM),
        zeros(1, M_HEADS, LANES), blk=N_META, n_streams=1, first_block=meta_block, col0=COL_QA)

    ck = cache_k[0].reshape(db, cache_rows, KV_WIDTH)
    cv = cache_v[0].reshape(db, cache_rows, KV_WIDTH)
    oa_s = _attend_sample(s16, s32, ck, cv, meta_block, sinks, db, ds)
    m0_s = jnp.broadcast_to(state_m[0][:, :, None], (db, M_HEADS, LANES))
    hn_s, c_s, n_s, m_s = _mlstm(s16, s32, bias, state_C[0], state_n[0], m0_s, blk=ds, n_streams=db,
                                 first_block=0, col0=COL_QA)
    y_s, wpa, wpm, wout = _merge(x_small, oa_s, hn_s, s16, g_mh, w_pa[0], w_pm[0], w_out[0], g_fin,
                                 rows=xs_rows, col0=COL_QA)

    k_meta = s32[xs_rows:, :KV_WIDTH]
    vt_meta = s32[xs_rows:, KV_WIDTH:2 * KV_WIDTH].T.astype(BF16)
    y_p, c_p, n_p, m_p = _mixer(xp, qvt, p32, p16, k_meta, vt_meta, sinks_t, bias, c_meta, n_meta, m_meta,
                                g_mh, wpa, wpm, wout, g_fin, col0=COL_ZA)

    kv_shape = (1, batch, cache_rows, N_KV_HEADS, HEAD_DIM)
    k_p = p32[seq - cache_rows:, :KV_WIDTH].reshape(kv_shape)
    v_p = p32[seq - cache_rows:, KV_WIDTH:2 * KV_WIDTH].reshape(kv_shape)
    new_k = s32[:xs_rows, :KV_WIDTH].reshape(db, ds, KV_WIDTH)
    new_v = s32[:xs_rows, KV_WIDTH:2 * KV_WIDTH].reshape(db, ds, KV_WIDTH)
    skv_shape = (1, db, cache_rows, N_KV_HEADS, HEAD_DIM)
    k_s = jnp.concatenate([ck, new_k], axis=1)[:, -cache_rows:].reshape(skv_shape)
    v_s = jnp.concatenate([cv, new_v], axis=1)[:, -cache_rows:].reshape(skv_shape)

    return (y_p.reshape(batch, seq, D_MODEL), y_s.reshape(db, ds, D_MODEL),
            k_p, v_p, c_p[None], n_p[None], m_p[None, :, :, 0],
            k_s, v_s, c_s[None], n_s[None], m_s[None, :, :, 0])
```

```python
import functools
import math

import jax
import jax.numpy as jnp
import numpy as np
from jax import lax
from jax.experimental import pallas as pl
from jax.experimental.pallas import tpu as pltpu

F32 = jnp.float32
BF16 = jnp.bfloat16

D_MODEL = 2048
CHUNK = 64
N_META = 16
HEAD_DIM = 64
N_Q_HEADS = 32
N_KV_HEADS = 4
GQA_GROUP = N_Q_HEADS // N_KV_HEADS
ATTN_WIDTH = N_Q_HEADS * HEAD_DIM
KV_WIDTH = N_KV_HEADS * HEAD_DIM
M_HEADS = 8
M_QK_DIM = 128
M_V_DIM = 256
M_QK_WIDTH = M_HEADS * M_QK_DIM
M_V_WIDTH = M_HEADS * M_V_DIM
EPS = 1e-6
NEG_INF = -1e30
COL_SIZES = (ATTN_WIDTH, KV_WIDTH, KV_WIDTH, ATTN_WIDTH, M_QK_WIDTH, M_QK_WIDTH, M_V_WIDTH, M_V_WIDTH,
             M_HEADS, M_HEADS, M_V_WIDTH, D_MODEL, D_MODEL)
COL_STARTS = tuple(int(v) for v in np.cumsum((0,) + COL_SIZES[:-1]))
(SRC_QA, SRC_KA, SRC_VA, SRC_ZA, SRC_QM, SRC_KM, SRC_VM, SRC_OM, SRC_IG, SRC_FG, SRC_ZM, SRC_GA,
 SRC_GM) = COL_STARTS

LANES = 128
VMEM_LIMIT = 56 * 1024 * 1024

KV_OUT = 2 * KV_WIDTH + LANES
GATE_BLOCK = 2 * KV_WIDTH // LANES
MAIN_OUT = 8 * D_MODEL
COL_QA, COL_ZA, COL_QKM, COL_VM, COL_OM, COL_ZM, COL_GA, COL_GM = range(8)
PROJ_TN = 1024
ROW_ALIGN = 16
MLSTM_K_SCALE = M_QK_DIM ** -0.5
Q_SCALE = 1.0 / math.sqrt(HEAD_DIM)
QV_ROWS = ATTN_WIDTH + KV_WIDTH
PAIR = 2 * CHUNK
PAIR_KEYS = 2 * PAIR + N_META
ONES_ROWS = 16
MIX_SLOTS = 2
MERGE_TN = 256
NORM_SLABS = 4


def _sigmoid(x):
    return 0.5 * jnp.tanh(0.5 * x) + 0.5


def _silu(x):
    return x * _sigmoid(x)


_NT = (((1,), (1,)), ((), ()))


def _main_row_offsets():
    groups = [(SRC_QA, ATTN_WIDTH), (SRC_ZA, ATTN_WIDTH), (SRC_QM, M_QK_WIDTH), (SRC_KM, M_QK_WIDTH),
              (SRC_VM, M_V_WIDTH), (SRC_OM, M_V_WIDTH), (SRC_ZM, M_V_WIDTH), (SRC_GA, D_MODEL),
              (SRC_GM, D_MODEL)]
    offs = []
    for start, width in groups:
        assert start % ROW_ALIGN == 0 and width % PROJ_TN == 0
        offs += [(start + b * PROJ_TN) // ROW_ALIGN for b in range(width // PROJ_TN)]
    assert len(offs) * PROJ_TN == MAIN_OUT
    return np.asarray(offs, np.int32)


def _proj_body(*refs, tile0, from_f32, skip=0, n_prior=0):
    if from_f32:
        _, x_ref, g_ref, wkv_ref, wg_ref, w_ref, o32_ref, o16_ref, xn_ref, wmain_out, wkv_out = refs
    else:
        x_ref, g_ref, wkv_ref, w_ref = refs[:4]
        o32_ref, o16_ref, xn_ref = refs[4 + n_prior:]

    @pl.when(pl.program_id(1) == 0)
    def _():
        if from_f32:
            wkv_out[:2 * KV_WIDTH, :] = wkv_ref[...].astype(BF16)
            wkv_out[2 * KV_WIDTH:2 * KV_WIDTH + ROW_ALIGN, :] = wg_ref[...].astype(BF16)
            wkv_out[2 * KV_WIDTH + ROW_ALIGN:, :] = jnp.zeros((LANES - ROW_ALIGN, D_MODEL), BF16)
            wkv = wkv_out[...]
        else:
            wkv = wkv_ref[...]
        tm = x_ref.shape[0]
        slab = tm // NORM_SLABS if tm % (NORM_SLABS * ROW_ALIGN) == 0 else tm
        for r in range(0, tm, slab):
            x = x_ref[r:r + slab, :]
            ms = jnp.mean(x * x, axis=-1, keepdims=True)
            xn = ((x * lax.rsqrt(ms + EPS)) * g_ref[...]).astype(BF16)
            xn_ref[r:r + slab, :] = xn
            o32_ref[r:r + slab, :] = lax.dot_general(xn, wkv, _NT, preferred_element_type=F32)

    group = (pl.program_id(1) + tile0) // (D_MODEL // PROJ_TN)
    is_sigmoid = (group == COL_OM) | (group == COL_GA) | (group == COL_GM)
    is_silu = (group == COL_ZA) | (group == COL_ZM)
    is_q = group == COL_QA

    def tile(act):
        w = w_ref[...].astype(BF16)
        if from_f32:
            wmain_out[...] = w
        acc = lax.dot_general(xn_ref[...], w, _NT, preferred_element_type=F32)
        o16_ref[...] = act(acc).astype(BF16)

    plain = jnp.logical_not(is_sigmoid | is_silu | is_q)
    if skip:
        convert_only = pl.program_id(1) < skip

        @pl.when(convert_only)
        def _():
            wmain_out[...] = w_ref[...].astype(BF16)

        computed = jnp.logical_not(convert_only)
        is_sigmoid, is_silu, is_q, plain = (is_sigmoid & computed, is_silu & computed, is_q & computed,
                                            plain & computed)
    pl.when(is_sigmoid)(lambda: tile(_sigmoid))
    pl.when(is_silu)(lambda: tile(_silu))
    pl.when(is_q)(lambda: tile(lambda acc: acc * Q_SCALE))
    pl.when(plain)(lambda: tile(lambda acc: acc))


def _project_first(x, g, w_t, tm, col0):
    rows = x.shape[0]
    assert SRC_KA % (2 * KV_WIDTH) == 0 and SRC_VA == SRC_KA + KV_WIDTH
    assert SRC_IG % ROW_ALIGN == 0 and SRC_FG == SRC_IG + M_HEADS and 2 * M_HEADS == ROW_ALIGN
    n_tiles = MAIN_OUT // PROJ_TN
    skip = col0 * (D_MODEL // PROJ_TN)
    return pl.pallas_call(
        functools.partial(_proj_body, tile0=0, from_f32=True, skip=skip),
        grid_spec=pltpu.PrefetchScalarGridSpec(
            num_scalar_prefetch=1,
            grid=(1, n_tiles),
            in_specs=[
                pl.BlockSpec((tm, D_MODEL), lambda i, j, off: (0, 0)),
                pl.BlockSpec((1, D_MODEL), lambda i, j, off: (0, 0)),
                pl.BlockSpec((2 * KV_WIDTH, D_MODEL), lambda i, j, off: (SRC_KA // (2 * KV_WIDTH), 0),
                             pipeline_mode=pl.Buffered(1)),
                pl.BlockSpec((ROW_ALIGN, D_MODEL), lambda i, j, off: (SRC_IG // ROW_ALIGN, 0),
                             pipeline_mode=pl.Buffered(1)),
                pl.BlockSpec((pl.Element(PROJ_TN), pl.Element(D_MODEL)),
                             lambda i, j, off: (off[j] * ROW_ALIGN, 0)),
            ],
            out_specs=[
                pl.BlockSpec((tm, KV_OUT), lambda i, j, off: (0, 0)),
                pl.BlockSpec((tm, PROJ_TN), lambda i, j, off: (0, jnp.maximum(j - skip, 0))),
                pl.BlockSpec((tm, D_MODEL), lambda i, j, off: (0, 0)),
                pl.BlockSpec((PROJ_TN, D_MODEL), lambda i, j, off: (j, 0)),
                pl.BlockSpec((KV_OUT, D_MODEL), lambda i, j, off: (0, 0)),
            ],
        ),
        out_shape=[jax.ShapeDtypeStruct((rows, KV_OUT), F32),
                   jax.ShapeDtypeStruct((rows, MAIN_OUT - skip * PROJ_TN), BF16),
                   jax.ShapeDtypeStruct((rows, D_MODEL), BF16),
                   jax.ShapeDtypeStruct((MAIN_OUT, D_MODEL), BF16),
                   jax.ShapeDtypeStruct((KV_OUT, D_MODEL), BF16)],
        compiler_params=pltpu.CompilerParams(
            dimension_semantics=("arbitrary", "arbitrary"), vmem_limit_bytes=VMEM_LIMIT),
        name="project_first",
    )(jnp.asarray(_main_row_offsets()), x, g, w_t, w_t, w_t)


def _project(x, g, wkv, wmain, tm, col0, prior=None):
    rows = x.shape[0]
    assert rows % tm == 0
    tile0 = col0 * (D_MODEL // PROJ_TN)
    n_tiles = MAIN_OUT // PROJ_TN - tile0
    row0 = 0 if prior is None else 1
    prior = () if prior is None else tuple(prior)
    return pl.pallas_call(
        functools.partial(_proj_body, tile0=tile0, from_f32=False, n_prior=len(prior)),
        grid=(rows // tm - row0, n_tiles),
        in_specs=[
            pl.BlockSpec((tm, D_MODEL), lambda i, j: (i + row0, 0)),
            pl.BlockSpec((1, D_MODEL), lambda i, j: (0, 0)),
            pl.BlockSpec((KV_OUT, D_MODEL), lambda i, j: (0, 0), pipeline_mode=pl.Buffered(1)),
            pl.BlockSpec((PROJ_TN, D_MODEL), lambda i, j: (j + tile0, 0)),
        ] + [pl.BlockSpec(memory_space=pl.ANY)] * len(prior),
        out_specs=[
            pl.BlockSpec((tm, KV_OUT), lambda i, j: (i + row0, 0)),
            pl.BlockSpec((tm, PROJ_TN), lambda i, j: (i + row0, j)),
            pl.BlockSpec((tm, D_MODEL), lambda i, j: (i + row0, 0)),
        ],
        out_shape=[jax.ShapeDtypeStruct((rows, KV_OUT), F32),
                   jax.ShapeDtypeStruct((rows, n_tiles * PROJ_TN), BF16),
                   jax.ShapeDtypeStruct((rows, D_MODEL), BF16)],
        input_output_aliases={4 + k: k for k in range(len(prior))},
        compiler_params=pltpu.CompilerParams(
            dimension_semantics=("parallel", "arbitrary"), vmem_limit_bytes=VMEM_LIMIT),
        name="project",
    )(x, g, wkv, wmain, *prior)


def _proj_t_body(wq_ref, wv_ref, xn_ref, o_ref):
    xn = xn_ref[...]
    q_t = (lax.dot_general(wq_ref[...], xn, _NT, preferred_element_type=F32) * Q_SCALE).astype(BF16)
    v_t = lax.dot_general(wv_ref[...], xn, _NT, preferred_element_type=F32).astype(BF16)
    for p in range(o_ref.shape[0]):
        o_ref[p, :ATTN_WIDTH, :] = q_t[:, p * PAIR:(p + 1) * PAIR]
        o_ref[p, ATTN_WIDTH:, :] = v_t[:, p * PAIR:(p + 1) * PAIR]


def _project_t(wmain, wkv, xn, tm):
    rows = xn.shape[0]
    assert rows % tm == 0 and tm % PAIR == 0
    return pl.pallas_call(
        _proj_t_body,
        grid=(rows // tm,),
        in_specs=[pl.BlockSpec((ATTN_WIDTH, D_MODEL), lambda i: (COL_QA, 0), pipeline_mode=pl.Buffered(1)),
                  pl.BlockSpec((KV_WIDTH, D_MODEL), lambda i: (1, 0), pipeline_mode=pl.Buffered(1)),
                  pl.BlockSpec((tm, D_MODEL), lambda i: (i, 0))],
        out_specs=pl.BlockSpec((tm // PAIR, QV_ROWS, PAIR), lambda i: (i, 0, 0)),
        out_shape=jax.ShapeDtypeStruct((rows // PAIR, QV_ROWS, PAIR), BF16),
        compiler_params=pltpu.CompilerParams(
            dimension_semantics=("parallel",), vmem_limit_bytes=VMEM_LIMIT),
        name="project_t",
    )(wmain, wkv, xn)


def _attn_pair_setup(p, ka_ref, kb_ref, km_ref, vta_ref, vtb_ref, vtm_ref):
    ka, kb = ka_ref[...], kb_ref[...]
    lane = lax.broadcasted_iota(jnp.int32, (CHUNK, LANES), 1)
    k_ad = []
    for t in range(KV_WIDTH // LANES):
        a_tile = ka[:CHUNK, t * LANES:(t + 1) * LANES]
        d_swapped = pltpu.roll(kb[CHUNK:, t * LANES:(t + 1) * LANES], HEAD_DIM, axis=1)
        k_ad.append(jnp.where(lane < HEAD_DIM, a_tile, d_swapped).astype(BF16))
        k_ad.append(jnp.where(lane < HEAD_DIM, d_swapped, a_tile).astype(BF16))
    k_rest = jnp.concatenate([kb[:CHUNK], ka[CHUNK:], km_ref[...]], axis=0).astype(BF16)
    vta, vtb = vta_ref[...], vtb_ref[...]
    first_half = lax.broadcasted_iota(jnp.int32, vta.shape, 1) < CHUNK
    vt = jnp.concatenate([jnp.where(first_half, vta, vtb), jnp.where(first_half, vtb, vta), vtm_ref[...]],
                         axis=1)
    even = (lax.broadcasted_iota(jnp.int32, (CHUNK, GQA_GROUP * PAIR), 1) % PAIR) < CHUNK
    return k_ad, k_rest, vt, (even, p == 0)


def _attn_pair_probs(setup, h, qt_ref, sink_ref):
    k_ad, k_rest, _, (even, first) = setup
    rows = [(h * GQA_GROUP + g) * HEAD_DIM for g in range(GQA_GROUP)]
    qth = jnp.concatenate([qt_ref[r:r + HEAD_DIM, :] for r in rows], axis=1)
    even_b = even.astype(BF16)
    q_even, q_odd = qth * even_b, qth * (1 - even_b)
    q2 = jnp.concatenate([q_even, q_odd] if h % 2 == 0 else [q_odd, q_even], axis=0)
    s_ad = jnp.dot(k_ad[h], q2, preferred_element_type=F32)
    s_rest = jnp.dot(k_rest[:, h * HEAD_DIM:(h + 1) * HEAD_DIM], qth, preferred_element_type=F32)
    st = jnp.concatenate([
        jnp.where(first & even, NEG_INF, s_ad),
        s_rest[:CHUNK],
        jnp.where(first, NEG_INF, s_rest[CHUNK:2 * CHUNK]),
        s_rest[2 * CHUNK:]], axis=0)
    sink = sink_ref[h]
    m = jnp.maximum(jnp.max(st, axis=0, keepdims=True), sink)
    e = jnp.exp(st - m)
    even_f = even.astype(F32)
    e = jnp.concatenate([e[:CHUNK] * even_f, e[:CHUNK] * (1.0 - even_f), e[CHUNK:]], axis=0)
    return e.astype(BF16), jnp.exp(sink - m)


def _attn_pair_out(setup, h, probs):
    _, _, vt, _ = setup
    e, e_sink = probs
    ones = jnp.ones((ONES_ROWS, PAIR_KEYS), BF16)
    vth = jnp.concatenate([vt[h * HEAD_DIM:(h + 1) * HEAD_DIM, :], ones], axis=0)
    ot = jnp.dot(vth, e, preferred_element_type=F32)
    denom = ot[HEAD_DIM:HEAD_DIM + 1, :] + e_sink
    ot = ot[:HEAD_DIM, :] * (1.0 / denom)
    return jnp.concatenate([ot[:, g * PAIR:(g + 1) * PAIR] for g in range(GQA_GROUP)], axis=0).T


def _attend_probs(q, k, sink_ref, lq):
    probs = []
    for h in range(N_KV_HEADS):
        kh = k[:, h * HEAD_DIM:(h + 1) * HEAD_DIM]
        heads = [h * GQA_GROUP + g for g in range(GQA_GROUP)]
        qh = jnp.concatenate([q[:, n * HEAD_DIM:(n + 1) * HEAD_DIM] for n in heads], axis=0)
        s = lax.dot_general(qh, kh, _NT, preferred_element_type=F32)
        sink = jnp.concatenate(
            [jnp.broadcast_to(sink_ref[n:n + 1, 0:1], (lq, 1)) for n in heads], axis=0)
        m = jnp.maximum(jnp.max(s, axis=-1, keepdims=True), sink)
        e = jnp.exp(s - m)
        probs.append((e.astype(BF16), jnp.sum(e, axis=-1, keepdims=True) + jnp.exp(sink - m)))
    return probs


def _attend_values(probs, v, o_ref, lq):
    for h in range(N_KV_HEADS):
        vh = v[:, h * HEAD_DIM:(h + 1) * HEAD_DIM]
        e, denom = probs[h]
        oh = jnp.dot(e, vh, preferred_element_type=F32) * (1.0 / denom)
        o_ref[:, h * GQA_GROUP * HEAD_DIM:(h + 1) * GQA_GROUP * HEAD_DIM] = jnp.concatenate(
            [oh[g * lq:(g + 1) * lq, :] for g in range(GQA_GROUP)], axis=1).astype(o_ref.dtype)


def _attn_sample_body(q_ref, ck_ref, cv_ref, kn_ref, vn_ref, km_ref, vm_ref, sink_ref, o_ref):
    lq = q_ref.shape[0]
    k = jnp.concatenate([ck_ref[...], kn_ref[...], km_ref[...]], axis=0).astype(BF16)
    probs = _attend_probs(q_ref[...], k, sink_ref, lq)
    v = jnp.concatenate([cv_ref[...], vn_ref[...], vm_ref[...]], axis=0).astype(BF16)
    _attend_values(probs, v, o_ref, lq)


def _attend_sample(o16, o32, cache_k, cache_v, meta_block, sinks, n_streams, ds):
    cache_rows = cache_k.shape[1]
    new = lambda col: pl.BlockSpec((ds, KV_WIDTH), lambda s: (s, col))
    meta = lambda col: pl.BlockSpec((N_META, KV_WIDTH), lambda s: (meta_block, col))
    cache = pl.BlockSpec((None, cache_rows, KV_WIDTH), lambda s: (s, 0, 0))
    return pl.pallas_call(
        _attn_sample_body,
        grid=(n_streams,),
        in_specs=[
            pl.BlockSpec((ds, ATTN_WIDTH), lambda s: (s, COL_QA)),
            cache, cache, new(0), new(1), meta(0), meta(1),
            pl.BlockSpec((N_Q_HEADS, LANES), lambda s: (0, 0)),
        ],
        out_specs=pl.BlockSpec((ds, ATTN_WIDTH), lambda s: (s, 0)),
        out_shape=jax.ShapeDtypeStruct((n_streams * ds, ATTN_WIDTH), BF16),
        compiler_params=pltpu.CompilerParams(
            dimension_semantics=("parallel",), vmem_limit_bytes=VMEM_LIMIT),
        name="attend_sample",
    )(o16, cache_k, cache_v, o32, o32, o32, o32, sinks)


def _mlstm_body(q_ref, k_ref, v_ref, gt_ref, bias_ref, c0_ref, n0_ref, m0_ref,
                hn_ref, c_ref, n_ref, m_ref, *, blk):
    setup = _mlstm_setup(gt_ref, bias_ref, blk)
    scores = [_mlstm_scores(setup, h, q_ref, k_ref, m0_ref) for h in range(M_HEADS)]
    for h in range(M_HEADS):
        hv = _mlstm_out(setup, h, scores[h], q_ref, v_ref, c0_ref, n0_ref)
        hn_ref[:, h * M_V_DIM:(h + 1) * M_V_DIM] = hv.astype(hn_ref.dtype)
    for h in range(M_HEADS):
        _mlstm_state(setup, h, scores[h], k_ref, v_ref, (c0_ref, n0_ref, m0_ref), (c_ref, n_ref, m_ref),
                     blk, None)


def _mlstm_setup(gt_ref, bias_ref, blk):
    x = gt_ref[...] + bias_ref[...]
    lane = lax.broadcasted_iota(jnp.int32, x.shape, 1)
    log_f = jnp.minimum(x, 0.0) - jnp.log(1.0 + jnp.exp(-jnp.abs(x)))
    gates = jnp.where(lane < M_HEADS, x, log_f)
    row = lax.broadcasted_iota(jnp.int32, (blk, blk), 0)
    col = lax.broadcasted_iota(jnp.int32, (blk, blk), 1)
    causal = row >= col
    hi = gates.astype(BF16)
    rest = gates - hi.astype(F32)
    mid = rest.astype(BF16)
    lo = (rest - mid.astype(F32)).astype(BF16)
    parts = jnp.dot(causal.astype(BF16), jnp.concatenate([hi, mid, lo], axis=1), preferred_element_type=F32)
    csum = parts[:, :LANES] + parts[:, LANES:2 * LANES] + parts[:, 2 * LANES:]
    z = jnp.where(lane < M_HEADS, gates, csum)
    pad = (-blk) % LANES
    zsq = z if pad == 0 else jnp.concatenate([z, jnp.zeros((pad, LANES), F32)], axis=0)
    zt = zsq.T[:, :blk]
    return z, zt, causal


def _mlstm_scores(setup, h, q_ref, k_ref, m_ref):
    z, zt, causal = setup
    g_r = zt[h:h + 1, :] - zt[M_HEADS + h:M_HEADS + h + 1, :]
    m_prev = m_ref[h:h + 1, 0:1]
    d0 = jnp.where(causal, g_r, -jnp.inf)
    mm = jnp.maximum(m_prev, jnp.max(d0, axis=-1, keepdims=True))
    w = jnp.exp(d0 - mm)
    qh = q_ref[:, h * M_QK_DIM:(h + 1) * M_QK_DIM]
    kh = k_ref[:, h * M_QK_DIM:(h + 1) * M_QK_DIM]
    qk = lax.dot_general(qh, kh, _NT, preferred_element_type=F32)
    wqk = w * (qk * MLSTM_K_SCALE)
    return m_prev, mm, wqk.astype(BF16), jnp.sum(wqk, axis=-1, keepdims=True)


def _mlstm_finish(setup, h, scores, q_ref, k_ref, v_ref, c_ref, n_ref, m_ref, blk, live):
    hv = _mlstm_out(setup, h, scores, q_ref, v_ref, c_ref, n_ref)
    state = (c_ref, n_ref, m_ref)
    _mlstm_state(setup, h, scores, k_ref, v_ref, state, state, blk, live)
    return hv


def _mlstm_out(setup, h, scores, q_ref, v_ref, c_ref, n_ref):
    z, _, _ = setup
    m_prev, mm, wqk, den_intra = scores
    m_t = z[:, M_HEADS + h:M_HEADS + h + 1] + mm
    a = jnp.exp(m_prev - mm)
    qh = q_ref[:, h * M_QK_DIM:(h + 1) * M_QK_DIM]
    vh = v_ref[:, h * M_V_DIM:(h + 1) * M_V_DIM]
    num = a * jnp.dot(qh, c_ref[h].astype(BF16), preferred_element_type=F32) + jnp.dot(
        wqk, vh, preferred_element_type=F32)
    den = a * jnp.sum(qh.astype(F32) * n_ref[h:h + 1, :], axis=-1, keepdims=True) + den_intra
    r = 1.0 / jnp.maximum(jnp.abs(den), jnp.exp(-m_t))
    rms_h = r * jnp.sqrt(jnp.mean(num * num, axis=-1, keepdims=True))
    return num * (r * lax.rsqrt(rms_h * rms_h + EPS))


def _mlstm_state(setup, h, scores, k_ref, v_ref, state_in, state_out, blk, live):
    c_ref, n_ref, m_in = state_in
    c_out, n_out, m_out = state_out
    z, _, _ = setup
    m_prev, mm, _, _ = scores
    keep = (lambda new, old: new) if live is None else (lambda new, old: jnp.where(live, new, old))
    li_c = z[:, h:h + 1]
    b_c = z[:, M_HEADS + h:M_HEADS + h + 1]
    m_t = b_c + mm
    kh = k_ref[:, h * M_QK_DIM:(h + 1) * M_QK_DIM]
    vh = v_ref[:, h * M_V_DIM:(h + 1) * M_V_DIM]
    c_h = c_ref[h]
    n_h = n_ref[h:h + 1, :]

    m_new = m_t[blk - 1:blk, :]
    b_last = b_c[blk - 1:blk, :]
    ws = jnp.exp(b_last - b_c + li_c - m_new) * MLSTM_K_SCALE
    a_last = jnp.exp(b_last + m_prev - m_new)
    vs = (vh.astype(F32) * ws).astype(BF16)
    ktv = lax.dot_general(kh, vs, (((0,), (0,)), ((), ())), preferred_element_type=F32)
    c_out[h] = keep(a_last * c_h + ktv, c_h)
    n_out[h:h + 1, :] = keep(a_last * n_h + jnp.sum(kh.astype(F32) * ws, axis=0, keepdims=True), n_h)
    m_out[h:h + 1, :] = keep(jnp.broadcast_to(m_new, (1, LANES)), m_in[h:h + 1, :])


def _mlstm(o16, o32, bias, c0, n0, m0, *, blk, n_streams, first_block, col0):
    rb = lambda s: first_block + s
    qk_blk = 2 * (COL_QKM - col0)
    state = lambda *shape: pl.BlockSpec((None,) + shape, lambda s: (s,) + (0,) * len(shape))
    return pl.pallas_call(
        functools.partial(_mlstm_body, blk=blk),
        grid=(n_streams,),
        in_specs=[
            pl.BlockSpec((blk, M_QK_WIDTH), lambda s: (rb(s), qk_blk)),
            pl.BlockSpec((blk, M_QK_WIDTH), lambda s: (rb(s), qk_blk + 1)),
            pl.BlockSpec((blk, M_V_WIDTH), lambda s: (rb(s), COL_VM - col0)),
            pl.BlockSpec((blk, LANES), lambda s: (rb(s), GATE_BLOCK)),
            pl.BlockSpec((1, LANES), lambda s: (0, 0)),
            state(M_HEADS, M_QK_DIM, M_V_DIM), state(M_HEADS, M_QK_DIM), state(M_HEADS, LANES),
        ],
        out_specs=[
            pl.BlockSpec((blk, M_V_WIDTH), lambda s: (s, 0)),
            state(M_HEADS, M_QK_DIM, M_V_DIM), state(M_HEADS, M_QK_DIM), state(M_HEADS, LANES),
        ],
        out_shape=[
            jax.ShapeDtypeStruct((n_streams * blk, M_V_WIDTH), BF16),
            jax.ShapeDtypeStruct((n_streams, M_HEADS, M_QK_DIM, M_V_DIM), F32),
            jax.ShapeDtypeStruct((n_streams, M_HEADS, M_QK_DIM), F32),
            jax.ShapeDtypeStruct((n_streams, M_HEADS, LANES), F32),
        ],
        compiler_params=pltpu.CompilerParams(
            dimension_semantics=("parallel",), vmem_limit_bytes=VMEM_LIMIT),
        name="mlstm_blk%d" % blk,
    )(o16, o16, o16, o32, bias, c0, n0, m0)


def _merge_body(x_ref, oa_ref, hn_ref, za_ref, om_ref, zm_ref, ga_ref, gm_ref, gmh_ref,
                wpa_ref, wpm_ref, wout_ref, gf_ref, y_ref, wpa_out, wpm_out, wout_out, a_scr, m_scr, acc_scr):
    j = pl.program_id(0)

    @pl.when(j == 0)
    def _():
        a_scr[...] = (oa_ref[...].astype(F32) * za_ref[...].astype(F32)).astype(BF16)
        m_scr[...] = ((hn_ref[...].astype(F32) * gmh_ref[...]) * om_ref[...].astype(F32)
                      * zm_ref[...].astype(F32)).astype(BF16)
        acc_scr[...] = x_ref[...]

    wpa_out[...] = wpa_ref[...].astype(BF16)
    wpm_out[...] = wpm_ref[...].astype(BF16)
    wout_out[...] = wout_ref[...].astype(BF16)
    ya = jnp.dot(a_scr[...], wpa_out[...], preferred_element_type=F32)
    ym = jnp.dot(m_scr[...], wpm_out[...], preferred_element_type=F32)
    merged = ga_ref[...].astype(F32) * ya + gm_ref[...].astype(F32) * ym
    acc_scr[...] += jnp.dot(merged.astype(BF16), wout_out[...], preferred_element_type=F32)

    @pl.when(j == pl.num_programs(0) - 1)
    def _():
        xo = acc_scr[...]
        ms = jnp.mean(xo * xo, axis=-1, keepdims=True)
        y_ref[...] = (xo * lax.rsqrt(ms + EPS)) * gf_ref[...]


def _merge(x, o_att, hn, o16, g_mh, w_pa, w_pm, w_out, g_final, rows, col0):
    n_chunks = D_MODEL // MERGE_TN
    per_group = D_MODEL // MERGE_TN
    full = lambda col: pl.BlockSpec((rows, D_MODEL), lambda j: (0, col))
    gate = lambda col: pl.BlockSpec((rows, MERGE_TN), lambda j: (0, col * per_group + j))
    vec = pl.BlockSpec((1, D_MODEL), lambda j: (0, 0))
    w_cols = pl.BlockSpec((D_MODEL, MERGE_TN), lambda j: (0, j))
    w_rows = pl.BlockSpec((MERGE_TN, D_MODEL), lambda j: (j, 0))
    return pl.pallas_call(
        _merge_body,
        grid=(n_chunks,),
        in_specs=[full(0), full(0), full(0), full(COL_ZA - col0), full(COL_OM - col0), full(COL_ZM - col0),
                  gate(COL_GA - col0), gate(COL_GM - col0), vec, w_cols, w_cols, w_rows, vec],
        out_specs=[full(0), w_cols, w_cols, w_rows],
        out_shape=[jax.ShapeDtypeStruct((rows, D_MODEL), F32)]
        + [jax.ShapeDtypeStruct((D_MODEL, D_MODEL), BF16)] * 3,
        scratch_shapes=[pltpu.VMEM((rows, D_MODEL), BF16), pltpu.VMEM((rows, D_MODEL), BF16),
                        pltpu.VMEM((rows, D_MODEL), F32)],
        compiler_params=pltpu.CompilerParams(
            dimension_semantics=("arbitrary",), vmem_limit_bytes=VMEM_LIMIT),
        name="merge",
    )(x, o_att, hn, o16, o16, o16, o16, o16, g_mh, w_pa, w_pm, w_out, g_final)


def _mixer_body(qt_ref, ka_ref, kb_ref, km_ref, vta_ref, vtb_ref, vtm_ref, sink_ref,
                q_ref, k_ref, v_ref, gt_ref, bias_ref, c0_ref, n0_ref, m0_ref,
                x_ref, za_ref, om_ref, zm_ref, ga_ref, gm_ref, gmh_ref, wpa_ref, wpm_ref, wout_ref, gf_ref,
                y_ref, c_ref, n_ref, m_ref, a_scr, m_scr, *, n_blk):
    c = pl.program_id(0)

    @pl.when(c == 0)
    def _():
        c_ref[...] = c0_ref[...]
        n_ref[...] = n0_ref[...]
        m_ref[...] = m0_ref[...]
        a_scr[...] = jnp.zeros(a_scr.shape, a_scr.dtype)
        m_scr[...] = jnp.zeros(m_scr.shape, m_scr.dtype)

    wr = c % MIX_SLOTS
    rd = 1 - wr
    blk = jnp.minimum(c, n_blk - 1)
    live = c < n_blk
    width = GQA_GROUP * HEAD_DIM
    heads_per_piece = width // M_V_DIM
    a_in = a_scr[rd]
    m_in = m_scr[rd]

    def mlstm_piece(h):
        hc = slice(h * M_V_DIM, (h + 1) * M_V_DIM)
        hv = _mlstm_finish(cell, h, scores[h], q_ref, k_ref, v_ref, c_ref, n_ref, m_ref, PAIR, live)
        m_scr[wr, :, hc] = ((hv * gmh_ref[:, hc]) * om_ref[:, hc].astype(F32)
                            * zm_ref[:, hc].astype(F32)).astype(BF16)

    def attn_piece(h, probs):
        cols = slice(h * width, (h + 1) * width)
        oa = _attn_pair_out(attn, h, probs)
        a_scr[wr, :, cols] = (oa * za_ref[:, cols].astype(F32)).astype(BF16)

    attn = _attn_pair_setup(blk, ka_ref, kb_ref, km_ref, vta_ref, vtb_ref, vtm_ref)
    cell = _mlstm_setup(gt_ref, bias_ref, PAIR)
    scores = [_mlstm_scores(cell, h, q_ref, k_ref, m_ref) for h in range(M_HEADS)]
    probs = _attn_pair_probs(attn, 0, qt_ref, sink_ref)
    merged = []
    mlstm_heads = ((0, 1, 2), (3, 4, 5), (6, 7), ())
    for i in range(N_KV_HEADS):
        last = i + 1 == N_KV_HEADS
        cols = slice(i * width, (i + 1) * width)
        ya = jnp.dot(a_in, wpa_ref[:, cols], preferred_element_type=F32)
        for h in mlstm_heads[i][:2]:
            mlstm_piece(h)
        next_probs = None if last else _attn_pair_probs(attn, i + 1, qt_ref, sink_ref)
        ym = jnp.dot(m_in, wpm_ref[:, cols], preferred_element_type=F32)
        for h in mlstm_heads[i][2:]:
            mlstm_piece(h)
        merged.append((ga_ref[:, cols].astype(F32) * ya + gm_ref[:, cols].astype(F32) * ym).astype(BF16))
        if last:
            xo = x_ref[...] + jnp.dot(jnp.concatenate(merged, axis=1), wout_ref[...],
                                      preferred_element_type=F32)
        attn_piece(i, probs)
        probs = next_probs
    ms = jnp.mean(xo * xo, axis=-1, keepdims=True)
    y_ref[...] = (xo * lax.rsqrt(ms + EPS)) * gf_ref[...]


def _mixer(x, qvt, o32, o16, k_meta, vt_meta, sinks_t, bias, c0, n0, m0, g_mh, w_pa, w_pm, w_out, g_final,
           col0):
    seq = x.shape[0]
    assert seq % PAIR == 0
    n_blk = seq // PAIR
    cur = lambda c: jnp.minimum(c, n_blk - 1)
    prv = lambda c: jnp.maximum(cur(c) - 1, 0)
    mrg = lambda c: jnp.maximum(c - 1, 0)
    v_row = ATTN_WIDTH // KV_WIDTH
    qk_blk = 2 * (COL_QKM - col0)
    tile = lambda col: pl.BlockSpec((PAIR, D_MODEL), lambda c: (mrg(c), col))
    tile_cur = lambda col: pl.BlockSpec((PAIR, D_MODEL), lambda c: (cur(c), col))
    vec = pl.BlockSpec((1, D_MODEL), lambda c: (0, 0))
    weight = pl.BlockSpec((D_MODEL, D_MODEL), lambda c: (0, 0), pipeline_mode=pl.Buffered(1))
    state = lambda *blk: pl.BlockSpec((None,) + blk, lambda c: (0,) * (len(blk) + 1))
    return pl.pallas_call(
        functools.partial(_mixer_body, n_blk=n_blk),
        grid=(n_blk + 1,),
        in_specs=[
            pl.BlockSpec((None, ATTN_WIDTH, PAIR), lambda c: (cur(c), 0, 0)),
            pl.BlockSpec((PAIR, KV_WIDTH), lambda c: (prv(c), 0)),
            pl.BlockSpec((PAIR, KV_WIDTH), lambda c: (cur(c), 0)),
            pl.BlockSpec((N_META, KV_WIDTH), lambda c: (0, 0)),
            pl.BlockSpec((None, KV_WIDTH, PAIR), lambda c: (prv(c), v_row, 0)),
            pl.BlockSpec((None, KV_WIDTH, PAIR), lambda c: (cur(c), v_row, 0)),
            pl.BlockSpec((KV_WIDTH, N_META), lambda c: (0, 0)),
            pl.BlockSpec((N_KV_HEADS, 1, GQA_GROUP * PAIR), lambda c: (0, 0, 0)),
            pl.BlockSpec((PAIR, M_QK_WIDTH), lambda c: (cur(c), qk_blk)),
            pl.BlockSpec((PAIR, M_QK_WIDTH), lambda c: (cur(c), qk_blk + 1)),
            pl.BlockSpec((PAIR, M_V_WIDTH), lambda c: (cur(c), COL_VM - col0)),
            pl.BlockSpec((PAIR, LANES), lambda c: (cur(c), GATE_BLOCK)),
            pl.BlockSpec((1, LANES), lambda c: (0, 0)),
            state(M_HEADS, M_QK_DIM, M_V_DIM), state(M_HEADS, M_QK_DIM), state(M_HEADS, LANES),
            tile(0), tile_cur(COL_ZA - col0), tile_cur(COL_OM - col0), tile_cur(COL_ZM - col0),
            tile(COL_GA - col0), tile(COL_GM - col0), vec, weight, weight, weight, vec,
        ],
        out_specs=[
            tile(0),
            state(M_HEADS, M_QK_DIM, M_V_DIM), state(M_HEADS, M_QK_DIM), state(M_HEADS, LANES),
        ],
        out_shape=[
            jax.ShapeDtypeStruct((seq, D_MODEL), F32),
            jax.ShapeDtypeStruct((1, M_HEADS, M_QK_DIM, M_V_DIM), F32),
            jax.ShapeDtypeStruct((1, M_HEADS, M_QK_DIM), F32),
            jax.ShapeDtypeStruct((1, M_HEADS, LANES), F32),
        ],
        scratch_shapes=[pltpu.VMEM((MIX_SLOTS, PAIR, ATTN_WIDTH), BF16),
                        pltpu.VMEM((MIX_SLOTS, PAIR, M_V_WIDTH), BF16)],
        compiler_params=pltpu.CompilerParams(
            dimension_semantics=("arbitrary",), vmem_limit_bytes=VMEM_LIMIT),
        name="mixer",
    )(qvt, o32, o32, k_meta, qvt, qvt, vt_meta, sinks_t,
      o16, o16, o16, o32, bias, c0, n0, m0,
      x, o16, o16, o16, o16, o16, g_mh, w_pa, w_pm, w_out, g_final)


def kernel(x_prompt, x_sample, cache_k, cache_v, state_C, state_n, state_m, meta_tokens, g_norm, w_in,
           b_igate, b_fgate, attn_sinks, g_mhnorm, w_pa, w_pm, w_out, g_final):
    batch, seq, _ = x_prompt.shape
    db, ds, _ = x_sample.shape
    depth = w_in.shape[0]
    assert batch == 1 and depth == 1 and ds == N_META
    cache_rows = cache_k.shape[2]

    w_t = w_in[0].T
    g_in = g_norm[0].reshape(1, D_MODEL)
    bias = jnp.concatenate([b_igate[0], b_fgate[0], jnp.zeros((LANES - 2 * M_HEADS,), F32)]).reshape(1, LANES)
    sinks = jnp.broadcast_to(attn_sinks[0][:, None], (N_Q_HEADS, LANES))
    sinks_t = jnp.broadcast_to(attn_sinks[0].reshape(N_KV_HEADS, 1, GQA_GROUP, 1),
                               (N_KV_HEADS, 1, GQA_GROUP, PAIR)).reshape(N_KV_HEADS, 1, GQA_GROUP * PAIR)
    g_mh = g_mhnorm[0].reshape(1, M_V_WIDTH)
    g_fin = g_final.reshape(1, D_MODEL)

    xp = x_prompt.reshape(seq, D_MODEL)
    xs_rows = db * ds
    x_small = jnp.concatenate([x_sample.reshape(xs_rows, D_MODEL), meta_tokens.astype(F32)], axis=0)
    meta_block = xs_rows // N_META
    p_tm = min(seq, 1024)
    *first, wmain, wkv = _project_first(xp, g_in, w_t, tm=p_tm, col0=COL_ZA)
    p32, p16, xn_p = first if seq == p_tm else _project(xp, g_in, wkv, wmain, tm=p_tm, col0=COL_ZA, prior=first)
    s32, s16, _ = _project(x_small, g_in, wkv, wmain, tm=x_small.shape[0], col0=COL_QA)
    qvt = _project_t(wmain, wkv, xn_p, tm=p_tm)

    zeros = lambda *shape: jnp.zeros(shape, F32)
    _, c_meta, n_meta, m_meta = _mlstm(
        s16, s32, bias, zeros(1, M_HEADS, M_QK_DIM, M_V_DIM), zeros(1, M_HEADS, M_QK_DIM),
        zeros(1, M_HEADS, LANES), blk=N_META, n_streams=1, first_block=meta_block, col0=COL_QA)

    ck = cache_k[0].reshape(db, cache_rows, KV_WIDTH)
    cv = cache_v[0].reshape(db, cache_rows, KV_WIDTH)
    oa_s = _attend_sample(s16, s32, ck, cv, meta_block, sinks, db, ds)
    m0_s = jnp.broadcast_to(state_m[0][:, :, None], (db, M_HEADS, LANES))
    hn_s, c_s, n_s, m_s = _mlstm(s16, s32, bias, state_C[0], state_n[0], m0_s, blk=ds, n_streams=db,
                                 first_block=0, col0=COL_QA)
    y_s, wpa, wpm, wout = _merge(x_small, oa_s, hn_s, s16, g_mh, w_pa[0], w_pm[0], w_out[0], g_fin,
                                 rows=xs_rows, col0=COL_QA)

    k_meta = s32[xs_rows:, :KV_WIDTH]
    vt_meta = s32[xs_rows:, KV_WIDTH:2 * KV_WIDTH].T.astype(BF16)
    y_p, c_p, n_p, m_p = _mixer(xp, qvt, p32, p16, k_meta, vt_meta, sinks_t, bias, c_meta, n_meta, m_meta,
                                g_mh, wpa, wpm, wout, g_fin, col0=COL_ZA)

    kv_shape = (1, batch, cache_rows, N_KV_HEADS, HEAD_DIM)
    k_p = p32[seq - cache_rows:, :KV_WIDTH].reshape(kv_shape)
    v_p = p32[seq - cache_rows:, KV_WIDTH:2 * KV_WIDTH].reshape(kv_shape)
    new_k = s32[:xs_rows, :KV_WIDTH].reshape(db, ds, KV_WIDTH)
    new_v = s32[:xs_rows, KV_WIDTH:2 * KV_WIDTH].reshape(db, ds, KV_WIDTH)
    skv_shape = (1, db, cache_rows, N_KV_HEADS, HEAD_DIM)
    k_s = jnp.concatenate([ck, new_k], axis=1)[:, -cache_rows:].reshape(skv_shape)
    v_s = jnp.concatenate([cv, new_v], axis=1)[:, -cache_rows:].reshape(skv_shape)

    return (y_p.reshape(batch, seq, D_MODEL), y_s.reshape(db, ds, D_MODEL),
            k_p, v_p, c_p[None], n_p[None], m_p[None, :, :, 0],
            k_s, v_s, c_s[None], n_s[None], m_s[None, :, :, 0])
```

```python
import functools
import math

import jax
import jax.numpy as jnp
import numpy as np
from jax import lax
from jax.experimental import pallas as pl
from jax.experimental.pallas import tpu as pltpu

F32 = jnp.float32
BF16 = jnp.bfloat16

D_MODEL = 2048
CHUNK = 64
N_META = 16
HEAD_DIM = 64
N_Q_HEADS = 32
N_KV_HEADS = 4
GQA_GROUP = N_Q_HEADS // N_KV_HEADS
ATTN_WIDTH = N_Q_HEADS * HEAD_DIM
KV_WIDTH = N_KV_HEADS * HEAD_DIM
M_HEADS = 8
M_QK_DIM = 128
M_V_DIM = 256
M_QK_WIDTH = M_HEADS * M_QK_DIM
M_V_WIDTH = M_HEADS * M_V_DIM
EPS = 1e-6
NEG_INF = -1e30
COL_SIZES = (ATTN_WIDTH, KV_WIDTH, KV_WIDTH, ATTN_WIDTH, M_QK_WIDTH, M_QK_WIDTH, M_V_WIDTH, M_V_WIDTH,
             M_HEADS, M_HEADS, M_V_WIDTH, D_MODEL, D_MODEL)
COL_STARTS = tuple(int(v) for v in np.cumsum((0,) + COL_SIZES[:-1]))
(SRC_QA, SRC_KA, SRC_VA, SRC_ZA, SRC_QM, SRC_KM, SRC_VM, SRC_OM, SRC_IG, SRC_FG, SRC_ZM, SRC_GA,
 SRC_GM) = COL_STARTS

LANES = 128
VMEM_LIMIT = 56 * 1024 * 1024

KV_OUT = 2 * KV_WIDTH + LANES
GATE_BLOCK = 2 * KV_WIDTH // LANES
MAIN_OUT = 8 * D_MODEL
COL_QA, COL_ZA, COL_QKM, COL_VM, COL_OM, COL_ZM, COL_GA, COL_GM = range(8)
PROJ_TN = 1024
FIRST_TN = 512
ROW_ALIGN = 16
MLSTM_K_SCALE = M_QK_DIM ** -0.5
Q_SCALE = 1.0 / math.sqrt(HEAD_DIM)
QV_ROWS = ATTN_WIDTH + KV_WIDTH
PAIR = 2 * CHUNK
PAIR_KEYS = 2 * PAIR + N_META
ONES_ROWS = 16
MIX_SLOTS = 2
MERGE_TN = 256
NORM_SLABS = 4


def _sigmoid(x):
    return 0.5 * jnp.tanh(0.5 * x) + 0.5


def _silu(x):
    return x * _sigmoid(x)


_NT = (((1,), (1,)), ((), ()))


def _main_row_offsets(tn):
    groups = [(SRC_QA, ATTN_WIDTH), (SRC_ZA, ATTN_WIDTH), (SRC_QM, M_QK_WIDTH), (SRC_KM, M_QK_WIDTH),
              (SRC_VM, M_V_WIDTH), (SRC_OM, M_V_WIDTH), (SRC_ZM, M_V_WIDTH), (SRC_GA, D_MODEL),
              (SRC_GM, D_MODEL)]
    offs = []
    for start, width in groups:
        assert start % ROW_ALIGN == 0 and width % tn == 0
        offs += [(start + b * tn) // ROW_ALIGN for b in range(width // tn)]
    assert len(offs) * tn == MAIN_OUT
    return np.asarray(offs, np.int32)


def _proj_body(*refs, tile0, from_f32, tn=PROJ_TN, skip=0, n_prior=0):
    if from_f32:
        (_, x_ref, g_ref, wkv_ref, wg_ref, w_ref, xs_ref, o32_ref, o16_ref, xn_ref, wmain_out, wkv_out,
         s32_ref, s16_ref, xsn_scr) = refs
    else:
        x_ref, g_ref, wkv_ref, w_ref = refs[:4]
        o32_ref, o16_ref, xn_ref = refs[4 + n_prior:]

    def normed(x):
        ms = jnp.mean(x * x, axis=-1, keepdims=True)
        return ((x * lax.rsqrt(ms + EPS)) * g_ref[...]).astype(BF16)

    @pl.when(pl.program_id(1) == 0)
    def _():
        if from_f32:
            wkv_out[:2 * KV_WIDTH, :] = wkv_ref[...].astype(BF16)
            wkv_out[2 * KV_WIDTH:2 * KV_WIDTH + ROW_ALIGN, :] = wg_ref[...].astype(BF16)
            wkv_out[2 * KV_WIDTH + ROW_ALIGN:, :] = jnp.zeros((LANES - ROW_ALIGN, D_MODEL), BF16)
            wkv = wkv_out[...]
            xsn_scr[...] = normed(xs_ref[...])
            s32_ref[...] = lax.dot_general(xsn_scr[...], wkv, _NT, preferred_element_type=F32)
        else:
            wkv = wkv_ref[...]
        tm = x_ref.shape[0]
        slab = tm // NORM_SLABS if tm % (NORM_SLABS * ROW_ALIGN) == 0 else tm
        for r in range(0, tm, slab):
            xn = normed(x_ref[r:r + slab, :])
            xn_ref[r:r + slab, :] = xn
            o32_ref[r:r + slab, :] = lax.dot_general(xn, wkv, _NT, preferred_element_type=F32)

    group = (pl.program_id(1) + tile0) // (D_MODEL // tn)
    is_sigmoid = (group == COL_OM) | (group == COL_GA) | (group == COL_GM)
    is_silu = (group == COL_ZA) | (group == COL_ZM)
    is_q = group == COL_QA

    def small_tile(act, w):
        acc = lax.dot_general(xsn_scr[...], w, _NT, preferred_element_type=F32)
        s16_ref[...] = act(acc).astype(BF16)

    def tile(act):
        w = w_ref[...].astype(BF16)
        if from_f32:
            wmain_out[...] = w
            small_tile(act, w)
        acc = lax.dot_general(xn_ref[...], w, _NT, preferred_element_type=F32)
        o16_ref[...] = act(acc).astype(BF16)

    plain = jnp.logical_not(is_sigmoid | is_silu | is_q)
    if skip:
        convert_only = pl.program_id(1) < skip

        @pl.when(convert_only)
        def _():
            w = w_ref[...].astype(BF16)
            wmain_out[...] = w
            small_tile(lambda acc: acc * Q_SCALE, w)

        computed = jnp.logical_not(convert_only)
        is_sigmoid, is_silu, is_q, plain = (is_sigmoid & computed, is_silu & computed, is_q & computed,
                                            plain & computed)
    pl.when(is_sigmoid)(lambda: tile(_sigmoid))
    pl.when(is_silu)(lambda: tile(_silu))
    pl.when(is_q)(lambda: tile(lambda acc: acc * Q_SCALE))
    pl.when(plain)(lambda: tile(lambda acc: acc))


def _project_first(x, g, w_t, xs, tm, col0):
    rows = x.shape[0]
    rows_s = xs.shape[0]
    assert SRC_KA % (2 * KV_WIDTH) == 0 and SRC_VA == SRC_KA + KV_WIDTH
    assert SRC_IG % ROW_ALIGN == 0 and SRC_FG == SRC_IG + M_HEADS and 2 * M_HEADS == ROW_ALIGN
    assert col0 <= COL_ZA and COL_QA == 0
    tn = FIRST_TN
    n_tiles = MAIN_OUT // tn
    skip = col0 * (D_MODEL // tn)
    return pl.pallas_call(
        functools.partial(_proj_body, tile0=0, from_f32=True, tn=tn, skip=skip),
        grid_spec=pltpu.PrefetchScalarGridSpec(
            num_scalar_prefetch=1,
            grid=(1, n_tiles),
            in_specs=[
                pl.BlockSpec((tm, D_MODEL), lambda i, j, off: (0, 0), pipeline_mode=pl.Buffered(1)),
                pl.BlockSpec((1, D_MODEL), lambda i, j, off: (0, 0)),
                pl.BlockSpec((2 * KV_WIDTH, D_MODEL), lambda i, j, off: (SRC_KA // (2 * KV_WIDTH), 0),
                             pipeline_mode=pl.Buffered(1)),
                pl.BlockSpec((ROW_ALIGN, D_MODEL), lambda i, j, off: (SRC_IG // ROW_ALIGN, 0),
                             pipeline_mode=pl.Buffered(1)),
                pl.BlockSpec((pl.Element(tn), pl.Element(D_MODEL)),
                             lambda i, j, off: (off[j] * ROW_ALIGN, 0)),
                pl.BlockSpec((rows_s, D_MODEL), lambda i, j, off: (0, 0), pipeline_mode=pl.Buffered(1)),
            ],
            out_specs=[
                pl.BlockSpec((tm, KV_OUT), lambda i, j, off: (0, 0)),
                pl.BlockSpec((tm, tn), lambda i, j, off: (0, jnp.maximum(j - skip, 0))),
                pl.BlockSpec((tm, D_MODEL), lambda i, j, off: (0, 0)),
                pl.BlockSpec((tn, D_MODEL), lambda i, j, off: (j, 0)),
                pl.BlockSpec((KV_OUT, D_MODEL), lambda i, j, off: (0, 0)),
                pl.BlockSpec((rows_s, KV_OUT), lambda i, j, off: (0, 0)),
                pl.BlockSpec((rows_s, tn), lambda i, j, off: (0, j)),
            ],
            scratch_shapes=[pltpu.VMEM((rows_s, D_MODEL), BF16)],
        ),
        out_shape=[jax.ShapeDtypeStruct((rows, KV_OUT), F32),
                   jax.ShapeDtypeStruct((rows, MAIN_OUT - skip * tn), BF16),
                   jax.ShapeDtypeStruct((rows, D_MODEL), BF16),
                   jax.ShapeDtypeStruct((MAIN_OUT, D_MODEL), BF16),
                   jax.ShapeDtypeStruct((KV_OUT, D_MODEL), BF16),
                   jax.ShapeDtypeStruct((rows_s, KV_OUT), F32),
                   jax.ShapeDtypeStruct((rows_s, MAIN_OUT), BF16)],
        compiler_params=pltpu.CompilerParams(
            dimension_semantics=("arbitrary", "arbitrary"), vmem_limit_bytes=VMEM_LIMIT),
        name="project_first",
    )(jnp.asarray(_main_row_offsets(tn)), x, g, w_t, w_t, w_t, xs)


def _project(x, g, wkv, wmain, tm, col0, prior=None):
    rows = x.shape[0]
    assert rows % tm == 0
    tile0 = col0 * (D_MODEL // PROJ_TN)
    n_tiles = MAIN_OUT // PROJ_TN - tile0
    row0 = 0 if prior is None else 1
    prior = () if prior is None else tuple(prior)
    return pl.pallas_call(
        functools.partial(_proj_body, tile0=tile0, from_f32=False, n_prior=len(prior)),
        grid=(rows // tm - row0, n_tiles),
        in_specs=[
            pl.BlockSpec((tm, D_MODEL), lambda i, j: (i + row0, 0)),
            pl.BlockSpec((1, D_MODEL), lambda i, j: (0, 0)),
            pl.BlockSpec((KV_OUT, D_MODEL), lambda i, j: (0, 0), pipeline_mode=pl.Buffered(1)),
            pl.BlockSpec((PROJ_TN, D_MODEL), lambda i, j: (j + tile0, 0)),
        ] + [pl.BlockSpec(memory_space=pl.ANY)] * len(prior),
        out_specs=[
            pl.BlockSpec((tm, KV_OUT), lambda i, j: (i + row0, 0)),
            pl.BlockSpec((tm, PROJ_TN), lambda i, j: (i + row0, j)),
            pl.BlockSpec((tm, D_MODEL), lambda i, j: (i + row0, 0)),
        ],
        out_shape=[jax.ShapeDtypeStruct((rows, KV_OUT), F32),
                   jax.ShapeDtypeStruct((rows, n_tiles * PROJ_TN), BF16),
                   jax.ShapeDtypeStruct((rows, D_MODEL), BF16)],
        input_output_aliases={4 + k: k for k in range(len(prior))},
        compiler_params=pltpu.CompilerParams(
            dimension_semantics=("parallel", "arbitrary"), vmem_limit_bytes=VMEM_LIMIT),
        name="project",
    )(x, g, wkv, wmain, *prior)


def _proj_t_body(wq_ref, wv_ref, xn_ref, o_ref):
    xn = xn_ref[...]
    q_t = (lax.dot_general(wq_ref[...], xn, _NT, preferred_element_type=F32) * Q_SCALE).astype(BF16)
    v_t = lax.dot_general(wv_ref[...], xn, _NT, preferred_element_type=F32).astype(BF16)
    for p in range(o_ref.shape[0]):
        o_ref[p, :ATTN_WIDTH, :] = q_t[:, p * PAIR:(p + 1) * PAIR]
        o_ref[p, ATTN_WIDTH:, :] = v_t[:, p * PAIR:(p + 1) * PAIR]


def _project_t(wmain, wkv, xn, tm):
    rows = xn.shape[0]
    assert rows % tm == 0 and tm % PAIR == 0
    return pl.pallas_call(
        _proj_t_body,
        grid=(rows // tm,),
        in_specs=[pl.BlockSpec((ATTN_WIDTH, D_MODEL), lambda i: (COL_QA, 0), pipeline_mode=pl.Buffered(1)),
                  pl.BlockSpec((KV_WIDTH, D_MODEL), lambda i: (1, 0), pipeline_mode=pl.Buffered(1)),
                  pl.BlockSpec((tm, D_MODEL), lambda i: (i, 0))],
        out_specs=pl.BlockSpec((tm // PAIR, QV_ROWS, PAIR), lambda i: (i, 0, 0)),
        out_shape=jax.ShapeDtypeStruct((rows // PAIR, QV_ROWS, PAIR), BF16),
        compiler_params=pltpu.CompilerParams(
            dimension_semantics=("parallel",), vmem_limit_bytes=VMEM_LIMIT),
        name="project_t",
    )(wmain, wkv, xn)


def _attn_pair_setup(p, ka_ref, kb_ref, km_ref, vta_ref, vtb_ref, vtm_ref):
    ka, kb = ka_ref[...], kb_ref[...]
    lane = lax.broadcasted_iota(jnp.int32, (CHUNK, LANES), 1)
    k_ad = []
    for t in range(KV_WIDTH // LANES):
        a_tile = ka[:CHUNK, t * LANES:(t + 1) * LANES]
        d_swapped = pltpu.roll(kb[CHUNK:, t * LANES:(t + 1) * LANES], HEAD_DIM, axis=1)
        k_ad.append(jnp.where(lane < HEAD_DIM, a_tile, d_swapped).astype(BF16))
        k_ad.append(jnp.where(lane < HEAD_DIM, d_swapped, a_tile).astype(BF16))
    k_rest = jnp.concatenate([kb[:CHUNK], ka[CHUNK:], km_ref[...]], axis=0).astype(BF16)
    vta, vtb = vta_ref[...], vtb_ref[...]
    first_half = lax.broadcasted_iota(jnp.int32, vta.shape, 1) < CHUNK
    vt = jnp.concatenate([jnp.where(first_half, vta, vtb), jnp.where(first_half, vtb, vta), vtm_ref[...]],
                         axis=1)
    even = (lax.broadcasted_iota(jnp.int32, (CHUNK, GQA_GROUP * PAIR), 1) % PAIR) < CHUNK
    return k_ad, k_rest, vt, (even, p == 0)


def _attn_pair_probs(setup, h, qt_ref, sink_ref):
    k_ad, k_rest, _, (even, first) = setup
    rows = [(h * GQA_GROUP + g) * HEAD_DIM for g in range(GQA_GROUP)]
    qth = jnp.concatenate([qt_ref[r:r + HEAD_DIM, :] for r in rows], axis=1)
    even_b = even.astype(BF16)
    q_even, q_odd = qth * even_b, qth * (1 - even_b)
    q2 = jnp.concatenate([q_even, q_odd] if h % 2 == 0 else [q_odd, q_even], axis=0)
    s_ad = jnp.dot(k_ad[h], q2, preferred_element_type=F32)
    s_rest = jnp.dot(k_rest[:, h * HEAD_DIM:(h + 1) * HEAD_DIM], qth, preferred_element_type=F32)
    st = jnp.concatenate([
        jnp.where(first & even, NEG_INF, s_ad),
        s_rest[:CHUNK],
        jnp.where(first, NEG_INF, s_rest[CHUNK:2 * CHUNK]),
        s_rest[2 * CHUNK:]], axis=0)
    sink = sink_ref[h]
    m = jnp.maximum(jnp.max(st, axis=0, keepdims=True), sink)
    e = jnp.exp(st - m)
    even_f = even.astype(F32)
    e = jnp.concatenate([e[:CHUNK] * even_f, e[:CHUNK] * (1.0 - even_f), e[CHUNK:]], axis=0)
    return e.astype(BF16), jnp.exp(sink - m)


def _attn_pair_out(setup, h, probs):
    _, _, vt, _ = setup
    e, e_sink = probs
    ones = jnp.ones((ONES_ROWS, PAIR_KEYS), BF16)
    vth = jnp.concatenate([vt[h * HEAD_DIM:(h + 1) * HEAD_DIM, :], ones], axis=0)
    ot = jnp.dot(vth, e, preferred_element_type=F32)
    denom = ot[HEAD_DIM:HEAD_DIM + 1, :] + e_sink
    ot = ot[:HEAD_DIM, :] * (1.0 / denom)
    return jnp.concatenate([ot[:, g * PAIR:(g + 1) * PAIR] for g in range(GQA_GROUP)], axis=0).T


def _attend_probs(q, k, sink_ref, lq):
    probs = []
    for h in range(N_KV_HEADS):
        kh = k[:, h * HEAD_DIM:(h + 1) * HEAD_DIM]
        heads = [h * GQA_GROUP + g for g in range(GQA_GROUP)]
        qh = jnp.concatenate([q[:, n * HEAD_DIM:(n + 1) * HEAD_DIM] for n in heads], axis=0)
        s = lax.dot_general(qh, kh, _NT, preferred_element_type=F32)
        sink = jnp.concatenate(
            [jnp.broadcast_to(sink_ref[n:n + 1, 0:1], (lq, 1)) for n in heads], axis=0)
        m = jnp.maximum(jnp.max(s, axis=-1, keepdims=True), sink)
        e = jnp.exp(s - m)
        probs.append((e.astype(BF16), jnp.sum(e, axis=-1, keepdims=True) + jnp.exp(sink - m)))
    return probs


def _attend_values(probs, v, o_ref, lq):
    for h in range(N_KV_HEADS):
        vh = v[:, h * HEAD_DIM:(h + 1) * HEAD_DIM]
        e, denom = probs[h]
        oh = jnp.dot(e, vh, preferred_element_type=F32) * (1.0 / denom)
        o_ref[:, h * GQA_GROUP * HEAD_DIM:(h + 1) * GQA_GROUP * HEAD_DIM] = jnp.concatenate(
            [oh[g * lq:(g + 1) * lq, :] for g in range(GQA_GROUP)], axis=1).astype(o_ref.dtype)


def _attn_sample_body(q_ref, ck_ref, cv_ref, kn_ref, vn_ref, km_ref, vm_ref, sink_ref, o_ref):
    lq = q_ref.shape[0]
    k = jnp.concatenate([ck_ref[...], kn_ref[...], km_ref[...]], axis=0).astype(BF16)
    probs = _attend_probs(q_ref[...], k, sink_ref, lq)
    v = jnp.concatenate([cv_ref[...], vn_ref[...], vm_ref[...]], axis=0).astype(BF16)
    _attend_values(probs, v, o_ref, lq)


def _attend_sample(o16, o32, cache_k, cache_v, meta_block, sinks, n_streams, ds):
    cache_rows = cache_k.shape[1]
    new = lambda col: pl.BlockSpec((ds, KV_WIDTH), lambda s: (s, col))
    meta = lambda col: pl.BlockSpec((N_META, KV_WIDTH), lambda s: (meta_block, col))
    cache = pl.BlockSpec((None, cache_rows, KV_WIDTH), lambda s: (s, 0, 0))
    return pl.pallas_call(
        _attn_sample_body,
        grid=(n_streams,),
        in_specs=[
            pl.BlockSpec((ds, ATTN_WIDTH), lambda s: (s, COL_QA)),
            cache, cache, new(0), new(1), meta(0), meta(1),
            pl.BlockSpec((N_Q_HEADS, LANES), lambda s: (0, 0)),
        ],
        out_specs=pl.BlockSpec((ds, ATTN_WIDTH), lambda s: (s, 0)),
        out_shape=jax.ShapeDtypeStruct((n_streams * ds, ATTN_WIDTH), BF16),
        compiler_params=pltpu.CompilerParams(
            dimension_semantics=("parallel",), vmem_limit_bytes=VMEM_LIMIT),
        name="attend_sample",
    )(o16, cache_k, cache_v, o32, o32, o32, o32, sinks)


def _mlstm_body(q_ref, k_ref, v_ref, gt_ref, bias_ref, c0_ref, n0_ref, m0_ref,
                hn_ref, c_ref, n_ref, m_ref, *, blk):
    setup = _mlstm_setup(gt_ref, bias_ref, blk)
    scores = [_mlstm_scores(setup, h, q_ref, k_ref, m0_ref) for h in range(M_HEADS)]
    for h in range(M_HEADS):
        hv = _mlstm_out(setup, h, scores[h], q_ref, v_ref, c0_ref, n0_ref)
        hn_ref[:, h * M_V_DIM:(h + 1) * M_V_DIM] = hv.astype(hn_ref.dtype)
    for h in range(M_HEADS):
        _mlstm_state(setup, h, scores[h], k_ref, v_ref, (c0_ref, n0_ref, m0_ref), (c_ref, n_ref, m_ref),
                     blk, None)


def _mlstm_setup(gt_ref, bias_ref, blk):
    x = gt_ref[...] + bias_ref[...]
    lane = lax.broadcasted_iota(jnp.int32, x.shape, 1)
    log_f = jnp.minimum(x, 0.0) - jnp.log(1.0 + jnp.exp(-jnp.abs(x)))
    gates = jnp.where(lane < M_HEADS, x, log_f)
    row = lax.broadcasted_iota(jnp.int32, (blk, blk), 0)
    col = lax.broadcasted_iota(jnp.int32, (blk, blk), 1)
    causal = row >= col
    hi = gates.astype(BF16)
    rest = gates - hi.astype(F32)
    mid = rest.astype(BF16)
    lo = (rest - mid.astype(F32)).astype(BF16)
    parts = jnp.dot(causal.astype(BF16), jnp.concatenate([hi, mid, lo], axis=1), preferred_element_type=F32)
    csum = parts[:, :LANES] + parts[:, LANES:2 * LANES] + parts[:, 2 * LANES:]
    z = jnp.where(lane < M_HEADS, gates, csum)
    pad = (-blk) % LANES
    zsq = z if pad == 0 else jnp.concatenate([z, jnp.zeros((pad, LANES), F32)], axis=0)
    zt = zsq.T[:, :blk]
    return z, zt, causal


def _mlstm_scores(setup, h, q_ref, k_ref, m_ref):
    z, zt, causal = setup
    g_r = zt[h:h + 1, :] - zt[M_HEADS + h:M_HEADS + h + 1, :]
    m_prev = m_ref[h:h + 1, 0:1]
    d0 = jnp.where(causal, g_r, -jnp.inf)
    mm = jnp.maximum(m_prev, jnp.max(d0, axis=-1, keepdims=True))
    w = jnp.exp(d0 - mm)
    qh = q_ref[:, h * M_QK_DIM:(h + 1) * M_QK_DIM]
    kh = k_ref[:, h * M_QK_DIM:(h + 1) * M_QK_DIM]
    qk = lax.dot_general(qh, kh, _NT, preferred_element_type=F32)
    wqk = w * (qk * MLSTM_K_SCALE)
    return m_prev, mm, wqk.astype(BF16), jnp.sum(wqk, axis=-1, keepdims=True)


def _mlstm_finish(setup, h, scores, q_ref, k_ref, v_ref, c_ref, n_ref, m_ref, blk, live):
    hv = _mlstm_out(setup, h, scores, q_ref, v_ref, c_ref, n_ref)
    state = (c_ref, n_ref, m_ref)
    _mlstm_state(setup, h, scores, k_ref, v_ref, state, state, blk, live)
    return hv


def _mlstm_out(setup, h, scores, q_ref, v_ref, c_ref, n_ref):
    z, _, _ = setup
    m_prev, mm, wqk, den_intra = scores
    m_t = z[:, M_HEADS + h:M_HEADS + h + 1] + mm
    a = jnp.exp(m_prev - mm)
    qh = q_ref[:, h * M_QK_DIM:(h + 1) * M_QK_DIM]
    vh = v_ref[:, h * M_V_DIM:(h + 1) * M_V_DIM]
    num = a * jnp.dot(qh, c_ref[h].astype(BF16), preferred_element_type=F32) + jnp.dot(
        wqk, vh, preferred_element_type=F32)
    den = a * jnp.sum(qh.astype(F32) * n_ref[h:h + 1, :], axis=-1, keepdims=True) + den_intra
    r = 1.0 / jnp.maximum(jnp.abs(den), jnp.exp(-m_t))
    rms_h = r * jnp.sqrt(jnp.mean(num * num, axis=-1, keepdims=True))
    return num * (r * lax.rsqrt(rms_h * rms_h + EPS))


def _mlstm_state(setup, h, scores, k_ref, v_ref, state_in, state_out, blk, live):
    c_ref, n_ref, m_in = state_in
    c_out, n_out, m_out = state_out
    z, _, _ = setup
    m_prev, mm, _, _ = scores
    keep = (lambda new, old: new) if live is None else (lambda new, old: jnp.where(live, new, old))
    li_c = z[:, h:h + 1]
    b_c = z[:, M_HEADS + h:M_HEADS + h + 1]
    m_t = b_c + mm
    kh = k_ref[:, h * M_QK_DIM:(h + 1) * M_QK_DIM]
    vh = v_ref[:, h * M_V_DIM:(h + 1) * M_V_DIM]
    c_h = c_ref[h]
    n_h = n_ref[h:h + 1, :]

    m_new = m_t[blk - 1:blk, :]
    b_last = b_c[blk - 1:blk, :]
    ws = jnp.exp(b_last - b_c + li_c - m_new) * MLSTM_K_SCALE
    a_last = jnp.exp(b_last + m_prev - m_new)
    vs = (vh.astype(F32) * ws).astype(BF16)
    ktv = lax.dot_general(kh, vs, (((0,), (0,)), ((), ())), preferred_element_type=F32)
    c_out[h] = keep(a_last * c_h + ktv, c_h)
    n_out[h:h + 1, :] = keep(a_last * n_h + jnp.sum(kh.astype(F32) * ws, axis=0, keepdims=True), n_h)
    m_out[h:h + 1, :] = keep(jnp.broadcast_to(m_new, (1, LANES)), m_in[h:h + 1, :])


def _mlstm(o16, o32, bias, c0, n0, m0, *, blk, n_streams, first_block, col0):
    rb = lambda s: first_block + s
    qk_blk = 2 * (COL_QKM - col0)
    state = lambda *shape: pl.BlockSpec((None,) + shape, lambda s: (s,) + (0,) * len(shape))
    return pl.pallas_call(
        functools.partial(_mlstm_body, blk=blk),
        grid=(n_streams,),
        in_specs=[
            pl.BlockSpec((blk, M_QK_WIDTH), lambda s: (rb(s), qk_blk)),
            pl.BlockSpec((blk, M_QK_WIDTH), lambda s: (rb(s), qk_blk + 1)),
            pl.BlockSpec((blk, M_V_WIDTH), lambda s: (rb(s), COL_VM - col0)),
            pl.BlockSpec((blk, LANES), lambda s: (rb(s), GATE_BLOCK)),
            pl.BlockSpec((1, LANES), lambda s: (0, 0)),
            state(M_HEADS, M_QK_DIM, M_V_DIM), state(M_HEADS, M_QK_DIM), state(M_HEADS, LANES),
        ],
        out_specs=[
            pl.BlockSpec((blk, M_V_WIDTH), lambda s: (s, 0)),
            state(M_HEADS, M_QK_DIM, M_V_DIM), state(M_HEADS, M_QK_DIM), state(M_HEADS, LANES),
        ],
        out_shape=[
            jax.ShapeDtypeStruct((n_streams * blk, M_V_WIDTH), BF16),
            jax.ShapeDtypeStruct((n_streams, M_HEADS, M_QK_DIM, M_V_DIM), F32),
            jax.ShapeDtypeStruct((n_streams, M_HEADS, M_QK_DIM), F32),
            jax.ShapeDtypeStruct((n_streams, M_HEADS, LANES), F32),
        ],
        compiler_params=pltpu.CompilerParams(
            dimension_semantics=("parallel",), vmem_limit_bytes=VMEM_LIMIT),
        name="mlstm_blk%d" % blk,
    )(o16, o16, o16, o32, bias, c0, n0, m0)


def _merge_body(x_ref, oa_ref, hn_ref, za_ref, om_ref, zm_ref, ga_ref, gm_ref, gmh_ref,
                wpa_ref, wpm_ref, wout_ref, gf_ref, y_ref, wpa_out, wpm_out, wout_out, a_scr, m_scr, acc_scr):
    j = pl.program_id(0)

    @pl.when(j == 0)
    def _():
        a_scr[...] = (oa_ref[...].astype(F32) * za_ref[...].astype(F32)).astype(BF16)
        m_scr[...] = ((hn_ref[...].astype(F32) * gmh_ref[...]) * om_ref[...].astype(F32)
                      * zm_ref[...].astype(F32)).astype(BF16)
        acc_scr[...] = x_ref[...]

    wpa_out[...] = wpa_ref[...].astype(BF16)
    wpm_out[...] = wpm_ref[...].astype(BF16)
    wout_out[...] = wout_ref[...].astype(BF16)
    ya = jnp.dot(a_scr[...], wpa_out[...], preferred_element_type=F32)
    ym = jnp.dot(m_scr[...], wpm_out[...], preferred_element_type=F32)
    merged = ga_ref[...].astype(F32) * ya + gm_ref[...].astype(F32) * ym
    acc_scr[...] += jnp.dot(merged.astype(BF16), wout_out[...], preferred_element_type=F32)

    @pl.when(j == pl.num_programs(0) - 1)
    def _():
        xo = acc_scr[...]
        ms = jnp.mean(xo * xo, axis=-1, keepdims=True)
        y_ref[...] = (xo * lax.rsqrt(ms + EPS)) * gf_ref[...]


def _merge(x, o_att, hn, o16, g_mh, w_pa, w_pm, w_out, g_final, rows, col0):
    n_chunks = D_MODEL // MERGE_TN
    per_group = D_MODEL // MERGE_TN
    full = lambda col: pl.BlockSpec((rows, D_MODEL), lambda j: (0, col))
    gate = lambda col: pl.BlockSpec((rows, MERGE_TN), lambda j: (0, col * per_group + j))
    vec = pl.BlockSpec((1, D_MODEL), lambda j: (0, 0))
    w_cols = pl.BlockSpec((D_MODEL, MERGE_TN), lambda j: (0, j))
    w_rows = pl.BlockSpec((MERGE_TN, D_MODEL), lambda j: (j, 0))
    return pl.pallas_call(
        _merge_body,
        grid=(n_chunks,),
        in_specs=[full(0), full(0), full(0), full(COL_ZA - col0), full(COL_OM - col0), full(COL_ZM - col0),
                  gate(COL_GA - col0), gate(COL_GM - col0), vec, w_cols, w_cols, w_rows, vec],
        out_specs=[full(0), w_cols, w_cols, w_rows],
        out_shape=[jax.ShapeDtypeStruct((rows, D_MODEL), F32)]
        + [jax.ShapeDtypeStruct((D_MODEL, D_MODEL), BF16)] * 3,
        scratch_shapes=[pltpu.VMEM((rows, D_MODEL), BF16), pltpu.VMEM((rows, D_MODEL), BF16),
                        pltpu.VMEM((rows, D_MODEL), F32)],
        compiler_params=pltpu.CompilerParams(
            dimension_semantics=("arbitrary",), vmem_limit_bytes=VMEM_LIMIT),
        name="merge",
    )(x, o_att, hn, o16, o16, o16, o16, o16, g_mh, w_pa, w_pm, w_out, g_final)


def _mixer_body(qt_ref, ka_ref, kb_ref, km_ref, vta_ref, vtb_ref, vtm_ref, sink_ref,
                q_ref, k_ref, v_ref, gt_ref, bias_ref, c0_ref, n0_ref, m0_ref,
                x_ref, za_ref, om_ref, zm_ref, ga_ref, gm_ref, gmh_ref, wpa_ref, wpm_ref, wout_ref, gf_ref,
                y_ref, c_ref, n_ref, m_ref, a_scr, m_scr, *, n_blk):
    c = pl.program_id(0)

    @pl.when(c == 0)
    def _():
        c_ref[...] = c0_ref[...]
        n_ref[...] = n0_ref[...]
        m_ref[...] = m0_ref[...]
        a_scr[...] = jnp.zeros(a_scr.shape, a_scr.dtype)
        m_scr[...] = jnp.zeros(m_scr.shape, m_scr.dtype)

    wr = c % MIX_SLOTS
    rd = 1 - wr
    blk = jnp.minimum(c, n_blk - 1)
    live = c < n_blk
    width = GQA_GROUP * HEAD_DIM
    heads_per_piece = width // M_V_DIM
    a_in = a_scr[rd]
    m_in = m_scr[rd]

    def mlstm_piece(h):
        hc = slice(h * M_V_DIM, (h + 1) * M_V_DIM)
        hv = _mlstm_finish(cell, h, scores[h], q_ref, k_ref, v_ref, c_ref, n_ref, m_ref, PAIR, live)
        m_scr[wr, :, hc] = ((hv * gmh_ref[:, hc]) * om_ref[:, hc].astype(F32)
                            * zm_ref[:, hc].astype(F32)).astype(BF16)

    def attn_piece(h, probs):
        cols = slice(h * width, (h + 1) * width)
        oa = _attn_pair_out(attn, h, probs)
        a_scr[wr, :, cols] = (oa * za_ref[:, cols].astype(F32)).astype(BF16)

    attn = _attn_pair_setup(blk, ka_ref, kb_ref, km_ref, vta_ref, vtb_ref, vtm_ref)
    cell = _mlstm_setup(gt_ref, bias_ref, PAIR)
    scores = [_mlstm_scores(cell, h, q_ref, k_ref, m_ref) for h in range(M_HEADS)]
    probs = _attn_pair_probs(attn, 0, qt_ref, sink_ref)
    merged = []
    mlstm_heads = ((0, 1, 2), (3, 4, 5), (6, 7), ())
    for i in range(N_KV_HEADS):
        last = i + 1 == N_KV_HEADS
        cols = slice(i * width, (i + 1) * width)
        ya = jnp.dot(a_in, wpa_ref[:, cols], preferred_element_type=F32)
        for h in mlstm_heads[i][:2]:
            mlstm_piece(h)
        next_probs = None if last else _attn_pair_probs(attn, i + 1, qt_ref, sink_ref)
        ym = jnp.dot(m_in, wpm_ref[:, cols], preferred_element_type=F32)
        for h in mlstm_heads[i][2:]:
            mlstm_piece(h)
        merged.append((ga_ref[:, cols].astype(F32) * ya + gm_ref[:, cols].astype(F32) * ym).astype(BF16))
        if last:
            xo = x_ref[...] + jnp.dot(jnp.concatenate(merged, axis=1), wout_ref[...],
                                      preferred_element_type=F32)
        attn_piece(i, probs)
        probs = next_probs
    ms = jnp.mean(xo * xo, axis=-1, keepdims=True)
    y_ref[...] = (xo * lax.rsqrt(ms + EPS)) * gf_ref[...]


def _mixer(x, qvt, o32, o16, k_meta, vt_meta, sinks_t, bias, c0, n0, m0, g_mh, w_pa, w_pm, w_out, g_final,
           col0):
    seq = x.shape[0]
    assert seq % PAIR == 0
    n_blk = seq // PAIR
    cur = lambda c: jnp.minimum(c, n_blk - 1)
    prv = lambda c: jnp.maximum(cur(c) - 1, 0)
    mrg = lambda c: jnp.maximum(c - 1, 0)
    v_row = ATTN_WIDTH // KV_WIDTH
    qk_blk = 2 * (COL_QKM - col0)
    tile = lambda col: pl.BlockSpec((PAIR, D_MODEL), lambda c: (mrg(c), col))
    tile_cur = lambda col: pl.BlockSpec((PAIR, D_MODEL), lambda c: (cur(c), col))
    vec = pl.BlockSpec((1, D_MODEL), lambda c: (0, 0))
    weight = pl.BlockSpec((D_MODEL, D_MODEL), lambda c: (0, 0), pipeline_mode=pl.Buffered(1))
    state = lambda *blk: pl.BlockSpec((None,) + blk, lambda c: (0,) * (len(blk) + 1))
    return pl.pallas_call(
        functools.partial(_mixer_body, n_blk=n_blk),
        grid=(n_blk + 1,),
        in_specs=[
            pl.BlockSpec((None, ATTN_WIDTH, PAIR), lambda c: (cur(c), 0, 0)),
            pl.BlockSpec((PAIR, KV_WIDTH), lambda c: (prv(c), 0)),
            pl.BlockSpec((PAIR, KV_WIDTH), lambda c: (cur(c), 0)),
            pl.BlockSpec((N_META, KV_WIDTH), lambda c: (0, 0)),
            pl.BlockSpec((None, KV_WIDTH, PAIR), lambda c: (prv(c), v_row, 0)),
            pl.BlockSpec((None, KV_WIDTH, PAIR), lambda c: (cur(c), v_row, 0)),
            pl.BlockSpec((KV_WIDTH, N_META), lambda c: (0, 0)),
            pl.BlockSpec((N_KV_HEADS, 1, GQA_GROUP * PAIR), lambda c: (0, 0, 0)),
            pl.BlockSpec((PAIR, M_QK_WIDTH), lambda c: (cur(c), qk_blk)),
            pl.BlockSpec((PAIR, M_QK_WIDTH), lambda c: (cur(c), qk_blk + 1)),
            pl.BlockSpec((PAIR, M_V_WIDTH), lambda c: (cur(c), COL_VM - col0)),
            pl.BlockSpec((PAIR, LANES), lambda c: (cur(c), GATE_BLOCK)),
            pl.BlockSpec((1, LANES), lambda c: (0, 0)),
            state(M_HEADS, M_QK_DIM, M_V_DIM), state(M_HEADS, M_QK_DIM), state(M_HEADS, LANES),
            tile(0), tile_cur(COL_ZA - col0), tile_cur(COL_OM - col0), tile_cur(COL_ZM - col0),
            tile(COL_GA - col0), tile(COL_GM - col0), vec, weight, weight, weight, vec,
        ],
        out_specs=[
            tile(0),
            state(M_HEADS, M_QK_DIM, M_V_DIM), state(M_HEADS, M_QK_DIM), state(M_HEADS, LANES),
        ],
        out_shape=[
            jax.ShapeDtypeStruct((seq, D_MODEL), F32),
            jax.ShapeDtypeStruct((1, M_HEADS, M_QK_DIM, M_V_DIM), F32),
            jax.ShapeDtypeStruct((1, M_HEADS, M_QK_DIM), F32),
            jax.ShapeDtypeStruct((1, M_HEADS, LANES), F32),
        ],
        scratch_shapes=[pltpu.VMEM((MIX_SLOTS, PAIR, ATTN_WIDTH), BF16),
                        pltpu.VMEM((MIX_SLOTS, PAIR, M_V_WIDTH), BF16)],
        compiler_params=pltpu.CompilerParams(
            dimension_semantics=("arbitrary",), vmem_limit_bytes=VMEM_LIMIT),
        name="mixer",
    )(qvt, o32, o32, k_meta, qvt, qvt, vt_meta, sinks_t,
      o16, o16, o16, o32, bias, c0, n0, m0,
      x, o16, o16, o16, o16, o16, g_mh, w_pa, w_pm, w_out, g_final)


def kernel(x_prompt, x_sample, cache_k, cache_v, state_C, state_n, state_m, meta_tokens, g_norm, w_in,
           b_igate, b_fgate, attn_sinks, g_mhnorm, w_pa, w_pm, w_out, g_final):
    batch, seq, _ = x_prompt.shape
    db, ds, _ = x_sample.shape
    depth = w_in.shape[0]
    assert batch == 1 and depth == 1 and ds == N_META
    cache_rows = cache_k.shape[2]

    w_t = w_in[0].T
    g_in = g_norm[0].reshape(1, D_MODEL)
    bias = jnp.concatenate([b_igate[0], b_fgate[0], jnp.zeros((LANES - 2 * M_HEADS,), F32)]).reshape(1, LANES)
    sinks = jnp.broadcast_to(attn_sinks[0][:, None], (N_Q_HEADS, LANES))
    sinks_t = jnp.broadcast_to(attn_sinks[0].reshape(N_KV_HEADS, 1, GQA_GROUP, 1),
                               (N_KV_HEADS, 1, GQA_GROUP, PAIR)).reshape(N_KV_HEADS, 1, GQA_GROUP * PAIR)
    g_mh = g_mhnorm[0].reshape(1, M_V_WIDTH)
    g_fin = g_final.reshape(1, D_MODEL)

    xp = x_prompt.reshape(seq, D_MODEL)
    xs_rows = db * ds
    x_small = jnp.concatenate([x_sample.reshape(xs_rows, D_MODEL), meta_tokens.astype(F32)], axis=0)
    meta_block = xs_rows // N_META
    p_tm = min(seq, 1024)
    *first, wmain, wkv, s32, s16 = _project_first(xp, g_in, w_t, x_small, tm=p_tm, col0=COL_ZA)
    p32, p16, xn_p = first if seq == p_tm else _project(xp, g_in, wkv, wmain, tm=p_tm, col0=COL_ZA, prior=first)
    qvt = _project_t(wmain, wkv, xn_p, tm=p_tm)

    zeros = lambda *shape: jnp.zeros(shape, F32)
    _, c_meta, n_meta, m_meta = _mlstm(
        s16, s32, bias, zeros(1, M_HEADS, M_QK_DIM, M_V_DIM), zeros(1, M_HEADS, M_QK_DIM),
        zeros(1, M_HEADS, LANES), blk=N_META, n_streams=1, first_block=meta_block, col0=COL_QA)

    ck = cache_k[0].reshape(db, cache_rows, KV_WIDTH)
    cv = cache_v[0].reshape(db, cache_rows, KV_WIDTH)
    oa_s = _attend_sample(s16, s32, ck, cv, meta_block, sinks, db, ds)
    m0_s = jnp.broadcast_to(state_m[0][:, :, None], (db, M_HEADS, LANES))
    hn_s, c_s, n_s, m_s = _mlstm(s16, s32, bias, state_C[0], state_n[0], m0_s, blk=ds, n_streams=db,
                                 first_block=0, col0=COL_QA)
    y_s, wpa, wpm, wout = _merge(x_small, oa_s, hn_s, s16, g_mh, w_pa[0], w_pm[0], w_out[0], g_fin,
                                 rows=xs_rows, col0=COL_QA)

    k_meta = s32[xs_rows:, :KV_WIDTH]
    vt_meta = s32[xs_rows:, KV_WIDTH:2 * KV_WIDTH].T.astype(BF16)
    y_p, c_p, n_p, m_p = _mixer(xp, qvt, p32, p16, k_meta, vt_meta, sinks_t, bias, c_meta, n_meta, m_meta,
                                g_mh, wpa, wpm, wout, g_fin, col0=COL_ZA)

    kv_shape = (1, batch, cache_rows, N_KV_HEADS, HEAD_DIM)
    k_p = p32[seq - cache_rows:, :KV_WIDTH].reshape(kv_shape)
    v_p = p32[seq - cache_rows:, KV_WIDTH:2 * KV_WIDTH].reshape(kv_shape)
    new_k = s32[:xs_rows, :KV_WIDTH].reshape(db, ds, KV_WIDTH)
    new_v = s32[:xs_rows, KV_WIDTH:2 * KV_WIDTH].reshape(db, ds, KV_WIDTH)
    skv_shape = (1, db, cache_rows, N_KV_HEADS, HEAD_DIM)
    k_s = jnp.concatenate([ck, new_k], axis=1)[:, -cache_rows:].reshape(skv_shape)
    v_s = jnp.concatenate([cv, new_v], axis=1)[:, -cache_rows:].reshape(skv_shape)

    return (y_p.reshape(batch, seq, D_MODEL), y_s.reshape(db, ds, D_MODEL),
            k_p, v_p, c_p[None], n_p[None], m_p[None, :, :, 0],
            k_s, v_s, c_s[None], n_s[None], m_s[None, :, :, 0])
```

```python
import functools
import math

import jax
import jax.numpy as jnp
import numpy as np
from jax import lax
from jax.experimental import pallas as pl
from jax.experimental.pallas import tpu as pltpu

F32 = jnp.float32
BF16 = jnp.bfloat16

D_MODEL = 2048
CHUNK = 64
N_META = 16
HEAD_DIM = 64
N_Q_HEADS = 32
N_KV_HEADS = 4
GQA_GROUP = N_Q_HEADS // N_KV_HEADS
ATTN_WIDTH = N_Q_HEADS * HEAD_DIM
KV_WIDTH = N_KV_HEADS * HEAD_DIM
M_HEADS = 8
M_QK_DIM = 128
M_V_DIM = 256
M_QK_WIDTH = M_HEADS * M_QK_DIM
M_V_WIDTH = M_HEADS * M_V_DIM
EPS = 1e-6
NEG_INF = -1e30
COL_SIZES = (ATTN_WIDTH, KV_WIDTH, KV_WIDTH, ATTN_WIDTH, M_QK_WIDTH, M_QK_WIDTH, M_V_WIDTH, M_V_WIDTH,
             M_HEADS, M_HEADS, M_V_WIDTH, D_MODEL, D_MODEL)
COL_STARTS = tuple(int(v) for v in np.cumsum((0,) + COL_SIZES[:-1]))
(SRC_QA, SRC_KA, SRC_VA, SRC_ZA, SRC_QM, SRC_KM, SRC_VM, SRC_OM, SRC_IG, SRC_FG, SRC_ZM, SRC_GA,
 SRC_GM) = COL_STARTS

LANES = 128
VMEM_LIMIT = 56 * 1024 * 1024

KV_OUT = 2 * KV_WIDTH + LANES
GATE_BLOCK = 2 * KV_WIDTH // LANES
MAIN_OUT = 8 * D_MODEL
COL_QA, COL_ZA, COL_QKM, COL_VM, COL_OM, COL_ZM, COL_GA, COL_GM = range(8)
PROJ_TN = 1024
ROW_ALIGN = 16
MLSTM_K_SCALE = M_QK_DIM ** -0.5
Q_SCALE = 1.0 / math.sqrt(HEAD_DIM)
QV_ROWS = ATTN_WIDTH + KV_WIDTH
PAIR = 2 * CHUNK
PAIR_KEYS = 2 * PAIR + N_META
ONES_ROWS = 16
MIX_SLOTS = 2
MERGE_TN = 256
NORM_SLABS = 4


def _sigmoid(x):
    return 0.5 * jnp.tanh(0.5 * x) + 0.5


def _silu(x):
    return x * _sigmoid(x)


_NT = (((1,), (1,)), ((), ()))


def _main_row_offsets():
    groups = [(SRC_QA, ATTN_WIDTH), (SRC_ZA, ATTN_WIDTH), (SRC_QM, M_QK_WIDTH), (SRC_KM, M_QK_WIDTH),
              (SRC_VM, M_V_WIDTH), (SRC_OM, M_V_WIDTH), (SRC_ZM, M_V_WIDTH), (SRC_GA, D_MODEL),
              (SRC_GM, D_MODEL)]
    offs = []
    for start, width in groups:
        assert start % ROW_ALIGN == 0 and width % PROJ_TN == 0
        offs += [(start + b * PROJ_TN) // ROW_ALIGN for b in range(width // PROJ_TN)]
    assert len(offs) * PROJ_TN == MAIN_OUT
    return np.asarray(offs, np.int32)


def _proj_body(*refs, tile0, from_f32, q_t=False):
    if from_f32:
        _, x_ref, g_ref, wkv_ref, wg_ref, w_ref, o32_ref, o16_ref, xn_ref, wmain_out, wkv_out = refs
    elif q_t:
        x_ref, g_ref, wkv_ref, w_ref, o32_ref, o16_ref, xn_ref, qt_ref, vt_ref = refs
    else:
        x_ref, g_ref, wkv_ref, w_ref, o32_ref, o16_ref, xn_ref = refs

    @pl.when(pl.program_id(1) == 0)
    def _():
        if from_f32:
            wkv_out[:2 * KV_WIDTH, :] = wkv_ref[...].astype(BF16)
            wkv_out[2 * KV_WIDTH:2 * KV_WIDTH + ROW_ALIGN, :] = wg_ref[...].astype(BF16)
            wkv_out[2 * KV_WIDTH + ROW_ALIGN:, :] = jnp.zeros((LANES - ROW_ALIGN, D_MODEL), BF16)
            wkv = wkv_out[...]
        else:
            wkv = wkv_ref[...]
        tm = x_ref.shape[0]
        slab = tm // NORM_SLABS if tm % (NORM_SLABS * ROW_ALIGN) == 0 else tm
        for r in range(0, tm, slab):
            x = x_ref[r:r + slab, :]
            ms = jnp.mean(x * x, axis=-1, keepdims=True)
            xn = ((x * lax.rsqrt(ms + EPS)) * g_ref[...]).astype(BF16)
            xn_ref[r:r + slab, :] = xn
            o32_ref[r:r + slab, :] = lax.dot_general(xn, wkv, _NT, preferred_element_type=F32)
        if q_t:
            v_t = lax.dot_general(wkv[KV_WIDTH:2 * KV_WIDTH, :], xn_ref[...], _NT,
                                  preferred_element_type=F32).astype(BF16)
            for p in range(vt_ref.shape[0]):
                vt_ref[p] = v_t[:, p * PAIR:(p + 1) * PAIR]

    group = (pl.program_id(1) + tile0) // (D_MODEL // PROJ_TN)
    is_sigmoid = (group == COL_OM) | (group == COL_GA) | (group == COL_GM)
    is_silu = (group == COL_ZA) | (group == COL_ZM)
    is_q = group == COL_QA

    def tile(act):
        w = w_ref[...].astype(BF16)
        if from_f32:
            wmain_out[...] = w
        acc = lax.dot_general(xn_ref[...], w, _NT, preferred_element_type=F32)
        o16_ref[...] = act(acc).astype(BF16)

    pl.when(is_sigmoid)(lambda: tile(_sigmoid))
    pl.when(is_silu)(lambda: tile(_silu))
    def q_tile():
        q = (lax.dot_general(w_ref[...], xn_ref[...], _NT, preferred_element_type=F32) * Q_SCALE).astype(BF16)
        for p in range(qt_ref.shape[0]):
            qt_ref[p] = q[:, p * PAIR:(p + 1) * PAIR]

    pl.when(is_q)(q_tile if q_t else (lambda: tile(lambda acc: acc * Q_SCALE)))
    pl.when(jnp.logical_not(is_sigmoid | is_silu | is_q))(lambda: tile(lambda acc: acc))


def _project_first(x, g, w_t):
    rows = x.shape[0]
    assert SRC_KA % (2 * KV_WIDTH) == 0 and SRC_VA == SRC_KA + KV_WIDTH
    assert SRC_IG % ROW_ALIGN == 0 and SRC_FG == SRC_IG + M_HEADS and 2 * M_HEADS == ROW_ALIGN
    n_tiles = MAIN_OUT // PROJ_TN
    return pl.pallas_call(
        functools.partial(_proj_body, tile0=0, from_f32=True),
        grid_spec=pltpu.PrefetchScalarGridSpec(
            num_scalar_prefetch=1,
            grid=(1, n_tiles),
            in_specs=[
                pl.BlockSpec((rows, D_MODEL), lambda i, j, off: (0, 0)),
                pl.BlockSpec((1, D_MODEL), lambda i, j, off: (0, 0)),
                pl.BlockSpec((2 * KV_WIDTH, D_MODEL), lambda i, j, off: (SRC_KA // (2 * KV_WIDTH), 0),
                             pipeline_mode=pl.Buffered(1)),
                pl.BlockSpec((ROW_ALIGN, D_MODEL), lambda i, j, off: (SRC_IG // ROW_ALIGN, 0),
                             pipeline_mode=pl.Buffered(1)),
                pl.BlockSpec((pl.Element(PROJ_TN), pl.Element(D_MODEL)),
                             lambda i, j, off: (off[j] * ROW_ALIGN, 0)),
            ],
            out_specs=[
                pl.BlockSpec((rows, KV_OUT), lambda i, j, off: (0, 0)),
                pl.BlockSpec((rows, PROJ_TN), lambda i, j, off: (0, j)),
                pl.BlockSpec((rows, D_MODEL), lambda i, j, off: (0, 0)),
                pl.BlockSpec((PROJ_TN, D_MODEL), lambda i, j, off: (j, 0)),
                pl.BlockSpec((KV_OUT, D_MODEL), lambda i, j, off: (0, 0)),
            ],
        ),
        out_shape=[jax.ShapeDtypeStruct((rows, KV_OUT), F32),
                   jax.ShapeDtypeStruct((rows, MAIN_OUT), BF16),
                   jax.ShapeDtypeStruct((rows, D_MODEL), BF16),
                   jax.ShapeDtypeStruct((MAIN_OUT, D_MODEL), BF16),
                   jax.ShapeDtypeStruct((KV_OUT, D_MODEL), BF16)],
        compiler_params=pltpu.CompilerParams(
            dimension_semantics=("arbitrary", "arbitrary"), vmem_limit_bytes=VMEM_LIMIT),
        name="project_first",
    )(jnp.asarray(_main_row_offsets()), x, g, w_t, w_t, w_t)


def _project(x, g, wkv, wmain, tm, col0):
    rows = x.shape[0]
    assert rows % tm == 0 and tm % PAIR == 0 and col0 == COL_ZA
    n_tiles = MAIN_OUT // PROJ_TN
    q_tiles = ATTN_WIDTH // PROJ_TN
    return pl.pallas_call(
        functools.partial(_proj_body, tile0=0, from_f32=False, q_t=True),
        grid=(rows // tm, n_tiles),
        in_specs=[
            pl.BlockSpec((tm, D_MODEL), lambda i, j: (i, 0)),
            pl.BlockSpec((1, D_MODEL), lambda i, j: (0, 0)),
            pl.BlockSpec((KV_OUT, D_MODEL), lambda i, j: (0, 0), pipeline_mode=pl.Buffered(1)),
            pl.BlockSpec((PROJ_TN, D_MODEL), lambda i, j: (j, 0)),
        ],
        out_specs=[
            pl.BlockSpec((tm, KV_OUT), lambda i, j: (i, 0)),
            pl.BlockSpec((tm, PROJ_TN), lambda i, j: (i, jnp.maximum(j - q_tiles, 0))),
            pl.BlockSpec((tm, D_MODEL), lambda i, j: (i, 0)),
            pl.BlockSpec((tm // PAIR, PROJ_TN, PAIR), lambda i, j: (i, jnp.minimum(j, q_tiles - 1), 0)),
            pl.BlockSpec((tm // PAIR, KV_WIDTH, PAIR), lambda i, j: (i, 0, 0)),
        ],
        out_shape=[jax.ShapeDtypeStruct((rows, KV_OUT), F32),
                   jax.ShapeDtypeStruct((rows, (n_tiles - q_tiles) * PROJ_TN), BF16),
                   jax.ShapeDtypeStruct((rows, D_MODEL), BF16),
                   jax.ShapeDtypeStruct((rows // PAIR, ATTN_WIDTH, PAIR), BF16),
                   jax.ShapeDtypeStruct((rows // PAIR, KV_WIDTH, PAIR), BF16)],
        compiler_params=pltpu.CompilerParams(
            dimension_semantics=("parallel", "arbitrary"), vmem_limit_bytes=VMEM_LIMIT),
        name="project",
    )(x, g, wkv, wmain)


def _proj_t_body(wq_ref, wv_ref, xn_ref, o_ref):
    xn = xn_ref[...]
    q_t = (lax.dot_general(wq_ref[...], xn, _NT, preferred_element_type=F32) * Q_SCALE).astype(BF16)
    v_t = lax.dot_general(wv_ref[...], xn, _NT, preferred_element_type=F32).astype(BF16)
    for p in range(o_ref.shape[0]):
        o_ref[p, :ATTN_WIDTH, :] = q_t[:, p * PAIR:(p + 1) * PAIR]
        o_ref[p, ATTN_WIDTH:, :] = v_t[:, p * PAIR:(p + 1) * PAIR]


def _project_t(wmain, wkv, xn, tm):
    rows = xn.shape[0]
    assert rows % tm == 0 and tm % PAIR == 0
    return pl.pallas_call(
        _proj_t_body,
        grid=(rows // tm,),
        in_specs=[pl.BlockSpec((ATTN_WIDTH, D_MODEL), lambda i: (COL_QA, 0), pipeline_mode=pl.Buffered(1)),
                  pl.BlockSpec((KV_WIDTH, D_MODEL), lambda i: (1, 0), pipeline_mode=pl.Buffered(1)),
                  pl.BlockSpec((tm, D_MODEL), lambda i: (i, 0))],
        out_specs=pl.BlockSpec((tm // PAIR, QV_ROWS, PAIR), lambda i: (i, 0, 0)),
        out_shape=jax.ShapeDtypeStruct((rows // PAIR, QV_ROWS, PAIR), BF16),
        compiler_params=pltpu.CompilerParams(
            dimension_semantics=("parallel",), vmem_limit_bytes=VMEM_LIMIT),
        name="project_t",
    )(wmain, wkv, xn)


def _attn_pair_setup(p, ka_ref, kb_ref, km_ref, vta_ref, vtb_ref, vtm_ref):
    ka, kb = ka_ref[...], kb_ref[...]
    lane = lax.broadcasted_iota(jnp.int32, (CHUNK, LANES), 1)
    k_ad = []
    for t in range(KV_WIDTH // LANES):
        a_tile = ka[:CHUNK, t * LANES:(t + 1) * LANES]
        d_swapped = pltpu.roll(kb[CHUNK:, t * LANES:(t + 1) * LANES], HEAD_DIM, axis=1)
        k_ad.append(jnp.where(lane < HEAD_DIM, a_tile, d_swapped).astype(BF16))
        k_ad.append(jnp.where(lane < HEAD_DIM, d_swapped, a_tile).astype(BF16))
    k_rest = jnp.concatenate([kb[:CHUNK], ka[CHUNK:], km_ref[...]], axis=0).astype(BF16)
    vta, vtb = vta_ref[...], vtb_ref[...]
    first_half = lax.broadcasted_iota(jnp.int32, vta.shape, 1) < CHUNK
    vt = jnp.concatenate([jnp.where(first_half, vta, vtb), jnp.where(first_half, vtb, vta), vtm_ref[...]],
                         axis=1)
    even = (lax.broadcasted_iota(jnp.int32, (CHUNK, GQA_GROUP * PAIR), 1) % PAIR) < CHUNK
    return k_ad, k_rest, vt, (even, p == 0)


def _attn_pair_probs(setup, h, qt_ref, sink_ref):
    k_ad, k_rest, _, (even, first) = setup
    rows = [(h * GQA_GROUP + g) * HEAD_DIM for g in range(GQA_GROUP)]
    qth = jnp.concatenate([qt_ref[r:r + HEAD_DIM, :] for r in rows], axis=1)
    even_b = even.astype(BF16)
    q_even, q_odd = qth * even_b, qth * (1 - even_b)
    q2 = jnp.concatenate([q_even, q_odd] if h % 2 == 0 else [q_odd, q_even], axis=0)
    s_ad = jnp.dot(k_ad[h], q2, preferred_element_type=F32)
    s_rest = jnp.dot(k_rest[:, h * HEAD_DIM:(h + 1) * HEAD_DIM], qth, preferred_element_type=F32)
    st = jnp.concatenate([
        jnp.where(first & even, NEG_INF, s_ad),
        s_rest[:CHUNK],
        jnp.where(first, NEG_INF, s_rest[CHUNK:2 * CHUNK]),
        s_rest[2 * CHUNK:]], axis=0)
    sink = sink_ref[h]
    m = jnp.maximum(jnp.max(st, axis=0, keepdims=True), sink)
    e = jnp.exp(st - m)
    even_f = even.astype(F32)
    e = jnp.concatenate([e[:CHUNK] * even_f, e[:CHUNK] * (1.0 - even_f), e[CHUNK:]], axis=0)
    return e.astype(BF16), jnp.exp(sink - m)


def _attn_pair_out(setup, h, probs):
    _, _, vt, _ = setup
    e, e_sink = probs
    ones = jnp.ones((ONES_ROWS, PAIR_KEYS), BF16)
    vth = jnp.concatenate([vt[h * HEAD_DIM:(h + 1) * HEAD_DIM, :], ones], axis=0)
    ot = jnp.dot(vth, e, preferred_element_type=F32)
    denom = ot[HEAD_DIM:HEAD_DIM + 1, :] + e_sink
    ot = ot[:HEAD_DIM, :] * (1.0 / denom)
    return jnp.concatenate([ot[:, g * PAIR:(g + 1) * PAIR] for g in range(GQA_GROUP)], axis=0).T


def _attend_probs(q, k, sink_ref, lq):
    probs = []
    for h in range(N_KV_HEADS):
        kh = k[:, h * HEAD_DIM:(h + 1) * HEAD_DIM]
        heads = [h * GQA_GROUP + g for g in range(GQA_GROUP)]
        qh = jnp.concatenate([q[:, n * HEAD_DIM:(n + 1) * HEAD_DIM] for n in heads], axis=0)
        s = lax.dot_general(qh, kh, _NT, preferred_element_type=F32)
        sink = jnp.concatenate(
            [jnp.broadcast_to(sink_ref[n:n + 1, 0:1], (lq, 1)) for n in heads], axis=0)
        m = jnp.maximum(jnp.max(s, axis=-1, keepdims=True), sink)
        e = jnp.exp(s - m)
        probs.append((e.astype(BF16), jnp.sum(e, axis=-1, keepdims=True) + jnp.exp(sink - m)))
    return probs


def _attend_values(probs, v, o_ref, lq):
    for h in range(N_KV_HEADS):
        vh = v[:, h * HEAD_DIM:(h + 1) * HEAD_DIM]
        e, denom = probs[h]
        oh = jnp.dot(e, vh, preferred_element_type=F32) * (1.0 / denom)
        o_ref[:, h * GQA_GROUP * HEAD_DIM:(h + 1) * GQA_GROUP * HEAD_DIM] = jnp.concatenate(
            [oh[g * lq:(g + 1) * lq, :] for g in range(GQA_GROUP)], axis=1).astype(o_ref.dtype)


def _attn_sample_body(q_ref, ck_ref, cv_ref, kn_ref, vn_ref, km_ref, vm_ref, sink_ref, o_ref):
    lq = q_ref.shape[0]
    k = jnp.concatenate([ck_ref[...], kn_ref[...], km_ref[...]], axis=0).astype(BF16)
    probs = _attend_probs(q_ref[...], k, sink_ref, lq)
    v = jnp.concatenate([cv_ref[...], vn_ref[...], vm_ref[...]], axis=0).astype(BF16)
    _attend_values(probs, v, o_ref, lq)


def _attend_sample(o16, o32, cache_k, cache_v, meta_block, sinks, n_streams, ds):
    cache_rows = cache_k.shape[1]
    new = lambda col: pl.BlockSpec((ds, KV_WIDTH), lambda s: (s, col))
    meta = lambda col: pl.BlockSpec((N_META, KV_WIDTH), lambda s: (meta_block, col))
    cache = pl.BlockSpec((None, cache_rows, KV_WIDTH), lambda s: (s, 0, 0))
    return pl.pallas_call(
        _attn_sample_body,
        grid=(n_streams,),
        in_specs=[
            pl.BlockSpec((ds, ATTN_WIDTH), lambda s: (s, COL_QA)),
            cache, cache, new(0), new(1), meta(0), meta(1),
            pl.BlockSpec((N_Q_HEADS, LANES), lambda s: (0, 0)),
        ],
        out_specs=pl.BlockSpec((ds, ATTN_WIDTH), lambda s: (s, 0)),
        out_shape=jax.ShapeDtypeStruct((n_streams * ds, ATTN_WIDTH), BF16),
        compiler_params=pltpu.CompilerParams(
            dimension_semantics=("parallel",), vmem_limit_bytes=VMEM_LIMIT),
        name="attend_sample",
    )(o16, cache_k, cache_v, o32, o32, o32, o32, sinks)


def _mlstm_body(q_ref, k_ref, v_ref, gt_ref, bias_ref, c0_ref, n0_ref, m0_ref,
                hn_ref, c_ref, n_ref, m_ref, *, blk):
    setup = _mlstm_setup(gt_ref, bias_ref, blk)
    scores = [_mlstm_scores(setup, h, q_ref, k_ref, m0_ref) for h in range(M_HEADS)]
    for h in range(M_HEADS):
        hv = _mlstm_out(setup, h, scores[h], q_ref, v_ref, c0_ref, n0_ref)
        hn_ref[:, h * M_V_DIM:(h + 1) * M_V_DIM] = hv.astype(hn_ref.dtype)
    for h in range(M_HEADS):
        _mlstm_state(setup, h, scores[h], k_ref, v_ref, (c0_ref, n0_ref, m0_ref), (c_ref, n_ref, m_ref),
                     blk, None)


def _mlstm_setup(gt_ref, bias_ref, blk):
    x = gt_ref[...] + bias_ref[...]
    lane = lax.broadcasted_iota(jnp.int32, x.shape, 1)
    log_f = jnp.minimum(x, 0.0) - jnp.log(1.0 + jnp.exp(-jnp.abs(x)))
    gates = jnp.where(lane < M_HEADS, x, log_f)
    row = lax.broadcasted_iota(jnp.int32, (blk, blk), 0)
    col = lax.broadcasted_iota(jnp.int32, (blk, blk), 1)
    causal = row >= col
    hi = gates.astype(BF16)
    rest = gates - hi.astype(F32)
    mid = rest.astype(BF16)
    lo = (rest - mid.astype(F32)).astype(BF16)
    parts = jnp.dot(causal.astype(BF16), jnp.concatenate([hi, mid, lo], axis=1), preferred_element_type=F32)
    csum = parts[:, :LANES] + parts[:, LANES:2 * LANES] + parts[:, 2 * LANES:]
    z = jnp.where(lane < M_HEADS, gates, csum)
    pad = (-blk) % LANES
    zsq = z if pad == 0 else jnp.concatenate([z, jnp.zeros((pad, LANES), F32)], axis=0)
    zt = zsq.T[:, :blk]
    return z, zt, causal


def _mlstm_scores(setup, h, q_ref, k_ref, m_ref):
    z, zt, causal = setup
    g_r = zt[h:h + 1, :] - zt[M_HEADS + h:M_HEADS + h + 1, :]
    m_prev = m_ref[h:h + 1, 0:1]
    d0 = jnp.where(causal, g_r, -jnp.inf)
    mm = jnp.maximum(m_prev, jnp.max(d0, axis=-1, keepdims=True))
    w = jnp.exp(d0 - mm)
    qh = q_ref[:, h * M_QK_DIM:(h + 1) * M_QK_DIM]
    kh = k_ref[:, h * M_QK_DIM:(h + 1) * M_QK_DIM]
    qk = lax.dot_general(qh, kh, _NT, preferred_element_type=F32)
    wqk = w * (qk * MLSTM_K_SCALE)
    return m_prev, mm, wqk.astype(BF16), jnp.sum(wqk, axis=-1, keepdims=True)


def _mlstm_finish(setup, h, scores, q_ref, k_ref, v_ref, c_ref, n_ref, m_ref, blk, live):
    hv = _mlstm_out(setup, h, scores, q_ref, v_ref, c_ref, n_ref)
    state = (c_ref, n_ref, m_ref)
    _mlstm_state(setup, h, scores, k_ref, v_ref, state, state, blk, live)
    return hv


def _mlstm_out(setup, h, scores, q_ref, v_ref, c_ref, n_ref):
    z, _, _ = setup
    m_prev, mm, wqk, den_intra = scores
    m_t = z[:, M_HEADS + h:M_HEADS + h + 1] + mm
    a = jnp.exp(m_prev - mm)
    qh = q_ref[:, h * M_QK_DIM:(h + 1) * M_QK_DIM]
    vh = v_ref[:, h * M_V_DIM:(h + 1) * M_V_DIM]
    num = a * jnp.dot(qh, c_ref[h].astype(BF16), preferred_element_type=F32) + jnp.dot(
        wqk, vh, preferred_element_type=F32)
    den = a * jnp.sum(qh.astype(F32) * n_ref[h:h + 1, :], axis=-1, keepdims=True) + den_intra
    r = 1.0 / jnp.maximum(jnp.abs(den), jnp.exp(-m_t))
    rms_h = r * jnp.sqrt(jnp.mean(num * num, axis=-1, keepdims=True))
    return num * (r * lax.rsqrt(rms_h * rms_h + EPS))


def _mlstm_state(setup, h, scores, k_ref, v_ref, state_in, state_out, blk, live):
    c_ref, n_ref, m_in = state_in
    c_out, n_out, m_out = state_out
    z, _, _ = setup
    m_prev, mm, _, _ = scores
    keep = (lambda new, old: new) if live is None else (lambda new, old: jnp.where(live, new, old))
    li_c = z[:, h:h + 1]
    b_c = z[:, M_HEADS + h:M_HEADS + h + 1]
    m_t = b_c + mm
    kh = k_ref[:, h * M_QK_DIM:(h + 1) * M_QK_DIM]
    vh = v_ref[:, h * M_V_DIM:(h + 1) * M_V_DIM]
    c_h = c_ref[h]
    n_h = n_ref[h:h + 1, :]

    m_new = m_t[blk - 1:blk, :]
    b_last = b_c[blk - 1:blk, :]
    ws = jnp.exp(b_last - b_c + li_c - m_new) * MLSTM_K_SCALE
    a_last = jnp.exp(b_last + m_prev - m_new)
    vs = (vh.astype(F32) * ws).astype(BF16)
    ktv = lax.dot_general(kh, vs, (((0,), (0,)), ((), ())), preferred_element_type=F32)
    c_out[h] = keep(a_last * c_h + ktv, c_h)
    n_out[h:h + 1, :] = keep(a_last * n_h + jnp.sum(kh.astype(F32) * ws, axis=0, keepdims=True), n_h)
    m_out[h:h + 1, :] = keep(jnp.broadcast_to(m_new, (1, LANES)), m_in[h:h + 1, :])


def _mlstm(o16, o32, bias, c0, n0, m0, *, blk, n_streams, first_block, col0):
    rb = lambda s: first_block + s
    qk_blk = 2 * (COL_QKM - col0)
    state = lambda *shape: pl.BlockSpec((None,) + shape, lambda s: (s,) + (0,) * len(shape))
    return pl.pallas_call(
        functools.partial(_mlstm_body, blk=blk),
        grid=(n_streams,),
        in_specs=[
            pl.BlockSpec((blk, M_QK_WIDTH), lambda s: (rb(s), qk_blk)),
            pl.BlockSpec((blk, M_QK_WIDTH), lambda s: (rb(s), qk_blk + 1)),
            pl.BlockSpec((blk, M_V_WIDTH), lambda s: (rb(s), COL_VM - col0)),
            pl.BlockSpec((blk, LANES), lambda s: (rb(s), GATE_BLOCK)),
            pl.BlockSpec((1, LANES), lambda s: (0, 0)),
            state(M_HEADS, M_QK_DIM, M_V_DIM), state(M_HEADS, M_QK_DIM), state(M_HEADS, LANES),
        ],
        out_specs=[
            pl.BlockSpec((blk, M_V_WIDTH), lambda s: (s, 0)),
            state(M_HEADS, M_QK_DIM, M_V_DIM), state(M_HEADS, M_QK_DIM), state(M_HEADS, LANES),
        ],
        out_shape=[
            jax.ShapeDtypeStruct((n_streams * blk, M_V_WIDTH), BF16),
            jax.ShapeDtypeStruct((n_streams, M_HEADS, M_QK_DIM, M_V_DIM), F32),
            jax.ShapeDtypeStruct((n_streams, M_HEADS, M_QK_DIM), F32),
            jax.ShapeDtypeStruct((n_streams, M_HEADS, LANES), F32),
        ],
        compiler_params=pltpu.CompilerParams(
            dimension_semantics=("parallel",), vmem_limit_bytes=VMEM_LIMIT),
        name="mlstm_blk%d" % blk,
    )(o16, o16, o16, o32, bias, c0, n0, m0)


def _merge_body(x_ref, oa_ref, hn_ref, za_ref, om_ref, zm_ref, ga_ref, gm_ref, gmh_ref,
                wpa_ref, wpm_ref, wout_ref, gf_ref, y_ref, wpa_out, wpm_out, wout_out, a_scr, m_scr, acc_scr):
    j = pl.program_id(0)

    @pl.when(j == 0)
    def _():
        a_scr[...] = (oa_ref[...].astype(F32) * za_ref[...].astype(F32)).astype(BF16)
        m_scr[...] = ((hn_ref[...].astype(F32) * gmh_ref[...]) * om_ref[...].astype(F32)
                      * zm_ref[...].astype(F32)).astype(BF16)
        acc_scr[...] = x_ref[...]

    wpa_out[...] = wpa_ref[...].astype(BF16)
    wpm_out[...] = wpm_ref[...].astype(BF16)
    wout_out[...] = wout_ref[...].astype(BF16)
    ya = jnp.dot(a_scr[...], wpa_out[...], preferred_element_type=F32)
    ym = jnp.dot(m_scr[...], wpm_out[...], preferred_element_type=F32)
    merged = ga_ref[...].astype(F32) * ya + gm_ref[...].astype(F32) * ym
    acc_scr[...] += jnp.dot(merged.astype(BF16), wout_out[...], preferred_element_type=F32)

    @pl.when(j == pl.num_programs(0) - 1)
    def _():
        xo = acc_scr[...]
        ms = jnp.mean(xo * xo, axis=-1, keepdims=True)
        y_ref[...] = (xo * lax.rsqrt(ms + EPS)) * gf_ref[...]


def _merge(x, o_att, hn, o16, g_mh, w_pa, w_pm, w_out, g_final, rows, col0):
    n_chunks = D_MODEL // MERGE_TN
    per_group = D_MODEL // MERGE_TN
    full = lambda col: pl.BlockSpec((rows, D_MODEL), lambda j: (0, col))
    gate = lambda col: pl.BlockSpec((rows, MERGE_TN), lambda j: (0, col * per_group + j))
    vec = pl.BlockSpec((1, D_MODEL), lambda j: (0, 0))
    w_cols = pl.BlockSpec((D_MODEL, MERGE_TN), lambda j: (0, j))
    w_rows = pl.BlockSpec((MERGE_TN, D_MODEL), lambda j: (j, 0))
    return pl.pallas_call(
        _merge_body,
        grid=(n_chunks,),
        in_specs=[full(0), full(0), full(0), full(COL_ZA - col0), full(COL_OM - col0), full(COL_ZM - col0),
                  gate(COL_GA - col0), gate(COL_GM - col0), vec, w_cols, w_cols, w_rows, vec],
        out_specs=[full(0), w_cols, w_cols, w_rows],
        out_shape=[jax.ShapeDtypeStruct((rows, D_MODEL), F32)]
        + [jax.ShapeDtypeStruct((D_MODEL, D_MODEL), BF16)] * 3,
        scratch_shapes=[pltpu.VMEM((rows, D_MODEL), BF16), pltpu.VMEM((rows, D_MODEL), BF16),
                        pltpu.VMEM((rows, D_MODEL), F32)],
        compiler_params=pltpu.CompilerParams(
            dimension_semantics=("arbitrary",), vmem_limit_bytes=VMEM_LIMIT),
        name="merge",
    )(x, o_att, hn, o16, o16, o16, o16, o16, g_mh, w_pa, w_pm, w_out, g_final)


def _mixer_body(qt_ref, ka_ref, kb_ref, km_ref, vta_ref, vtb_ref, vtm_ref, sink_ref,
                q_ref, k_ref, v_ref, gt_ref, bias_ref, c0_ref, n0_ref, m0_ref,
                x_ref, za_ref, om_ref, zm_ref, ga_ref, gm_ref, gmh_ref, wpa_ref, wpm_ref, wout_ref, gf_ref,
                y_ref, c_ref, n_ref, m_ref, a_scr, m_scr, *, n_blk):
    c = pl.program_id(0)

    @pl.when(c == 0)
    def _():
        c_ref[...] = c0_ref[...]
        n_ref[...] = n0_ref[...]
        m_ref[...] = m0_ref[...]
        a_scr[...] = jnp.zeros(a_scr.shape, a_scr.dtype)
        m_scr[...] = jnp.zeros(m_scr.shape, m_scr.dtype)

    wr = c % MIX_SLOTS
    rd = 1 - wr
    blk = jnp.minimum(c, n_blk - 1)
    live = c < n_blk
    width = GQA_GROUP * HEAD_DIM
    heads_per_piece = width // M_V_DIM
    a_in = a_scr[rd]
    m_in = m_scr[rd]

    def mlstm_piece(h):
        hc = slice(h * M_V_DIM, (h + 1) * M_V_DIM)
        hv = _mlstm_finish(cell, h, scores[h], q_ref, k_ref, v_ref, c_ref, n_ref, m_ref, PAIR, live)
        m_scr[wr, :, hc] = ((hv * gmh_ref[:, hc]) * om_ref[:, hc].astype(F32)
                            * zm_ref[:, hc].astype(F32)).astype(BF16)

    def attn_piece(h, probs):
        cols = slice(h * width, (h + 1) * width)
        oa = _attn_pair_out(attn, h, probs)
        a_scr[wr, :, cols] = (oa * za_ref[:, cols].astype(F32)).astype(BF16)

    attn = _attn_pair_setup(blk, ka_ref, kb_ref, km_ref, vta_ref, vtb_ref, vtm_ref)
    cell = _mlstm_setup(gt_ref, bias_ref, PAIR)
    scores = [_mlstm_scores(cell, h, q_ref, k_ref, m_ref) for h in range(M_HEADS)]
    probs = _attn_pair_probs(attn, 0, qt_ref, sink_ref)
    merged = []
    mlstm_heads = ((0, 1, 2), (3, 4, 5), (6, 7), ())
    for i in range(N_KV_HEADS):
        last = i + 1 == N_KV_HEADS
        cols = slice(i * width, (i + 1) * width)
        ya = jnp.dot(a_in, wpa_ref[:, cols], preferred_element_type=F32)
        for h in mlstm_heads[i][:2]:
            mlstm_piece(h)
        next_probs = None if last else _attn_pair_probs(attn, i + 1, qt_ref, sink_ref)
        ym = jnp.dot(m_in, wpm_ref[:, cols], preferred_element_type=F32)
        for h in mlstm_heads[i][2:]:
            mlstm_piece(h)
        merged.append((ga_ref[:, cols].astype(F32) * ya + gm_ref[:, cols].astype(F32) * ym).astype(BF16))
        if last:
            xo = x_ref[...] + jnp.dot(jnp.concatenate(merged, axis=1), wout_ref[...],
                                      preferred_element_type=F32)
        attn_piece(i, probs)
        probs = next_probs
    ms = jnp.mean(xo * xo, axis=-1, keepdims=True)
    y_ref[...] = (xo * lax.rsqrt(ms + EPS)) * gf_ref[...]


def _mixer(x, qt, vt, o32, o16, k_meta, vt_meta, sinks_t, bias, c0, n0, m0, g_mh, w_pa, w_pm, w_out, g_final,
           col0):
    seq = x.shape[0]
    assert seq % PAIR == 0
    n_blk = seq // PAIR
    cur = lambda c: jnp.minimum(c, n_blk - 1)
    prv = lambda c: jnp.maximum(cur(c) - 1, 0)
    mrg = lambda c: jnp.maximum(c - 1, 0)
    v_row = ATTN_WIDTH // KV_WIDTH
    qk_blk = 2 * (COL_QKM - col0)
    tile = lambda col: pl.BlockSpec((PAIR, D_MODEL), lambda c: (mrg(c), col))
    tile_cur = lambda col: pl.BlockSpec((PAIR, D_MODEL), lambda c: (cur(c), col))
    vec = pl.BlockSpec((1, D_MODEL), lambda c: (0, 0))
    weight = pl.BlockSpec((D_MODEL, D_MODEL), lambda c: (0, 0), pipeline_mode=pl.Buffered(1))
    state = lambda *blk: pl.BlockSpec((None,) + blk, lambda c: (0,) * (len(blk) + 1))
    return pl.pallas_call(
        functools.partial(_mixer_body, n_blk=n_blk),
        grid=(n_blk + 1,),
        in_specs=[
            pl.BlockSpec((None, ATTN_WIDTH, PAIR), lambda c: (cur(c), 0, 0)),
            pl.BlockSpec((PAIR, KV_WIDTH), lambda c: (prv(c), 0)),
            pl.BlockSpec((PAIR, KV_WIDTH), lambda c: (cur(c), 0)),
            pl.BlockSpec((N_META, KV_WIDTH), lambda c: (0, 0)),
            pl.BlockSpec((None, KV_WIDTH, PAIR), lambda c: (prv(c), 0, 0)),
            pl.BlockSpec((None, KV_WIDTH, PAIR), lambda c: (cur(c), 0, 0)),
            pl.BlockSpec((KV_WIDTH, N_META), lambda c: (0, 0)),
            pl.BlockSpec((N_KV_HEADS, 1, GQA_GROUP * PAIR), lambda c: (0, 0, 0)),
            pl.BlockSpec((PAIR, M_QK_WIDTH), lambda c: (cur(c), qk_blk)),
            pl.BlockSpec((PAIR, M_QK_WIDTH), lambda c: (cur(c), qk_blk + 1)),
            pl.BlockSpec((PAIR, M_V_WIDTH), lambda c: (cur(c), COL_VM - col0)),
            pl.BlockSpec((PAIR, LANES), lambda c: (cur(c), GATE_BLOCK)),
            pl.BlockSpec((1, LANES), lambda c: (0, 0)),
            state(M_HEADS, M_QK_DIM, M_V_DIM), state(M_HEADS, M_QK_DIM), state(M_HEADS, LANES),
            tile(0), tile_cur(COL_ZA - col0), tile_cur(COL_OM - col0), tile_cur(COL_ZM - col0),
            tile(COL_GA - col0), tile(COL_GM - col0), vec, weight, weight, weight, vec,
        ],
        out_specs=[
            tile(0),
            state(M_HEADS, M_QK_DIM, M_V_DIM), state(M_HEADS, M_QK_DIM), state(M_HEADS, LANES),
        ],
        out_shape=[
            jax.ShapeDtypeStruct((seq, D_MODEL), F32),
            jax.ShapeDtypeStruct((1, M_HEADS, M_QK_DIM, M_V_DIM), F32),
            jax.ShapeDtypeStruct((1, M_HEADS, M_QK_DIM), F32),
            jax.ShapeDtypeStruct((1, M_HEADS, LANES), F32),
        ],
        scratch_shapes=[pltpu.VMEM((MIX_SLOTS, PAIR, ATTN_WIDTH), BF16),
                        pltpu.VMEM((MIX_SLOTS, PAIR, M_V_WIDTH), BF16)],
        compiler_params=pltpu.CompilerParams(
            dimension_semantics=("arbitrary",), vmem_limit_bytes=VMEM_LIMIT),
        name="mixer",
    )(qt, o32, o32, k_meta, vt, vt, vt_meta, sinks_t,
      o16, o16, o16, o32, bias, c0, n0, m0,
      x, o16, o16, o16, o16, o16, g_mh, w_pa, w_pm, w_out, g_final)


def kernel(x_prompt, x_sample, cache_k, cache_v, state_C, state_n, state_m, meta_tokens, g_norm, w_in,
           b_igate, b_fgate, attn_sinks, g_mhnorm, w_pa, w_pm, w_out, g_final):
    batch, seq, _ = x_prompt.shape
    db, ds, _ = x_sample.shape
    depth = w_in.shape[0]
    assert batch == 1 and depth == 1 and ds == N_META
    cache_rows = cache_k.shape[2]

    w_t = w_in[0].T
    g_in = g_norm[0].reshape(1, D_MODEL)
    bias = jnp.concatenate([b_igate[0], b_fgate[0], jnp.zeros((LANES - 2 * M_HEADS,), F32)]).reshape(1, LANES)
    sinks = jnp.broadcast_to(attn_sinks[0][:, None], (N_Q_HEADS, LANES))
    sinks_t = jnp.broadcast_to(attn_sinks[0].reshape(N_KV_HEADS, 1, GQA_GROUP, 1),
                               (N_KV_HEADS, 1, GQA_GROUP, PAIR)).reshape(N_KV_HEADS, 1, GQA_GROUP * PAIR)
    g_mh = g_mhnorm[0].reshape(1, M_V_WIDTH)
    g_fin = g_final.reshape(1, D_MODEL)

    xp = x_prompt.reshape(seq, D_MODEL)
    xs_rows = db * ds
    x_small = jnp.concatenate([x_sample.reshape(xs_rows, D_MODEL), meta_tokens.astype(F32)], axis=0)
    meta_block = xs_rows // N_META
    p_tm = min(seq, 1024)
    s32, s16, _, wmain, wkv = _project_first(x_small, g_in, w_t)
    p32, p16, _, qt, vt = _project(xp, g_in, wkv, wmain, tm=p_tm, col0=COL_ZA)

    zeros = lambda *shape: jnp.zeros(shape, F32)
    _, c_meta, n_meta, m_meta = _mlstm(
        s16, s32, bias, zeros(1, M_HEADS, M_QK_DIM, M_V_DIM), zeros(1, M_HEADS, M_QK_DIM),
        zeros(1, M_HEADS, LANES), blk=N_META, n_streams=1, first_block=meta_block, col0=COL_QA)

    ck = cache_k[0].reshape(db, cache_rows, KV_WIDTH)
    cv = cache_v[0].reshape(db, cache_rows, KV_WIDTH)
    oa_s = _attend_sample(s16, s32, ck, cv, meta_block, sinks, db, ds)
    m0_s = jnp.broadcast_to(state_m[0][:, :, None], (db, M_HEADS, LANES))
    hn_s, c_s, n_s, m_s = _mlstm(s16, s32, bias, state_C[0], state_n[0], m0_s, blk=ds, n_streams=db,
                                 first_block=0, col0=COL_QA)
    y_s, wpa, wpm, wout = _merge(x_small, oa_s, hn_s, s16, g_mh, w_pa[0], w_pm[0], w_out[0], g_fin,
                                 rows=xs_rows, col0=COL_QA)

    k_meta = s32[xs_rows:, :KV_WIDTH]
    vt_meta = s32[xs_rows:, KV_WIDTH:2 * KV_WIDTH].T.astype(BF16)
    y_p, c_p, n_p, m_p = _mixer(xp, qt, vt, p32, p16, k_meta, vt_meta, sinks_t, bias, c_meta, n_meta, m_meta,
                                g_mh, wpa, wpm, wout, g_fin, col0=COL_ZA)

    kv_shape = (1, batch, cache_rows, N_KV_HEADS, HEAD_DIM)
    k_p = p32[seq - cache_rows:, :KV_WIDTH].reshape(kv_shape)
    v_p = p32[seq - cache_rows:, KV_WIDTH:2 * KV_WIDTH].reshape(kv_shape)
    new_k = s32[:xs_rows, :KV_WIDTH].reshape(db, ds, KV_WIDTH)
    new_v = s32[:xs_rows, KV_WIDTH:2 * KV_WIDTH].reshape(db, ds, KV_WIDTH)
    skv_shape = (1, db, cache_rows, N_KV_HEADS, HEAD_DIM)
    k_s = jnp.concatenate([ck, new_k], axis=1)[:, -cache_rows:].reshape(skv_shape)
    v_s = jnp.concatenate([cv, new_v], axis=1)[:, -cache_rows:].reshape(skv_shape)

    return (y_p.reshape(batch, seq, D_MODEL), y_s.reshape(db, ds, D_MODEL),
            k_p, v_p, c_p[None], n_p[None], m_p[None, :, :, 0],
            k_s, v_s, c_s[None], n_s[None], m_s[None, :, :, 0])
```

```python
import functools
import math

import jax
import jax.numpy as jnp
import numpy as np
from jax import lax
from jax.experimental import pallas as pl
from jax.experimental.pallas import tpu as pltpu

F32 = jnp.float32
BF16 = jnp.bfloat16

D_MODEL = 2048
CHUNK = 64
N_META = 16
HEAD_DIM = 64
N_Q_HEADS = 32
N_KV_HEADS = 4
GQA_GROUP = N_Q_HEADS // N_KV_HEADS
ATTN_WIDTH = N_Q_HEADS * HEAD_DIM
KV_WIDTH = N_KV_HEADS * HEAD_DIM
M_HEADS = 8
M_QK_DIM = 128
M_V_DIM = 256
M_QK_WIDTH = M_HEADS * M_QK_DIM
M_V_WIDTH = M_HEADS * M_V_DIM
EPS = 1e-6
NEG_INF = -1e30
COL_SIZES = (ATTN_WIDTH, KV_WIDTH, KV_WIDTH, ATTN_WIDTH, M_QK_WIDTH, M_QK_WIDTH, M_V_WIDTH, M_V_WIDTH,
             M_HEADS, M_HEADS, M_V_WIDTH, D_MODEL, D_MODEL)
COL_STARTS = tuple(int(v) for v in np.cumsum((0,) + COL_SIZES[:-1]))
(SRC_QA, SRC_KA, SRC_VA, SRC_ZA, SRC_QM, SRC_KM, SRC_VM, SRC_OM, SRC_IG, SRC_FG, SRC_ZM, SRC_GA,
 SRC_GM) = COL_STARTS

LANES = 128
VMEM_LIMIT = 56 * 1024 * 1024

KV_OUT = 2 * KV_WIDTH + LANES
GATE_BLOCK = 2 * KV_WIDTH // LANES
MAIN_OUT = 8 * D_MODEL
COL_QA, COL_ZA, COL_QKM, COL_VM, COL_OM, COL_ZM, COL_GA, COL_GM = range(8)
PROJ_TN = 1024
ROW_ALIGN = 16
MLSTM_K_SCALE = M_QK_DIM ** -0.5
Q_SCALE = 1.0 / math.sqrt(HEAD_DIM)
QV_ROWS = ATTN_WIDTH + KV_WIDTH
PAIR = 2 * CHUNK
PAIR_KEYS = 2 * PAIR + N_META
ONES_ROWS = 16
MIX_SLOTS = 2
MERGE_TN = 256
NORM_SLABS = 4


def _sigmoid(x):
    return 0.5 * jnp.tanh(0.5 * x) + 0.5


def _silu(x):
    return x * _sigmoid(x)


_NT = (((1,), (1,)), ((), ()))


def _main_row_offsets():
    groups = [(SRC_QA, ATTN_WIDTH), (SRC_ZA, ATTN_WIDTH), (SRC_QM, M_QK_WIDTH), (SRC_KM, M_QK_WIDTH),
              (SRC_VM, M_V_WIDTH), (SRC_OM, M_V_WIDTH), (SRC_ZM, M_V_WIDTH), (SRC_GA, D_MODEL),
              (SRC_GM, D_MODEL)]
    offs = []
    for start, width in groups:
        assert start % ROW_ALIGN == 0 and width % PROJ_TN == 0
        offs += [(start + b * PROJ_TN) // ROW_ALIGN for b in range(width // PROJ_TN)]
    assert len(offs) * PROJ_TN == MAIN_OUT
    return np.asarray(offs, np.int32)


def _proj_body(*refs, tile0, from_f32):
    if from_f32:
        _, x_ref, g_ref, wkv_ref, wg_ref, w_ref, o32_ref, o16_ref, xn_ref, wmain_out, wkv_out = refs
    else:
        x_ref, g_ref, wkv_ref, w_ref, o32_ref, o16_ref, xn_ref = refs

    @pl.when(pl.program_id(1) == 0)
    def _():
        if from_f32:
            wkv_out[:2 * KV_WIDTH, :] = wkv_ref[...].astype(BF16)
            wkv_out[2 * KV_WIDTH:2 * KV_WIDTH + ROW_ALIGN, :] = wg_ref[...].astype(BF16)
            wkv_out[2 * KV_WIDTH + ROW_ALIGN:, :] = jnp.zeros((LANES - ROW_ALIGN, D_MODEL), BF16)
            wkv = wkv_out[...]
        else:
            wkv = wkv_ref[...]
        tm = x_ref.shape[0]
        slab = tm // NORM_SLABS if tm % (NORM_SLABS * ROW_ALIGN) == 0 else tm
        for r in range(0, tm, slab):
            x = x_ref[r:r + slab, :]
            ms = jnp.mean(x * x, axis=-1, keepdims=True)
            xn = ((x * lax.rsqrt(ms + EPS)) * g_ref[...]).astype(BF16)
            xn_ref[r:r + slab, :] = xn
            o32_ref[r:r + slab, :] = lax.dot_general(xn, wkv, _NT, preferred_element_type=F32)

    group = (pl.program_id(1) + tile0) // (D_MODEL // PROJ_TN)
    is_sigmoid = (group == COL_OM) | (group == COL_GA) | (group == COL_GM)
    is_silu = (group == COL_ZA) | (group == COL_ZM)
    is_q = group == COL_QA

    def tile(act):
        w = w_ref[...].astype(BF16)
        if from_f32:
            wmain_out[...] = w
        acc = lax.dot_general(xn_ref[...], w, _NT, preferred_element_type=F32)
        o16_ref[...] = act(acc).astype(BF16)

    pl.when(is_sigmoid)(lambda: tile(_sigmoid))
    pl.when(is_silu)(lambda: tile(_silu))
    pl.when(is_q)(lambda: tile(lambda acc: acc * Q_SCALE))
    pl.when(jnp.logical_not(is_sigmoid | is_silu | is_q))(lambda: tile(lambda acc: acc))


def _project_first(x, g, w_t):
    rows = x.shape[0]
    assert SRC_KA % (2 * KV_WIDTH) == 0 and SRC_VA == SRC_KA + KV_WIDTH
    assert SRC_IG % ROW_ALIGN == 0 and SRC_FG == SRC_IG + M_HEADS and 2 * M_HEADS == ROW_ALIGN
    n_tiles = MAIN_OUT // PROJ_TN
    return pl.pallas_call(
        functools.partial(_proj_body, tile0=0, from_f32=True),
        grid_spec=pltpu.PrefetchScalarGridSpec(
            num_scalar_prefetch=1,
            grid=(1, n_tiles),
            in_specs=[
                pl.BlockSpec((rows, D_MODEL), lambda i, j, off: (0, 0)),
                pl.BlockSpec((1, D_MODEL), lambda i, j, off: (0, 0)),
                pl.BlockSpec((2 * KV_WIDTH, D_MODEL), lambda i, j, off: (SRC_KA // (2 * KV_WIDTH), 0),
                             pipeline_mode=pl.Buffered(1)),
                pl.BlockSpec((ROW_ALIGN, D_MODEL), lambda i, j, off: (SRC_IG // ROW_ALIGN, 0),
                             pipeline_mode=pl.Buffered(1)),
                pl.BlockSpec((pl.Element(PROJ_TN), pl.Element(D_MODEL)),
                             lambda i, j, off: (off[j] * ROW_ALIGN, 0)),
            ],
            out_specs=[
                pl.BlockSpec((rows, KV_OUT), lambda i, j, off: (0, 0)),
                pl.BlockSpec((rows, PROJ_TN), lambda i, j, off: (0, j)),
                pl.BlockSpec((rows, D_MODEL), lambda i, j, off: (0, 0)),
                pl.BlockSpec((PROJ_TN, D_MODEL), lambda i, j, off: (j, 0)),
                pl.BlockSpec((KV_OUT, D_MODEL), lambda i, j, off: (0, 0)),
            ],
        ),
        out_shape=[jax.ShapeDtypeStruct((rows, KV_OUT), F32),
                   jax.ShapeDtypeStruct((rows, MAIN_OUT), BF16),
                   jax.ShapeDtypeStruct((rows, D_MODEL), BF16),
                   jax.ShapeDtypeStruct((MAIN_OUT, D_MODEL), BF16),
                   jax.ShapeDtypeStruct((KV_OUT, D_MODEL), BF16)],
        compiler_params=pltpu.CompilerParams(
            dimension_semantics=("arbitrary", "arbitrary"), vmem_limit_bytes=VMEM_LIMIT),
        name="project_first",
    )(jnp.asarray(_main_row_offsets()), x, g, w_t, w_t, w_t)


def _project(x, g, wkv, wmain, tm, col0):
    rows = x.shape[0]
    assert rows % tm == 0
    tile0 = col0 * (D_MODEL // PROJ_TN)
    n_tiles = MAIN_OUT // PROJ_TN - tile0
    return pl.pallas_call(
        functools.partial(_proj_body, tile0=tile0, from_f32=False),
        grid=(rows // tm, n_tiles),
        in_specs=[
            pl.BlockSpec((tm, D_MODEL), lambda i, j: (i, 0)),
            pl.BlockSpec((1, D_MODEL), lambda i, j: (0, 0)),
            pl.BlockSpec((KV_OUT, D_MODEL), lambda i, j: (0, 0), pipeline_mode=pl.Buffered(1)),
            pl.BlockSpec((PROJ_TN, D_MODEL), lambda i, j: (j + tile0, 0)),
        ],
        out_specs=[
            pl.BlockSpec((tm, KV_OUT), lambda i, j: (i, 0)),
            pl.BlockSpec((tm, PROJ_TN), lambda i, j: (i, j)),
            pl.BlockSpec((tm, D_MODEL), lambda i, j: (i, 0)),
        ],
        out_shape=[jax.ShapeDtypeStruct((rows, KV_OUT), F32),
                   jax.ShapeDtypeStruct((rows, n_tiles * PROJ_TN), BF16),
                   jax.ShapeDtypeStruct((rows, D_MODEL), BF16)],
        compiler_params=pltpu.CompilerParams(
            dimension_semantics=("parallel", "arbitrary"), vmem_limit_bytes=VMEM_LIMIT),
        name="project",
    )(x, g, wkv, wmain)


def _proj_t_body(wq_ref, wv_ref, xn_ref, o_ref):
    xn = xn_ref[...]
    q_t = (lax.dot_general(wq_ref[...], xn, _NT, preferred_element_type=F32) * Q_SCALE).astype(BF16)
    v_t = lax.dot_general(wv_ref[...], xn, _NT, preferred_element_type=F32).astype(BF16)
    for p in range(o_ref.shape[0]):
        o_ref[p, :ATTN_WIDTH, :] = q_t[:, p * PAIR:(p + 1) * PAIR]
        o_ref[p, ATTN_WIDTH:, :] = v_t[:, p * PAIR:(p + 1) * PAIR]


def _project_t(wmain, wkv, xn, tm):
    rows = xn.shape[0]
    assert rows % tm == 0 and tm % PAIR == 0
    return pl.pallas_call(
        _proj_t_body,
        grid=(rows // tm,),
        in_specs=[pl.BlockSpec((ATTN_WIDTH, D_MODEL), lambda i: (COL_QA, 0), pipeline_mode=pl.Buffered(1)),
                  pl.BlockSpec((KV_WIDTH, D_MODEL), lambda i: (1, 0), pipeline_mode=pl.Buffered(1)),
                  pl.BlockSpec((tm, D_MODEL), lambda i: (i, 0))],
        out_specs=pl.BlockSpec((tm // PAIR, QV_ROWS, PAIR), lambda i: (i, 0, 0)),
        out_shape=jax.ShapeDtypeStruct((rows // PAIR, QV_ROWS, PAIR), BF16),
        compiler_params=pltpu.CompilerParams(
            dimension_semantics=("parallel",), vmem_limit_bytes=VMEM_LIMIT),
        name="project_t",
    )(wmain, wkv, xn)


def _attn_pair_setup(p, ka_ref, kb_ref, km_ref, vta_ref, vtb_ref, vtm_ref):
    ka, kb = ka_ref[...], kb_ref[...]
    lane = lax.broadcasted_iota(jnp.int32, (CHUNK, LANES), 1)
    k_ad = []
    for t in range(KV_WIDTH // LANES):
        a_tile = ka[:CHUNK, t * LANES:(t + 1) * LANES]
        d_swapped = pltpu.roll(kb[CHUNK:, t * LANES:(t + 1) * LANES], HEAD_DIM, axis=1)
        k_ad.append(jnp.where(lane < HEAD_DIM, a_tile, d_swapped).astype(BF16))
        k_ad.append(jnp.where(lane < HEAD_DIM, d_swapped, a_tile).astype(BF16))
    k_rest = jnp.concatenate([kb[:CHUNK], ka[CHUNK:], km_ref[...]], axis=0).astype(BF16)
    vta, vtb = vta_ref[...], vtb_ref[...]
    first_half = lax.broadcasted_iota(jnp.int32, vta.shape, 1) < CHUNK
    vt = jnp.concatenate([jnp.where(first_half, vta, vtb), jnp.where(first_half, vtb, vta), vtm_ref[...]],
                         axis=1)
    even = (lax.broadcasted_iota(jnp.int32, (CHUNK, GQA_GROUP * PAIR), 1) % PAIR) < CHUNK
    return k_ad, k_rest, vt, (even, p == 0)


def _attn_pair_probs(setup, h, qt_ref, sink_ref):
    k_ad, k_rest, _, (even, first) = setup
    rows = [(h * GQA_GROUP + g) * HEAD_DIM for g in range(GQA_GROUP)]
    qth = jnp.concatenate([qt_ref[r:r + HEAD_DIM, :] for r in rows], axis=1)
    even_b = even.astype(BF16)
    q_even, q_odd = qth * even_b, qth * (1 - even_b)
    q2 = jnp.concatenate([q_even, q_odd] if h % 2 == 0 else [q_odd, q_even], axis=0)
    s_ad = jnp.dot(k_ad[h], q2, preferred_element_type=F32)
    s_rest = jnp.dot(k_rest[:, h * HEAD_DIM:(h + 1) * HEAD_DIM], qth, preferred_element_type=F32)
    st = jnp.concatenate([
        jnp.where(first & even, NEG_INF, s_ad),
        s_rest[:CHUNK],
        jnp.where(first, NEG_INF, s_rest[CHUNK:2 * CHUNK]),
        s_rest[2 * CHUNK:]], axis=0)
    sink = sink_ref[h]
    m = jnp.maximum(jnp.max(st, axis=0, keepdims=True), sink)
    e = jnp.exp(st - m)
    even_f = even.astype(F32)
    e = jnp.concatenate([e[:CHUNK] * even_f, e[:CHUNK] * (1.0 - even_f), e[CHUNK:]], axis=0)
    return e.astype(BF16), jnp.exp(sink - m)


def _attn_pair_out(setup, h, probs):
    _, _, vt, _ = setup
    e, e_sink = probs
    ones = jnp.ones((ONES_ROWS, PAIR_KEYS), BF16)
    vth = jnp.concatenate([vt[h * HEAD_DIM:(h + 1) * HEAD_DIM, :], ones], axis=0)
    ot = jnp.dot(vth, e, preferred_element_type=F32)
    denom = ot[HEAD_DIM:HEAD_DIM + 1, :] + e_sink
    ot = ot[:HEAD_DIM, :] * (1.0 / denom)
    return jnp.concatenate([ot[:, g * PAIR:(g + 1) * PAIR] for g in range(GQA_GROUP)], axis=0).T


def _attend_probs(q, k, sink_ref, lq):
    probs = []
    for h in range(N_KV_HEADS):
        kh = k[:, h * HEAD_DIM:(h + 1) * HEAD_DIM]
        heads = [h * GQA_GROUP + g for g in range(GQA_GROUP)]
        qh = jnp.concatenate([q[:, n * HEAD_DIM:(n + 1) * HEAD_DIM] for n in heads], axis=0)
        s = lax.dot_general(qh, kh, _NT, preferred_element_type=F32)
        sink = jnp.concatenate(
            [jnp.broadcast_to(sink_ref[n:n + 1, 0:1], (lq, 1)) for n in heads], axis=0)
        m = jnp.maximum(jnp.max(s, axis=-1, keepdims=True), sink)
        e = jnp.exp(s - m)
        probs.append((e.astype(BF16), jnp.sum(e, axis=-1, keepdims=True) + jnp.exp(sink - m)))
    return probs


def _attend_values(probs, v, o_ref, lq):
    for h in range(N_KV_HEADS):
        vh = v[:, h * HEAD_DIM:(h + 1) * HEAD_DIM]
        e, denom = probs[h]
        oh = jnp.dot(e, vh, preferred_element_type=F32) * (1.0 / denom)
        o_ref[:, h * GQA_GROUP * HEAD_DIM:(h + 1) * GQA_GROUP * HEAD_DIM] = jnp.concatenate(
            [oh[g * lq:(g + 1) * lq, :] for g in range(GQA_GROUP)], axis=1).astype(o_ref.dtype)


def _attn_sample_body(q_ref, ck_ref, cv_ref, kn_ref, vn_ref, km_ref, vm_ref, sink_ref, o_ref, ko_ref, vo_ref):
    lq = q_ref.shape[0]
    rows = ko_ref.shape[0]
    k_all = jnp.concatenate([ck_ref[...], kn_ref[...]], axis=0)
    v_all = jnp.concatenate([cv_ref[...], vn_ref[...]], axis=0)
    ko_ref[...] = k_all[k_all.shape[0] - rows:, :]
    vo_ref[...] = v_all[v_all.shape[0] - rows:, :]
    k = jnp.concatenate([ck_ref[...], kn_ref[...], km_ref[...]], axis=0).astype(BF16)
    probs = _attend_probs(q_ref[...], k, sink_ref, lq)
    v = jnp.concatenate([cv_ref[...], vn_ref[...], vm_ref[...]], axis=0).astype(BF16)
    _attend_values(probs, v, o_ref, lq)


def _attend_sample(o16, o32, cache_k, cache_v, meta_block, sinks, n_streams, ds):
    cache_rows = cache_k.shape[1]
    new = lambda col: pl.BlockSpec((ds, KV_WIDTH), lambda s: (s, col))
    meta = lambda col: pl.BlockSpec((N_META, KV_WIDTH), lambda s: (meta_block, col))
    cache = pl.BlockSpec((None, cache_rows, KV_WIDTH), lambda s: (s, 0, 0))
    return pl.pallas_call(
        _attn_sample_body,
        grid=(n_streams,),
        in_specs=[
            pl.BlockSpec((ds, ATTN_WIDTH), lambda s: (s, COL_QA)),
            cache, cache, new(0), new(1), meta(0), meta(1),
            pl.BlockSpec((N_Q_HEADS, LANES), lambda s: (0, 0)),
        ],
        out_specs=[pl.BlockSpec((ds, ATTN_WIDTH), lambda s: (s, 0)), cache, cache],
        out_shape=[jax.ShapeDtypeStruct((n_streams * ds, ATTN_WIDTH), BF16),
                   jax.ShapeDtypeStruct(cache_k.shape, cache_k.dtype),
                   jax.ShapeDtypeStruct(cache_v.shape, cache_v.dtype)],
        compiler_params=pltpu.CompilerParams(
            dimension_semantics=("parallel",), vmem_limit_bytes=VMEM_LIMIT),
        name="attend_sample",
    )(o16, cache_k, cache_v, o32, o32, o32, o32, sinks)


def _mlstm_body(q_ref, k_ref, v_ref, gt_ref, bias_ref, c0_ref, n0_ref, m0_ref,
                hn_ref, c_ref, n_ref, m_ref, *, blk):
    setup = _mlstm_setup(gt_ref, bias_ref, blk)
    scores = [_mlstm_scores(setup, h, q_ref, k_ref, m0_ref) for h in range(M_HEADS)]
    for h in range(M_HEADS):
        hv = _mlstm_out(setup, h, scores[h], q_ref, v_ref, c0_ref, n0_ref)
        hn_ref[:, h * M_V_DIM:(h + 1) * M_V_DIM] = hv.astype(hn_ref.dtype)
    for h in range(M_HEADS):
        _mlstm_state(setup, h, scores[h], k_ref, v_ref, (c0_ref, n0_ref, m0_ref), (c_ref, n_ref, m_ref),
                     blk, None)


def _mlstm_setup(gt_ref, bias_ref, blk):
    x = gt_ref[...] + bias_ref[...]
    lane = lax.broadcasted_iota(jnp.int32, x.shape, 1)
    log_f = jnp.minimum(x, 0.0) - jnp.log(1.0 + jnp.exp(-jnp.abs(x)))
    gates = jnp.where(lane < M_HEADS, x, log_f)
    row = lax.broadcasted_iota(jnp.int32, (blk, blk), 0)
    col = lax.broadcasted_iota(jnp.int32, (blk, blk), 1)
    causal = row >= col
    hi = gates.astype(BF16)
    rest = gates - hi.astype(F32)
    mid = rest.astype(BF16)
    lo = (rest - mid.astype(F32)).astype(BF16)
    parts = jnp.dot(causal.astype(BF16), jnp.concatenate([hi, mid, lo], axis=1), preferred_element_type=F32)
    csum = parts[:, :LANES] + parts[:, LANES:2 * LANES] + parts[:, 2 * LANES:]
    z = jnp.where(lane < M_HEADS, gates, csum)
    pad = (-blk) % LANES
    zsq = z if pad == 0 else jnp.concatenate([z, jnp.zeros((pad, LANES), F32)], axis=0)
    zt = zsq.T[:, :blk]
    return z, zt, causal


def _mlstm_scores(setup, h, q_ref, k_ref, m_ref):
    z, zt, causal = setup
    g_r = zt[h:h + 1, :] - zt[M_HEADS + h:M_HEADS + h + 1, :]
    m_prev = m_ref[h:h + 1, 0:1]
    d0 = jnp.where(causal, g_r, -jnp.inf)
    mm = jnp.maximum(m_prev, jnp.max(d0, axis=-1, keepdims=True))
    w = jnp.exp(d0 - mm)
    qh = q_ref[:, h * M_QK_DIM:(h + 1) * M_QK_DIM]
    kh = k_ref[:, h * M_QK_DIM:(h + 1) * M_QK_DIM]
    qk = lax.dot_general(qh, kh, _NT, preferred_element_type=F32)
    wqk = w * (qk * MLSTM_K_SCALE)
    return m_prev, mm, wqk.astype(BF16), jnp.sum(wqk, axis=-1, keepdims=True)


def _mlstm_finish(setup, h, scores, q_ref, k_ref, v_ref, c_ref, n_ref, m_ref, blk, live):
    hv = _mlstm_out(setup, h, scores, q_ref, v_ref, c_ref, n_ref)
    state = (c_ref, n_ref, m_ref)
    _mlstm_state(setup, h, scores, k_ref, v_ref, state, state, blk, live)
    return hv


def _mlstm_out(setup, h, scores, q_ref, v_ref, c_ref, n_ref):
    z, _, _ = setup
    m_prev, mm, wqk, den_intra = scores
    m_t = z[:, M_HEADS + h:M_HEADS + h + 1] + mm
    a = jnp.exp(m_prev - mm)
    qh = q_ref[:, h * M_QK_DIM:(h + 1) * M_QK_DIM]
    vh = v_ref[:, h * M_V_DIM:(h + 1) * M_V_DIM]
    num = a * jnp.dot(qh, c_ref[h].astype(BF16), preferred_element_type=F32) + jnp.dot(
        wqk, vh, preferred_element_type=F32)
    den = a * jnp.sum(qh.astype(F32) * n_ref[h:h + 1, :], axis=-1, keepdims=True) + den_intra
    r = 1.0 / jnp.maximum(jnp.abs(den), jnp.exp(-m_t))
    rms_h = r * jnp.sqrt(jnp.mean(num * num, axis=-1, keepdims=True))
    return num * (r * lax.rsqrt(rms_h * rms_h + EPS))


def _mlstm_state(setup, h, scores, k_ref, v_ref, state_in, state_out, blk, live):
    c_ref, n_ref, m_in = state_in
    c_out, n_out, m_out = state_out
    z, _, _ = setup
    m_prev, mm, _, _ = scores
    keep = (lambda new, old: new) if live is None else (lambda new, old: jnp.where(live, new, old))
    li_c = z[:, h:h + 1]
    b_c = z[:, M_HEADS + h:M_HEADS + h + 1]
    m_t = b_c + mm
    kh = k_ref[:, h * M_QK_DIM:(h + 1) * M_QK_DIM]
    vh = v_ref[:, h * M_V_DIM:(h + 1) * M_V_DIM]
    c_h = c_ref[h]
    n_h = n_ref[h:h + 1, :]

    m_new = m_t[blk - 1:blk, :]
    b_last = b_c[blk - 1:blk, :]
    ws = jnp.exp(b_last - b_c + li_c - m_new) * MLSTM_K_SCALE
    a_last = jnp.exp(b_last + m_prev - m_new)
    vs = (vh.astype(F32) * ws).astype(BF16)
    ktv = lax.dot_general(kh, vs, (((0,), (0,)), ((), ())), preferred_element_type=F32)
    c_out[h] = keep(a_last * c_h + ktv, c_h)
    n_out[h:h + 1, :] = keep(a_last * n_h + jnp.sum(kh.astype(F32) * ws, axis=0, keepdims=True), n_h)
    m_out[h:h + 1, :] = keep(jnp.broadcast_to(m_new, (1, LANES)), m_in[h:h + 1, :])


def _mlstm(o16, o32, bias, c0, n0, m0, *, blk, n_streams, first_block, col0):
    rb = lambda s: first_block + s
    qk_blk = 2 * (COL_QKM - col0)
    state = lambda *shape: pl.BlockSpec((None,) + shape, lambda s: (s,) + (0,) * len(shape))
    return pl.pallas_call(
        functools.partial(_mlstm_body, blk=blk),
        grid=(n_streams,),
        in_specs=[
            pl.BlockSpec((blk, M_QK_WIDTH), lambda s: (rb(s), qk_blk)),
            pl.BlockSpec((blk, M_QK_WIDTH), lambda s: (rb(s), qk_blk + 1)),
            pl.BlockSpec((blk, M_V_WIDTH), lambda s: (rb(s), COL_VM - col0)),
            pl.BlockSpec((blk, LANES), lambda s: (rb(s), GATE_BLOCK)),
            pl.BlockSpec((1, LANES), lambda s: (0, 0)),
            state(M_HEADS, M_QK_DIM, M_V_DIM), state(M_HEADS, M_QK_DIM), state(M_HEADS, LANES),
        ],
        out_specs=[
            pl.BlockSpec((blk, M_V_WIDTH), lambda s: (s, 0)),
            state(M_HEADS, M_QK_DIM, M_V_DIM), state(M_HEADS, M_QK_DIM), state(M_HEADS, LANES),
        ],
        out_shape=[
            jax.ShapeDtypeStruct((n_streams * blk, M_V_WIDTH), BF16),
            jax.ShapeDtypeStruct((n_streams, M_HEADS, M_QK_DIM, M_V_DIM), F32),
            jax.ShapeDtypeStruct((n_streams, M_HEADS, M_QK_DIM), F32),
            jax.ShapeDtypeStruct((n_streams, M_HEADS, LANES), F32),
        ],
        compiler_params=pltpu.CompilerParams(
            dimension_semantics=("parallel",), vmem_limit_bytes=VMEM_LIMIT),
        name="mlstm_blk%d" % blk,
    )(o16, o16, o16, o32, bias, c0, n0, m0)


def _merge_body(x_ref, oa_ref, hn_ref, za_ref, om_ref, zm_ref, ga_ref, gm_ref, gmh_ref,
                wpa_ref, wpm_ref, wout_ref, gf_ref, y_ref, wpa_out, wpm_out, wout_out, a_scr, m_scr, acc_scr):
    j = pl.program_id(0)

    @pl.when(j == 0)
    def _():
        a_scr[...] = (oa_ref[...].astype(F32) * za_ref[...].astype(F32)).astype(BF16)
        m_scr[...] = ((hn_ref[...].astype(F32) * gmh_ref[...]) * om_ref[...].astype(F32)
                      * zm_ref[...].astype(F32)).astype(BF16)
        acc_scr[...] = x_ref[...]

    wpa_out[...] = wpa_ref[...].astype(BF16)
    wpm_out[...] = wpm_ref[...].astype(BF16)
    wout_out[...] = wout_ref[...].astype(BF16)
    ya = jnp.dot(a_scr[...], wpa_out[...], preferred_element_type=F32)
    ym = jnp.dot(m_scr[...], wpm_out[...], preferred_element_type=F32)
    merged = ga_ref[...].astype(F32) * ya + gm_ref[...].astype(F32) * ym
    acc_scr[...] += jnp.dot(merged.astype(BF16), wout_out[...], preferred_element_type=F32)

    @pl.when(j == pl.num_programs(0) - 1)
    def _():
        xo = acc_scr[...]
        ms = jnp.mean(xo * xo, axis=-1, keepdims=True)
        y_ref[...] = (xo * lax.rsqrt(ms + EPS)) * gf_ref[...]


def _merge(x, o_att, hn, o16, g_mh, w_pa, w_pm, w_out, g_final, rows, col0):
    n_chunks = D_MODEL // MERGE_TN
    per_group = D_MODEL // MERGE_TN
    full = lambda col: pl.BlockSpec((rows, D_MODEL), lambda j: (0, col))
    gate = lambda col: pl.BlockSpec((rows, MERGE_TN), lambda j: (0, col * per_group + j))
    vec = pl.BlockSpec((1, D_MODEL), lambda j: (0, 0))
    w_cols = pl.BlockSpec((D_MODEL, MERGE_TN), lambda j: (0, j))
    w_rows = pl.BlockSpec((MERGE_TN, D_MODEL), lambda j: (j, 0))
    return pl.pallas_call(
        _merge_body,
        grid=(n_chunks,),
        in_specs=[full(0), full(0), full(0), full(COL_ZA - col0), full(COL_OM - col0), full(COL_ZM - col0),
                  gate(COL_GA - col0), gate(COL_GM - col0), vec, w_cols, w_cols, w_rows, vec],
        out_specs=[full(0), w_cols, w_cols, w_rows],
        out_shape=[jax.ShapeDtypeStruct((rows, D_MODEL), F32)]
        + [jax.ShapeDtypeStruct((D_MODEL, D_MODEL), BF16)] * 3,
        scratch_shapes=[pltpu.VMEM((rows, D_MODEL), BF16), pltpu.VMEM((rows, D_MODEL), BF16),
                        pltpu.VMEM((rows, D_MODEL), F32)],
        compiler_params=pltpu.CompilerParams(
            dimension_semantics=("arbitrary",), vmem_limit_bytes=VMEM_LIMIT),
        name="merge",
    )(x, o_att, hn, o16, o16, o16, o16, o16, g_mh, w_pa, w_pm, w_out, g_final)


def _mixer_body(qt_ref, ka_ref, kb_ref, km_ref, vta_ref, vtb_ref, vtm_ref, sink_ref,
                q_ref, k_ref, v_ref, gt_ref, bias_ref, c0_ref, n0_ref, m0_ref,
                x_ref, za_ref, om_ref, zm_ref, ga_ref, gm_ref, gmh_ref, wpa_ref, wpm_ref, wout_ref, gf_ref,
                y_ref, c_ref, n_ref, m_ref, a_scr, m_scr, *, n_blk):
    c = pl.program_id(0)

    @pl.when(c == 0)
    def _():
        c_ref[...] = c0_ref[...]
        n_ref[...] = n0_ref[...]
        m_ref[...] = m0_ref[...]
        a_scr[...] = jnp.zeros(a_scr.shape, a_scr.dtype)
        m_scr[...] = jnp.zeros(m_scr.shape, m_scr.dtype)

    wr = c % MIX_SLOTS
    rd = 1 - wr
    blk = jnp.minimum(c, n_blk - 1)
    live = c < n_blk
    width = GQA_GROUP * HEAD_DIM
    heads_per_piece = width // M_V_DIM
    a_in = a_scr[rd]
    m_in = m_scr[rd]

    def mlstm_piece(h):
        hc = slice(h * M_V_DIM, (h + 1) * M_V_DIM)
        hv = _mlstm_finish(cell, h, scores[h], q_ref, k_ref, v_ref, c_ref, n_ref, m_ref, PAIR, live)
        m_scr[wr, :, hc] = ((hv * gmh_ref[:, hc]) * om_ref[:, hc].astype(F32)
                            * zm_ref[:, hc].astype(F32)).astype(BF16)

    def attn_piece(h, probs):
        cols = slice(h * width, (h + 1) * width)
        oa = _attn_pair_out(attn, h, probs)
        a_scr[wr, :, cols] = (oa * za_ref[:, cols].astype(F32)).astype(BF16)

    attn = _attn_pair_setup(blk, ka_ref, kb_ref, km_ref, vta_ref, vtb_ref, vtm_ref)
    cell = _mlstm_setup(gt_ref, bias_ref, PAIR)
    scores = [_mlstm_scores(cell, h, q_ref, k_ref, m_ref) for h in range(M_HEADS)]
    probs = _attn_pair_probs(attn, 0, qt_ref, sink_ref)
    merged = []
    mlstm_heads = ((0, 1, 2), (3, 4, 5), (6, 7), ())
    for i in range(N_KV_HEADS):
        last = i + 1 == N_KV_HEADS
        cols = slice(i * width, (i + 1) * width)
        ya = jnp.dot(a_in, wpa_ref[:, cols], preferred_element_type=F32)
        for h in mlstm_heads[i][:2]:
            mlstm_piece(h)
        next_probs = None if last else _attn_pair_probs(attn, i + 1, qt_ref, sink_ref)
        ym = jnp.dot(m_in, wpm_ref[:, cols], preferred_element_type=F32)
        for h in mlstm_heads[i][2:]:
            mlstm_piece(h)
        merged.append((ga_ref[:, cols].astype(F32) * ya + gm_ref[:, cols].astype(F32) * ym).astype(BF16))
        if last:
            xo = x_ref[...] + jnp.dot(jnp.concatenate(merged, axis=1), wout_ref[...],
                                      preferred_element_type=F32)
        attn_piece(i, probs)
        probs = next_probs
    ms = jnp.mean(xo * xo, axis=-1, keepdims=True)
    y_ref[...] = (xo * lax.rsqrt(ms + EPS)) * gf_ref[...]


def _mixer(x, qvt, o32, o16, k_meta, vt_meta, sinks_t, bias, c0, n0, m0, g_mh, w_pa, w_pm, w_out, g_final,
           col0):
    seq = x.shape[0]
    assert seq % PAIR == 0
    n_blk = seq // PAIR
    cur = lambda c: jnp.minimum(c, n_blk - 1)
    prv = lambda c: jnp.maximum(cur(c) - 1, 0)
    mrg = lambda c: jnp.maximum(c - 1, 0)
    v_row = ATTN_WIDTH // KV_WIDTH
    qk_blk = 2 * (COL_QKM - col0)
    tile = lambda col: pl.BlockSpec((PAIR, D_MODEL), lambda c: (mrg(c), col))
    tile_cur = lambda col: pl.BlockSpec((PAIR, D_MODEL), lambda c: (cur(c), col))
    vec = pl.BlockSpec((1, D_MODEL), lambda c: (0, 0))
    weight = pl.BlockSpec((D_MODEL, D_MODEL), lambda c: (0, 0), pipeline_mode=pl.Buffered(1))
    state = lambda *blk: pl.BlockSpec((None,) + blk, lambda c: (0,) * (len(blk) + 1))
    return pl.pallas_call(
        functools.partial(_mixer_body, n_blk=n_blk),
        grid=(n_blk + 1,),
        in_specs=[
            pl.BlockSpec((None, ATTN_WIDTH, PAIR), lambda c: (cur(c), 0, 0)),
            pl.BlockSpec((PAIR, KV_WIDTH), lambda c: (prv(c), 0)),
            pl.BlockSpec((PAIR, KV_WIDTH), lambda c: (cur(c), 0)),
            pl.BlockSpec((N_META, KV_WIDTH), lambda c: (0, 0)),
            pl.BlockSpec((None, KV_WIDTH, PAIR), lambda c: (prv(c), v_row, 0)),
            pl.BlockSpec((None, KV_WIDTH, PAIR), lambda c: (cur(c), v_row, 0)),
            pl.BlockSpec((KV_WIDTH, N_META), lambda c: (0, 0)),
            pl.BlockSpec((N_KV_HEADS, 1, GQA_GROUP * PAIR), lambda c: (0, 0, 0)),
            pl.BlockSpec((PAIR, M_QK_WIDTH), lambda c: (cur(c), qk_blk)),
            pl.BlockSpec((PAIR, M_QK_WIDTH), lambda c: (cur(c), qk_blk + 1)),
            pl.BlockSpec((PAIR, M_V_WIDTH), lambda c: (cur(c), COL_VM - col0)),
            pl.BlockSpec((PAIR, LANES), lambda c: (cur(c), GATE_BLOCK)),
            pl.BlockSpec((1, LANES), lambda c: (0, 0)),
            state(M_HEADS, M_QK_DIM, M_V_DIM), state(M_HEADS, M_QK_DIM), state(M_HEADS, LANES),
            tile(0), tile_cur(COL_ZA - col0), tile_cur(COL_OM - col0), tile_cur(COL_ZM - col0),
            tile(COL_GA - col0), tile(COL_GM - col0), vec, weight, weight, weight, vec,
        ],
        out_specs=[
            tile(0),
            state(M_HEADS, M_QK_DIM, M_V_DIM), state(M_HEADS, M_QK_DIM), state(M_HEADS, LANES),
        ],
        out_shape=[
            jax.ShapeDtypeStruct((seq, D_MODEL), F32),
            jax.ShapeDtypeStruct((1, M_HEADS, M_QK_DIM, M_V_DIM), F32),
            jax.ShapeDtypeStruct((1, M_HEADS, M_QK_DIM), F32),
            jax.ShapeDtypeStruct((1, M_HEADS, LANES), F32),
        ],
        scratch_shapes=[pltpu.VMEM((MIX_SLOTS, PAIR, ATTN_WIDTH), BF16),
                        pltpu.VMEM((MIX_SLOTS, PAIR, M_V_WIDTH), BF16)],
        compiler_params=pltpu.CompilerParams(
            dimension_semantics=("arbitrary",), vmem_limit_bytes=VMEM_LIMIT),
        name="mixer",
    )(qvt, o32, o32, k_meta, qvt, qvt, vt_meta, sinks_t,
      o16, o16, o16, o32, bias, c0, n0, m0,
      x, o16, o16, o16, o16, o16, g_mh, w_pa, w_pm, w_out, g_final)


def kernel(x_prompt, x_sample, cache_k, cache_v, state_C, state_n, state_m, meta_tokens, g_norm, w_in,
           b_igate, b_fgate, attn_sinks, g_mhnorm, w_pa, w_pm, w_out, g_final):
    batch, seq, _ = x_prompt.shape
    db, ds, _ = x_sample.shape
    depth = w_in.shape[0]
    assert batch == 1 and depth == 1 and ds == N_META
    cache_rows = cache_k.shape[2]

    w_t = w_in[0].T
    g_in = g_norm[0].reshape(1, D_MODEL)
    bias = jnp.concatenate([b_igate[0], b_fgate[0], jnp.zeros((LANES - 2 * M_HEADS,), F32)]).reshape(1, LANES)
    sinks = jnp.broadcast_to(attn_sinks[0][:, None], (N_Q_HEADS, LANES))
    sinks_t = jnp.broadcast_to(attn_sinks[0].reshape(N_KV_HEADS, 1, GQA_GROUP, 1),
                               (N_KV_HEADS, 1, GQA_GROUP, PAIR)).reshape(N_KV_HEADS, 1, GQA_GROUP * PAIR)
    g_mh = g_mhnorm[0].reshape(1, M_V_WIDTH)
    g_fin = g_final.reshape(1, D_MODEL)

    xp = x_prompt.reshape(seq, D_MODEL)
    xs_rows = db * ds
    x_small = jnp.concatenate([x_sample.reshape(xs_rows, D_MODEL), meta_tokens.astype(F32)], axis=0)
    meta_block = xs_rows // N_META
    p_tm = min(seq, 1024)
    s32, s16, _, wmain, wkv = _project_first(x_small, g_in, w_t)
    p32, p16, xn_p = _project(xp, g_in, wkv, wmain, tm=p_tm, col0=COL_ZA)
    qvt = _project_t(wmain, wkv, xn_p, tm=p_tm)

    zeros = lambda *shape: jnp.zeros(shape, F32)
    _, c_meta, n_meta, m_meta = _mlstm(
        s16, s32, bias, zeros(1, M_HEADS, M_QK_DIM, M_V_DIM), zeros(1, M_HEADS, M_QK_DIM),
        zeros(1, M_HEADS, LANES), blk=N_META, n_streams=1, first_block=meta_block, col0=COL_QA)

    ck = cache_k[0].reshape(db, cache_rows, KV_WIDTH)
    cv = cache_v[0].reshape(db, cache_rows, KV_WIDTH)
    oa_s, k_roll, v_roll = _attend_sample(s16, s32, ck, cv, meta_block, sinks, db, ds)
    m0_s = jnp.broadcast_to(state_m[0][:, :, None], (db, M_HEADS, LANES))
    hn_s, c_s, n_s, m_s = _mlstm(s16, s32, bias, state_C[0], state_n[0], m0_s, blk=ds, n_streams=db,
                                 first_block=0, col0=COL_QA)
    y_s, wpa, wpm, wout = _merge(x_small, oa_s, hn_s, s16, g_mh, w_pa[0], w_pm[0], w_out[0], g_fin,
                                 rows=xs_rows, col0=COL_QA)

    k_meta = s32[xs_rows:, :KV_WIDTH]
    vt_meta = s32[xs_rows:, KV_WIDTH:2 * KV_WIDTH].T.astype(BF16)
    y_p, c_p, n_p, m_p = _mixer(xp, qvt, p32, p16, k_meta, vt_meta, sinks_t, bias, c_meta, n_meta, m_meta,
                                g_mh, wpa, wpm, wout, g_fin, col0=COL_ZA)

    kv_shape = (1, batch, cache_rows, N_KV_HEADS, HEAD_DIM)
    k_p = p32[seq - cache_rows:, :KV_WIDTH].reshape(kv_shape)
    v_p = p32[seq - cache_rows:, KV_WIDTH:2 * KV_WIDTH].reshape(kv_shape)
    new_k = s32[:xs_rows, :KV_WIDTH].reshape(db, ds, KV_WIDTH)
    new_v = s32[:xs_rows, KV_WIDTH:2 * KV_WIDTH].reshape(db, ds, KV_WIDTH)
    skv_shape = (1, db, cache_rows, N_KV_HEADS, HEAD_DIM)
    k_s = k_roll.reshape(skv_shape)
    v_s = v_roll.reshape(skv_shape)

    return (y_p.reshape(batch, seq, D_MODEL), y_s.reshape(db, ds, D_MODEL),
            k_p, v_p, c_p[None], n_p[None], m_p[None, :, :, 0],
            k_s, v_s, c_s[None], n_s[None], m_s[None, :, :, 0])
```
